```python
import math
import jax, jax.numpy as jnp
from jax import lax
import numpy as np

D_MODEL = 1024
BATCH = 2
SEQ = 8192
DEPTH = 1

GRID_W = 64
CTX_LEN = 256
NORM_EPS = 1e-6
ROPE_BASE = 10000.0
MASK_VALUE = -1e30

DA_HEADS = 4
DA_HEAD_DIM = 64
Q_BLOCK = 128
WA_HEADS = 8
WA_KV_HEADS = 2
WA_GROUP = WA_HEADS // WA_KV_HEADS
WA_HEAD_DIM = 64
WINDOW = 128
BAND_BLOCK = WINDOW

QA_COLS = DA_HEADS * 2 * DA_HEAD_DIM
KA_COLS = DA_HEADS * 2 * DA_HEAD_DIM
VA_COLS = DA_HEADS * 2 * DA_HEAD_DIM
QB_COLS = WA_HEADS * WA_HEAD_DIM
KB_COLS = WA_KV_HEADS * WA_HEAD_DIM
VB_COLS = WA_KV_HEADS * WA_HEAD_DIM
IN_COLS = QA_COLS + KA_COLS + VA_COLS + QB_COLS + KB_COLS + VB_COLS
SPLIT_POINTS = [QA_COLS, QA_COLS + KA_COLS, QA_COLS + KA_COLS + VA_COLS,
                QA_COLS + KA_COLS + VA_COLS + QB_COLS,
                QA_COLS + KA_COLS + VA_COLS + QB_COLS + KB_COLS]
DA_OUT = DA_HEADS * 2 * DA_HEAD_DIM
WA_OUT = WA_HEADS * WA_HEAD_DIM
MIX_WIDTH = DA_OUT + WA_OUT

N_EXPERTS = 32
TOP_K = 4
D_EXPERT = 1024
SWIGLU_LIMIT = 7.0
SWIGLU_ALPHA = 1.702
EXPERT_BLOCK = 128

kernel_name = "hymba_diff_swa_moe_dit_block"


def rmsnorm(x, g):
    xf = x.astype(jnp.float32)
    y = xf * lax.rsqrt(jnp.mean(xf * xf, axis=-1, keepdims=True) + NORM_EPS)
    return (y * g.astype(jnp.float32)).astype(x.dtype)


def adaln(cvec, w_mod, b_mod):
    mod = jax.nn.silu(cvec) @ w_mod + b_mod
    return jnp.split(mod, 6, axis=-1)


def modulate(h, shift, scale):
    return h * (1.0 + scale) + shift


def rope_tables(n_tok, head_dim):
    rows_count = n_tok // GRID_W
    rows = jnp.repeat(jnp.arange(rows_count), GRID_W).astype(jnp.float32)
    cols = jnp.tile(jnp.arange(GRID_W), rows_count).astype(jnp.float32)
    nf = head_dim // 4
    inv = ROPE_BASE ** (-jnp.arange(nf, dtype=jnp.float32) / nf)
    ang_r = rows[:, None] * inv
    ang_c = cols[:, None] * inv
    return (jnp.cos(ang_r), jnp.sin(ang_r), jnp.cos(ang_c), jnp.sin(ang_c))


def _rotate(xh, cos, sin):
    nf = xh.shape[-1] // 2
    x1, x2 = xh[..., :nf], xh[..., nf:]
    return jnp.concatenate([x1 * cos - x2 * sin, x2 * cos + x1 * sin], axis=-1)


def axial_rope(x, tabs):
    d = x.shape[-1]
    da = d // 2
    shape = (1, x.shape[1]) + (1,) * (x.ndim - 3) + (da // 2,)
    cr, sr, cc, sc = [t.reshape(shape).astype(x.dtype) for t in tabs]
    return jnp.concatenate([_rotate(x[..., :da], cr, sr), _rotate(x[..., da:], cc, sc)], axis=-1)


def split_proj(p):
    B, N, _ = p.shape
    qa, ka, va, qb, kb, vb = jnp.split(p, SPLIT_POINTS, axis=-1)
    qa = qa.reshape(B, N, DA_HEADS, 2, DA_HEAD_DIM)
    ka = ka.reshape(B, N, DA_HEADS, 2, DA_HEAD_DIM)
    va = va.reshape(B, N, DA_HEADS, 2 * DA_HEAD_DIM)
    qb = qb.reshape(B, N, WA_KV_HEADS, WA_GROUP, WA_HEAD_DIM)
    kb = kb.reshape(B, N, WA_KV_HEADS, WA_HEAD_DIM)
    vb = vb.reshape(B, N, WA_KV_HEADS, WA_HEAD_DIM)
    return qa, ka, va, qb, kb, vb


def diff_attend(q, k, v, lam):
    s = jnp.einsum('bqhcd,bkhcd->bhcqk', q, k).astype(jnp.float32)
    p = jax.nn.softmax(s, axis=-1)
    a = p[:, :, 0] - lam * p[:, :, 1]
    return jnp.einsum('bhqk,bkhe->bqhe', a.astype(v.dtype), v)


def diff_attend_latent(q, k_all, v_all, lam):
    B, S, H, _, d = q.shape
    nqb = S // Q_BLOCK
    qb = jnp.moveaxis(q.reshape(B, nqb, Q_BLOCK, H, 2, d), 1, 0)
    out = lax.map(lambda qq: diff_attend(qq, k_all, v_all, lam), qb)
    return jnp.moveaxis(out, 0, 1).reshape(B, S, H, 2 * d)


def diff_post(o, subln_g, lam_init):
    B, N = o.shape[:2]
    return (rmsnorm(o, subln_g) * (1.0 - lam_init)).reshape(B, N, DA_OUT)


def window_gqa_latent(q, k, v, k_ctx, v_ctx, sink):
    B, S, KV, G, d = q.shape
    W = BAND_BLOCK
    nb = S // W
    pad = ((0, 0), (W, W), (0, 0), (0, 0))
    kp = jnp.pad(k, pad).reshape(B, nb + 2, W, KV, d)
    vp = jnp.pad(v, pad).reshape(B, nb + 2, W, KV, d)
    kband = jnp.concatenate([kp[:, :-2], kp[:, 1:-1], kp[:, 2:]], axis=2)
    vband = jnp.concatenate([vp[:, :-2], vp[:, 1:-1], vp[:, 2:]], axis=2)
    qbk = q.reshape(B, nb, W, KV, G, d)
    s_loc = jnp.einsum('bnqkgd,bnjkd->bnkgqj', qbk, kband).astype(jnp.float32)
    s_ctx = jnp.einsum('bnqkgd,bjkd->bnkgqj', qbk, k_ctx).astype(jnp.float32)
    n_i = jnp.arange(nb)[:, None, None]
    q_i = jnp.arange(W)[None, :, None]
    k_j = jnp.arange(3 * W)[None, None, :]
    kpos = (n_i - 1) * W + k_j
    rel = k_j - W - q_i
    valid = (jnp.abs(rel) <= WINDOW) & (kpos >= 0) & (kpos < S)
    s_loc = jnp.where(valid[None, :, None, None], s_loc, MASK_VALUE)
    sink_col = jnp.broadcast_to(sink.astype(jnp.float32).reshape(1, 1, KV, G, 1, 1),
                                s_loc.shape[:-1] + (1,))
    p = jax.nn.softmax(jnp.concatenate([s_loc, s_ctx, sink_col], axis=-1), axis=-1)
    p_loc = p[..., :3 * W].astype(v.dtype)
    p_ctx = p[..., 3 * W:-1].astype(v.dtype)
    o = (jnp.einsum('bnkgqj,bnjkd->bnqkgd', p_loc, vband)
         + jnp.einsum('bnkgqj,bjkd->bnqkgd', p_ctx, v_ctx))
    return o.reshape(B, S, WA_OUT)


def sink_attend_ctx(q, k, v, sink):
    B, C, KV, G, d = q.shape
    s = jnp.einsum('bqkgd,bjkd->bkgqj', q, k).astype(jnp.float32)
    sink_col = jnp.broadcast_to(sink.astype(jnp.float32).reshape(1, KV, G, 1, 1), s.shape[:-1] + (1,))
    p = jax.nn.softmax(jnp.concatenate([s, sink_col], axis=-1), axis=-1)[..., :-1]
    o = jnp.einsum('bkgqj,bjkd->bqkgd', p.astype(v.dtype), v)
    return o.reshape(B, C, WA_OUT)


def moe(x2d, w_r, b_r, w_gu, b_gu, w_dn, b_dn):
    T, D = x2d.shape
    logits = (x2d @ w_r + b_r).astype(jnp.float32)
    top_v, top_i = lax.top_k(logits, TOP_K)
    gates = jax.nn.softmax(top_v, axis=-1)
    flat_e = top_i.reshape(-1).astype(jnp.int32)
    n_assign = T * TOP_K
    order = jnp.argsort(flat_e)
    sorted_e = flat_e[order]
    counts = jnp.zeros((N_EXPERTS,), jnp.int32).at[flat_e].add(1)
    padded = ((counts + EXPERT_BLOCK - 1) // EXPERT_BLOCK) * EXPERT_BLOCK
    start = jnp.cumsum(counts) - counts
    pend = jnp.cumsum(padded)
    pstart = pend - padded
    rank = jnp.arange(n_assign, dtype=jnp.int32) - start[sorted_e]
    dest_sorted = (pstart[sorted_e] + rank).astype(jnp.int32)
    n_rows = n_assign + N_EXPERTS * EXPERT_BLOCK
    n_blocks = n_rows // EXPERT_BLOCK
    row_tok = jnp.full((n_rows,), T, jnp.int32).at[dest_sorted].set((order // TOP_K).astype(jnp.int32))
    block_start = jnp.arange(n_blocks, dtype=jnp.int32) * EXPERT_BLOCK
    block_exp = jnp.clip(jnp.searchsorted(pend, block_start, side='right'), 0, N_EXPERTS - 1)
    x_pad = jnp.concatenate([x2d, jnp.zeros((1, D), x2d.dtype)], axis=0)

    def expert_block(args):
        tok, e = args
        xb = x_pad[tok]
        h = xb @ w_gu[e] + b_gu[e]
        g = jnp.minimum(h[:, ::2], SWIGLU_LIMIT)
        lin = jnp.clip(h[:, 1::2], -SWIGLU_LIMIT, SWIGLU_LIMIT)
        a = g * jax.nn.sigmoid(SWIGLU_ALPHA * g) * (lin + 1.0)
        return a @ w_dn[e] + b_dn[e]

    rows = lax.map(expert_block, (row_tok.reshape(n_blocks, EXPERT_BLOCK), block_exp))
    rows = rows.reshape(n_rows, D)
    dest = jnp.zeros((n_assign,), jnp.int32).at[order].set(dest_sorted)
    y = rows[dest].reshape(T, TOP_K, D)
    return jnp.einsum('tk,tkd->td', gates.astype(y.dtype), y)


def setup_inputs(seed: int = 0) -> dict:
    key = jax.random.key(seed)
    ks = jax.random.split(key, 24)
    D, L, E, F = D_MODEL, DEPTH, N_EXPERTS, D_EXPERT
    nrm = lambda k, s: jax.random.normal(k, s, jnp.float32)
    return {
        "x": nrm(ks[0], (BATCH, SEQ, D)),
        "c": nrm(ks[1], (BATCH, D)),
        "ctx": nrm(ks[2], (BATCH, CTX_LEN, D)),
        "c_ctx": nrm(ks[3], (D,)),
        "w_mod": nrm(ks[4], (L, D, 6 * D)) * (0.5 * D ** -0.5),
        "b_mod": nrm(ks[5], (L, 6 * D)) * 0.02,
        "norm1_g": 1.0 + 0.05 * nrm(ks[6], (L, D)),
        "w_in": nrm(ks[7], (L, D, IN_COLS)) * D ** -0.5,
        "lam_q1": 0.1 * nrm(ks[8], (L, DA_HEAD_DIM)),
        "lam_k1": 0.1 * nrm(ks[9], (L, DA_HEAD_DIM)),
        "lam_q2": 0.1 * nrm(ks[10], (L, DA_HEAD_DIM)),
        "lam_k2": 0.1 * nrm(ks[11], (L, DA_HEAD_DIM)),
        "subln_g": 1.0 + 0.05 * nrm(ks[12], (L, 2 * DA_HEAD_DIM)),
        "sink": 0.5 * nrm(ks[13], (L, WA_HEADS)),
        "w_out": nrm(ks[14], (L, MIX_WIDTH, D)) * MIX_WIDTH ** -0.5,
        "norm2_g": 1.0 + 0.05 * nrm(ks[15], (L, D)),
        "w_router": nrm(ks[16], (L, D, E)) * D ** -0.5,
        "b_router": 0.01 * nrm(ks[17], (L, E)),
        "w_gate_up": nrm(ks[18], (L, E, D, 2 * F)) * D ** -0.5,
        "b_gate_up": 0.01 * nrm(ks[19], (L, E, 2 * F)),
        "w_down": nrm(ks[20], (L, E, F, D)) * F ** -0.5,
        "b_down": 0.01 * nrm(ks[21], (L, E, D)),
        "final_g": 1.0 + 0.05 * nrm(ks[22], (D,)),
    }


def reference(x, c, ctx, c_ctx, w_mod, b_mod, norm1_g, w_in, lam_q1, lam_k1, lam_q2, lam_k2,
              subln_g, sink, w_out, norm2_g, w_router, b_router, w_gate_up, b_gate_up,
              w_down, b_down, final_g):
    B, S, D = x.shape
    C = ctx.shape[1]
    ROWS = S // GRID_W
    tabs = rope_tables(ROWS * GRID_W, DA_HEAD_DIM)
    scale_a = DA_HEAD_DIM ** -0.5
    scale_b = WA_HEAD_DIM ** -0.5
    cs = ctx
    for l in range(DEPTH):
        lam_init = 0.8 - 0.6 * math.exp(-0.3 * l)
        sh1, sc1, g1, sh2, sc2, g2 = [m[:, None, :] for m in adaln(c, w_mod[l], b_mod[l])]
        csh1, csc1, cg1, csh2, csc2, cg2 = [m[None, None, :] for m in adaln(c_ctx, w_mod[l], b_mod[l])]

        hx = modulate(rmsnorm(x, norm1_g[l]), sh1, sc1)
        hc = modulate(rmsnorm(cs, norm1_g[l]), csh1, csc1)
        qa, ka, va, qb, kb, vb = split_proj(hx @ w_in[l])
        qac, kac, vac, qbc, kbc, vbc = split_proj(hc @ w_in[l])
        qa = axial_rope(qa * scale_a, tabs)
        ka = axial_rope(ka, tabs)
        qb = axial_rope(qb * scale_b, tabs)
        kb = axial_rope(kb, tabs)
        lam = (jnp.exp(jnp.sum(lam_q1[l].astype(jnp.float32) * lam_k1[l].astype(jnp.float32)))
               - jnp.exp(jnp.sum(lam_q2[l].astype(jnp.float32) * lam_k2[l].astype(jnp.float32)))
               + lam_init)
        ya = diff_attend_latent(qa, jnp.concatenate([ka, kac], axis=1),
                                jnp.concatenate([va, vac], axis=1), lam)
        ya = diff_post(ya, subln_g[l], lam_init)
        yb = window_gqa_latent(qb, kb, vb, kbc, vbc, sink[l])
        x = x + g1 * (jnp.concatenate([ya, yb], axis=-1) @ w_out[l])
        if l < DEPTH - 1:
            yac = diff_post(diff_attend(qac * scale_a, kac, vac, lam), subln_g[l], lam_init)
            ybc = sink_attend_ctx(qbc * scale_b, kbc, vbc, sink[l])
            cs = cs + cg1 * (jnp.concatenate([yac, ybc], axis=-1) @ w_out[l])

        hx2 = modulate(rmsnorm(x, norm2_g[l]), sh2, sc2)
        x = x + g2 * moe(hx2.reshape(B * S, D), w_router[l], b_router[l], w_gate_up[l],
                         b_gate_up[l], w_down[l], b_down[l]).reshape(B, S, D)
        if l < DEPTH - 1:
            hc2 = modulate(rmsnorm(cs, norm2_g[l]), csh2, csc2)
            cs = cs + cg2 * moe(hc2.reshape(B * C, D), w_router[l], b_router[l], w_gate_up[l],
                                b_gate_up[l], w_down[l], b_down[l]).reshape(B, C, D)
    return rmsnorm(x, final_g)
```

```python
import functools
import math

import jax
import jax.numpy as jnp
from jax import lax
from jax.experimental import pallas as pl
from jax.experimental.pallas import tpu as pltpu

F32 = jnp.float32
BF16 = jnp.bfloat16
HIGHEST = lax.Precision.HIGHEST

GRID_W = 64
NORM_EPS = 1e-6
ROPE_BASE = 10000.0
MASK_VALUE = -1e30
DA_HEADS = 4
HEAD_DIM = 64
WA_HEADS = 8
WA_KV_HEADS = 2
WA_GROUP = WA_HEADS // WA_KV_HEADS
WINDOW = 128
N_EXPERTS = 32
TOP_K = 4
SWIGLU_LIMIT = 7.0
SWIGLU_ALPHA = 1.702
EXPERT_BLOCK = 128
LAM_INIT = 0.8 - 0.6 * math.exp(-0.3 * 0)

QA_COLS = DA_HEADS * 2 * HEAD_DIM
KA_COLS = QA_COLS
VA_COLS = QA_COLS
QB_COLS = WA_HEADS * HEAD_DIM
KB_COLS = WA_KV_HEADS * HEAD_DIM
VB_COLS = KB_COLS
O_QA = 0
O_KA = O_QA + QA_COLS
O_VA = O_KA + KA_COLS
O_QB = O_VA + VA_COLS
O_KB = O_QB + QB_COLS
O_VB = O_KB + KB_COLS
IN_COLS = O_VB + VB_COLS

LANES = 128
VMEM_LIMIT = 56 * 1024 * 1024


def _cparams(sem):
    return pltpu.CompilerParams(dimension_semantics=sem, vmem_limit_bytes=VMEM_LIMIT)


def _adaln_kernel(c_ref, w_ref, b_ref, o_ref):
    cv = c_ref[...]
    s = cv * (1.0 / (1.0 + jnp.exp(-cv)))
    o_ref[...] = jnp.dot(s, w_ref[...], precision=HIGHEST, preferred_element_type=F32) + b_ref[...]


def _adaln(cvecs, w_mod, b_mod):
    rows, d = cvecs.shape
    n = w_mod.shape[1]
    tn = 1024
    return pl.pallas_call(
        _adaln_kernel,
        grid=(n // tn,),
        in_specs=[pl.BlockSpec((rows, d), lambda j: (0, 0)),
                  pl.BlockSpec((d, tn), lambda j: (0, j)),
                  pl.BlockSpec((1, tn), lambda j: (0, j))],
        out_specs=pl.BlockSpec((rows, tn), lambda j: (0, j)),
        out_shape=jax.ShapeDtypeStruct((rows, n), F32),
        compiler_params=_cparams(("arbitrary",)),
        name="adaln",
    )(cvecs, w_mod, b_mod.reshape(1, n))


def _rope_section(sec, cos, sin):
    tm = sec.shape[0]
    lane = lax.broadcasted_iota(jnp.int32, (tm, LANES), 1)
    low = (lane % 32) < 16
    outs = []
    for j in range(sec.shape[1] // LANES):
        c = sec[:, j * LANES:(j + 1) * LANES]
        partner = jnp.where(low, pltpu.roll(c, LANES - 16, 1), pltpu.roll(c, 16, 1))
        outs.append(c * cos + partner * sin)
    return jnp.concatenate(outs, axis=1)


def _modulated_norm(x, g, shift, scale):
    ms = jnp.mean(x * x, axis=-1, keepdims=True)
    return (x * lax.rsqrt(ms + NORM_EPS) * g) * (1.0 + scale) + shift


def _inproj_latent_kernel(x_ref, g_ref, sh_ref, sc_ref, w_ref, cos_ref, sin_ref,
                          qa_ref, kat_ref, va_ref, qb_ref, kbt_ref, vb_ref):
    h = _modulated_norm(x_ref[0], g_ref[...], sh_ref[0], sc_ref[0])
    p = jnp.dot(h.astype(BF16), w_ref[...], preferred_element_type=F32)
    cos = cos_ref[...]
    sin = sin_ref[...]
    scale = HEAD_DIM ** -0.5
    qa_ref[0] = (_rope_section(p[:, O_QA:O_KA], cos, sin) * scale).astype(BF16)
    kat_ref[0] = _rope_section(p[:, O_KA:O_VA], cos, sin).T.astype(BF16)
    va_ref[0] = p[:, O_VA:O_QB].astype(BF16)
    qb_ref[0] = (_rope_section(p[:, O_QB:O_KB], cos, sin) * scale).astype(BF16)
    kbt_ref[0] = _rope_section(p[:, O_KB:O_VB], cos, sin).T.astype(BF16)
    vb_ref[0] = p[:, O_VB:IN_COLS].astype(BF16)


def _inproj_ctx_kernel(x_ref, g_ref, sh_ref, sc_ref, w_ref, kat_ref, va_ref, kbt_ref, vb_ref):
    h = _modulated_norm(x_ref[0], g_ref[...], sh_ref[0], sc_ref[0])
    p = jnp.dot(h.astype(BF16), w_ref[...], preferred_element_type=F32)
    kat_ref[0] = p[:, 0:KA_COLS].T.astype(BF16)
    va_ref[0] = p[:, KA_COLS:KA_COLS + VA_COLS].astype(BF16)
    kbt_ref[0] = p[:, KA_COLS + VA_COLS:KA_COLS + VA_COLS + KB_COLS].T.astype(BF16)
    vb_ref[0] = p[:, KA_COLS + VA_COLS + KB_COLS:].astype(BF16)


def _rope_tables(n_tok):
    rows = (jnp.arange(n_tok) // GRID_W).astype(F32)
    cols = (jnp.arange(n_tok) % GRID_W).astype(F32)
    nf = HEAD_DIM // 4
    inv = ROPE_BASE ** (-jnp.arange(nf, dtype=F32) / nf)
    ar = rows[:, None] * inv
    ac = cols[:, None] * inv
    cos = jnp.concatenate([jnp.cos(ar), jnp.cos(ar), jnp.cos(ac), jnp.cos(ac)], axis=1)
    sin = jnp.concatenate([-jnp.sin(ar), jnp.sin(ar), -jnp.sin(ac), jnp.sin(ac)], axis=1)
    return jnp.tile(cos, (1, LANES // HEAD_DIM)), jnp.tile(sin, (1, LANES // HEAD_DIM))


def _inproj_latent(x, g, shift, scale, w_bf16, cos, sin, tm):
    b, s, d = x.shape
    row = lambda bi, i: (bi, i, 0)
    colt = lambda bi, i: (bi, 0, i)
    mod = lambda bi, i: (bi, 0, 0)
    fixed = lambda bi, i: (0, 0)
    return pl.pallas_call(
        _inproj_latent_kernel,
        grid=(b, s // tm),
        in_specs=[pl.BlockSpec((1, tm, d), row),
                  pl.BlockSpec((1, d), fixed),
                  pl.BlockSpec((1, 1, d), mod),
                  pl.BlockSpec((1, 1, d), mod),
                  pl.BlockSpec((d, IN_COLS), fixed),
                  pl.BlockSpec((tm, LANES), lambda bi, i: (i, 0)),
                  pl.BlockSpec((tm, LANES), lambda bi, i: (i, 0))],
        out_specs=[pl.BlockSpec((1, tm, QA_COLS), row),
                   pl.BlockSpec((1, KA_COLS, tm), colt),
                   pl.BlockSpec((1, tm, VA_COLS), row),
                   pl.BlockSpec((1, tm, QB_COLS), row),
                   pl.BlockSpec((1, KB_COLS, tm), colt),
                   pl.BlockSpec((1, tm, VB_COLS), row)],
        out_shape=[jax.ShapeDtypeStruct((b, s, QA_COLS), BF16),
                   jax.ShapeDtypeStruct((b, KA_COLS, s), BF16),
                   jax.ShapeDtypeStruct((b, s, VA_COLS), BF16),
                   jax.ShapeDtypeStruct((b, s, QB_COLS), BF16),
                   jax.ShapeDtypeStruct((b, KB_COLS, s), BF16),
                   jax.ShapeDtypeStruct((b, s, VB_COLS), BF16)],
        compiler_params=_cparams(("arbitrary", "arbitrary")),
        name="inproj_latent",
    )(x, g, shift, scale, w_bf16, cos, sin)


def _inproj_ctx(ctx, g, shift, scale, w_ctx_bf16):
    b, c, d = ctx.shape
    n = w_ctx_bf16.shape[1]
    whole = lambda bi: (bi, 0, 0)
    mod = lambda bi: (0, 0, 0)
    fixed = lambda bi: (0, 0)
    return pl.pallas_call(
        _inproj_ctx_kernel,
        grid=(b,),
        in_specs=[pl.BlockSpec((1, c, d), whole),
                  pl.BlockSpec((1, d), fixed),
                  pl.BlockSpec((1, 1, d), mod),
                  pl.BlockSpec((1, 1, d), mod),
                  pl.BlockSpec((d, n), fixed)],
        out_specs=[pl.BlockSpec((1, KA_COLS, c), whole),
                   pl.BlockSpec((1, c, VA_COLS), whole),
                   pl.BlockSpec((1, KB_COLS, c), whole),
                   pl.BlockSpec((1, c, VB_COLS), whole)],
        out_shape=[jax.ShapeDtypeStruct((b, KA_COLS, c), BF16),
                   jax.ShapeDtypeStruct((b, c, VA_COLS), BF16),
                   jax.ShapeDtypeStruct((b, KB_COLS, c), BF16),
                   jax.ShapeDtypeStruct((b, c, VB_COLS), BF16)],
        compiler_params=_cparams(("arbitrary",)),
        name="inproj_ctx",
    )(ctx, g, shift, scale, w_ctx_bf16)


def _diff_attn_kernel(q_ref, kt_ref, v_ref, lq1_ref, lk1_ref, lq2_ref, lk2_ref, sg_ref, o_ref,
                      m_ref, l_ref, acc_ref, *, tk):
    d = HEAD_DIM
    q = q_ref[0]
    qs = (q[:, :d], q[:, d:])
    n_chunks = kt_ref.shape[2] // tk
    m_ref[...] = jnp.full(m_ref.shape, -jnp.inf, F32)
    l_ref[...] = jnp.zeros(l_ref.shape, F32)
    acc_ref[...] = jnp.zeros(acc_ref.shape, F32)

    def chunk(j, carry):
        off = pl.multiple_of(j * tk, tk)
        v = v_ref[0, pl.ds(off, tk), :]
        for c in range(2):
            kc = kt_ref[0, c * d:(c + 1) * d, pl.ds(off, tk)]
            s = jnp.dot(qs[c], kc, preferred_element_type=F32)
            m_old = m_ref[c]
            m_new = jnp.maximum(m_old, jnp.max(s, axis=-1, keepdims=True))
            alpha = jnp.exp(m_old - m_new)
            e = jnp.exp(s - m_new)
            l_ref[c] = alpha * l_ref[c] + jnp.sum(e, axis=-1, keepdims=True)
            acc_ref[c] = alpha * acc_ref[c] + jnp.dot(e.astype(BF16), v, preferred_element_type=F32)
            m_ref[c] = m_new
        return carry

    lax.fori_loop(0, n_chunks, chunk, 0)

    lam = (jnp.exp(jnp.sum(lq1_ref[...] * lk1_ref[...], axis=-1, keepdims=True))
           - jnp.exp(jnp.sum(lq2_ref[...] * lk2_ref[...], axis=-1, keepdims=True)) + LAM_INIT)
    o = acc_ref[0] / l_ref[0] - lam * (acc_ref[1] / l_ref[1])
    ms = jnp.mean(o * o, axis=-1, keepdims=True)
    o = o * lax.rsqrt(ms + NORM_EPS) * sg_ref[...] * (1.0 - LAM_INIT)
    o_ref[0] = o.astype(o_ref.dtype)


def _diff_attn(qa, kat, va, lq1, lk1, lq2, lk2, subln_g, tq, tk):
    b, s, _ = qa.shape
    sk = kat.shape[2]
    hw = 2 * HEAD_DIM
    vec = lambda bi, h, i: (0, 0)
    return pl.pallas_call(
        functools.partial(_diff_attn_kernel, tk=tk),
        grid=(b, DA_HEADS, s // tq),
        in_specs=[pl.BlockSpec((1, tq, hw), lambda bi, h, i: (bi, i, h)),
                  pl.BlockSpec((1, hw, sk), lambda bi, h, i: (bi, h, 0)),
                  pl.BlockSpec((1, sk, hw), lambda bi, h, i: (bi, 0, h)),
                  pl.BlockSpec((1, HEAD_DIM), vec),
                  pl.BlockSpec((1, HEAD_DIM), vec),
                  pl.BlockSpec((1, HEAD_DIM), vec),
                  pl.BlockSpec((1, HEAD_DIM), vec),
                  pl.BlockSpec((1, hw), vec)],
        out_specs=pl.BlockSpec((1, tq, hw), lambda bi, h, i: (bi, i, h)),
        out_shape=jax.ShapeDtypeStruct((b, s, DA_HEADS * hw), BF16),
        scratch_shapes=[pltpu.VMEM((2, tq, 1), F32),
                        pltpu.VMEM((2, tq, 1), F32),
                        pltpu.VMEM((2, tq, hw), F32)],
        compiler_params=_cparams(("arbitrary", "arbitrary", "arbitrary")),
        name="diff_attn",
    )(qa, kat, va, lq1, lk1, lq2, lk2, subln_g)


def _win_attn_kernel(q_ref, kt_ref, v_ref, kct_ref, vc_ref, sink_ref, o_ref, *, tq, lk):
    d = HEAD_DIM
    s_len = kt_ref.shape[2]
    i = pl.program_id(1)
    q0 = i * tq
    start = pl.multiple_of(jnp.clip(q0 - WINDOW, 0, s_len - lk), LANES)
    qpos = q0 + lax.broadcasted_iota(jnp.int32, (tq, 1), 0)
    kpos = start + lax.broadcasted_iota(jnp.int32, (1, lk), 1)
    valid = jnp.abs(kpos - qpos) <= WINDOW
    outs = []
    for kv in range(WA_KV_HEADS):
        kt = kt_ref[0, kv * d:(kv + 1) * d, pl.ds(start, lk)]
        v = v_ref[0, pl.ds(start, lk), kv * d:(kv + 1) * d]
        kct = kct_ref[0, kv * d:(kv + 1) * d, :]
        vc = vc_ref[0, :, kv * d:(kv + 1) * d]
        for g in range(WA_GROUP):
            h = kv * WA_GROUP + g
            qg = q_ref[0, :, h * d:(h + 1) * d]
            s_loc = jnp.where(valid, jnp.dot(qg, kt, preferred_element_type=F32), MASK_VALUE)
            s_ctx = jnp.dot(qg, kct, preferred_element_type=F32)
            sink = sink_ref[:, h:h + 1]
            m = jnp.maximum(jnp.maximum(jnp.max(s_loc, axis=-1, keepdims=True),
                                        jnp.max(s_ctx, axis=-1, keepdims=True)), sink)
            e_loc = jnp.exp(s_loc - m)
            e_ctx = jnp.exp(s_ctx - m)
            denom = (jnp.sum(e_loc, axis=-1, keepdims=True) + jnp.sum(e_ctx, axis=-1, keepdims=True)
                     + jnp.exp(sink - m))
            o = (jnp.dot(e_loc.astype(BF16), v, preferred_element_type=F32)
                 + jnp.dot(e_ctx.astype(BF16), vc, preferred_element_type=F32))
            outs.append(o / denom)
    o_ref[0] = jnp.concatenate(outs, axis=1).astype(o_ref.dtype)


def _win_attn(qb, kbt, vb, kbct, vbc, sink, tq):
    b, s, _ = qb.shape
    c = kbct.shape[2]
    lk = tq + 2 * WINDOW
    assert s >= lk and tq % LANES == 0
    whole = lambda bi, i: (bi, 0, 0)
    return pl.pallas_call(
        functools.partial(_win_attn_kernel, tq=tq, lk=lk),
        grid=(b, s // tq),
        in_specs=[pl.BlockSpec((1, tq, QB_COLS), lambda bi, i: (bi, i, 0)),
                  pl.BlockSpec((1, KB_COLS, s), whole),
                  pl.BlockSpec((1, s, VB_COLS), whole),
                  pl.BlockSpec((1, KB_COLS, c), whole),
                  pl.BlockSpec((1, c, VB_COLS), whole),
                  pl.BlockSpec((1, WA_HEADS), lambda bi, i: (0, 0))],
        out_specs=pl.BlockSpec((1, tq, QB_COLS), lambda bi, i: (bi, i, 0)),
        out_shape=jax.ShapeDtypeStruct((b, s, QB_COLS), BF16),
        compiler_params=_cparams(("arbitrary", "arbitrary")),
        name="win_attn",
    )(qb, kbt, vb, kbct, vbc, sink)


def _outproj_router_kernel(ya_ref, yb_ref, woa_ref, wob_ref, x_ref, g1_ref, n2_ref, sh_ref, sc_ref,
                           wr_ref, br_ref, x1_ref, hx_ref, code_ref, gate_ref, cnt_ref, carry_ref):
    first = jnp.logical_and(pl.program_id(0) == 0, pl.program_id(1) == 0)

    @pl.when(first)
    def _():
        carry_ref[...] = jnp.zeros(carry_ref.shape, F32)

    y = (jnp.dot(ya_ref[0], woa_ref[...], preferred_element_type=F32)
         + jnp.dot(yb_ref[0], wob_ref[...], preferred_element_type=F32))
    x1 = x_ref[0] + g1_ref[0] * y
    x1_ref[0] = x1
    hx = _modulated_norm(x1, n2_ref[...], sh_ref[0], sc_ref[0])
    hx_ref[0] = hx
    logits = jnp.dot(hx, wr_ref[...], precision=HIGHEST, preferred_element_type=F32) + br_ref[...]

    tm = logits.shape[0]
    lane_e = lax.broadcasted_iota(jnp.int32, (tm, N_EXPERTS), 1).astype(F32)
    work = logits
    tops, idxs, hots = [], [], []
    for _k in range(TOP_K):
        m = jnp.max(work, axis=-1, keepdims=True)
        idx = jnp.min(jnp.where(work == m, lane_e, float(N_EXPERTS)), axis=-1, keepdims=True)
        hot = lane_e == idx
        work = jnp.where(hot, -jnp.inf, work)
        tops.append(m)
        idxs.append(idx)
        hots.append(hot)
    es = [jnp.exp(t - tops[0]) for t in tops]
    den = es[0] + es[1] + es[2] + es[3]

    multi = jnp.zeros((tm, N_EXPERTS), F32)
    for hot in hots:
        multi = multi + hot.astype(F32)
    r_i = lax.broadcasted_iota(jnp.int32, (tm, tm), 0)
    c_i = lax.broadcasted_iota(jnp.int32, (tm, tm), 1)
    tri = (c_i <= r_i).astype(BF16)
    incl = jnp.dot(tri, multi.astype(BF16), preferred_element_type=F32)
    before = carry_ref[...] + incl - 1.0

    lane_k = lax.broadcasted_iota(jnp.int32, (tm, TOP_K), 1)
    code = jnp.zeros((tm, TOP_K), jnp.int32)
    gate = jnp.zeros((tm, TOP_K), F32)
    for k in range(TOP_K):
        rank = jnp.sum(jnp.where(hots[k], before, 0.0), axis=-1, keepdims=True)
        ck = idxs[k].astype(jnp.int32) * 65536 + rank.astype(jnp.int32)
        code = jnp.where(lane_k == k, ck, code)
        gate = jnp.where(lane_k == k, es[k] / den, gate)
    code_ref[...] = code
    gate_ref[...] = gate
    carry_ref[...] = carry_ref[...] + jnp.sum(multi, axis=0, keepdims=True)
    cnt_ref[...] = carry_ref[...]


def _outproj_router(ya, yb, woa, wob, x, g1, n2, sh2, sc2, w_r, b_r, tm):
    b, s, d = x.shape
    nb = s // tm
    row = lambda bi, i: (bi, i, 0)
    mod = lambda bi, i: (bi, 0, 0)
    fixed = lambda bi, i: (0, 0)
    tok = lambda bi, i: (bi * nb + i, 0)
    half = ya.shape[2]
    return pl.pallas_call(
        _outproj_router_kernel,
        grid=(b, nb),
        in_specs=[pl.BlockSpec((1, tm, half), row),
                  pl.BlockSpec((1, tm, half), row),
                  pl.BlockSpec((half, d), fixed),
                  pl.BlockSpec((half, d), fixed),
                  pl.BlockSpec((1, tm, d), row),
                  pl.BlockSpec((1, 1, d), mod),
                  pl.BlockSpec((1, d), fixed),
                  pl.BlockSpec((1, 1, d), mod),
                  pl.BlockSpec((1, 1, d), mod),
                  pl.BlockSpec((d, N_EXPERTS), fixed),
                  pl.BlockSpec((1, N_EXPERTS), fixed)],
        out_specs=[pl.BlockSpec((1, tm, d), row),
                   pl.BlockSpec((1, tm, d), row),
                   pl.BlockSpec((tm, TOP_K), tok),
                   pl.BlockSpec((tm, TOP_K), tok),
                   pl.BlockSpec((1, N_EXPERTS), fixed)],
        out_shape=[jax.ShapeDtypeStruct((b, s, d), F32),
                   jax.ShapeDtypeStruct((b, s, d), F32),
                   jax.ShapeDtypeStruct((b * s, TOP_K), jnp.int32),
                   jax.ShapeDtypeStruct((b * s, TOP_K), F32),
                   jax.ShapeDtypeStruct((1, N_EXPERTS), F32)],
        scratch_shapes=[pltpu.VMEM((1, N_EXPERTS), F32)],
        compiler_params=_cparams(("arbitrary", "arbitrary")),
        name="outproj_router",
    )(ya, yb, woa, wob, x, g1, n2, sh2, sc2, w_r, b_r)


def _row_copy(src_hbm, src_row, dst_ref, dst_row, sem):
    return pltpu.make_async_copy(src_hbm.at[pl.ds(src_row, 1)], dst_ref.at[pl.ds(dst_row, 1)], sem)


def _dispatch_kernel(pstart_ref, code_ref, hx_hbm, init_hbm, xs_hbm, dest_ref, sem, *, chunk):
    del init_hbm
    base_tok = pl.program_id(0) * (chunk // TOP_K)

    def issue(j, carry):
        cd = code_ref[j]
        dst = pstart_ref[cd >> 16] + (cd & 0xFFFF)
        dest_ref[j] = dst
        _row_copy(hx_hbm, base_tok + (j >> 2), xs_hbm, dst, sem).start()
        return carry

    lax.fori_loop(0, chunk, issue, 0)

    def drain(j, carry):
        _row_copy(hx_hbm, 0, xs_hbm, 0, sem).wait()
        return carry

    lax.fori_loop(0, chunk, drain, 0)


def _dispatch(pstart, code_flat, hx2d, xs_init, chunk):
    n_assign = code_flat.shape[0]
    n_rows, d = xs_init.shape
    return pl.pallas_call(
        functools.partial(_dispatch_kernel, chunk=chunk),
        grid_spec=pltpu.PrefetchScalarGridSpec(
            num_scalar_prefetch=1,
            grid=(n_assign // chunk,),
            in_specs=[pl.BlockSpec((chunk,), lambda i, ps: (i,), memory_space=pltpu.SMEM),
                      pl.BlockSpec(memory_space=pl.ANY),
                      pl.BlockSpec(memory_space=pl.ANY)],
            out_specs=[pl.BlockSpec(memory_space=pl.ANY),
                       pl.BlockSpec((chunk,), lambda i, ps: (i,), memory_space=pltpu.SMEM)],
            scratch_shapes=[pltpu.SemaphoreType.DMA(())]),
        out_shape=[jax.ShapeDtypeStruct((n_rows, d), F32),
                   jax.ShapeDtypeStruct((n_assign,), jnp.int32)],
        input_output_aliases={3: 0},
        compiler_params=_cparams(("arbitrary",)),
        name="moe_dispatch",
    )(pstart, code_flat, hx2d, xs_init)


def _expert_kernel(bexp_ref, xs_ref, wg_ref, wu_ref, wd_ref, bg_ref, bu_ref, bd_ref, o_ref):
    del bexp_ref
    xb = xs_ref[...].astype(BF16)
    g = jnp.dot(xb, wg_ref[0], preferred_element_type=F32) + bg_ref[0]
    u = jnp.dot(xb, wu_ref[0], preferred_element_type=F32) + bu_ref[0]
    g = jnp.minimum(g, SWIGLU_LIMIT)
    u = jnp.clip(u, -SWIGLU_LIMIT, SWIGLU_LIMIT)
    a = g * (1.0 / (1.0 + jnp.exp(-SWIGLU_ALPHA * g))) * (u + 1.0)
    o_ref[...] = jnp.dot(a.astype(BF16), wd_ref[0], preferred_element_type=F32) + bd_ref[0]


def _experts(block_exp, xs, wg, wu, wd, bg, bu, bd):
    n_rows, d = xs.shape
    f = wg.shape[2]
    nblk = n_rows // EXPERT_BLOCK
    wsel = lambda i, be: (be[i], 0, 0)
    return pl.pallas_call(
        _expert_kernel,
        grid_spec=pltpu.PrefetchScalarGridSpec(
            num_scalar_prefetch=1,
            grid=(nblk,),
            in_specs=[pl.BlockSpec((EXPERT_BLOCK, d), lambda i, be: (i, 0)),
                      pl.BlockSpec((1, d, f), wsel),
                      pl.BlockSpec((1, d, f), wsel),
                      pl.BlockSpec((1, f, d), wsel),
                      pl.BlockSpec((1, 1, f), wsel),
                      pl.BlockSpec((1, 1, f), wsel),
                      pl.BlockSpec((1, 1, d), wsel)],
            out_specs=pl.BlockSpec((EXPERT_BLOCK, d), lambda i, be: (i, 0))),
        out_shape=jax.ShapeDtypeStruct((n_rows, d), F32),
        compiler_params=_cparams(("arbitrary",)),
        name="moe_experts",
    )(block_exp, xs, wg, wu, wd, bg, bu, bd)


def _combine_kernel(dest_ref, rows_hbm, gate_ref, x1_ref, g2_ref, fg_ref, o_ref, buf, sem, *, tm):
    def issue(j, carry):
        _row_copy(rows_hbm, dest_ref[j], buf.at[j & (TOP_K - 1)], j >> 2, sem).start()
        return carry

    lax.fori_loop(0, tm * TOP_K, issue, 0)

    def drain(j, carry):
        _row_copy(rows_hbm, 0, buf.at[0], 0, sem).wait()
        return carry

    lax.fori_loop(0, tm * TOP_K, drain, 0)

    gate = gate_ref[...]
    y = gate[:, 0:1] * buf[0]
    for k in range(1, TOP_K):
        y = y + gate[:, k:k + 1] * buf[k]
    xo = x1_ref[0] + g2_ref[0] * y
    ms = jnp.mean(xo * xo, axis=-1, keepdims=True)
    o_ref[0] = xo * lax.rsqrt(ms + NORM_EPS) * fg_ref[...]


def _combine(dest, rows, gates, x1, g2, final_g, tm):
    b, s, d = x1.shape
    nb = s // tm
    return pl.pallas_call(
        functools.partial(_combine_kernel, tm=tm),
        grid=(b, nb),
        in_specs=[pl.BlockSpec((tm * TOP_K,), lambda bi, i: (bi * nb + i,), memory_space=pltpu.SMEM),
                  pl.BlockSpec(memory_space=pl.ANY),
                  pl.BlockSpec((tm, TOP_K), lambda bi, i: (bi * nb + i, 0)),
                  pl.BlockSpec((1, tm, d), lambda bi, i: (bi, i, 0)),
                  pl.BlockSpec((1, 1, d), lambda bi, i: (bi, 0, 0)),
                  pl.BlockSpec((1, d), lambda bi, i: (0, 0))],
        out_specs=pl.BlockSpec((1, tm, d), lambda bi, i: (bi, i, 0)),
        out_shape=jax.ShapeDtypeStruct((b, s, d), F32),
        scratch_shapes=[pltpu.VMEM((TOP_K, tm, d), F32), pltpu.SemaphoreType.DMA(())],
        compiler_params=_cparams(("arbitrary", "arbitrary")),
        name="moe_combine",
    )(dest, rows, gates, x1, g2, final_g)


def _pick(n, prefs):
    for p in prefs:
        if n % p == 0:
            return p
    raise ValueError(f"no tile for {n}")


def kernel(x, c, ctx, c_ctx, w_mod, b_mod, norm1_g, w_in, lam_q1, lam_k1, lam_q2, lam_k2, subln_g,
           sink, w_out, norm2_g, w_router, b_router, w_gate_up, b_gate_up, w_down, b_down, final_g):
    b, s, d = x.shape
    c_len = ctx.shape[1]
    assert w_mod.shape[0] == 1, "single-layer block"
    t = b * s

    pad = (-(b + 1)) % 8
    cvecs = jnp.concatenate([c, c_ctx[None, :], jnp.zeros((pad, d), F32)], axis=0)
    mod = _adaln(cvecs, w_mod[0], b_mod[0])
    sh1, sc1, g1, sh2, sc2, g2 = [mod[:b, k * d:(k + 1) * d].reshape(b, 1, d) for k in range(6)]
    csh1 = mod[b:b + 1, 0:d].reshape(1, 1, d)
    csc1 = mod[b:b + 1, d:2 * d].reshape(1, 1, d)

    w_in_bf = w_in[0].astype(BF16)
    cos, sin = _rope_tables(s)
    n1 = norm1_g[0].reshape(1, d)
    qa, kat, va, qb, kbt, vb = _inproj_latent(x, n1, sh1, sc1, w_in_bf, cos, sin, _pick(s, (512, 256, 128)))
    w_ctx_bf = jnp.concatenate([w_in_bf[:, O_KA:O_QB], w_in_bf[:, O_KB:IN_COLS]], axis=1)
    kact, vac, kbct, vbc = _inproj_ctx(ctx, n1, csh1, csc1, w_ctx_bf)

    kat_all = jnp.concatenate([kat, kact], axis=2)
    va_all = jnp.concatenate([va, vac], axis=1)
    sk = s + c_len
    ya = _diff_attn(qa, kat_all, va_all,
                    lam_q1[0].reshape(1, -1), lam_k1[0].reshape(1, -1),
                    lam_q2[0].reshape(1, -1), lam_k2[0].reshape(1, -1),
                    subln_g[0].reshape(1, -1),
                    _pick(s, (256, 128)), _pick(sk, (768, 512, 384, 256, 128)))

    yb = _win_attn(qb, kbt, vb, kbct, vbc, sink[0].reshape(1, -1), _pick(s, (256, 128)))

    w_out_bf = w_out[0].astype(BF16)
    x1, hx2, code, gates, counts = _outproj_router(
        ya, yb, w_out_bf[:QA_COLS], w_out_bf[QA_COLS:], x, g1, norm2_g[0].reshape(1, d), sh2, sc2,
        w_router[0], b_router[0].reshape(1, -1), _pick(s, (256, 128)))

    n_assign = t * TOP_K
    n_rows = n_assign + N_EXPERTS * EXPERT_BLOCK
    counts_i = counts.reshape(-1).astype(jnp.int32)
    padded = ((counts_i + EXPERT_BLOCK - 1) // EXPERT_BLOCK) * EXPERT_BLOCK
    pend = jnp.cumsum(padded)
    pstart = (pend - padded).astype(jnp.int32)
    block_start = jnp.arange(n_rows // EXPERT_BLOCK, dtype=jnp.int32) * EXPERT_BLOCK
    block_exp = jnp.clip(jnp.searchsorted(pend, block_start, side="right"), 0, N_EXPERTS - 1).astype(jnp.int32)

    xs, dest = _dispatch(pstart, code.reshape(-1), hx2.reshape(t, d), jnp.zeros((n_rows, d), F32),
                         _pick(n_assign, (1024,)))

    f = w_down.shape[2]
    wg = w_gate_up[0, :, :, 0::2].astype(BF16)
    wu = w_gate_up[0, :, :, 1::2].astype(BF16)
    wd = w_down[0].astype(BF16)
    bg = b_gate_up[0, :, 0::2].reshape(N_EXPERTS, 1, f)
    bu = b_gate_up[0, :, 1::2].reshape(N_EXPERTS, 1, f)
    bd = b_down[0].reshape(N_EXPERTS, 1, d)
    rows = _experts(block_exp, xs, wg, wu, wd, bg, bu, bd)

    return _combine(dest, rows, gates, x1, g2, final_g.reshape(1, d), _pick(s, (256,)))
```

```python
import functools
import math

import jax
import jax.numpy as jnp
from jax import lax
from jax.experimental import pallas as pl
from jax.experimental.pallas import tpu as pltpu

F32 = jnp.float32
BF16 = jnp.bfloat16
HIGHEST = lax.Precision.HIGHEST

GRID_W = 64
NORM_EPS = 1e-6
ROPE_BASE = 10000.0
MASK_VALUE = -1e30
DA_HEADS = 4
HEAD_DIM = 64
WA_HEADS = 8
WA_KV_HEADS = 2
WA_GROUP = WA_HEADS // WA_KV_HEADS
WINDOW = 128
N_EXPERTS = 32
TOP_K = 4
SWIGLU_LIMIT = 7.0
SWIGLU_ALPHA = 1.702
EXPERT_BLOCK = 128
LAM_INIT = 0.8 - 0.6 * math.exp(-0.3 * 0)

QA_COLS = DA_HEADS * 2 * HEAD_DIM
KA_COLS = QA_COLS
VA_COLS = QA_COLS
QB_COLS = WA_HEADS * HEAD_DIM
KB_COLS = WA_KV_HEADS * HEAD_DIM
VB_COLS = KB_COLS
O_QA = 0
O_KA = O_QA + QA_COLS
O_VA = O_KA + KA_COLS
O_QB = O_VA + VA_COLS
O_KB = O_QB + QB_COLS
O_VB = O_KB + KB_COLS
IN_COLS = O_VB + VB_COLS

LANES = 128
MXU_COLS = 256
DMA_UNROLL = 4
VMEM_LIMIT = 56 * 1024 * 1024


def _cparams(sem):
    return pltpu.CompilerParams(dimension_semantics=sem, vmem_limit_bytes=VMEM_LIMIT)


def _adaln_kernel(c_ref, w_ref, b_ref, o_ref):
    cv = c_ref[...]
    s = cv * (1.0 / (1.0 + jnp.exp(-cv)))
    o_ref[...] = jnp.dot(s, w_ref[...], precision=HIGHEST, preferred_element_type=F32) + b_ref[...]


def _adaln(cvecs, w_mod, b_mod):
    rows, d = cvecs.shape
    n = w_mod.shape[1]
    tn = 1024
    return pl.pallas_call(
        _adaln_kernel,
        grid=(n // tn,),
        in_specs=[pl.BlockSpec((rows, d), lambda j: (0, 0)),
                  pl.BlockSpec((d, tn), lambda j: (0, j)),
                  pl.BlockSpec((1, tn), lambda j: (0, j))],
        out_specs=pl.BlockSpec((rows, tn), lambda j: (0, j)),
        out_shape=jax.ShapeDtypeStruct((rows, n), F32),
        compiler_params=_cparams(("arbitrary",)),
        name="adaln",
    )(cvecs, w_mod, b_mod.reshape(1, n))


def _rope_section(sec, cos, sin):
    tm = sec.shape[0]
    lane = lax.broadcasted_iota(jnp.int32, (tm, LANES), 1)
    low = (lane % 32) < 16
    outs = []
    for j in range(sec.shape[1] // LANES):
        c = sec[:, j * LANES:(j + 1) * LANES]
        partner = jnp.where(low, pltpu.roll(c, LANES - 16, 1), pltpu.roll(c, 16, 1))
        outs.append(c * cos + partner * sin)
    return jnp.concatenate(outs, axis=1)


def _modulated_norm(x, g, shift, scale):
    ms = jnp.mean(x * x, axis=-1, keepdims=True)
    return (x * lax.rsqrt(ms + NORM_EPS) * g) * (1.0 + scale) + shift


def _inproj_latent_kernel(x_ref, g_ref, sh_ref, sc_ref, w_ref, cos_ref, sin_ref,
                          qa_ref, kat_ref, va_ref, qb_ref, kbt_ref, vb_ref):
    h = _modulated_norm(x_ref[0], g_ref[...], sh_ref[0], sc_ref[0])
    p = jnp.dot(h.astype(BF16), w_ref[...], preferred_element_type=F32)
    cos = cos_ref[...]
    sin = sin_ref[...]
    scale = HEAD_DIM ** -0.5
    qa_ref[0] = (_rope_section(p[:, O_QA:O_KA], cos, sin) * scale).astype(BF16)
    kat_ref[0] = _rope_section(p[:, O_KA:O_VA], cos, sin).T.astype(BF16)
    va_ref[0] = p[:, O_VA:O_QB].astype(BF16)
    qb_ref[0] = (_rope_section(p[:, O_QB:O_KB], cos, sin) * scale).astype(BF16)
    kbt_ref[0] = _rope_section(p[:, O_KB:O_VB], cos, sin).T.astype(BF16)
    vb_ref[0] = p[:, O_VB:IN_COLS].astype(BF16)


def _inproj_ctx_kernel(x_ref, g_ref, sh_ref, sc_ref, w_ref, kat_ref, va_ref, kbt_ref, vb_ref):
    h = _modulated_norm(x_ref[0], g_ref[...], sh_ref[0], sc_ref[0])
    p = jnp.dot(h.astype(BF16), w_ref[...], preferred_element_type=F32)
    kat_ref[0] = p[:, 0:KA_COLS].T.astype(BF16)
    va_ref[0] = p[:, KA_COLS:KA_COLS + VA_COLS].astype(BF16)
    kbt_ref[0] = p[:, KA_COLS + VA_COLS:KA_COLS + VA_COLS + KB_COLS].T.astype(BF16)
    vb_ref[0] = p[:, KA_COLS + VA_COLS + KB_COLS:].astype(BF16)


def _rope_tables(n_tok):
    rows = (jnp.arange(n_tok) // GRID_W).astype(F32)
    cols = (jnp.arange(n_tok) % GRID_W).astype(F32)
    nf = HEAD_DIM // 4
    inv = ROPE_BASE ** (-jnp.arange(nf, dtype=F32) / nf)
    ar = rows[:, None] * inv
    ac = cols[:, None] * inv
    cos = jnp.concatenate([jnp.cos(ar), jnp.cos(ar), jnp.cos(ac), jnp.cos(ac)], axis=1)
    sin = jnp.concatenate([-jnp.sin(ar), jnp.sin(ar), -jnp.sin(ac), jnp.sin(ac)], axis=1)
    return jnp.tile(cos, (1, LANES // HEAD_DIM)), jnp.tile(sin, (1, LANES // HEAD_DIM))


def _inproj_latent(x, g, shift, scale, w_bf16, cos, sin, tm):
    b, s, d = x.shape
    row = lambda bi, i: (bi, i, 0)
    colt = lambda bi, i: (bi, 0, i)
    mod = lambda bi, i: (bi, 0, 0)
    fixed = lambda bi, i: (0, 0)
    return pl.pallas_call(
        _inproj_latent_kernel,
        grid=(b, s // tm),
        in_specs=[pl.BlockSpec((1, tm, d), row),
                  pl.BlockSpec((1, d), fixed),
                  pl.BlockSpec((1, 1, d), mod),
                  pl.BlockSpec((1, 1, d), mod),
                  pl.BlockSpec((d, IN_COLS), fixed),
                  pl.BlockSpec((tm, LANES), lambda bi, i: (i, 0)),
                  pl.BlockSpec((tm, LANES), lambda bi, i: (i, 0))],
        out_specs=[pl.BlockSpec((1, tm, QA_COLS), row),
                   pl.BlockSpec((1, KA_COLS, tm), colt),
                   pl.BlockSpec((1, tm, VA_COLS), row),
                   pl.BlockSpec((1, tm, QB_COLS), row),
                   pl.BlockSpec((1, KB_COLS, tm), colt),
                   pl.BlockSpec((1, tm, VB_COLS), row)],
        out_shape=[jax.ShapeDtypeStruct((b, s, QA_COLS), BF16),
                   jax.ShapeDtypeStruct((b, KA_COLS, s), BF16),
                   jax.ShapeDtypeStruct((b, s, VA_COLS), BF16),
                   jax.ShapeDtypeStruct((b, s, QB_COLS), BF16),
                   jax.ShapeDtypeStruct((b, KB_COLS, s), BF16),
                   jax.ShapeDtypeStruct((b, s, VB_COLS), BF16)],
        compiler_params=_cparams(("arbitrary", "arbitrary")),
        name="inproj_latent",
    )(x, g, shift, scale, w_bf16, cos, sin)


def _inproj_ctx(ctx, g, shift, scale, w_ctx_bf16):
    b, c, d = ctx.shape
    n = w_ctx_bf16.shape[1]
    whole = lambda bi: (bi, 0, 0)
    mod = lambda bi: (0, 0, 0)
    fixed = lambda bi: (0, 0)
    return pl.pallas_call(
        _inproj_ctx_kernel,
        grid=(b,),
        in_specs=[pl.BlockSpec((1, c, d), whole),
                  pl.BlockSpec((1, d), fixed),
                  pl.BlockSpec((1, 1, d), mod),
                  pl.BlockSpec((1, 1, d), mod),
                  pl.BlockSpec((d, n), fixed)],
        out_specs=[pl.BlockSpec((1, KA_COLS, c), whole),
                   pl.BlockSpec((1, c, VA_COLS), whole),
                   pl.BlockSpec((1, KB_COLS, c), whole),
                   pl.BlockSpec((1, c, VB_COLS), whole)],
        out_shape=[jax.ShapeDtypeStruct((b, KA_COLS, c), BF16),
                   jax.ShapeDtypeStruct((b, c, VA_COLS), BF16),
                   jax.ShapeDtypeStruct((b, KB_COLS, c), BF16),
                   jax.ShapeDtypeStruct((b, c, VB_COLS), BF16)],
        compiler_params=_cparams(("arbitrary",)),
        name="inproj_ctx",
    )(ctx, g, shift, scale, w_ctx_bf16)


def _diff_attn_kernel(q_ref, kt_ref, v_ref, lq1_ref, lk1_ref, lq2_ref, lk2_ref, sg_ref, o_ref,
                      m_ref, l_ref, acc_ref, *, tk):
    d = HEAD_DIM
    q = q_ref[0]
    qs = (q[:, :d], q[:, d:])
    n_chunks = kt_ref.shape[2] // tk
    m_ref[...] = jnp.full(m_ref.shape, -jnp.inf, F32)
    l_ref[...] = jnp.zeros(l_ref.shape, F32)
    acc_ref[...] = jnp.zeros(acc_ref.shape, F32)

    def chunk(j, carry):
        off = pl.multiple_of(j * tk, tk)
        v = v_ref[0, pl.ds(off, tk), :]
        for c in range(2):
            kc = kt_ref[0, c * d:(c + 1) * d, pl.ds(off, tk)]
            s = jnp.dot(qs[c], kc, preferred_element_type=F32)
            m_old = m_ref[c]
            m_new = jnp.maximum(m_old, jnp.max(s, axis=-1, keepdims=True))
            alpha = jnp.exp(m_old - m_new)
            e = jnp.exp(s - m_new)
            l_ref[c] = alpha * l_ref[c] + jnp.sum(e, axis=-1, keepdims=True)
            acc_ref[c] = alpha * acc_ref[c] + jnp.dot(e.astype(BF16), v, preferred_element_type=F32)
            m_ref[c] = m_new
        return carry

    lax.fori_loop(0, n_chunks, chunk, 0)

    lam = (jnp.exp(jnp.sum(lq1_ref[...] * lk1_ref[...], axis=-1, keepdims=True))
           - jnp.exp(jnp.sum(lq2_ref[...] * lk2_ref[...], axis=-1, keepdims=True)) + LAM_INIT)
    o = acc_ref[0] / l_ref[0] - lam * (acc_ref[1] / l_ref[1])
    ms = jnp.mean(o * o, axis=-1, keepdims=True)
    o = o * lax.rsqrt(ms + NORM_EPS) * sg_ref[...] * (1.0 - LAM_INIT)
    o_ref[0] = o.astype(o_ref.dtype)


def _diff_attn(qa, kat, va, lq1, lk1, lq2, lk2, subln_g, tq, tk):
    b, s, _ = qa.shape
    sk = kat.shape[2]
    hw = 2 * HEAD_DIM
    vec = lambda bi, h, i: (0, 0)
    return pl.pallas_call(
        functools.partial(_diff_attn_kernel, tk=tk),
        grid=(b, DA_HEADS, s // tq),
        in_specs=[pl.BlockSpec((1, tq, hw), lambda bi, h, i: (bi, i, h)),
                  pl.BlockSpec((1, hw, sk), lambda bi, h, i: (bi, h, 0)),
                  pl.BlockSpec((1, sk, hw), lambda bi, h, i: (bi, 0, h)),
                  pl.BlockSpec((1, HEAD_DIM), vec),
                  pl.BlockSpec((1, HEAD_DIM), vec),
                  pl.BlockSpec((1, HEAD_DIM), vec),
                  pl.BlockSpec((1, HEAD_DIM), vec),
                  pl.BlockSpec((1, hw), vec)],
        out_specs=pl.BlockSpec((1, tq, hw), lambda bi, h, i: (bi, i, h)),
        out_shape=jax.ShapeDtypeStruct((b, s, DA_HEADS * hw), BF16),
        scratch_shapes=[pltpu.VMEM((2, tq, 1), F32),
                        pltpu.VMEM((2, tq, 1), F32),
                        pltpu.VMEM((2, tq, hw), F32)],
        compiler_params=_cparams(("arbitrary", "arbitrary", "arbitrary")),
        name="diff_attn",
    )(qa, kat, va, lq1, lk1, lq2, lk2, subln_g)


def _win_attn_kernel(q_ref, kt_ref, v_ref, kct_ref, vc_ref, sink_ref, o_ref, *, tq, lk):
    d = HEAD_DIM
    s_len = kt_ref.shape[2]
    i = pl.program_id(1)
    q0 = i * tq
    start = pl.multiple_of(jnp.clip(q0 - WINDOW, 0, s_len - lk), LANES)
    qpos = q0 + lax.broadcasted_iota(jnp.int32, (tq, 1), 0)
    kpos = start + lax.broadcasted_iota(jnp.int32, (1, lk), 1)
    valid = jnp.abs(kpos - qpos) <= WINDOW
    outs = []
    for kv in range(WA_KV_HEADS):
        kt = kt_ref[0, kv * d:(kv + 1) * d, pl.ds(start, lk)]
        v = v_ref[0, pl.ds(start, lk), kv * d:(kv + 1) * d]
        kct = kct_ref[0, kv * d:(kv + 1) * d, :]
        vc = vc_ref[0, :, kv * d:(kv + 1) * d]
        for g in range(WA_GROUP):
            h = kv * WA_GROUP + g
            qg = q_ref[0, :, h * d:(h + 1) * d]
            s_loc = jnp.where(valid, jnp.dot(qg, kt, preferred_element_type=F32), MASK_VALUE)
            s_ctx = jnp.dot(qg, kct, preferred_element_type=F32)
            sink = sink_ref[:, h:h + 1]
            m = jnp.maximum(jnp.maximum(jnp.max(s_loc, axis=-1, keepdims=True),
                                        jnp.max(s_ctx, axis=-1, keepdims=True)), sink)
            e_loc = jnp.exp(s_loc - m)
            e_ctx = jnp.exp(s_ctx - m)
            denom = (jnp.sum(e_loc, axis=-1, keepdims=True) + jnp.sum(e_ctx, axis=-1, keepdims=True)
                     + jnp.exp(sink - m))
            o = (jnp.dot(e_loc.astype(BF16), v, preferred_element_type=F32)
                 + jnp.dot(e_ctx.astype(BF16), vc, preferred_element_type=F32))
            outs.append(o / denom)
    o_ref[0] = jnp.concatenate(outs, axis=1).astype(o_ref.dtype)


def _win_attn(qb, kbt, vb, kbct, vbc, sink, tq):
    b, s, _ = qb.shape
    c = kbct.shape[2]
    lk = tq + 2 * WINDOW
    assert s >= lk and tq % LANES == 0
    whole = lambda bi, i: (bi, 0, 0)
    return pl.pallas_call(
        functools.partial(_win_attn_kernel, tq=tq, lk=lk),
        grid=(b, s // tq),
        in_specs=[pl.BlockSpec((1, tq, QB_COLS), lambda bi, i: (bi, i, 0)),
                  pl.BlockSpec((1, KB_COLS, s), whole),
                  pl.BlockSpec((1, s, VB_COLS), whole),
                  pl.BlockSpec((1, KB_COLS, c), whole),
                  pl.BlockSpec((1, c, VB_COLS), whole),
                  pl.BlockSpec((1, WA_HEADS), lambda bi, i: (0, 0))],
        out_specs=pl.BlockSpec((1, tq, QB_COLS), lambda bi, i: (bi, i, 0)),
        out_shape=jax.ShapeDtypeStruct((b, s, QB_COLS), BF16),
        compiler_params=_cparams(("arbitrary", "arbitrary")),
        name="win_attn",
    )(qb, kbt, vb, kbct, vbc, sink)


def _outproj_router_kernel(ya_ref, yb_ref, woa_ref, wob_ref, x_ref, g1_ref, n2_ref, sh_ref, sc_ref,
                           wr_ref, br_ref, x1_ref, hx_ref, code_ref, gate_ref, cnt_ref, carry_ref):
    first = jnp.logical_and(pl.program_id(0) == 0, pl.program_id(1) == 0)

    @pl.when(first)
    def _():
        carry_ref[...] = jnp.zeros(carry_ref.shape, F32)

    y = (jnp.dot(ya_ref[0], woa_ref[...], preferred_element_type=F32)
         + jnp.dot(yb_ref[0], wob_ref[...], preferred_element_type=F32))
    x1 = x_ref[0] + g1_ref[0] * y
    x1_ref[0] = x1
    hx = _modulated_norm(x1, n2_ref[...], sh_ref[0], sc_ref[0])
    hx_ref[0] = hx
    logits = jnp.dot(hx, wr_ref[...], precision=HIGHEST, preferred_element_type=F32) + br_ref[...]

    tm = logits.shape[0]
    lane_e = lax.broadcasted_iota(jnp.int32, (tm, N_EXPERTS), 1).astype(F32)
    work = logits
    tops, idxs, hots = [], [], []
    for _k in range(TOP_K):
        m = jnp.max(work, axis=-1, keepdims=True)
        idx = jnp.min(jnp.where(work == m, lane_e, float(N_EXPERTS)), axis=-1, keepdims=True)
        hot = lane_e == idx
        work = jnp.where(hot, -jnp.inf, work)
        tops.append(m)
        idxs.append(idx)
        hots.append(hot)
    es = [jnp.exp(t - tops[0]) for t in tops]
    den = es[0] + es[1] + es[2] + es[3]

    multi = jnp.zeros((tm, N_EXPERTS), F32)
    for hot in hots:
        multi = multi + hot.astype(F32)
    r_i = lax.broadcasted_iota(jnp.int32, (tm, tm), 0)
    c_i = lax.broadcasted_iota(jnp.int32, (tm, tm), 1)
    tri = (c_i <= r_i).astype(BF16)
    incl = jnp.dot(tri, multi.astype(BF16), preferred_element_type=F32)
    before = carry_ref[...] + incl - 1.0

    lane_k = lax.broadcasted_iota(jnp.int32, (tm, TOP_K), 1)
    code = jnp.zeros((tm, TOP_K), jnp.int32)
    gate = jnp.zeros((tm, TOP_K), F32)
    for k in range(TOP_K):
        rank = jnp.sum(jnp.where(hots[k], before, 0.0), axis=-1, keepdims=True)
        ck = idxs[k].astype(jnp.int32) * 65536 + rank.astype(jnp.int32)
        code = jnp.where(lane_k == k, ck, code)
        gate = jnp.where(lane_k == k, es[k] / den, gate)
    code_ref[...] = code
    gate_ref[...] = gate
    carry_ref[...] = carry_ref[...] + jnp.sum(multi, axis=0, keepdims=True)
    cnt_ref[...] = carry_ref[...]


def _outproj_router(ya, yb, woa, wob, x, g1, n2, sh2, sc2, w_r, b_r, tm):
    b, s, d = x.shape
    nb = s // tm
    row = lambda bi, i: (bi, i, 0)
    mod = lambda bi, i: (bi, 0, 0)
    fixed = lambda bi, i: (0, 0)
    tok = lambda bi, i: (bi * nb + i, 0)
    half = ya.shape[2]
    return pl.pallas_call(
        _outproj_router_kernel,
        grid=(b, nb),
        in_specs=[pl.BlockSpec((1, tm, half), row),
                  pl.BlockSpec((1, tm, half), row),
                  pl.BlockSpec((half, d), fixed),
                  pl.BlockSpec((half, d), fixed),
                  pl.BlockSpec((1, tm, d), row),
                  pl.BlockSpec((1, 1, d), mod),
                  pl.BlockSpec((1, d), fixed),
                  pl.BlockSpec((1, 1, d), mod),
                  pl.BlockSpec((1, 1, d), mod),
                  pl.BlockSpec((d, N_EXPERTS), fixed),
                  pl.BlockSpec((1, N_EXPERTS), fixed)],
        out_specs=[pl.BlockSpec((1, tm, d), row),
                   pl.BlockSpec((1, tm, d), row),
                   pl.BlockSpec((tm, TOP_K), tok),
                   pl.BlockSpec((tm, TOP_K), tok),
                   pl.BlockSpec((1, N_EXPERTS), fixed)],
        out_shape=[jax.ShapeDtypeStruct((b, s, d), F32),
                   jax.ShapeDtypeStruct((b, s, d), F32),
                   jax.ShapeDtypeStruct((b * s, TOP_K), jnp.int32),
                   jax.ShapeDtypeStruct((b * s, TOP_K), F32),
                   jax.ShapeDtypeStruct((1, N_EXPERTS), F32)],
        scratch_shapes=[pltpu.VMEM((1, N_EXPERTS), F32)],
        compiler_params=_cparams(("arbitrary", "arbitrary")),
        name="outproj_router",
    )(ya, yb, woa, wob, x, g1, n2, sh2, sc2, w_r, b_r)


def _row_copy(src_hbm, src_row, dst_ref, dst_row, sem):
    return pltpu.make_async_copy(src_hbm.at[pl.ds(src_row, 1)], dst_ref.at[pl.ds(dst_row, 1)], sem)


def _dispatch_kernel(pstart_ref, code_ref, hx_ref, init_hbm, xs_hbm, dest_ref, sem, *, chunk):
    del init_hbm

    def issue(tok, carry):
        for k in range(TOP_K):
            j = tok * TOP_K + k
            cd = code_ref[j]
            dst = pstart_ref[cd >> 16] + (cd & 0xFFFF)
            dest_ref[j] = dst
            _row_copy(hx_ref, tok, xs_hbm, dst, sem).start()
        return carry

    lax.fori_loop(0, chunk // TOP_K, issue, 0, unroll=DMA_UNROLL)
    pltpu.make_async_copy(xs_hbm.at[pl.ds(0, chunk)], xs_hbm.at[pl.ds(0, chunk)], sem).wait()


def _dispatch(pstart, code_flat, hx2d, xs_init, chunk):
    n_assign = code_flat.shape[0]
    n_rows, d = xs_init.shape
    return pl.pallas_call(
        functools.partial(_dispatch_kernel, chunk=chunk),
        grid_spec=pltpu.PrefetchScalarGridSpec(
            num_scalar_prefetch=1,
            grid=(n_assign // chunk,),
            in_specs=[pl.BlockSpec((chunk,), lambda i, ps: (i,), memory_space=pltpu.SMEM),
                      pl.BlockSpec((chunk // TOP_K, d), lambda i, ps: (i, 0)),
                      pl.BlockSpec(memory_space=pl.ANY)],
            out_specs=[pl.BlockSpec(memory_space=pl.ANY),
                       pl.BlockSpec((chunk,), lambda i, ps: (i,), memory_space=pltpu.SMEM)],
            scratch_shapes=[pltpu.SemaphoreType.DMA(())]),
        out_shape=[jax.ShapeDtypeStruct((n_rows, d), F32),
                   jax.ShapeDtypeStruct((n_assign,), jnp.int32)],
        input_output_aliases={3: 0},
        compiler_params=_cparams(("arbitrary",)),
        name="moe_dispatch",
    )(pstart, code_flat, hx2d, xs_init)


def _expert_kernel(bexp_ref, xs_ref, wgu_ref, wd_ref, bg_ref, bu_ref, bd_ref, o_ref, wg_s, wu_s, wd_s):
    i = pl.program_id(0)
    changed = jnp.logical_or(i == 0, bexp_ref[i] != bexp_ref[jnp.maximum(i - 1, 0)])

    @pl.when(changed)
    def _():
        half = MXU_COLS // 2
        src = lax.broadcasted_iota(jnp.int32, (MXU_COLS, MXU_COLS), 0)
        dst = lax.broadcasted_iota(jnp.int32, (MXU_COLS, MXU_COLS), 1)
        perm = (src == jnp.where(dst < half, 2 * dst, 2 * (dst - half) + 1)).astype(BF16)
        for k in range(wgu_ref.shape[2] // MXU_COLS):
            w = wgu_ref[0, :, k * MXU_COLS:(k + 1) * MXU_COLS].astype(BF16)
            sep = jnp.dot(w, perm, preferred_element_type=F32).astype(BF16)
            wg_s[:, k * half:(k + 1) * half] = sep[:, :half]
            wu_s[:, k * half:(k + 1) * half] = sep[:, half:]
        wd_s[...] = wd_ref[0].astype(BF16)

    xb = xs_ref[...].astype(BF16)
    g = jnp.dot(xb, wg_s[...], preferred_element_type=F32) + bg_ref[0]
    u = jnp.dot(xb, wu_s[...], preferred_element_type=F32) + bu_ref[0]
    g = jnp.minimum(g, SWIGLU_LIMIT)
    u = jnp.clip(u, -SWIGLU_LIMIT, SWIGLU_LIMIT)
    a = g * (1.0 / (1.0 + jnp.exp(-SWIGLU_ALPHA * g))) * (u + 1.0)
    o_ref[...] = jnp.dot(a.astype(BF16), wd_s[...], preferred_element_type=F32) + bd_ref[0]


def _experts(block_exp, xs, w_gu, w_dn, bg, bu, bd):
    n_rows, d = xs.shape
    f = w_dn.shape[1]
    nblk = n_rows // EXPERT_BLOCK
    wsel = lambda i, be: (be[i], 0, 0)
    return pl.pallas_call(
        _expert_kernel,
        grid_spec=pltpu.PrefetchScalarGridSpec(
            num_scalar_prefetch=1,
            grid=(nblk,),
            in_specs=[pl.BlockSpec((EXPERT_BLOCK, d), lambda i, be: (i, 0)),
                      pl.BlockSpec((1, d, 2 * f), wsel),
                      pl.BlockSpec((1, f, d), wsel),
                      pl.BlockSpec((1, 1, f), wsel),
                      pl.BlockSpec((1, 1, f), wsel),
                      pl.BlockSpec((1, 1, d), wsel)],
            out_specs=pl.BlockSpec((EXPERT_BLOCK, d), lambda i, be: (i, 0)),
            scratch_shapes=[pltpu.VMEM((d, f), BF16), pltpu.VMEM((d, f), BF16), pltpu.VMEM((f, d), BF16)]),
        out_shape=jax.ShapeDtypeStruct((n_rows, d), F32),
        compiler_params=_cparams(("arbitrary",)),
        name="moe_experts",
    )(block_exp, xs, w_gu, w_dn, bg, bu, bd)


def _combine_kernel(dest_ref, rows_hbm, gate_ref, x1_ref, g2_ref, fg_ref, o_ref, buf, sems, *, tm):
    n = pl.program_id(0)
    slot = n % 2

    def gather(step, to_slot):
        base = step * (tm * TOP_K)

        def issue(tok, carry):
            for k in range(TOP_K):
                _row_copy(rows_hbm, dest_ref[base + tok * TOP_K + k], buf.at[to_slot, k], tok,
                          sems.at[to_slot]).start()
            return carry

        lax.fori_loop(0, tm, issue, 0, unroll=DMA_UNROLL)

    @pl.when(n == 0)
    def _():
        gather(0, 0)

    @pl.when(n + 1 < pl.num_programs(0))
    def _():
        gather(n + 1, 1 - slot)

    for k in range(TOP_K):
        pltpu.make_async_copy(rows_hbm.at[pl.ds(0, tm)], buf.at[slot, k], sems.at[slot]).wait()

    gate = gate_ref[...]
    y = gate[:, 0:1] * buf[slot, 0]
    for k in range(1, TOP_K):
        y = y + gate[:, k:k + 1] * buf[slot, k]
    xo = x1_ref[...] + g2_ref[0] * y
    ms = jnp.mean(xo * xo, axis=-1, keepdims=True)
    o_ref[...] = xo * lax.rsqrt(ms + NORM_EPS) * fg_ref[...]


def _combine(dest, rows, gates, x1, g2, final_g, tm):
    b, s, d = x1.shape
    nb = s // tm
    out = pl.pallas_call(
        functools.partial(_combine_kernel, tm=tm),
        grid_spec=pltpu.PrefetchScalarGridSpec(
            num_scalar_prefetch=1,
            grid=(b * nb,),
            in_specs=[pl.BlockSpec(memory_space=pl.ANY),
                      pl.BlockSpec((tm, TOP_K), lambda n, ds: (n, 0)),
                      pl.BlockSpec((tm, d), lambda n, ds: (n, 0)),
                      pl.BlockSpec((1, 1, d), lambda n, ds: (n // nb, 0, 0)),
                      pl.BlockSpec((1, d), lambda n, ds: (0, 0))],
            out_specs=pl.BlockSpec((tm, d), lambda n, ds: (n, 0)),
            scratch_shapes=[pltpu.VMEM((2, TOP_K, tm, d), F32), pltpu.SemaphoreType.DMA((2,))]),
        out_shape=jax.ShapeDtypeStruct((b * s, d), F32),
        compiler_params=_cparams(("arbitrary",)),
        name="moe_combine",
    )(dest, rows, gates, x1.reshape(b * s, d), g2, final_g)
    return out.reshape(b, s, d)


def _pick(n, prefs):
    for p in prefs:
        if n % p == 0:
            return p
    raise ValueError(f"no tile for {n}")


def kernel(x, c, ctx, c_ctx, w_mod, b_mod, norm1_g, w_in, lam_q1, lam_k1, lam_q2, lam_k2, subln_g,
           sink, w_out, norm2_g, w_router, b_router, w_gate_up, b_gate_up, w_down, b_down, final_g):
    b, s, d = x.shape
    c_len = ctx.shape[1]
    assert w_mod.shape[0] == 1, "single-layer block"
    t = b * s

    pad = (-(b + 1)) % 8
    cvecs = jnp.concatenate([c, c_ctx[None, :], jnp.zeros((pad, d), F32)], axis=0)
    mod = _adaln(cvecs, w_mod[0], b_mod[0])
    sh1, sc1, g1, sh2, sc2, g2 = [mod[:b, k * d:(k + 1) * d].reshape(b, 1, d) for k in range(6)]
    csh1 = mod[b:b + 1, 0:d].reshape(1, 1, d)
    csc1 = mod[b:b + 1, d:2 * d].reshape(1, 1, d)

    w_in_bf = w_in[0].astype(BF16)
    cos, sin = _rope_tables(s)
    n1 = norm1_g[0].reshape(1, d)
    qa, kat, va, qb, kbt, vb = _inproj_latent(x, n1, sh1, sc1, w_in_bf, cos, sin, _pick(s, (512, 256, 128)))
    w_ctx_bf = jnp.concatenate([w_in_bf[:, O_KA:O_QB], w_in_bf[:, O_KB:IN_COLS]], axis=1)
    kact, vac, kbct, vbc = _inproj_ctx(ctx, n1, csh1, csc1, w_ctx_bf)

    kat_all = jnp.concatenate([kat, kact], axis=2)
    va_all = jnp.concatenate([va, vac], axis=1)
    sk = s + c_len
    ya = _diff_attn(qa, kat_all, va_all,
                    lam_q1[0].reshape(1, -1), lam_k1[0].reshape(1, -1),
                    lam_q2[0].reshape(1, -1), lam_k2[0].reshape(1, -1),
                    subln_g[0].reshape(1, -1),
                    _pick(s, (256, 128)), _pick(sk, (768, 512, 384, 256, 128)))

    yb = _win_attn(qb, kbt, vb, kbct, vbc, sink[0].reshape(1, -1), _pick(s, (256, 128)))

    w_out_bf = w_out[0].astype(BF16)
    x1, hx2, code, gates, counts = _outproj_router(
        ya, yb, w_out_bf[:QA_COLS], w_out_bf[QA_COLS:], x, g1, norm2_g[0].reshape(1, d), sh2, sc2,
        w_router[0], b_router[0].reshape(1, -1), _pick(s, (256, 128)))

    n_assign = t * TOP_K
    n_rows = n_assign + N_EXPERTS * EXPERT_BLOCK
    counts_i = counts.reshape(-1).astype(jnp.int32)
    padded = ((counts_i + EXPERT_BLOCK - 1) // EXPERT_BLOCK) * EXPERT_BLOCK
    pend = jnp.cumsum(padded)
    pstart = (pend - padded).astype(jnp.int32)
    block_start = jnp.arange(n_rows // EXPERT_BLOCK, dtype=jnp.int32) * EXPERT_BLOCK
    block_exp = jnp.minimum(jnp.sum((pend[None, :] <= block_start[:, None]).astype(jnp.int32), axis=1),
                            N_EXPERTS - 1)

    xs, dest = _dispatch(pstart, code.reshape(-1), hx2.reshape(t, d), jnp.zeros((n_rows, d), F32),
                         _pick(n_assign, (1024,)))

    f = w_down.shape[2]
    bg = b_gate_up[0, :, 0::2].reshape(N_EXPERTS, 1, f)
    bu = b_gate_up[0, :, 1::2].reshape(N_EXPERTS, 1, f)
    bd = b_down[0].reshape(N_EXPERTS, 1, d)
    rows = _experts(block_exp, xs, w_gate_up[0], w_down[0], bg, bu, bd)

    return _combine(dest, rows, gates, x1, g2, final_g.reshape(1, d), _pick(s, (256,)))
```

```python
import functools
import math

import jax
import jax.numpy as jnp
from jax import lax
from jax.experimental import pallas as pl
from jax.experimental.pallas import tpu as pltpu

F32 = jnp.float32
BF16 = jnp.bfloat16
HIGHEST = lax.Precision.HIGHEST

GRID_W = 64
NORM_EPS = 1e-6
ROPE_BASE = 10000.0
MASK_VALUE = -1e30
DA_HEADS = 4
HEAD_DIM = 64
WA_HEADS = 8
WA_KV_HEADS = 2
WA_GROUP = WA_HEADS // WA_KV_HEADS
WINDOW = 128
N_EXPERTS = 32
TOP_K = 4
SWIGLU_LIMIT = 7.0
SWIGLU_ALPHA = 1.702
EXPERT_BLOCK = 256
LAM_INIT = 0.8 - 0.6 * math.exp(-0.3 * 0)
LOG2E = math.log2(math.e)

QA_COLS = DA_HEADS * 2 * HEAD_DIM
KA_COLS = QA_COLS
VA_COLS = QA_COLS
QB_COLS = WA_HEADS * HEAD_DIM
KB_COLS = WA_KV_HEADS * HEAD_DIM
VB_COLS = KB_COLS
O_QA = 0
O_KA = O_QA + QA_COLS
O_VA = O_KA + KA_COLS
O_QB = O_VA + VA_COLS
O_KB = O_QB + QB_COLS
O_VB = O_KB + KB_COLS
IN_COLS = O_VB + VB_COLS

LANES = 128
MXU_COLS = 256
DMA_UNROLL = 4
VMEM_LIMIT = 56 * 1024 * 1024


def _cparams(sem):
    return pltpu.CompilerParams(dimension_semantics=sem, vmem_limit_bytes=VMEM_LIMIT)


def _adaln_kernel(c_ref, w_ref, b_ref, o_ref):
    cv = c_ref[...]
    s = cv * (1.0 / (1.0 + jnp.exp(-cv)))
    o_ref[...] = jnp.dot(s, w_ref[...], precision=HIGHEST, preferred_element_type=F32) + b_ref[...]


def _adaln(cvecs, w_mod, b_mod):
    rows, d = cvecs.shape
    n = w_mod.shape[1]
    tn = 1024
    return pl.pallas_call(
        _adaln_kernel,
        grid=(n // tn,),
        in_specs=[pl.BlockSpec((rows, d), lambda j: (0, 0)),
                  pl.BlockSpec((d, tn), lambda j: (0, j)),
                  pl.BlockSpec((1, tn), lambda j: (0, j))],
        out_specs=pl.BlockSpec((rows, tn), lambda j: (0, j)),
        out_shape=jax.ShapeDtypeStruct((rows, n), F32),
        compiler_params=_cparams(("arbitrary",)),
        name="adaln",
    )(cvecs, w_mod, b_mod.reshape(1, n))


def _rope_section(sec, cos, sin):
    tm = sec.shape[0]
    lane = lax.broadcasted_iota(jnp.int32, (tm, LANES), 1)
    low = (lane % 32) < 16
    outs = []
    for j in range(sec.shape[1] // LANES):
        c = sec[:, j * LANES:(j + 1) * LANES]
        partner = jnp.where(low, pltpu.roll(c, LANES - 16, 1), pltpu.roll(c, 16, 1))
        outs.append(c * cos + partner * sin)
    return jnp.concatenate(outs, axis=1)


def _modulated_norm(x, g, shift, scale):
    ms = jnp.mean(x * x, axis=-1, keepdims=True)
    return (x * lax.rsqrt(ms + NORM_EPS) * g) * (1.0 + scale) + shift


def _inproj_latent_kernel(x_ref, g_ref, sh_ref, sc_ref, w_ref, cos_ref, sin_ref,
                          qat_ref, ka_ref, vat_ref, qb_ref, kbt_ref, vb_ref):
    h = _modulated_norm(x_ref[0], g_ref[...], sh_ref[0], sc_ref[0])
    p = jnp.dot(h.astype(BF16), w_ref[...], preferred_element_type=F32)
    cos = cos_ref[...]
    sin = sin_ref[...]
    scale = HEAD_DIM ** -0.5
    qat_ref[0] = (_rope_section(p[:, O_QA:O_KA], cos, sin) * (scale * LOG2E)).T.astype(BF16)
    ka_ref[0] = _rope_section(p[:, O_KA:O_VA], cos, sin).astype(BF16)
    vat_ref[0] = p[:, O_VA:O_QB].T.astype(BF16)
    qb_ref[0] = (_rope_section(p[:, O_QB:O_KB], cos, sin) * scale).astype(BF16)
    kbt_ref[0] = _rope_section(p[:, O_KB:O_VB], cos, sin).T.astype(BF16)
    vb_ref[0] = p[:, O_VB:IN_COLS].astype(BF16)


def _inproj_ctx_kernel(x_ref, g_ref, sh_ref, sc_ref, w_ref, ka_ref, vat_ref, kbt_ref, vb_ref):
    h = _modulated_norm(x_ref[0], g_ref[...], sh_ref[0], sc_ref[0])
    p = jnp.dot(h.astype(BF16), w_ref[...], preferred_element_type=F32)
    ka_ref[0] = p[:, 0:KA_COLS].astype(BF16)
    vat_ref[0] = p[:, KA_COLS:KA_COLS + VA_COLS].T.astype(BF16)
    kbt_ref[0] = p[:, KA_COLS + VA_COLS:KA_COLS + VA_COLS + KB_COLS].T.astype(BF16)
    vb_ref[0] = p[:, KA_COLS + VA_COLS + KB_COLS:].astype(BF16)


def _rope_tables(n_tok):
    rows = (jnp.arange(n_tok) // GRID_W).astype(F32)
    cols = (jnp.arange(n_tok) % GRID_W).astype(F32)
    nf = HEAD_DIM // 4
    inv = ROPE_BASE ** (-jnp.arange(nf, dtype=F32) / nf)
    ar = rows[:, None] * inv
    ac = cols[:, None] * inv
    cos = jnp.concatenate([jnp.cos(ar), jnp.cos(ar), jnp.cos(ac), jnp.cos(ac)], axis=1)
    sin = jnp.concatenate([-jnp.sin(ar), jnp.sin(ar), -jnp.sin(ac), jnp.sin(ac)], axis=1)
    return jnp.tile(cos, (1, LANES // HEAD_DIM)), jnp.tile(sin, (1, LANES // HEAD_DIM))


def _inproj_latent(x, g, shift, scale, w_bf16, cos, sin, tm):
    b, s, d = x.shape
    row = lambda bi, i: (bi, i, 0)
    colt = lambda bi, i: (bi, 0, i)
    mod = lambda bi, i: (bi, 0, 0)
    fixed = lambda bi, i: (0, 0)
    return pl.pallas_call(
        _inproj_latent_kernel,
        grid=(b, s // tm),
        in_specs=[pl.BlockSpec((1, tm, d), row),
                  pl.BlockSpec((1, d), fixed),
                  pl.BlockSpec((1, 1, d), mod),
                  pl.BlockSpec((1, 1, d), mod),
                  pl.BlockSpec((d, IN_COLS), fixed),
                  pl.BlockSpec((tm, LANES), lambda bi, i: (i, 0)),
                  pl.BlockSpec((tm, LANES), lambda bi, i: (i, 0))],
        out_specs=[pl.BlockSpec((1, QA_COLS, tm), colt),
                   pl.BlockSpec((1, tm, KA_COLS), row),
                   pl.BlockSpec((1, VA_COLS, tm), colt),
                   pl.BlockSpec((1, tm, QB_COLS), row),
                   pl.BlockSpec((1, KB_COLS, tm), colt),
                   pl.BlockSpec((1, tm, VB_COLS), row)],
        out_shape=[jax.ShapeDtypeStruct((b, QA_COLS, s), BF16),
                   jax.ShapeDtypeStruct((b, s, KA_COLS), BF16),
                   jax.ShapeDtypeStruct((b, VA_COLS, s), BF16),
                   jax.ShapeDtypeStruct((b, s, QB_COLS), BF16),
                   jax.ShapeDtypeStruct((b, KB_COLS, s), BF16),
                   jax.ShapeDtypeStruct((b, s, VB_COLS), BF16)],
        compiler_params=_cparams(("arbitrary", "arbitrary")),
        name="inproj_latent",
    )(x, g, shift, scale, w_bf16, cos, sin)


def _inproj_ctx(ctx, g, shift, scale, w_ctx_bf16):
    b, c, d = ctx.shape
    n = w_ctx_bf16.shape[1]
    whole = lambda bi: (bi, 0, 0)
    mod = lambda bi: (0, 0, 0)
    fixed = lambda bi: (0, 0)
    return pl.pallas_call(
        _inproj_ctx_kernel,
        grid=(b,),
        in_specs=[pl.BlockSpec((1, c, d), whole),
                  pl.BlockSpec((1, d), fixed),
                  pl.BlockSpec((1, 1, d), mod),
                  pl.BlockSpec((1, 1, d), mod),
                  pl.BlockSpec((d, n), fixed)],
        out_specs=[pl.BlockSpec((1, c, KA_COLS), whole),
                   pl.BlockSpec((1, VA_COLS, c), whole),
                   pl.BlockSpec((1, KB_COLS, c), whole),
                   pl.BlockSpec((1, c, VB_COLS), whole)],
        out_shape=[jax.ShapeDtypeStruct((b, c, KA_COLS), BF16),
                   jax.ShapeDtypeStruct((b, VA_COLS, c), BF16),
                   jax.ShapeDtypeStruct((b, KB_COLS, c), BF16),
                   jax.ShapeDtypeStruct((b, c, VB_COLS), BF16)],
        compiler_params=_cparams(("arbitrary",)),
        name="inproj_ctx",
    )(ctx, g, shift, scale, w_ctx_bf16)


ONES_ROWS = 16


def _diff_attn_kernel(qt_ref, k_ref, vt_ref, lq1_ref, lk1_ref, lq2_ref, lk2_ref, sg_ref, o_ref,
                      m_ref, acc_ref, s_ref, *, tk):
    d = HEAD_DIM
    hw = 2 * d
    qt = qt_ref[0]
    row = lax.broadcasted_iota(jnp.int32, qt.shape, 0)
    zero = jnp.zeros_like(qt)
    rhs = (jnp.where(row < d, qt, zero), jnp.where(row >= d, qt, zero))
    n_chunks = k_ref.shape[1] // tk
    m_ref[...] = jnp.full(m_ref.shape, -jnp.inf, F32)
    acc_ref[...] = jnp.zeros(acc_ref.shape, F32)
    ones = jnp.ones((ONES_ROWS, tk), BF16)

    def scores(c, j):
        off = pl.multiple_of(j * tk, tk)
        s_ref[c] = jnp.dot(k_ref[0, pl.ds(off, tk), :], rhs[c], preferred_element_type=F32)

    def accumulate(c, j):
        off = pl.multiple_of(j * tk, tk)
        vt = jnp.concatenate([vt_ref[0, :, pl.ds(off, tk)], ones], axis=0)
        st = s_ref[c]
        m_old = m_ref[c]
        m_new = jnp.maximum(m_old, jnp.max(st, axis=0, keepdims=True))
        alpha = jnp.exp2(m_old - m_new)
        p = jnp.exp2(st - m_new).astype(BF16)
        acc_ref[c] = alpha * acc_ref[c] + jnp.dot(vt, p, preferred_element_type=F32)
        m_ref[c] = m_new

    scores(0, 0)

    def chunk(j, carry):
        scores(1, j)
        accumulate(0, j)
        scores(0, j + 1)
        accumulate(1, j)
        return carry

    lax.fori_loop(0, n_chunks - 1, chunk, 0)
    scores(1, n_chunks - 1)
    accumulate(0, n_chunks - 1)
    accumulate(1, n_chunks - 1)

    lam = (jnp.exp(jnp.sum(lq1_ref[...] * lk1_ref[...], axis=-1, keepdims=True))
           - jnp.exp(jnp.sum(lq2_ref[...] * lk2_ref[...], axis=-1, keepdims=True)) + LAM_INIT)
    a1 = acc_ref[0]
    a2 = acc_ref[1]
    ot = a1[:hw] / a1[hw:hw + 1] - lam * (a2[:hw] / a2[hw:hw + 1])
    ms = jnp.mean(ot * ot, axis=0, keepdims=True)
    ot = ot * lax.rsqrt(ms + NORM_EPS)
    o_ref[0] = (ot.T * (sg_ref[...] * (1.0 - LAM_INIT))).astype(o_ref.dtype)


def _diff_attn(qat, ka, vat, lq1, lk1, lq2, lk2, subln_g, tq, tk):
    b, _, s = qat.shape
    sk = ka.shape[1]
    hw = 2 * HEAD_DIM
    vec = lambda bi, h, i: (0, 0)
    return pl.pallas_call(
        functools.partial(_diff_attn_kernel, tk=tk),
        grid=(b, DA_HEADS, s // tq),
        in_specs=[pl.BlockSpec((1, hw, tq), lambda bi, h, i: (bi, h, i)),
                  pl.BlockSpec((1, sk, hw), lambda bi, h, i: (bi, 0, h)),
                  pl.BlockSpec((1, hw, sk), lambda bi, h, i: (bi, h, 0)),
                  pl.BlockSpec((1, HEAD_DIM), vec),
                  pl.BlockSpec((1, HEAD_DIM), vec),
                  pl.BlockSpec((1, HEAD_DIM), vec),
                  pl.BlockSpec((1, HEAD_DIM), vec),
                  pl.BlockSpec((1, hw), vec)],
        out_specs=pl.BlockSpec((1, tq, hw), lambda bi, h, i: (bi, i, h)),
        out_shape=jax.ShapeDtypeStruct((b, s, DA_HEADS * hw), BF16),
        scratch_shapes=[pltpu.VMEM((2, 1, tq), F32),
                        pltpu.VMEM((2, hw + ONES_ROWS, tq), F32),
                        pltpu.VMEM((2, tk, tq), F32)],
        compiler_params=_cparams(("arbitrary", "arbitrary", "arbitrary")),
        name="diff_attn",
    )(qat, ka, vat, lq1, lk1, lq2, lk2, subln_g)


def _win_attn_kernel(q_ref, kt_ref, v_ref, kct_ref, vc_ref, sink_ref, o_ref, *, tq, lk):
    d = HEAD_DIM
    s_len = kt_ref.shape[2]
    i = pl.program_id(1)
    q0 = i * tq
    start = pl.multiple_of(jnp.clip(q0 - WINDOW, 0, s_len - lk), LANES)
    qpos = q0 + lax.broadcasted_iota(jnp.int32, (tq, 1), 0)
    kpos = start + lax.broadcasted_iota(jnp.int32, (1, lk), 1)
    valid = jnp.abs(kpos - qpos) <= WINDOW
    outs = []
    for kv in range(WA_KV_HEADS):
        kt = kt_ref[0, kv * d:(kv + 1) * d, pl.ds(start, lk)]
        v = v_ref[0, pl.ds(start, lk), kv * d:(kv + 1) * d]
        kct = kct_ref[0, kv * d:(kv + 1) * d, :]
        vc = vc_ref[0, :, kv * d:(kv + 1) * d]
        for g in range(WA_GROUP):
            h = kv * WA_GROUP + g
            qg = q_ref[0, :, h * d:(h + 1) * d]
            s_loc = jnp.where(valid, jnp.dot(qg, kt, preferred_element_type=F32), MASK_VALUE)
            s_ctx = jnp.dot(qg, kct, preferred_element_type=F32)
            sink = sink_ref[:, h:h + 1]
            m = jnp.maximum(jnp.maximum(jnp.max(s_loc, axis=-1, keepdims=True),
                                        jnp.max(s_ctx, axis=-1, keepdims=True)), sink)
            e_loc = jnp.exp(s_loc - m)
            e_ctx = jnp.exp(s_ctx - m)
            denom = (jnp.sum(e_loc, axis=-1, keepdims=True) + jnp.sum(e_ctx, axis=-1, keepdims=True)
                     + jnp.exp(sink - m))
            o = (jnp.dot(e_loc.astype(BF16), v, preferred_element_type=F32)
                 + jnp.dot(e_ctx.astype(BF16), vc, preferred_element_type=F32))
            outs.append(o / denom)
    o_ref[0] = jnp.concatenate(outs, axis=1).astype(o_ref.dtype)


def _win_attn(qb, kbt, vb, kbct, vbc, sink, tq):
    b, s, _ = qb.shape
    c = kbct.shape[2]
    lk = tq + 2 * WINDOW
    assert s >= lk and tq % LANES == 0
    whole = lambda bi, i: (bi, 0, 0)
    return pl.pallas_call(
        functools.partial(_win_attn_kernel, tq=tq, lk=lk),
        grid=(b, s // tq),
        in_specs=[pl.BlockSpec((1, tq, QB_COLS), lambda bi, i: (bi, i, 0)),
                  pl.BlockSpec((1, KB_COLS, s), whole),
                  pl.BlockSpec((1, s, VB_COLS), whole),
                  pl.BlockSpec((1, KB_COLS, c), whole),
                  pl.BlockSpec((1, c, VB_COLS), whole),
                  pl.BlockSpec((1, WA_HEADS), lambda bi, i: (0, 0))],
        out_specs=pl.BlockSpec((1, tq, QB_COLS), lambda bi, i: (bi, i, 0)),
        out_shape=jax.ShapeDtypeStruct((b, s, QB_COLS), BF16),
        compiler_params=_cparams(("arbitrary", "arbitrary")),
        name="win_attn",
    )(qb, kbt, vb, kbct, vbc, sink)


def _outproj_router_kernel(ya_ref, yb_ref, woa_ref, wob_ref, x_ref, g1_ref, n2_ref, sh_ref, sc_ref,
                           wr_ref, br_ref, x1_ref, hx_ref, code_ref, gate_ref, cnt_ref, carry_ref):
    first = jnp.logical_and(pl.program_id(0) == 0, pl.program_id(1) == 0)

    @pl.when(first)
    def _():
        carry_ref[...] = jnp.zeros(carry_ref.shape, F32)

    y = (jnp.dot(ya_ref[0], woa_ref[...], preferred_element_type=F32)
         + jnp.dot(yb_ref[0], wob_ref[...], preferred_element_type=F32))
    x1 = x_ref[0] + g1_ref[0] * y
    x1_ref[0] = x1
    hx = _modulated_norm(x1, n2_ref[...], sh_ref[0], sc_ref[0])
    hx_ref[0] = hx
    logits = jnp.dot(hx, wr_ref[...], precision=HIGHEST, preferred_element_type=F32) + br_ref[...]

    tm = logits.shape[0]
    lane_e = lax.broadcasted_iota(jnp.int32, (tm, N_EXPERTS), 1).astype(F32)
    work = logits
    tops, idxs, hots = [], [], []
    for _k in range(TOP_K):
        m = jnp.max(work, axis=-1, keepdims=True)
        idx = jnp.min(jnp.where(work == m, lane_e, float(N_EXPERTS)), axis=-1, keepdims=True)
        hot = lane_e == idx
        work = jnp.where(hot, -jnp.inf, work)
        tops.append(m)
        idxs.append(idx)
        hots.append(hot)
    es = [jnp.exp(t - tops[0]) for t in tops]
    den = es[0] + es[1] + es[2] + es[3]

    multi = jnp.zeros((tm, N_EXPERTS), F32)
    for hot in hots:
        multi = multi + hot.astype(F32)
    r_i = lax.broadcasted_iota(jnp.int32, (tm, tm), 0)
    c_i = lax.broadcasted_iota(jnp.int32, (tm, tm), 1)
    tri = (c_i <= r_i).astype(BF16)
    incl = jnp.dot(tri, multi.astype(BF16), preferred_element_type=F32)
    before = carry_ref[...] + incl - 1.0

    lane_k = lax.broadcasted_iota(jnp.int32, (tm, TOP_K), 1)
    code = jnp.zeros((tm, TOP_K), jnp.int32)
    gate = jnp.zeros((tm, TOP_K), F32)
    for k in range(TOP_K):
        rank = jnp.sum(jnp.where(hots[k], before, 0.0), axis=-1, keepdims=True)
        ck = idxs[k].astype(jnp.int32) * 65536 + rank.astype(jnp.int32)
        code = jnp.where(lane_k == k, ck, code)
        gate = jnp.where(lane_k == k, es[k] / den, gate)
    code_ref[...] = code
    gate_ref[...] = gate
    carry_ref[...] = carry_ref[...] + jnp.sum(multi, axis=0, keepdims=True)
    cnt_ref[...] = carry_ref[...]


def _outproj_router(ya, yb, woa, wob, x, g1, n2, sh2, sc2, w_r, b_r, tm):
    b, s, d = x.shape
    nb = s // tm
    row = lambda bi, i: (bi, i, 0)
    mod = lambda bi, i: (bi, 0, 0)
    fixed = lambda bi, i: (0, 0)
    tok = lambda bi, i: (bi * nb + i, 0)
    half = ya.shape[2]
    return pl.pallas_call(
        _outproj_router_kernel,
        grid=(b, nb),
        in_specs=[pl.BlockSpec((1, tm, half), row),
                  pl.BlockSpec((1, tm, half), row),
                  pl.BlockSpec((half, d), fixed),
                  pl.BlockSpec((half, d), fixed),
                  pl.BlockSpec((1, tm, d), row),
                  pl.BlockSpec((1, 1, d), mod),
                  pl.BlockSpec((1, d), fixed),
                  pl.BlockSpec((1, 1, d), mod),
                  pl.BlockSpec((1, 1, d), mod),
                  pl.BlockSpec((d, N_EXPERTS), fixed),
                  pl.BlockSpec((1, N_EXPERTS), fixed)],
        out_specs=[pl.BlockSpec((1, tm, d), row),
                   pl.BlockSpec((1, tm, d), row),
                   pl.BlockSpec((tm, TOP_K), tok),
                   pl.BlockSpec((tm, TOP_K), tok),
                   pl.BlockSpec((1, N_EXPERTS), fixed)],
        out_shape=[jax.ShapeDtypeStruct((b, s, d), F32),
                   jax.ShapeDtypeStruct((b, s, d), F32),
                   jax.ShapeDtypeStruct((b * s, TOP_K), jnp.int32),
                   jax.ShapeDtypeStruct((b * s, TOP_K), F32),
                   jax.ShapeDtypeStruct((1, N_EXPERTS), F32)],
        scratch_shapes=[pltpu.VMEM((1, N_EXPERTS), F32)],
        compiler_params=_cparams(("arbitrary", "arbitrary")),
        name="outproj_router",
    )(ya, yb, woa, wob, x, g1, n2, sh2, sc2, w_r, b_r)


def _row_copy(src_hbm, src_row, dst_ref, dst_row, sem):
    return pltpu.make_async_copy(src_hbm.at[pl.ds(src_row, 1)], dst_ref.at[pl.ds(dst_row, 1)], sem)


def _dispatch_kernel(pstart_ref, code_ref, hx_ref, init_hbm, xs_hbm, dest_ref, sem, *, chunk):
    del init_hbm

    def issue(tok, carry):
        for k in range(TOP_K):
            j = tok * TOP_K + k
            cd = code_ref[j]
            dst = pstart_ref[cd >> 16] + (cd & 0xFFFF)
            dest_ref[j] = dst
            _row_copy(hx_ref, tok, xs_hbm, dst, sem).start()
        return carry

    lax.fori_loop(0, chunk // TOP_K, issue, 0, unroll=DMA_UNROLL)
    pltpu.make_async_copy(xs_hbm.at[pl.ds(0, chunk)], xs_hbm.at[pl.ds(0, chunk)], sem).wait()


def _dispatch(pstart, code_flat, hx2d, xs_init, chunk):
    n_assign = code_flat.shape[0]
    n_rows, d = xs_init.shape
    return pl.pallas_call(
        functools.partial(_dispatch_kernel, chunk=chunk),
        grid_spec=pltpu.PrefetchScalarGridSpec(
            num_scalar_prefetch=1,
            grid=(n_assign // chunk,),
            in_specs=[pl.BlockSpec((chunk,), lambda i, ps: (i,), memory_space=pltpu.SMEM),
                      pl.BlockSpec((chunk // TOP_K, d), lambda i, ps: (i, 0)),
                      pl.BlockSpec(memory_space=pl.ANY)],
            out_specs=[pl.BlockSpec(memory_space=pl.ANY),
                       pl.BlockSpec((chunk,), lambda i, ps: (i,), memory_space=pltpu.SMEM)],
            scratch_shapes=[pltpu.SemaphoreType.DMA(())]),
        out_shape=[jax.ShapeDtypeStruct((n_rows, d), F32),
                   jax.ShapeDtypeStruct((n_assign,), jnp.int32)],
        input_output_aliases={3: 0},
        compiler_params=_cparams(("arbitrary",)),
        name="moe_dispatch",
    )(pstart, code_flat, hx2d, xs_init)


def _expert_kernel(bexp_ref, nact_ref, xs_ref, wgu_ref, wd_ref, bg_ref, bu_ref, bd_ref, o_ref, wg_s, wu_s, wd_s):
    i = pl.program_id(0)
    changed = jnp.logical_or(i == 0, bexp_ref[i] != bexp_ref[jnp.maximum(i - 1, 0)])

    @pl.when(changed)
    def _():
        half = MXU_COLS // 2
        src = lax.broadcasted_iota(jnp.int32, (MXU_COLS, MXU_COLS), 0)
        dst = lax.broadcasted_iota(jnp.int32, (MXU_COLS, MXU_COLS), 1)
        perm = (src == jnp.where(dst < half, 2 * dst, 2 * (dst - half) + 1)).astype(BF16)
        for k in range(wgu_ref.shape[2] // MXU_COLS):
            w = wgu_ref[0, :, k * MXU_COLS:(k + 1) * MXU_COLS].astype(BF16)
            sep = jnp.dot(w, perm, preferred_element_type=F32).astype(BF16)
            wg_s[:, k * half:(k + 1) * half] = sep[:, :half]
            wu_s[:, k * half:(k + 1) * half] = sep[:, half:]
        wd_s[...] = wd_ref[0].astype(BF16)

    @pl.when(i < nact_ref[0])
    def _():
        xb = xs_ref[...].astype(BF16)
        g = jnp.dot(xb, wg_s[...], preferred_element_type=F32) + bg_ref[0]
        u = jnp.dot(xb, wu_s[...], preferred_element_type=F32) + bu_ref[0]
        g = jnp.minimum(g, SWIGLU_LIMIT)
        u = jnp.clip(u, -SWIGLU_LIMIT, SWIGLU_LIMIT)
        a = g * (1.0 / (1.0 + jnp.exp(-SWIGLU_ALPHA * g))) * (u + 1.0)
        o_ref[...] = jnp.dot(a.astype(BF16), wd_s[...], preferred_element_type=F32) + bd_ref[0]

    @pl.when(i >= nact_ref[0])
    def _():
        o_ref[...] = jnp.zeros(o_ref.shape, F32)


def _experts(block_exp, n_active, xs, w_gu, w_dn, bg, bu, bd):
    n_rows, d = xs.shape
    f = w_dn.shape[1]
    nblk = n_rows // EXPERT_BLOCK
    wsel = lambda i, be, na: (be[i], 0, 0)
    return pl.pallas_call(
        _expert_kernel,
        grid_spec=pltpu.PrefetchScalarGridSpec(
            num_scalar_prefetch=2,
            grid=(nblk,),
            in_specs=[pl.BlockSpec((EXPERT_BLOCK, d), lambda i, be, na: (i, 0)),
                      pl.BlockSpec((1, d, 2 * f), wsel),
                      pl.BlockSpec((1, f, d), wsel),
                      pl.BlockSpec((1, 1, f), wsel),
                      pl.BlockSpec((1, 1, f), wsel),
                      pl.BlockSpec((1, 1, d), wsel)],
            out_specs=pl.BlockSpec((EXPERT_BLOCK, d), lambda i, be, na: (i, 0)),
            scratch_shapes=[pltpu.VMEM((d, f), BF16), pltpu.VMEM((d, f), BF16), pltpu.VMEM((f, d), BF16)]),
        out_shape=jax.ShapeDtypeStruct((n_rows, d), F32),
        compiler_params=_cparams(("arbitrary",)),
        name="moe_experts",
    )(block_exp, n_active, xs, w_gu, w_dn, bg, bu, bd)


def _combine_kernel(dest_ref, rows_hbm, gate_ref, x1_ref, g2_ref, fg_ref, o_ref, buf, sems, *, tm):
    n = pl.program_id(0)
    slot = n % 2

    def gather(step, to_slot):
        base = step * (tm * TOP_K)

        def issue(tok, carry):
            for k in range(TOP_K):
                _row_copy(rows_hbm, dest_ref[base + tok * TOP_K + k], buf.at[to_slot, k], tok,
                          sems.at[to_slot]).start()
            return carry

        lax.fori_loop(0, tm, issue, 0, unroll=DMA_UNROLL)

    @pl.when(n == 0)
    def _():
        gather(0, 0)

    @pl.when(n + 1 < pl.num_programs(0))
    def _():
        gather(n + 1, 1 - slot)

    for k in range(TOP_K):
        pltpu.make_async_copy(rows_hbm.at[pl.ds(0, tm)], buf.at[slot, k], sems.at[slot]).wait()

    gate = gate_ref[...]
    y = gate[:, 0:1] * buf[slot, 0]
    for k in range(1, TOP_K):
        y = y + gate[:, k:k + 1] * buf[slot, k]
    xo = x1_ref[...] + g2_ref[0] * y
    ms = jnp.mean(xo * xo, axis=-1, keepdims=True)
    o_ref[...] = xo * lax.rsqrt(ms + NORM_EPS) * fg_ref[...]


def _combine(dest, rows, gates, x1, g2, final_g, tm):
    b, s, d = x1.shape
    nb = s // tm
    out = pl.pallas_call(
        functools.partial(_combine_kernel, tm=tm),
        grid_spec=pltpu.PrefetchScalarGridSpec(
            num_scalar_prefetch=1,
            grid=(b * nb,),
            in_specs=[pl.BlockSpec(memory_space=pl.ANY),
                      pl.BlockSpec((tm, TOP_K), lambda n, ds: (n, 0)),
                      pl.BlockSpec((tm, d), lambda n, ds: (n, 0)),
                      pl.BlockSpec((1, 1, d), lambda n, ds: (n // nb, 0, 0)),
                      pl.BlockSpec((1, d), lambda n, ds: (0, 0))],
            out_specs=pl.BlockSpec((tm, d), lambda n, ds: (n, 0)),
            scratch_shapes=[pltpu.VMEM((2, TOP_K, tm, d), F32), pltpu.SemaphoreType.DMA((2,))]),
        out_shape=jax.ShapeDtypeStruct((b * s, d), F32),
        compiler_params=_cparams(("arbitrary",)),
        name="moe_combine",
    )(dest, rows, gates, x1.reshape(b * s, d), g2, final_g)
    return out.reshape(b, s, d)


def _pick(n, prefs):
    for p in prefs:
        if n % p == 0:
            return p
    raise ValueError(f"no tile for {n}")


def kernel(x, c, ctx, c_ctx, w_mod, b_mod, norm1_g, w_in, lam_q1, lam_k1, lam_q2, lam_k2, subln_g,
           sink, w_out, norm2_g, w_router, b_router, w_gate_up, b_gate_up, w_down, b_down, final_g):
    b, s, d = x.shape
    c_len = ctx.shape[1]
    assert w_mod.shape[0] == 1, "single-layer block"
    t = b * s

    pad = (-(b + 1)) % 8
    cvecs = jnp.concatenate([c, c_ctx[None, :], jnp.zeros((pad, d), F32)], axis=0)
    mod = _adaln(cvecs, w_mod[0], b_mod[0])
    sh1, sc1, g1, sh2, sc2, g2 = [mod[:b, k * d:(k + 1) * d].reshape(b, 1, d) for k in range(6)]
    csh1 = mod[b:b + 1, 0:d].reshape(1, 1, d)
    csc1 = mod[b:b + 1, d:2 * d].reshape(1, 1, d)

    w_in_bf = w_in[0].astype(BF16)
    cos, sin = _rope_tables(s)
    n1 = norm1_g[0].reshape(1, d)
    qat, ka, vat, qb, kbt, vb = _inproj_latent(x, n1, sh1, sc1, w_in_bf, cos, sin, _pick(s, (512, 256, 128)))
    w_ctx_bf = jnp.concatenate([w_in_bf[:, O_KA:O_QB], w_in_bf[:, O_KB:IN_COLS]], axis=1)
    kac, vact, kbct, vbc = _inproj_ctx(ctx, n1, csh1, csc1, w_ctx_bf)

    ka_all = jnp.concatenate([ka, kac], axis=1)
    vat_all = jnp.concatenate([vat, vact], axis=2)
    sk = s + c_len
    ya = _diff_attn(qat, ka_all, vat_all,
                    lam_q1[0].reshape(1, -1), lam_k1[0].reshape(1, -1),
                    lam_q2[0].reshape(1, -1), lam_k2[0].reshape(1, -1),
                    subln_g[0].reshape(1, -1),
                    _pick(s, (1024, 512, 256, 128)), _pick(sk, (768, 512, 384, 256, 128)))

    yb = _win_attn(qb, kbt, vb, kbct, vbc, sink[0].reshape(1, -1), _pick(s, (256, 128)))

    w_out_bf = w_out[0].astype(BF16)
    x1, hx2, code, gates, counts = _outproj_router(
        ya, yb, w_out_bf[:QA_COLS], w_out_bf[QA_COLS:], x, g1, norm2_g[0].reshape(1, d), sh2, sc2,
        w_router[0], b_router[0].reshape(1, -1), _pick(s, (256, 128)))

    n_assign = t * TOP_K
    n_rows = n_assign + N_EXPERTS * EXPERT_BLOCK
    counts_i = counts.reshape(-1).astype(jnp.int32)
    padded = ((counts_i + EXPERT_BLOCK - 1) // EXPERT_BLOCK) * EXPERT_BLOCK
    pend = jnp.cumsum(padded)
    pstart = (pend - padded).astype(jnp.int32)
    block_start = jnp.arange(n_rows // EXPERT_BLOCK, dtype=jnp.int32) * EXPERT_BLOCK
    block_exp = jnp.minimum(jnp.sum((pend[None, :] <= block_start[:, None]).astype(jnp.int32), axis=1),
                            N_EXPERTS - 1)

    xs, dest = _dispatch(pstart, code.reshape(-1), hx2.reshape(t, d), jnp.zeros((n_rows, d), F32),
                         _pick(n_assign, (1024,)))

    f = w_down.shape[2]
    bg = b_gate_up[0, :, 0::2].reshape(N_EXPERTS, 1, f)
    bu = b_gate_up[0, :, 1::2].reshape(N_EXPERTS, 1, f)
    bd = b_down[0].reshape(N_EXPERTS, 1, d)
    n_active = (pend[-1:] // EXPERT_BLOCK).astype(jnp.int32)
    rows = _experts(block_exp, n_active, xs, w_gate_up[0], w_down[0], bg, bu, bd)

    return _combine(dest, rows, gates, x1, g2, final_g.reshape(1, d), _pick(s, (256,)))
```

```python
import functools
import math

import jax
import jax.numpy as jnp
from jax import lax
from jax.experimental import pallas as pl
from jax.experimental.pallas import tpu as pltpu

F32 = jnp.float32
BF16 = jnp.bfloat16
HIGHEST = lax.Precision.HIGHEST

GRID_W = 64
NORM_EPS = 1e-6
ROPE_BASE = 10000.0
MASK_VALUE = -1e30
DA_HEADS = 4
HEAD_DIM = 64
WA_HEADS = 8
WA_KV_HEADS = 2
WA_GROUP = WA_HEADS // WA_KV_HEADS
WINDOW = 128
N_EXPERTS = 32
TOP_K = 4
SWIGLU_LIMIT = 7.0
SWIGLU_ALPHA = 1.702
EXPERT_BLOCK = 256
LAM_INIT = 0.8 - 0.6 * math.exp(-0.3 * 0)
LOG2E = math.log2(math.e)

QA_COLS = DA_HEADS * 2 * HEAD_DIM
KA_COLS = QA_COLS
VA_COLS = QA_COLS
QB_COLS = WA_HEADS * HEAD_DIM
KB_COLS = WA_KV_HEADS * HEAD_DIM
VB_COLS = KB_COLS
O_QA = 0
O_KA = O_QA + QA_COLS
O_VA = O_KA + KA_COLS
O_QB = O_VA + VA_COLS
O_KB = O_QB + QB_COLS
O_VB = O_KB + KB_COLS
IN_COLS = O_VB + VB_COLS

LANES = 128
MXU_COLS = 256
ROW_TILE = 8
VMEM_LIMIT = 56 * 1024 * 1024


def _cparams(sem):
    return pltpu.CompilerParams(dimension_semantics=sem, vmem_limit_bytes=VMEM_LIMIT)


def _adaln_kernel(c_ref, w_ref, b_ref, o_ref):
    cv = c_ref[...]
    s = cv * (1.0 / (1.0 + jnp.exp(-cv)))
    o_ref[...] = jnp.dot(s, w_ref[...], precision=HIGHEST, preferred_element_type=F32) + b_ref[...]


def _adaln(cvecs, w_mod, b_mod):
    rows, d = cvecs.shape
    n = w_mod.shape[1]
    tn = 1024
    return pl.pallas_call(
        _adaln_kernel,
        grid=(n // tn,),
        in_specs=[pl.BlockSpec((rows, d), lambda j: (0, 0)),
                  pl.BlockSpec((d, tn), lambda j: (0, j)),
                  pl.BlockSpec((1, tn), lambda j: (0, j))],
        out_specs=pl.BlockSpec((rows, tn), lambda j: (0, j)),
        out_shape=jax.ShapeDtypeStruct((rows, n), F32),
        compiler_params=_cparams(("arbitrary",)),
        name="adaln",
    )(cvecs, w_mod, b_mod.reshape(1, n))


def _rope_section(sec, cos, sin):
    tm = sec.shape[0]
    lane = lax.broadcasted_iota(jnp.int32, (tm, LANES), 1)
    low = (lane % 32) < 16
    outs = []
    for j in range(sec.shape[1] // LANES):
        c = sec[:, j * LANES:(j + 1) * LANES]
        partner = jnp.where(low, pltpu.roll(c, LANES - 16, 1), pltpu.roll(c, 16, 1))
        outs.append(c * cos + partner * sin)
    return jnp.concatenate(outs, axis=1)


def _modulated_norm(x, g, shift, scale):
    ms = jnp.mean(x * x, axis=-1, keepdims=True)
    return (x * lax.rsqrt(ms + NORM_EPS) * g) * (1.0 + scale) + shift


def _inproj_latent_kernel(x_ref, g_ref, sh_ref, sc_ref, w_ref, cos_ref, sin_ref,
                          qat_ref, ka_ref, vat_ref, qb_ref, kbt_ref, vb_ref):
    h = _modulated_norm(x_ref[0], g_ref[...], sh_ref[0], sc_ref[0])
    p = jnp.dot(h.astype(BF16), w_ref[...], preferred_element_type=F32)
    cos = cos_ref[...]
    sin = sin_ref[...]
    scale = HEAD_DIM ** -0.5
    qat_ref[0] = (_rope_section(p[:, O_QA:O_KA], cos, sin) * (scale * LOG2E)).T.astype(BF16)
    ka_ref[0] = _rope_section(p[:, O_KA:O_VA], cos, sin).astype(BF16)
    vat_ref[0] = p[:, O_VA:O_QB].T.astype(BF16)
    qb_ref[0] = (_rope_section(p[:, O_QB:O_KB], cos, sin) * scale).astype(BF16)
    kbt_ref[0] = _rope_section(p[:, O_KB:O_VB], cos, sin).T.astype(BF16)
    vb_ref[0] = p[:, O_VB:IN_COLS].astype(BF16)


def _inproj_ctx_kernel(x_ref, g_ref, sh_ref, sc_ref, w_ref, ka_ref, vat_ref, kbt_ref, vb_ref):
    h = _modulated_norm(x_ref[0], g_ref[...], sh_ref[0], sc_ref[0])
    p = jnp.dot(h.astype(BF16), w_ref[...], preferred_element_type=F32)
    ka_ref[0] = p[:, 0:KA_COLS].astype(BF16)
    vat_ref[0] = p[:, KA_COLS:KA_COLS + VA_COLS].T.astype(BF16)
    kbt_ref[0] = p[:, KA_COLS + VA_COLS:KA_COLS + VA_COLS + KB_COLS].T.astype(BF16)
    vb_ref[0] = p[:, KA_COLS + VA_COLS + KB_COLS:].astype(BF16)


def _rope_tables(n_tok):
    rows = (jnp.arange(n_tok) // GRID_W).astype(F32)
    cols = (jnp.arange(n_tok) % GRID_W).astype(F32)
    nf = HEAD_DIM // 4
    inv = ROPE_BASE ** (-jnp.arange(nf, dtype=F32) / nf)
    ar = rows[:, None] * inv
    ac = cols[:, None] * inv
    cos = jnp.concatenate([jnp.cos(ar), jnp.cos(ar), jnp.cos(ac), jnp.cos(ac)], axis=1)
    sin = jnp.concatenate([-jnp.sin(ar), jnp.sin(ar), -jnp.sin(ac), jnp.sin(ac)], axis=1)
    return jnp.tile(cos, (1, LANES // HEAD_DIM)), jnp.tile(sin, (1, LANES // HEAD_DIM))


def _inproj_latent(x, g, shift, scale, w_bf16, cos, sin, tm):
    b, s, d = x.shape
    row = lambda bi, i: (bi, i, 0)
    colt = lambda bi, i: (bi, 0, i)
    mod = lambda bi, i: (bi, 0, 0)
    fixed = lambda bi, i: (0, 0)
    return pl.pallas_call(
        _inproj_latent_kernel,
        grid=(b, s // tm),
        in_specs=[pl.BlockSpec((1, tm, d), row),
                  pl.BlockSpec((1, d), fixed),
                  pl.BlockSpec((1, 1, d), mod),
                  pl.BlockSpec((1, 1, d), mod),
                  pl.BlockSpec((d, IN_COLS), fixed),
                  pl.BlockSpec((tm, LANES), lambda bi, i: (i, 0)),
                  pl.BlockSpec((tm, LANES), lambda bi, i: (i, 0))],
        out_specs=[pl.BlockSpec((1, QA_COLS, tm), colt),
                   pl.BlockSpec((1, tm, KA_COLS), row),
                   pl.BlockSpec((1, VA_COLS, tm), colt),
                   pl.BlockSpec((1, tm, QB_COLS), row),
                   pl.BlockSpec((1, KB_COLS, tm), colt),
                   pl.BlockSpec((1, tm, VB_COLS), row)],
        out_shape=[jax.ShapeDtypeStruct((b, QA_COLS, s), BF16),
                   jax.ShapeDtypeStruct((b, s, KA_COLS), BF16),
                   jax.ShapeDtypeStruct((b, VA_COLS, s), BF16),
                   jax.ShapeDtypeStruct((b, s, QB_COLS), BF16),
                   jax.ShapeDtypeStruct((b, KB_COLS, s), BF16),
                   jax.ShapeDtypeStruct((b, s, VB_COLS), BF16)],
        compiler_params=_cparams(("arbitrary", "arbitrary")),
        name="inproj_latent",
    )(x, g, shift, scale, w_bf16, cos, sin)


def _inproj_ctx(ctx, g, shift, scale, w_ctx_bf16):
    b, c, d = ctx.shape
    n = w_ctx_bf16.shape[1]
    whole = lambda bi: (bi, 0, 0)
    mod = lambda bi: (0, 0, 0)
    fixed = lambda bi: (0, 0)
    return pl.pallas_call(
        _inproj_ctx_kernel,
        grid=(b,),
        in_specs=[pl.BlockSpec((1, c, d), whole),
                  pl.BlockSpec((1, d), fixed),
                  pl.BlockSpec((1, 1, d), mod),
                  pl.BlockSpec((1, 1, d), mod),
                  pl.BlockSpec((d, n), fixed)],
        out_specs=[pl.BlockSpec((1, c, KA_COLS), whole),
                   pl.BlockSpec((1, VA_COLS, c), whole),
                   pl.BlockSpec((1, KB_COLS, c), whole),
                   pl.BlockSpec((1, c, VB_COLS), whole)],
        out_shape=[jax.ShapeDtypeStruct((b, c, KA_COLS), BF16),
                   jax.ShapeDtypeStruct((b, VA_COLS, c), BF16),
                   jax.ShapeDtypeStruct((b, KB_COLS, c), BF16),
                   jax.ShapeDtypeStruct((b, c, VB_COLS), BF16)],
        compiler_params=_cparams(("arbitrary",)),
        name="inproj_ctx",
    )(ctx, g, shift, scale, w_ctx_bf16)


ONES_ROWS = 16


def _diff_attn_kernel(qt_ref, k_ref, vt_ref, lq1_ref, lk1_ref, lq2_ref, lk2_ref, sg_ref, o_ref,
                      m_ref, acc_ref, s_ref, *, tk):
    d = HEAD_DIM
    hw = 2 * d
    qt = qt_ref[0]
    row = lax.broadcasted_iota(jnp.int32, qt.shape, 0)
    zero = jnp.zeros_like(qt)
    rhs = (jnp.where(row < d, qt, zero), jnp.where(row >= d, qt, zero))
    n_chunks = k_ref.shape[1] // tk
    m_ref[...] = jnp.full(m_ref.shape, -jnp.inf, F32)
    acc_ref[...] = jnp.zeros(acc_ref.shape, F32)
    ones = jnp.ones((ONES_ROWS, tk), BF16)

    def scores(c, j):
        off = pl.multiple_of(j * tk, tk)
        s_ref[c] = jnp.dot(k_ref[0, pl.ds(off, tk), :], rhs[c], preferred_element_type=F32)

    def accumulate(c, j):
        off = pl.multiple_of(j * tk, tk)
        vt = jnp.concatenate([vt_ref[0, :, pl.ds(off, tk)], ones], axis=0)
        st = s_ref[c]
        m_old = m_ref[c]
        m_new = jnp.maximum(m_old, jnp.max(st, axis=0, keepdims=True))
        alpha = jnp.exp2(m_old - m_new)
        p = jnp.exp2(st - m_new).astype(BF16)
        acc_ref[c] = alpha * acc_ref[c] + jnp.dot(vt, p, preferred_element_type=F32)
        m_ref[c] = m_new

    scores(0, 0)

    def chunk(j, carry):
        scores(1, j)
        accumulate(0, j)
        scores(0, j + 1)
        accumulate(1, j)
        return carry

    lax.fori_loop(0, n_chunks - 1, chunk, 0)
    scores(1, n_chunks - 1)
    accumulate(0, n_chunks - 1)
    accumulate(1, n_chunks - 1)

    lam = (jnp.exp(jnp.sum(lq1_ref[...] * lk1_ref[...], axis=-1, keepdims=True))
           - jnp.exp(jnp.sum(lq2_ref[...] * lk2_ref[...], axis=-1, keepdims=True)) + LAM_INIT)
    a1 = acc_ref[0]
    a2 = acc_ref[1]
    ot = a1[:hw] / a1[hw:hw + 1] - lam * (a2[:hw] / a2[hw:hw + 1])
    ms = jnp.mean(ot * ot, axis=0, keepdims=True)
    ot = ot * lax.rsqrt(ms + NORM_EPS)
    o_ref[0] = (ot.T * (sg_ref[...] * (1.0 - LAM_INIT))).astype(o_ref.dtype)


def _diff_attn(qat, ka, vat, lq1, lk1, lq2, lk2, subln_g, tq, tk):
    b, _, s = qat.shape
    sk = ka.shape[1]
    hw = 2 * HEAD_DIM
    vec = lambda bi, h, i: (0, 0)
    return pl.pallas_call(
        functools.partial(_diff_attn_kernel, tk=tk),
        grid=(b, DA_HEADS, s // tq),
        in_specs=[pl.BlockSpec((1, hw, tq), lambda bi, h, i: (bi, h, i)),
                  pl.BlockSpec((1, sk, hw), lambda bi, h, i: (bi, 0, h)),
                  pl.BlockSpec((1, hw, sk), lambda bi, h, i: (bi, h, 0)),
                  pl.BlockSpec((1, HEAD_DIM), vec),
                  pl.BlockSpec((1, HEAD_DIM), vec),
                  pl.BlockSpec((1, HEAD_DIM), vec),
                  pl.BlockSpec((1, HEAD_DIM), vec),
                  pl.BlockSpec((1, hw), vec)],
        out_specs=pl.BlockSpec((1, tq, hw), lambda bi, h, i: (bi, i, h)),
        out_shape=jax.ShapeDtypeStruct((b, s, DA_HEADS * hw), BF16),
        scratch_shapes=[pltpu.VMEM((2, 1, tq), F32),
                        pltpu.VMEM((2, hw + ONES_ROWS, tq), F32),
                        pltpu.VMEM((2, tk, tq), F32)],
        compiler_params=_cparams(("arbitrary", "arbitrary", "arbitrary")),
        name="diff_attn",
    )(qat, ka, vat, lq1, lk1, lq2, lk2, subln_g)


def _win_attn_kernel(q_ref, kt_ref, v_ref, kct_ref, vc_ref, sink_ref, o_ref, *, tq, lk):
    d = HEAD_DIM
    s_len = kt_ref.shape[2]
    i = pl.program_id(1)
    q0 = i * tq
    start = pl.multiple_of(jnp.clip(q0 - WINDOW, 0, s_len - lk), LANES)
    qpos = q0 + lax.broadcasted_iota(jnp.int32, (tq, 1), 0)
    kpos = start + lax.broadcasted_iota(jnp.int32, (1, lk), 1)
    valid = jnp.abs(kpos - qpos) <= WINDOW
    outs = []
    for kv in range(WA_KV_HEADS):
        kt = kt_ref[0, kv * d:(kv + 1) * d, pl.ds(start, lk)]
        v = v_ref[0, pl.ds(start, lk), kv * d:(kv + 1) * d]
        kct = kct_ref[0, kv * d:(kv + 1) * d, :]
        vc = vc_ref[0, :, kv * d:(kv + 1) * d]
        for g in range(WA_GROUP):
            h = kv * WA_GROUP + g
            qg = q_ref[0, :, h * d:(h + 1) * d]
            s_loc = jnp.where(valid, jnp.dot(qg, kt, preferred_element_type=F32), MASK_VALUE)
            s_ctx = jnp.dot(qg, kct, preferred_element_type=F32)
            sink = sink_ref[:, h:h + 1]
            m = jnp.maximum(jnp.maximum(jnp.max(s_loc, axis=-1, keepdims=True),
                                        jnp.max(s_ctx, axis=-1, keepdims=True)), sink)
            e_loc = jnp.exp(s_loc - m)
            e_ctx = jnp.exp(s_ctx - m)
            denom = (jnp.sum(e_loc, axis=-1, keepdims=True) + jnp.sum(e_ctx, axis=-1, keepdims=True)
                     + jnp.exp(sink - m))
            o = (jnp.dot(e_loc.astype(BF16), v, preferred_element_type=F32)
                 + jnp.dot(e_ctx.astype(BF16), vc, preferred_element_type=F32))
            outs.append(o / denom)
    o_ref[0] = jnp.concatenate(outs, axis=1).astype(o_ref.dtype)


def _win_attn(qb, kbt, vb, kbct, vbc, sink, tq):
    b, s, _ = qb.shape
    c = kbct.shape[2]
    lk = tq + 2 * WINDOW
    assert s >= lk and tq % LANES == 0
    whole = lambda bi, i: (bi, 0, 0)
    return pl.pallas_call(
        functools.partial(_win_attn_kernel, tq=tq, lk=lk),
        grid=(b, s // tq),
        in_specs=[pl.BlockSpec((1, tq, QB_COLS), lambda bi, i: (bi, i, 0)),
                  pl.BlockSpec((1, KB_COLS, s), whole),
                  pl.BlockSpec((1, s, VB_COLS), whole),
                  pl.BlockSpec((1, KB_COLS, c), whole),
                  pl.BlockSpec((1, c, VB_COLS), whole),
                  pl.BlockSpec((1, WA_HEADS), lambda bi, i: (0, 0))],
        out_specs=pl.BlockSpec((1, tq, QB_COLS), lambda bi, i: (bi, i, 0)),
        out_shape=jax.ShapeDtypeStruct((b, s, QB_COLS), BF16),
        compiler_params=_cparams(("arbitrary", "arbitrary")),
        name="win_attn",
    )(qb, kbt, vb, kbct, vbc, sink)


def _outproj_router_kernel(ya_ref, yb_ref, woa_ref, wob_ref, x_ref, g1_ref, n2_ref, sh_ref, sc_ref,
                           wr_ref, br_ref, x1_ref, hx_ref, code_ref, gate_ref, cnt_ref, carry_ref):
    first = jnp.logical_and(pl.program_id(0) == 0, pl.program_id(1) == 0)

    @pl.when(first)
    def _():
        carry_ref[...] = jnp.zeros(carry_ref.shape, F32)

    y = (jnp.dot(ya_ref[0], woa_ref[...], preferred_element_type=F32)
         + jnp.dot(yb_ref[0], wob_ref[...], preferred_element_type=F32))
    x1 = x_ref[0] + g1_ref[0] * y
    x1_ref[0] = x1
    hx = _modulated_norm(x1, n2_ref[...], sh_ref[0], sc_ref[0])
    _store_row_tiles(hx_ref, hx)
    logits =jnp.dot(hx, wr_ref[...], precision=HIGHEST, preferred_element_type=F32) + br_ref[...]

    tm = logits.shape[0]
    lane_e = lax.broadcasted_iota(jnp.int32, (tm, N_EXPERTS), 1).astype(F32)
    work = logits
    tops, idxs, hots = [], [], []
    for _k in range(TOP_K):
        m = jnp.max(work, axis=-1, keepdims=True)
        idx = jnp.min(jnp.where(work == m, lane_e, float(N_EXPERTS)), axis=-1, keepdims=True)
        hot = lane_e == idx
        work = jnp.where(hot, -jnp.inf, work)
        tops.append(m)
        idxs.append(idx)
        hots.append(hot)
    es = [jnp.exp(t - tops[0]) for t in tops]
    den = es[0] + es[1] + es[2] + es[3]

    multi = jnp.zeros((tm, N_EXPERTS), F32)
    for hot in hots:
        multi = multi + hot.astype(F32)
    r_i = lax.broadcasted_iota(jnp.int32, (tm, tm), 0)
    c_i = lax.broadcasted_iota(jnp.int32, (tm, tm), 1)
    tri = (c_i <= r_i).astype(BF16)
    incl = jnp.dot(tri, multi.astype(BF16), preferred_element_type=F32)
    before = carry_ref[...] + incl - 1.0

    lane_k = lax.broadcasted_iota(jnp.int32, (tm, TOP_K), 1)
    code = jnp.zeros((tm, TOP_K), jnp.int32)
    gate = jnp.zeros((tm, TOP_K), F32)
    for k in range(TOP_K):
        rank = jnp.sum(jnp.where(hots[k], before, 0.0), axis=-1, keepdims=True)
        ck = idxs[k].astype(jnp.int32) * 65536 + rank.astype(jnp.int32)
        code = jnp.where(lane_k == k, ck, code)
        gate = jnp.where(lane_k == k, es[k] / den, gate)
    code_ref[...] = code
    gate_ref[...] = gate
    carry_ref[...] = carry_ref[...] + jnp.sum(multi, axis=0, keepdims=True)
    cnt_ref[...] = carry_ref[...]


def _outproj_router(ya, yb, woa, wob, x, g1, n2, sh2, sc2, w_r, b_r, tm):
    b, s, d = x.shape
    nb = s // tm
    row = lambda bi, i: (bi, i, 0)
    mod = lambda bi, i: (bi, 0, 0)
    fixed = lambda bi, i: (0, 0)
    tok = lambda bi, i: (bi * nb + i, 0)
    half = ya.shape[2]
    return pl.pallas_call(
        _outproj_router_kernel,
        grid=(b, nb),
        in_specs=[pl.BlockSpec((1, tm, half), row),
                  pl.BlockSpec((1, tm, half), row),
                  pl.BlockSpec((half, d), fixed),
                  pl.BlockSpec((half, d), fixed),
                  pl.BlockSpec((1, tm, d), row),
                  pl.BlockSpec((1, 1, d), mod),
                  pl.BlockSpec((1, d), fixed),
                  pl.BlockSpec((1, 1, d), mod),
                  pl.BlockSpec((1, 1, d), mod),
                  pl.BlockSpec((d, N_EXPERTS), fixed),
                  pl.BlockSpec((1, N_EXPERTS), fixed)],
        out_specs=[pl.BlockSpec((1, tm, d), row),
                   pl.BlockSpec((tm * ROW_TILE, LANES), tok),
                   pl.BlockSpec((tm, TOP_K), tok),
                   pl.BlockSpec((tm, TOP_K), tok),
                   pl.BlockSpec((1, N_EXPERTS), fixed)],
        out_shape=[jax.ShapeDtypeStruct((b, s, d), F32),
                   jax.ShapeDtypeStruct((b * s * ROW_TILE, LANES), F32),
                   jax.ShapeDtypeStruct((b * s, TOP_K), jnp.int32),
                   jax.ShapeDtypeStruct((b * s, TOP_K), F32),
                   jax.ShapeDtypeStruct((1, N_EXPERTS), F32)],
        scratch_shapes=[pltpu.VMEM((1, N_EXPERTS), F32)],
        compiler_params=_cparams(("arbitrary", "arbitrary")),
        name="outproj_router",
    )(ya, yb, woa, wob, x, g1, n2, sh2, sc2, w_r, b_r)


def _store_row_tiles(ref, val):
    n = val.shape[0]
    for c in range(ROW_TILE):
        ref[pl.ds(c, n, stride=ROW_TILE), :] = val[:, c * LANES:(c + 1) * LANES]


def _load_row_tiles(ref, n, first=0, stride=ROW_TILE):
    return [ref[pl.ds(first + c, n, stride=stride), :] for c in range(ROW_TILE)]


def _tile_rows(row):
    if isinstance(row, int):
        return pl.ds(row * ROW_TILE, ROW_TILE)
    return pl.ds(pl.multiple_of(row * ROW_TILE, ROW_TILE), ROW_TILE)


def _tile_copy(src_ref, src_row, dst_ref, dst_row, sem):
    return pltpu.make_async_copy(src_ref.at[_tile_rows(src_row)], dst_ref.at[_tile_rows(dst_row)], sem)


def _route_kernel(pbound_ref, cnt_ref, code_ref, inv_ref):
    n_rows = inv_ref.shape[0]
    chunk = code_ref.shape[0]
    step = pl.program_id(0)

    @pl.when(step == 0)
    def _():
        def clear(r, carry):
            inv_ref[r] = -1
            return carry

        def clear_expert(e, carry):
            lax.fori_loop(pbound_ref[e] + cnt_ref[e], pbound_ref[e + 1], clear, 0)
            return carry

        lax.fori_loop(0, N_EXPERTS, clear_expert, 0)
        lax.fori_loop(pbound_ref[N_EXPERTS], n_rows, clear, 0)

    def assign(j, carry):
        cd = code_ref[j]
        inv_ref[pbound_ref[cd >> 16] + (cd & 0xFFFF)] = step * chunk + j
        return carry

    lax.fori_loop(0, chunk, assign, 0, unroll=8)


def _route(pbound, counts_i, code_flat, n_rows, chunk):
    n_assign = code_flat.shape[0]
    return pl.pallas_call(
        _route_kernel,
        grid_spec=pltpu.PrefetchScalarGridSpec(
            num_scalar_prefetch=2,
            grid=(n_assign // chunk,),
            in_specs=[pl.BlockSpec((chunk,), lambda i, pb, ct: (i,), memory_space=pltpu.SMEM)],
            out_specs=pl.BlockSpec((n_rows,), lambda i, pb, ct: (0,), memory_space=pltpu.SMEM)),
        out_shape=jax.ShapeDtypeStruct((n_rows,), jnp.int32),
        compiler_params=_cparams(("arbitrary",)),
        name="moe_route",
    )(pbound, counts_i, code_flat)


def _expert_kernel(bexp_ref, run_ref, nxt_ref, nact_ref, inv_ref,
                   hx_hbm, wgu_hbm, wdn_hbm, bg_ref, bu_ref, bd_ref, y_hbm,
                   xbuf, obuf, wgu_f, wdn_f, wg_s, wu_s, wd_s, gsem, ssem, wsem, *, n_assign):
    blk = EXPERT_BLOCK
    i = pl.program_id(0)
    nact = nact_ref[0]
    slot = i % 2
    other = 1 - slot

    def gather_copy(block, r, to_slot):
        tok = jnp.maximum(inv_ref[block * blk + r], 0) >> 2
        return _tile_copy(hx_hbm, tok, xbuf.at[to_slot], r, gsem.at[to_slot])

    def scatter_copy(block, r, from_slot):
        a_idx = jnp.where(block >= 0, inv_ref[jnp.maximum(block, 0) * blk + r], -1)
        dst = jnp.where(a_idx >= 0, a_idx, n_assign + r)
        return _tile_copy(obuf.at[from_slot], r, y_hbm, dst, ssem.at[from_slot])

    def weight_copies(expert, w):
        return (pltpu.make_async_copy(wgu_hbm.at[expert], wgu_f.at[w], wsem.at[w, 0]),
                pltpu.make_async_copy(wdn_hbm.at[expert], wdn_f.at[w], wsem.at[w, 1]))

    def scatter_wait(s):
        pltpu.make_async_copy(obuf.at[s], y_hbm.at[pl.ds(0, blk * ROW_TILE)], ssem.at[s]).wait()

    @pl.when(i == 0)
    def _():
        for cp in weight_copies(bexp_ref[0], 0):
            cp.start()

        def first_rows(r, carry):
            gather_copy(0, r, 0).start()
            return carry
        lax.fori_loop(0, blk, first_rows, 0, unroll=8)
        obuf[1] = jnp.zeros(obuf.shape[1:], F32)

    @pl.when(i <= nact)
    def _():
        pltpu.make_async_copy(hx_hbm.at[pl.ds(0, blk * ROW_TILE)], xbuf.at[slot], gsem.at[slot]).wait()

        @pl.when(i >= 1)
        def _():
            scatter_wait(slot)

    @pl.when(i < nact)
    def _():
        changed = jnp.logical_or(i == 0, bexp_ref[i] != bexp_ref[jnp.maximum(i - 1, 0)])

        @pl.when(changed)
        def _():
            w = run_ref[i] % 2
            for cp in weight_copies(bexp_ref[i], w):
                cp.wait()
            half = MXU_COLS // 2
            src = lax.broadcasted_iota(jnp.int32, (MXU_COLS, MXU_COLS), 0)
            dst = lax.broadcasted_iota(jnp.int32, (MXU_COLS, MXU_COLS), 1)
            perm = (src == jnp.where(dst < half, 2 * dst, 2 * (dst - half) + 1)).astype(BF16)
            for k in range(wgu_f.shape[2] // MXU_COLS):
                wk = wgu_f[w, :, k * MXU_COLS:(k + 1) * MXU_COLS].astype(BF16)
                sep = jnp.dot(wk, perm, preferred_element_type=F32).astype(BF16)
                wg_s[:, k * half:(k + 1) * half] = sep[:, :half]
                wu_s[:, k * half:(k + 1) * half] = sep[:, half:]
            wd_s[...] = wdn_f[w].astype(BF16)

            @pl.when(nxt_ref[i] >= 0)
            def _():
                for cp in weight_copies(nxt_ref[i], 1 - w):
                    cp.start()

        xb = jnp.concatenate([c.astype(BF16) for c in _load_row_tiles(xbuf.at[slot], blk)], axis=1)
        for r in range(blk):
            scatter_copy(i - 1, r, other).start()
        for r in range(blk):
            gather_copy(i + 1, r, other).start()
        g = jnp.dot(xb, wg_s[...], preferred_element_type=F32) + bg_ref[0]
        u = jnp.dot(xb, wu_s[...], preferred_element_type=F32) + bu_ref[0]
        g = jnp.minimum(g, SWIGLU_LIMIT)
        u = jnp.clip(u, -SWIGLU_LIMIT, SWIGLU_LIMIT)
        a = g * (1.0 / (1.0 + jnp.exp(-SWIGLU_ALPHA * g))) * (u + 1.0)
        out = jnp.dot(a.astype(BF16), wd_s[...], preferred_element_type=F32) + bd_ref[0]
        _store_row_tiles(obuf.at[slot], out)

    @pl.when(i == nact)
    def _():
        def last_rows(r, carry):
            scatter_copy(i - 1, r, other).start()
            return carry
        lax.fori_loop(0, blk, last_rows, 0, unroll=8)
        scatter_wait(other)


def _experts(block_exp, run_id, next_exp, n_active, inv, hx_tiles, w_gu, w_dn, bg, bu, bd, n_assign):
    n_rows = inv.shape[0]
    e, d, f2 = w_gu.shape
    f = f2 // 2
    nblk = n_rows // EXPERT_BLOCK
    bsel = lambda i, be, ru, nx, na, iv: (be[i], 0, 0)
    anyspace = pl.BlockSpec(memory_space=pl.ANY)
    return pl.pallas_call(
        functools.partial(_expert_kernel, n_assign=n_assign),
        grid_spec=pltpu.PrefetchScalarGridSpec(
            num_scalar_prefetch=5,
            grid=(nblk,),
            in_specs=[anyspace, anyspace, anyspace,
                      pl.BlockSpec((1, 1, f), bsel),
                      pl.BlockSpec((1, 1, f), bsel),
                      pl.BlockSpec((1, 1, d), bsel)],
            out_specs=anyspace,
            scratch_shapes=[pltpu.VMEM((2, EXPERT_BLOCK * ROW_TILE, LANES), F32),
                            pltpu.VMEM((2, EXPERT_BLOCK * ROW_TILE, LANES), F32),
                            pltpu.VMEM((2, d, f2), F32),
                            pltpu.VMEM((2, f, d), F32),
                            pltpu.VMEM((d, f), BF16), pltpu.VMEM((d, f), BF16), pltpu.VMEM((f, d), BF16),
                            pltpu.SemaphoreType.DMA((2,)),
                            pltpu.SemaphoreType.DMA((2,)),
                            pltpu.SemaphoreType.DMA((2, 2))]),
        out_shape=jax.ShapeDtypeStruct(((n_assign + EXPERT_BLOCK) * ROW_TILE, LANES), F32),
        compiler_params=_cparams(("arbitrary",)),
        name="moe_experts",
    )(block_exp, run_id, next_exp, n_active, inv, hx_tiles, w_gu, w_dn, bg, bu, bd)


def _combine_kernel(y_ref, gate_ref, x1_ref, g2_ref, fg_ref, o_ref):
    tm = gate_ref.shape[0]
    gate = gate_ref[...]
    parts = None
    for k in range(TOP_K):
        gk = gate[:, k:k + 1]
        rows = _load_row_tiles(y_ref, tm, first=k * ROW_TILE, stride=TOP_K * ROW_TILE)
        parts = [gk * r for r in rows] if parts is None else [p + gk * r for p, r in zip(parts, rows)]
    y = jnp.concatenate(parts, axis=1)
    xo = x1_ref[...] + g2_ref[0] * y
    ms = jnp.mean(xo * xo, axis=-1, keepdims=True)
    o_ref[...] = xo * lax.rsqrt(ms + NORM_EPS) * fg_ref[...]


def _combine(y_tiles, gates, x1, g2, final_g, tm):
    b, s, d = x1.shape
    nb = s // tm
    out = pl.pallas_call(
        _combine_kernel,
        grid=(b * nb,),
        in_specs=[pl.BlockSpec((tm * TOP_K * ROW_TILE, LANES), lambda n: (n, 0)),
                  pl.BlockSpec((tm, TOP_K), lambda n: (n, 0)),
                  pl.BlockSpec((tm, d), lambda n: (n, 0)),
                  pl.BlockSpec((1, 1, d), lambda n: (n // nb, 0, 0)),
                  pl.BlockSpec((1, d), lambda n: (0, 0))],
        out_specs=pl.BlockSpec((tm, d), lambda n: (n, 0)),
        out_shape=jax.ShapeDtypeStruct((b * s, d), F32),
        compiler_params=_cparams(("arbitrary",)),
        name="moe_combine",
    )(y_tiles, gates, x1.reshape(b * s, d), g2, final_g)
    return out.reshape(b, s, d)


def _pick(n, prefs):
    for p in prefs:
        if n % p == 0:
            return p
    raise ValueError(f"no tile for {n}")


def kernel(x, c, ctx, c_ctx, w_mod, b_mod, norm1_g, w_in, lam_q1, lam_k1, lam_q2, lam_k2, subln_g,
           sink, w_out, norm2_g, w_router, b_router, w_gate_up, b_gate_up, w_down, b_down, final_g):
    b, s, d = x.shape
    c_len = ctx.shape[1]
    assert w_mod.shape[0] == 1, "single-layer block"
    assert d == ROW_TILE * LANES, "MoE row DMAs move one (8, 128) f32 tile per token row"
    t = b * s

    pad = (-(b + 1)) % 8
    cvecs = jnp.concatenate([c, c_ctx[None, :], jnp.zeros((pad, d), F32)], axis=0)
    mod = _adaln(cvecs, w_mod[0], b_mod[0])
    sh1, sc1, g1, sh2, sc2, g2 = [mod[:b, k * d:(k + 1) * d].reshape(b, 1, d) for k in range(6)]
    csh1 = mod[b:b + 1, 0:d].reshape(1, 1, d)
    csc1 = mod[b:b + 1, d:2 * d].reshape(1, 1, d)

    w_in_bf = w_in[0].astype(BF16)
    cos, sin = _rope_tables(s)
    n1 = norm1_g[0].reshape(1, d)
    qat, ka, vat, qb, kbt, vb = _inproj_latent(x, n1, sh1, sc1, w_in_bf, cos, sin, _pick(s, (512, 256, 128)))
    w_ctx_bf = jnp.concatenate([w_in_bf[:, O_KA:O_QB], w_in_bf[:, O_KB:IN_COLS]], axis=1)
    kac, vact, kbct, vbc = _inproj_ctx(ctx, n1, csh1, csc1, w_ctx_bf)

    ka_all = jnp.concatenate([ka, kac], axis=1)
    vat_all = jnp.concatenate([vat, vact], axis=2)
    sk = s + c_len
    ya = _diff_attn(qat, ka_all, vat_all,
                    lam_q1[0].reshape(1, -1), lam_k1[0].reshape(1, -1),
                    lam_q2[0].reshape(1, -1), lam_k2[0].reshape(1, -1),
                    subln_g[0].reshape(1, -1),
                    _pick(s, (1024, 512, 256, 128)), _pick(sk, (768, 512, 384, 256, 128)))

    yb = _win_attn(qb, kbt, vb, kbct, vbc, sink[0].reshape(1, -1), _pick(s, (256, 128)))

    w_out_bf = w_out[0].astype(BF16)
    x1, hx2, code, gates, counts = _outproj_router(
        ya, yb, w_out_bf[:QA_COLS], w_out_bf[QA_COLS:], x, g1, norm2_g[0].reshape(1, d), sh2, sc2,
        w_router[0], b_router[0].reshape(1, -1), _pick(s, (256, 128)))

    n_assign = t * TOP_K
    n_rows = n_assign + N_EXPERTS * EXPERT_BLOCK
    counts_i = counts.reshape(-1).astype(jnp.int32)
    padded = ((counts_i + EXPERT_BLOCK - 1) // EXPERT_BLOCK) * EXPERT_BLOCK
    pend = jnp.cumsum(padded).astype(jnp.int32)
    pbound = jnp.concatenate([jnp.zeros((1,), jnp.int32), pend])
    nblk = n_rows // EXPERT_BLOCK
    block_start = jnp.arange(nblk, dtype=jnp.int32) * EXPERT_BLOCK
    block_exp = jnp.minimum(jnp.sum((pend[None, :] <= block_start[:, None]).astype(jnp.int32), axis=1),
                            N_EXPERTS - 1)
    n_active = pend[-1:] // EXPERT_BLOCK
    changed = jnp.concatenate([jnp.ones((1,), jnp.int32), (block_exp[1:] != block_exp[:-1]).astype(jnp.int32)])
    run_id = jnp.cumsum(changed).astype(jnp.int32) - 1
    e_ids = jnp.arange(N_EXPERTS, dtype=jnp.int32)
    later_nonempty = (e_ids[None, :] > e_ids[:, None]) & (padded[None, :] > 0)
    next_of_expert = jnp.min(jnp.where(later_nonempty, e_ids[None, :], N_EXPERTS), axis=1)
    next_of_expert = jnp.where(next_of_expert < N_EXPERTS, next_of_expert, -1).astype(jnp.int32)
    next_exp = next_of_expert[block_exp]

    inv = _route(pbound, counts_i, code.reshape(-1), n_rows, _pick(n_assign, (4096, 1024)))

    f = w_down.shape[2]
    bg = b_gate_up[0, :, 0::2].reshape(N_EXPERTS, 1, f)
    bu = b_gate_up[0, :, 1::2].reshape(N_EXPERTS, 1, f)
    bd = b_down[0].reshape(N_EXPERTS, 1, d)
    y_tiles = _experts(block_exp, run_id, next_exp, n_active, inv, hx2, w_gate_up[0], w_down[0],
                       bg, bu, bd, n_assign)

    return _combine(y_tiles, gates, x1, g2, final_g.reshape(1, d), _pick(s, (256,)))
```

```python
import functools
import math

import jax
import jax.numpy as jnp
from jax import lax
from jax.experimental import pallas as pl
from jax.experimental.pallas import tpu as pltpu

F32 = jnp.float32
BF16 = jnp.bfloat16
HIGHEST = lax.Precision.HIGHEST

GRID_W = 64
NORM_EPS = 1e-6
ROPE_BASE = 10000.0
MASK_VALUE = -1e30
DA_HEADS = 4
HEAD_DIM = 64
WA_HEADS = 8
WA_KV_HEADS = 2
WA_GROUP = WA_HEADS // WA_KV_HEADS
WINDOW = 128
N_EXPERTS = 32
TOP_K = 4
SWIGLU_LIMIT = 7.0
SWIGLU_ALPHA = 1.702
EXPERT_BLOCK = 256
LAM_INIT = 0.8 - 0.6 * math.exp(-0.3 * 0)
LOG2E = math.log2(math.e)

QA_COLS = DA_HEADS * 2 * HEAD_DIM
KA_COLS = QA_COLS
VA_COLS = QA_COLS
QB_COLS = WA_HEADS * HEAD_DIM
KB_COLS = WA_KV_HEADS * HEAD_DIM
VB_COLS = KB_COLS
O_QA = 0
O_KA = O_QA + QA_COLS
O_VA = O_KA + KA_COLS
O_QB = O_VA + VA_COLS
O_KB = O_QB + QB_COLS
O_VB = O_KB + KB_COLS
IN_COLS = O_VB + VB_COLS

LANES = 128
MXU_COLS = 256
DMA_UNROLL = 4
ONES_ROWS = 16
VMEM_LIMIT = 56 * 1024 * 1024


def _cparams(sem):
    return pltpu.CompilerParams(dimension_semantics=sem, vmem_limit_bytes=VMEM_LIMIT)


def _adaln_kernel(c_ref, w_ref, b_ref, o_ref):
    cv = c_ref[...]
    s = cv * (1.0 / (1.0 + jnp.exp(-cv)))
    o_ref[...] = jnp.dot(s, w_ref[...], precision=HIGHEST, preferred_element_type=F32) + b_ref[...]


def _adaln(cvecs, w_mod, b_mod):
    rows, d = cvecs.shape
    n = w_mod.shape[1]
    tn = 1024
    return pl.pallas_call(
        _adaln_kernel,
        grid=(n // tn,),
        in_specs=[pl.BlockSpec((rows, d), lambda j: (0, 0)),
                  pl.BlockSpec((d, tn), lambda j: (0, j)),
                  pl.BlockSpec((1, tn), lambda j: (0, j))],
        out_specs=pl.BlockSpec((rows, tn), lambda j: (0, j)),
        out_shape=jax.ShapeDtypeStruct((rows, n), F32),
        compiler_params=_cparams(("arbitrary",)),
        name="adaln",
    )(cvecs, w_mod, b_mod.reshape(1, n))


def _rope_section(sec, cos, sin):
    tm = sec.shape[0]
    lane = lax.broadcasted_iota(jnp.int32, (tm, LANES), 1)
    low = (lane % 32) < 16
    outs = []
    for j in range(sec.shape[1] // LANES):
        c = sec[:, j * LANES:(j + 1) * LANES]
        partner = jnp.where(low, pltpu.roll(c, LANES - 16, 1), pltpu.roll(c, 16, 1))
        outs.append(c * cos + partner * sin)
    return jnp.concatenate(outs, axis=1)


def _modulated_norm(x, g, shift, scale):
    ms = jnp.mean(x * x, axis=-1, keepdims=True)
    return (x * lax.rsqrt(ms + NORM_EPS) * g) * (1.0 + scale) + shift


def _inproj_latent_kernel(x_ref, g_ref, sh_ref, sc_ref, w_ref, cos_ref, sin_ref,
                          qat_ref, ka_ref, vat_ref, qbt_ref, kb_ref, vbt_ref):
    h = _modulated_norm(x_ref[0], g_ref[...], sh_ref[0], sc_ref[0])
    p = jnp.dot(h.astype(BF16), w_ref[...], preferred_element_type=F32)
    cos = cos_ref[...]
    sin = sin_ref[...]
    qscale = HEAD_DIM ** -0.5 * LOG2E
    qat_ref[0] = (_rope_section(p[:, O_QA:O_KA], cos, sin) * qscale).T.astype(BF16)
    ka_ref[0] = _rope_section(p[:, O_KA:O_VA], cos, sin).astype(BF16)
    vat_ref[0] = p[:, O_VA:O_QB].T.astype(BF16)
    qbt_ref[0] = (_rope_section(p[:, O_QB:O_KB], cos, sin) * qscale).T.astype(BF16)
    kb_ref[0] = _rope_section(p[:, O_KB:O_VB], cos, sin).astype(BF16)
    vbt_ref[0] = p[:, O_VB:IN_COLS].T.astype(BF16)


def _inproj_ctx_kernel(x_ref, g_ref, sh_ref, sc_ref, w_ref, ka_ref, vat_ref, kb_ref, vbt_ref):
    h = _modulated_norm(x_ref[0], g_ref[...], sh_ref[0], sc_ref[0])
    p = jnp.dot(h.astype(BF16), w_ref[...], preferred_element_type=F32)
    ka_ref[0] = p[:, 0:KA_COLS].astype(BF16)
    vat_ref[0] = p[:, KA_COLS:KA_COLS + VA_COLS].T.astype(BF16)
    kb_ref[0] = p[:, KA_COLS + VA_COLS:KA_COLS + VA_COLS + KB_COLS].astype(BF16)
    vbt_ref[0] = p[:, KA_COLS + VA_COLS + KB_COLS:].T.astype(BF16)


def _rope_tables(n_tok):
    rows = (jnp.arange(n_tok) // GRID_W).astype(F32)
    cols = (jnp.arange(n_tok) % GRID_W).astype(F32)
    nf = HEAD_DIM // 4
    inv = ROPE_BASE ** (-jnp.arange(nf, dtype=F32) / nf)
    ar = rows[:, None] * inv
    ac = cols[:, None] * inv
    cos = jnp.concatenate([jnp.cos(ar), jnp.cos(ar), jnp.cos(ac), jnp.cos(ac)], axis=1)
    sin = jnp.concatenate([-jnp.sin(ar), jnp.sin(ar), -jnp.sin(ac), jnp.sin(ac)], axis=1)
    return jnp.tile(cos, (1, LANES // HEAD_DIM)), jnp.tile(sin, (1, LANES // HEAD_DIM))


def _inproj_latent(x, g, shift, scale, w_bf16, cos, sin, tm):
    b, s, d = x.shape
    row = lambda bi, i: (bi, i, 0)
    colt = lambda bi, i: (bi, 0, i)
    mod = lambda bi, i: (bi, 0, 0)
    fixed = lambda bi, i: (0, 0)
    return pl.pallas_call(
        _inproj_latent_kernel,
        grid=(b, s // tm),
        in_specs=[pl.BlockSpec((1, tm, d), row),
                  pl.BlockSpec((1, d), fixed),
                  pl.BlockSpec((1, 1, d), mod),
                  pl.BlockSpec((1, 1, d), mod),
                  pl.BlockSpec((d, IN_COLS), fixed),
                  pl.BlockSpec((tm, LANES), lambda bi, i: (i, 0)),
                  pl.BlockSpec((tm, LANES), lambda bi, i: (i, 0))],
        out_specs=[pl.BlockSpec((1, QA_COLS, tm), colt),
                   pl.BlockSpec((1, tm, KA_COLS), row),
                   pl.BlockSpec((1, VA_COLS, tm), colt),
                   pl.BlockSpec((1, QB_COLS, tm), colt),
                   pl.BlockSpec((1, tm, KB_COLS), row),
                   pl.BlockSpec((1, VB_COLS, tm), colt)],
        out_shape=[jax.ShapeDtypeStruct((b, QA_COLS, s), BF16),
                   jax.ShapeDtypeStruct((b, s, KA_COLS), BF16),
                   jax.ShapeDtypeStruct((b, VA_COLS, s), BF16),
                   jax.ShapeDtypeStruct((b, QB_COLS, s), BF16),
                   jax.ShapeDtypeStruct((b, s, KB_COLS), BF16),
                   jax.ShapeDtypeStruct((b, VB_COLS, s), BF16)],
        compiler_params=_cparams(("arbitrary", "arbitrary")),
        name="inproj_latent",
    )(x, g, shift, scale, w_bf16, cos, sin)


def _inproj_ctx(ctx, g, shift, scale, w_ctx_bf16):
    b, c, d = ctx.shape
    n = w_ctx_bf16.shape[1]
    whole = lambda bi: (bi, 0, 0)
    mod = lambda bi: (0, 0, 0)
    fixed = lambda bi: (0, 0)
    return pl.pallas_call(
        _inproj_ctx_kernel,
        grid=(b,),
        in_specs=[pl.BlockSpec((1, c, d), whole),
                  pl.BlockSpec((1, d), fixed),
                  pl.BlockSpec((1, 1, d), mod),
                  pl.BlockSpec((1, 1, d), mod),
                  pl.BlockSpec((d, n), fixed)],
        out_specs=[pl.BlockSpec((1, c, KA_COLS), whole),
                   pl.BlockSpec((1, VA_COLS, c), whole),
                   pl.BlockSpec((1, c, KB_COLS), whole),
                   pl.BlockSpec((1, VB_COLS, c), whole)],
        out_shape=[jax.ShapeDtypeStruct((b, c, KA_COLS), BF16),
                   jax.ShapeDtypeStruct((b, VA_COLS, c), BF16),
                   jax.ShapeDtypeStruct((b, c, KB_COLS), BF16),
                   jax.ShapeDtypeStruct((b, VB_COLS, c), BF16)],
        compiler_params=_cparams(("arbitrary",)),
        name="inproj_ctx",
    )(ctx, g, shift, scale, w_ctx_bf16)


def _diff_attn_kernel(qt_ref, k_ref, vt_ref, lq1_ref, lk1_ref, lq2_ref, lk2_ref, sg_ref, o_ref,
                      m_ref, acc_ref, s_ref, *, tk):
    d = HEAD_DIM
    hw = 2 * d
    qt = qt_ref[0]
    row = lax.broadcasted_iota(jnp.int32, qt.shape, 0)
    zero = jnp.zeros_like(qt)
    rhs = (jnp.where(row < d, qt, zero), jnp.where(row >= d, qt, zero))
    n_chunks = k_ref.shape[1] // tk
    m_ref[...] = jnp.full(m_ref.shape, -jnp.inf, F32)
    acc_ref[...] = jnp.zeros(acc_ref.shape, F32)
    ones = jnp.ones((ONES_ROWS, tk), BF16)

    def scores(c, j):
        off = pl.multiple_of(j * tk, tk)
        s_ref[c] = jnp.dot(k_ref[0, pl.ds(off, tk), :], rhs[c], preferred_element_type=F32)

    def accumulate(c, j):
        off = pl.multiple_of(j * tk, tk)
        vt = jnp.concatenate([vt_ref[0, :, pl.ds(off, tk)], ones], axis=0)
        st = s_ref[c]
        m_old = m_ref[c]
        m_new = jnp.maximum(m_old, jnp.max(st, axis=0, keepdims=True))
        alpha = jnp.exp2(m_old - m_new)
        p = jnp.exp2(st - m_new).astype(BF16)
        acc_ref[c] = alpha * acc_ref[c] + jnp.dot(vt, p, preferred_element_type=F32)
        m_ref[c] = m_new

    scores(0, 0)

    def chunk(j, carry):
        scores(1, j)
        accumulate(0, j)
        scores(0, j + 1)
        accumulate(1, j)
        return carry

    lax.fori_loop(0, n_chunks - 1, chunk, 0)
    scores(1, n_chunks - 1)
    accumulate(0, n_chunks - 1)
    accumulate(1, n_chunks - 1)

    lam = (jnp.exp(jnp.sum(lq1_ref[...] * lk1_ref[...], axis=-1, keepdims=True))
           - jnp.exp(jnp.sum(lq2_ref[...] * lk2_ref[...], axis=-1, keepdims=True)) + LAM_INIT)
    a1 = acc_ref[0]
    a2 = acc_ref[1]
    ot = a1[:hw] / a1[hw:hw + 1] - lam * (a2[:hw] / a2[hw:hw + 1])
    ms = jnp.mean(ot * ot, axis=0, keepdims=True)
    ot = ot * lax.rsqrt(ms + NORM_EPS)
    o_ref[0] = (ot.T * (sg_ref[...] * (1.0 - LAM_INIT))).astype(o_ref.dtype)


def _diff_attn(qat, ka, vat, lq1, lk1, lq2, lk2, subln_g, tq, tk):
    b, _, s = qat.shape
    sk = ka.shape[1]
    hw = 2 * HEAD_DIM
    vec = lambda bi, h, i: (0, 0)
    return pl.pallas_call(
        functools.partial(_diff_attn_kernel, tk=tk),
        grid=(b, DA_HEADS, s // tq),
        in_specs=[pl.BlockSpec((1, hw, tq), lambda bi, h, i: (bi, h, i)),
                  pl.BlockSpec((1, sk, hw), lambda bi, h, i: (bi, 0, h)),
                  pl.BlockSpec((1, hw, sk), lambda bi, h, i: (bi, h, 0)),
                  pl.BlockSpec((1, HEAD_DIM), vec),
                  pl.BlockSpec((1, HEAD_DIM), vec),
                  pl.BlockSpec((1, HEAD_DIM), vec),
                  pl.BlockSpec((1, HEAD_DIM), vec),
                  pl.BlockSpec((1, hw), vec)],
        out_specs=pl.BlockSpec((1, tq, hw), lambda bi, h, i: (bi, i, h)),
        out_shape=jax.ShapeDtypeStruct((b, s, DA_HEADS * hw), BF16),
        scratch_shapes=[pltpu.VMEM((2, 1, tq), F32),
                        pltpu.VMEM((2, hw + ONES_ROWS, tq), F32),
                        pltpu.VMEM((2, tk, tq), F32)],
        compiler_params=_cparams(("arbitrary", "arbitrary", "arbitrary")),
        name="diff_attn",
    )(qat, ka, vat, lq1, lk1, lq2, lk2, subln_g)


def _win_attn_kernel(qt_ref, k_ref, vt_ref, kc_ref, vct_ref, sink_ref, o_ref, *, tq, lk):
    d = HEAD_DIM
    grp = WA_GROUP
    s_len = k_ref.shape[1]
    c_len = kc_ref.shape[1]
    nk = lk + c_len
    i = pl.program_id(1)
    q0 = i * tq
    start = pl.multiple_of(jnp.clip(q0 - WINDOW, 0, s_len - lk), LANES)
    keys = jnp.concatenate([k_ref[0, pl.ds(start, lk), :], kc_ref[0]], axis=0)
    kpos = start + lax.broadcasted_iota(jnp.int32, (nk, tq), 0)
    qpos = q0 + lax.broadcasted_iota(jnp.int32, (nk, tq), 1)
    visible = jnp.logical_or(kpos >= start + lk, jnp.abs(kpos - qpos) <= WINDOW)
    visible = jnp.concatenate([visible] * grp, axis=1)
    ones = jnp.ones((ONES_ROWS, nk), BF16)
    qt = qt_ref[0]
    blank = jnp.zeros((d, grp * tq), BF16)
    outs = []
    for kv in range(WA_KV_HEADS):
        heads = range(kv * grp, (kv + 1) * grp)
        qcat = jnp.concatenate([qt[h * d:(h + 1) * d, :] for h in heads], axis=1)
        rhs = jnp.concatenate([qcat if j == kv else blank for j in range(WA_KV_HEADS)], axis=0)
        st = jnp.dot(keys, rhs, preferred_element_type=F32)
        st = jnp.where(visible, st, MASK_VALUE)
        sink = jnp.concatenate([jnp.broadcast_to(sink_ref[:, h:h + 1] * LOG2E, (1, tq)) for h in heads], axis=1)
        m = jnp.maximum(jnp.max(st, axis=0, keepdims=True), sink)
        p = jnp.exp2(st - m).astype(BF16)
        vt = jnp.concatenate([vt_ref[0, kv * d:(kv + 1) * d, pl.ds(start, lk)],
                              vct_ref[0, kv * d:(kv + 1) * d, :]], axis=1)
        acc = jnp.dot(jnp.concatenate([vt, ones], axis=0), p, preferred_element_type=F32)
        o = acc[:d] / (acc[d:d + 1] + jnp.exp2(sink - m))
        outs.extend(o[:, g * tq:(g + 1) * tq] for g in range(grp))
    o_ref[0] = jnp.concatenate(outs, axis=0).T.astype(o_ref.dtype)


def _win_attn(qbt, kb, vbt, kbc, vbct, sink, tq):
    b, _, s = qbt.shape
    c = kbc.shape[1]
    lk = tq + 2 * WINDOW
    assert s >= lk and tq % LANES == 0
    whole = lambda bi, i: (bi, 0, 0)
    return pl.pallas_call(
        functools.partial(_win_attn_kernel, tq=tq, lk=lk),
        grid=(b, s // tq),
        in_specs=[pl.BlockSpec((1, QB_COLS, tq), lambda bi, i: (bi, 0, i)),
                  pl.BlockSpec((1, s, KB_COLS), whole),
                  pl.BlockSpec((1, VB_COLS, s), whole),
                  pl.BlockSpec((1, c, KB_COLS), whole),
                  pl.BlockSpec((1, VB_COLS, c), whole),
                  pl.BlockSpec((1, WA_HEADS), lambda bi, i: (0, 0))],
        out_specs=pl.BlockSpec((1, tq, QB_COLS), lambda bi, i: (bi, i, 0)),
        out_shape=jax.ShapeDtypeStruct((b, s, QB_COLS), BF16),
        compiler_params=_cparams(("arbitrary", "arbitrary")),
        name="win_attn",
    )(qbt, kb, vbt, kbc, vbct, sink)


def _outproj_router_kernel(ya_ref, yb_ref, woa_ref, wob_ref, x_ref, g1_ref, n2_ref, sh_ref, sc_ref,
                           wr_ref, br_ref, x1_ref, hx_ref, code_ref, gate_ref, cnt_ref, carry_ref):
    first = jnp.logical_and(pl.program_id(0) == 0, pl.program_id(1) == 0)

    @pl.when(first)
    def _():
        carry_ref[...] = jnp.zeros(carry_ref.shape, F32)

    y = (jnp.dot(ya_ref[0], woa_ref[...], preferred_element_type=F32)
         + jnp.dot(yb_ref[0], wob_ref[...], preferred_element_type=F32))
    x1 = x_ref[0] + g1_ref[0] * y
    x1_ref[0] = x1
    hx = _modulated_norm(x1, n2_ref[...], sh_ref[0], sc_ref[0])
    hx_ref[0] = hx
    hx_hi = hx.astype(BF16)
    hx_lo = (hx - hx_hi.astype(F32)).astype(BF16)
    wr = wr_ref[...]
    part = jnp.dot(hx_hi, wr, preferred_element_type=F32)
    logits = (part[:, :N_EXPERTS] + part[:, N_EXPERTS:]
              + jnp.dot(hx_lo, wr[:, :N_EXPERTS], preferred_element_type=F32) + br_ref[...])

    tm = logits.shape[0]
    lane_e = lax.broadcasted_iota(jnp.int32, (tm, N_EXPERTS), 1).astype(F32)
    work = logits
    tops, idxs, hots = [], [], []
    for _k in range(TOP_K):
        m = jnp.max(work, axis=-1, keepdims=True)
        idx = jnp.min(jnp.where(work == m, lane_e, float(N_EXPERTS)), axis=-1, keepdims=True)
        hot = lane_e == idx
        work = jnp.where(hot, -jnp.inf, work)
        tops.append(m)
        idxs.append(idx)
        hots.append(hot)
    es = [jnp.exp(t - tops[0]) for t in tops]
    den = es[0] + es[1] + es[2] + es[3]

    multi = jnp.zeros((tm, N_EXPERTS), F32)
    for hot in hots:
        multi = multi + hot.astype(F32)
    r_i = lax.broadcasted_iota(jnp.int32, (tm, tm), 0)
    c_i = lax.broadcasted_iota(jnp.int32, (tm, tm), 1)
    tri = (c_i <= r_i).astype(BF16)
    incl = jnp.dot(tri, multi.astype(BF16), preferred_element_type=F32)
    before = carry_ref[...] + incl - 1.0

    lane_k = lax.broadcasted_iota(jnp.int32, (tm, TOP_K), 1)
    code = jnp.zeros((tm, TOP_K), jnp.int32)
    gate = jnp.zeros((tm, TOP_K), F32)
    for k in range(TOP_K):
        rank = jnp.sum(jnp.where(hots[k], before, 0.0), axis=-1, keepdims=True)
        ck = idxs[k].astype(jnp.int32) * 65536 + rank.astype(jnp.int32)
        code = jnp.where(lane_k == k, ck, code)
        gate = jnp.where(lane_k == k, es[k] / den, gate)
    code_ref[...] = code
    gate_ref[...] = gate
    carry_ref[...] = carry_ref[...] + jnp.sum(multi, axis=0, keepdims=True)
    cnt_ref[...] = carry_ref[...]


def _outproj_router(ya, yb, woa, wob, x, g1, n2, sh2, sc2, w_r_parts, b_r, tm):
    b, s, d = x.shape
    nb = s // tm
    row = lambda bi, i: (bi, i, 0)
    mod = lambda bi, i: (bi, 0, 0)
    fixed = lambda bi, i: (0, 0)
    tok = lambda bi, i: (bi * nb + i, 0)
    half = ya.shape[2]
    return pl.pallas_call(
        _outproj_router_kernel,
        grid=(b, nb),
        in_specs=[pl.BlockSpec((1, tm, half), row),
                  pl.BlockSpec((1, tm, half), row),
                  pl.BlockSpec((half, d), fixed),
                  pl.BlockSpec((half, d), fixed),
                  pl.BlockSpec((1, tm, d), row),
                  pl.BlockSpec((1, 1, d), mod),
                  pl.BlockSpec((1, d), fixed),
                  pl.BlockSpec((1, 1, d), mod),
                  pl.BlockSpec((1, 1, d), mod),
                  pl.BlockSpec((d, 2 * N_EXPERTS), fixed),
                  pl.BlockSpec((1, N_EXPERTS), fixed)],
        out_specs=[pl.BlockSpec((1, tm, d), row),
                   pl.BlockSpec((1, tm, d), row),
                   pl.BlockSpec((tm, TOP_K), tok),
                   pl.BlockSpec((tm, TOP_K), tok),
                   pl.BlockSpec((1, N_EXPERTS), fixed)],
        out_shape=[jax.ShapeDtypeStruct((b, s, d), F32),
                   jax.ShapeDtypeStruct((b, s, d), F32),
                   jax.ShapeDtypeStruct((b * s, TOP_K), jnp.int32),
                   jax.ShapeDtypeStruct((b * s, TOP_K), F32),
                   jax.ShapeDtypeStruct((1, N_EXPERTS), F32)],
        scratch_shapes=[pltpu.VMEM((1, N_EXPERTS), F32)],
        compiler_params=_cparams(("arbitrary", "arbitrary")),
        name="outproj_router",
    )(ya, yb, woa, wob, x, g1, n2, sh2, sc2, w_r_parts, b_r)


def _row_copy(src_hbm, src_row, dst_ref, dst_row, sem):
    return pltpu.make_async_copy(src_hbm.at[pl.ds(src_row, 1)], dst_ref.at[pl.ds(dst_row, 1)], sem)


def _dispatch_kernel(pstart_ref, code_ref, hx_ref, init_hbm, xs_hbm, dest_ref, sem, *, chunk):
    del init_hbm

    def issue(tok, carry):
        for k in range(TOP_K):
            j = tok * TOP_K + k
            cd = code_ref[j]
            dst = pstart_ref[cd >> 16] + (cd & 0xFFFF)
            dest_ref[j] = dst
            _row_copy(hx_ref, tok, xs_hbm, dst, sem).start()
        return carry

    lax.fori_loop(0, chunk // TOP_K, issue, 0, unroll=DMA_UNROLL)
    pltpu.make_async_copy(xs_hbm.at[pl.ds(0, chunk)], xs_hbm.at[pl.ds(0, chunk)], sem).wait()


def _dispatch(pstart, code_flat, hx2d, xs_init, chunk):
    n_assign = code_flat.shape[0]
    n_rows, d = xs_init.shape
    return pl.pallas_call(
        functools.partial(_dispatch_kernel, chunk=chunk),
        grid_spec=pltpu.PrefetchScalarGridSpec(
            num_scalar_prefetch=1,
            grid=(n_assign // chunk,),
            in_specs=[pl.BlockSpec((chunk,), lambda i, ps: (i,), memory_space=pltpu.SMEM),
                      pl.BlockSpec((chunk // TOP_K, d), lambda i, ps: (i, 0)),
                      pl.BlockSpec(memory_space=pl.ANY)],
            out_specs=[pl.BlockSpec(memory_space=pl.ANY),
                       pl.BlockSpec((chunk,), lambda i, ps: (i,), memory_space=pltpu.SMEM)],
            scratch_shapes=[pltpu.SemaphoreType.DMA(())]),
        out_shape=[jax.ShapeDtypeStruct((n_rows, d), F32),
                   jax.ShapeDtypeStruct((n_assign,), jnp.int32)],
        input_output_aliases={3: 0},
        compiler_params=_cparams(("arbitrary",)),
        name="moe_dispatch",
    )(pstart, code_flat, hx2d, xs_init)


def _expert_kernel(bexp_ref, run_ref, nxt_ref, nact_ref, xs_ref, wgu_hbm, wdn_hbm, bg_ref, bu_ref, bd_ref,
                   o_ref, wgu_f, wdn_f, wg_s, wu_s, wd_s, wsem):
    i = pl.program_id(0)
    nact = nact_ref[0]

    def weight_copies(expert, w):
        return (pltpu.make_async_copy(wgu_hbm.at[expert], wgu_f.at[w], wsem.at[w, 0]),
                pltpu.make_async_copy(wdn_hbm.at[expert], wdn_f.at[w], wsem.at[w, 1]))

    @pl.when(i == 0)
    def _():
        for cp in weight_copies(bexp_ref[0], 0):
            cp.start()

    @pl.when(i < nact)
    def _():
        changed = jnp.logical_or(i == 0, bexp_ref[i] != bexp_ref[jnp.maximum(i - 1, 0)])

        @pl.when(changed)
        def _():
            w = run_ref[i] % 2
            for cp in weight_copies(bexp_ref[i], w):
                cp.wait()
            half = MXU_COLS // 2
            src = lax.broadcasted_iota(jnp.int32, (MXU_COLS, MXU_COLS), 0)
            dst = lax.broadcasted_iota(jnp.int32, (MXU_COLS, MXU_COLS), 1)
            perm = (src == jnp.where(dst < half, 2 * dst, 2 * (dst - half) + 1)).astype(BF16)
            for k in range(wgu_f.shape[2] // MXU_COLS):
                wk = wgu_f[w, :, k * MXU_COLS:(k + 1) * MXU_COLS].astype(BF16)
                sep = jnp.dot(wk, perm, preferred_element_type=F32).astype(BF16)
                wg_s[:, k * half:(k + 1) * half] = sep[:, :half]
                wu_s[:, k * half:(k + 1) * half] = sep[:, half:]
            wd_s[...] = wdn_f[w].astype(BF16)

            @pl.when(nxt_ref[i] >= 0)
            def _():
                for cp in weight_copies(nxt_ref[i], 1 - w):
                    cp.start()

        xb = xs_ref[...].astype(BF16)
        g = jnp.dot(xb, wg_s[...], preferred_element_type=F32) + bg_ref[0]
        u = jnp.dot(xb, wu_s[...], preferred_element_type=F32) + bu_ref[0]
        g = jnp.minimum(g, SWIGLU_LIMIT)
        u = jnp.clip(u, -SWIGLU_LIMIT, SWIGLU_LIMIT)
        a = g * (1.0 / (1.0 + jnp.exp(-SWIGLU_ALPHA * g))) * (u + 1.0)
        o_ref[...] = jnp.dot(a.astype(BF16), wd_s[...], preferred_element_type=F32) + bd_ref[0]

    @pl.when(i >= nact)
    def _():
        o_ref[...] = jnp.zeros(o_ref.shape, F32)


def _experts(block_exp, run_id, next_exp, n_active, xs, w_gu, w_dn, bg, bu, bd):
    n_rows, d = xs.shape
    f = w_dn.shape[1]
    nblk = n_rows // EXPERT_BLOCK
    bsel = lambda i, be, ru, nx, na: (be[i], 0, 0)
    blk = lambda i, be, ru, nx, na: (i, 0)
    anyspace = pl.BlockSpec(memory_space=pl.ANY)
    return pl.pallas_call(
        _expert_kernel,
        grid_spec=pltpu.PrefetchScalarGridSpec(
            num_scalar_prefetch=4,
            grid=(nblk,),
            in_specs=[pl.BlockSpec((EXPERT_BLOCK, d), blk),
                      anyspace, anyspace,
                      pl.BlockSpec((1, 1, f), bsel),
                      pl.BlockSpec((1, 1, f), bsel),
                      pl.BlockSpec((1, 1, d), bsel)],
            out_specs=pl.BlockSpec((EXPERT_BLOCK, d), blk),
            scratch_shapes=[pltpu.VMEM((2, d, 2 * f), F32),
                            pltpu.VMEM((2, f, d), F32),
                            pltpu.VMEM((d, f), BF16), pltpu.VMEM((d, f), BF16), pltpu.VMEM((f, d), BF16),
                            pltpu.SemaphoreType.DMA((2, 2))]),
        out_shape=jax.ShapeDtypeStruct((n_rows, d), F32),
        compiler_params=_cparams(("arbitrary",)),
        name="moe_experts",
    )(block_exp, run_id, next_exp, n_active, xs, w_gu, w_dn, bg, bu, bd)


def _combine_kernel(dest_ref, rows_hbm, gate_ref, x1_ref, g2_ref, fg_ref, o_ref, buf, sems, *, tm):
    n = pl.program_id(0)
    slot = n % 2

    def gather(step, to_slot):
        base = step * (tm * TOP_K)

        def issue(tok, carry):
            for k in range(TOP_K):
                _row_copy(rows_hbm, dest_ref[base + tok * TOP_K + k], buf.at[to_slot, k], tok,
                          sems.at[to_slot]).start()
            return carry

        lax.fori_loop(0, tm, issue, 0, unroll=DMA_UNROLL)

    @pl.when(n == 0)
    def _():
        gather(0, 0)

    @pl.when(n + 1 < pl.num_programs(0))
    def _():
        gather(n + 1, 1 - slot)

    for k in range(TOP_K):
        pltpu.make_async_copy(rows_hbm.at[pl.ds(0, tm)], buf.at[slot, k], sems.at[slot]).wait()

    gate = gate_ref[...]
    y = gate[:, 0:1] * buf[slot, 0]
    for k in range(1, TOP_K):
        y = y + gate[:, k:k + 1] * buf[slot, k]
    xo = x1_ref[...] + g2_ref[0] * y
    ms = jnp.mean(xo * xo, axis=-1, keepdims=True)
    o_ref[...] = xo * lax.rsqrt(ms + NORM_EPS) * fg_ref[...]


def _combine(dest, rows, gates, x1, g2, final_g, tm):
    b, s, d = x1.shape
    nb = s // tm
    out = pl.pallas_call(
        functools.partial(_combine_kernel, tm=tm),
        grid_spec=pltpu.PrefetchScalarGridSpec(
            num_scalar_prefetch=1,
            grid=(b * nb,),
            in_specs=[pl.BlockSpec(memory_space=pl.ANY),
                      pl.BlockSpec((tm, TOP_K), lambda n, ds: (n, 0)),
                      pl.BlockSpec((tm, d), lambda n, ds: (n, 0)),
                      pl.BlockSpec((1, 1, d), lambda n, ds: (n // nb, 0, 0)),
                      pl.BlockSpec((1, d), lambda n, ds: (0, 0))],
            out_specs=pl.BlockSpec((tm, d), lambda n, ds: (n, 0)),
            scratch_shapes=[pltpu.VMEM((2, TOP_K, tm, d), F32), pltpu.SemaphoreType.DMA((2,))]),
        out_shape=jax.ShapeDtypeStruct((b * s, d), F32),
        compiler_params=_cparams(("arbitrary",)),
        name="moe_combine",
    )(dest, rows, gates, x1.reshape(b * s, d), g2, final_g)
    return out.reshape(b, s, d)


def _pick(n, prefs):
    for p in prefs:
        if n % p == 0:
            return p
    raise ValueError(f"no tile for {n}")


def kernel(x, c, ctx, c_ctx, w_mod, b_mod, norm1_g, w_in, lam_q1, lam_k1, lam_q2, lam_k2, subln_g,
           sink, w_out, norm2_g, w_router, b_router, w_gate_up, b_gate_up, w_down, b_down, final_g):
    b, s, d = x.shape
    c_len = ctx.shape[1]
    assert w_mod.shape[0] == 1, "single-layer block"
    t = b * s

    pad = (-(b + 1)) % 8
    cvecs = jnp.concatenate([c, c_ctx[None, :], jnp.zeros((pad, d), F32)], axis=0)
    mod = _adaln(cvecs, w_mod[0], b_mod[0])
    sh1, sc1, g1, sh2, sc2, g2 = [mod[:b, k * d:(k + 1) * d].reshape(b, 1, d) for k in range(6)]
    csh1 = mod[b:b + 1, 0:d].reshape(1, 1, d)
    csc1 = mod[b:b + 1, d:2 * d].reshape(1, 1, d)

    w_in_bf = w_in[0].astype(BF16)
    cos, sin = _rope_tables(s)
    n1 = norm1_g[0].reshape(1, d)
    qat, ka, vat, qbt, kb, vbt = _inproj_latent(x, n1, sh1, sc1, w_in_bf, cos, sin, _pick(s, (512, 256, 128)))
    w_ctx_bf = jnp.concatenate([w_in_bf[:, O_KA:O_QB], w_in_bf[:, O_KB:IN_COLS]], axis=1)
    kac, vact, kbc, vbct = _inproj_ctx(ctx, n1, csh1, csc1, w_ctx_bf)

    ka_all = jnp.concatenate([ka, kac], axis=1)
    vat_all = jnp.concatenate([vat, vact], axis=2)
    sk = s + c_len
    ya = _diff_attn(qat, ka_all, vat_all,
                    lam_q1[0].reshape(1, -1), lam_k1[0].reshape(1, -1),
                    lam_q2[0].reshape(1, -1), lam_k2[0].reshape(1, -1),
                    subln_g[0].reshape(1, -1),
                    _pick(s, (1024, 512, 256, 128)), _pick(sk, (768, 512, 384, 256, 128)))

    yb = _win_attn(qbt, kb, vbt, kbc, vbct, sink[0].reshape(1, -1), _pick(s, (256, 128)))

    w_out_bf = w_out[0].astype(BF16)
    w_r_hi = w_router[0].astype(BF16)
    w_r_lo = (w_router[0] - w_r_hi.astype(F32)).astype(BF16)
    x1, hx2, code, gates, counts = _outproj_router(
        ya, yb, w_out_bf[:QA_COLS], w_out_bf[QA_COLS:], x, g1, norm2_g[0].reshape(1, d), sh2, sc2,
        jnp.concatenate([w_r_hi, w_r_lo], axis=1), b_router[0].reshape(1, -1), _pick(s, (256, 128)))

    n_assign = t * TOP_K
    n_rows = n_assign + N_EXPERTS * EXPERT_BLOCK
    counts_i = counts.reshape(-1).astype(jnp.int32)
    padded = ((counts_i + EXPERT_BLOCK - 1) // EXPERT_BLOCK) * EXPERT_BLOCK
    pend = jnp.cumsum(padded).astype(jnp.int32)
    pstart = pend - padded
    block_start = jnp.arange(n_rows // EXPERT_BLOCK, dtype=jnp.int32) * EXPERT_BLOCK
    block_exp = jnp.minimum(jnp.sum((pend[None, :] <= block_start[:, None]).astype(jnp.int32), axis=1),
                            N_EXPERTS - 1)
    n_active = pend[-1:] // EXPERT_BLOCK
    changed = jnp.concatenate([jnp.ones((1,), jnp.int32), (block_exp[1:] != block_exp[:-1]).astype(jnp.int32)])
    run_id = jnp.cumsum(changed).astype(jnp.int32) - 1
    e_ids = jnp.arange(N_EXPERTS, dtype=jnp.int32)
    later_nonempty = (e_ids[None, :] > e_ids[:, None]) & (padded[None, :] > 0)
    next_of_expert = jnp.min(jnp.where(later_nonempty, e_ids[None, :], N_EXPERTS), axis=1)
    next_of_expert = jnp.where(next_of_expert < N_EXPERTS, next_of_expert, -1).astype(jnp.int32)
    next_exp = next_of_expert[block_exp]

    xs, dest = _dispatch(pstart, code.reshape(-1), hx2.reshape(t, d), jnp.zeros((n_rows, d), F32),
                         _pick(n_assign, (1024,)))

    f = w_down.shape[2]
    bg = b_gate_up[0, :, 0::2].reshape(N_EXPERTS, 1, f)
    bu = b_gate_up[0, :, 1::2].reshape(N_EXPERTS, 1, f)
    bd = b_down[0].reshape(N_EXPERTS, 1, d)
    rows = _experts(block_exp, run_id, next_exp, n_active, xs, w_gate_up[0], w_down[0], bg, bu, bd)

    return _combine(dest, rows, gates, x1, g2, final_g.reshape(1, d), _pick(s, (256,)))
```

```python
import functools
import math

import jax
import jax.numpy as jnp
from jax import lax
from jax.experimental import pallas as pl
from jax.experimental.pallas import tpu as pltpu

F32 = jnp.float32
BF16 = jnp.bfloat16
HIGHEST = lax.Precision.HIGHEST

GRID_W = 64
NORM_EPS = 1e-6
ROPE_BASE = 10000.0
MASK_VALUE = -1e30
DA_HEADS = 4
HEAD_DIM = 64
WA_HEADS = 8
WA_KV_HEADS = 2
WA_GROUP = WA_HEADS // WA_KV_HEADS
WINDOW = 128
N_EXPERTS = 32
TOP_K = 4
SWIGLU_LIMIT = 7.0
SWIGLU_ALPHA = 1.702
EXPERT_BLOCK = 256
LAM_INIT = 0.8 - 0.6 * math.exp(-0.3 * 0)
LOG2E = math.log2(math.e)

QA_COLS = DA_HEADS * 2 * HEAD_DIM
KA_COLS = QA_COLS
VA_COLS = QA_COLS
QB_COLS = WA_HEADS * HEAD_DIM
KB_COLS = WA_KV_HEADS * HEAD_DIM
VB_COLS = KB_COLS
O_QA = 0
O_KA = O_QA + QA_COLS
O_VA = O_KA + KA_COLS
O_QB = O_VA + VA_COLS
O_KB = O_QB + QB_COLS
O_VB = O_KB + KB_COLS
IN_COLS = O_VB + VB_COLS

LANES = 128
ROW_TILE = 8
MXU_COLS = 256
ONES_ROWS = 16
VMEM_LIMIT = 56 * 1024 * 1024


def _cparams(sem):
    return pltpu.CompilerParams(dimension_semantics=sem, vmem_limit_bytes=VMEM_LIMIT)


def _adaln_kernel(c_ref, w_ref, b_ref, o_ref):
    cv = c_ref[...]
    s = cv * (1.0 / (1.0 + jnp.exp(-cv)))
    o_ref[...] = jnp.dot(s, w_ref[...], precision=HIGHEST, preferred_element_type=F32) + b_ref[...]


def _adaln(cvecs, w_mod, b_mod):
    rows, d = cvecs.shape
    n = w_mod.shape[1]
    tn = 1024
    return pl.pallas_call(
        _adaln_kernel,
        grid=(n // tn,),
        in_specs=[pl.BlockSpec((rows, d), lambda j: (0, 0)),
                  pl.BlockSpec((d, tn), lambda j: (0, j)),
                  pl.BlockSpec((1, tn), lambda j: (0, j))],
        out_specs=pl.BlockSpec((rows, tn), lambda j: (0, j)),
        out_shape=jax.ShapeDtypeStruct((rows, n), F32),
        compiler_params=_cparams(("arbitrary",)),
        name="adaln",
    )(cvecs, w_mod, b_mod.reshape(1, n))


def _rope_section(sec, cos, sin):
    tm = sec.shape[0]
    lane = lax.broadcasted_iota(jnp.int32, (tm, LANES), 1)
    low = (lane % 32) < 16
    outs = []
    for j in range(sec.shape[1] // LANES):
        c = sec[:, j * LANES:(j + 1) * LANES]
        partner = jnp.where(low, pltpu.roll(c, LANES - 16, 1), pltpu.roll(c, 16, 1))
        outs.append(c * cos + partner * sin)
    return jnp.concatenate(outs, axis=1)


def _modulated_norm(x, g, shift, scale):
    ms = jnp.mean(x * x, axis=-1, keepdims=True)
    return (x * lax.rsqrt(ms + NORM_EPS) * g) * (1.0 + scale) + shift


def _inproj_latent_kernel(x_ref, g_ref, sh_ref, sc_ref, w_ref, cos_ref, sin_ref,
                          qat_ref, ka_ref, vat_ref, qbt_ref, kb_ref, vbt_ref):
    h = _modulated_norm(x_ref[0], g_ref[...], sh_ref[0], sc_ref[0])
    p = jnp.dot(h.astype(BF16), w_ref[...], preferred_element_type=F32)
    cos = cos_ref[...]
    sin = sin_ref[...]
    qscale = HEAD_DIM ** -0.5 * LOG2E
    qat_ref[0] = (_rope_section(p[:, O_QA:O_KA], cos, sin) * qscale).T.astype(BF16)
    ka_ref[0] = _rope_section(p[:, O_KA:O_VA], cos, sin).astype(BF16)
    vat_ref[0] = p[:, O_VA:O_QB].T.astype(BF16)
    qbt_ref[0] = (_rope_section(p[:, O_QB:O_KB], cos, sin) * qscale).T.astype(BF16)
    kb_ref[0] = _rope_section(p[:, O_KB:O_VB], cos, sin).astype(BF16)
    vbt_ref[0] = p[:, O_VB:IN_COLS].T.astype(BF16)


def _inproj_ctx_kernel(x_ref, g_ref, sh_ref, sc_ref, w_ref, ka_ref, vat_ref, kb_ref, vbt_ref):
    h = _modulated_norm(x_ref[0], g_ref[...], sh_ref[0], sc_ref[0])
    p = jnp.dot(h.astype(BF16), w_ref[...], preferred_element_type=F32)
    ka_ref[0] = p[:, 0:KA_COLS].astype(BF16)
    vat_ref[0] = p[:, KA_COLS:KA_COLS + VA_COLS].T.astype(BF16)
    kb_ref[0] = p[:, KA_COLS + VA_COLS:KA_COLS + VA_COLS + KB_COLS].astype(BF16)
    vbt_ref[0] = p[:, KA_COLS + VA_COLS + KB_COLS:].T.astype(BF16)


def _rope_tables(n_tok):
    rows = (jnp.arange(n_tok) // GRID_W).astype(F32)
    cols = (jnp.arange(n_tok) % GRID_W).astype(F32)
    nf = HEAD_DIM // 4
    inv = ROPE_BASE ** (-jnp.arange(nf, dtype=F32) / nf)
    ar = rows[:, None] * inv
    ac = cols[:, None] * inv
    cos = jnp.concatenate([jnp.cos(ar), jnp.cos(ar), jnp.cos(ac), jnp.cos(ac)], axis=1)
    sin = jnp.concatenate([-jnp.sin(ar), jnp.sin(ar), -jnp.sin(ac), jnp.sin(ac)], axis=1)
    return jnp.tile(cos, (1, LANES // HEAD_DIM)), jnp.tile(sin, (1, LANES // HEAD_DIM))


def _inproj_latent(x, g, shift, scale, w_bf16, cos, sin, tm):
    b, s, d = x.shape
    row = lambda bi, i: (bi, i, 0)
    colt = lambda bi, i: (bi, 0, i)
    mod = lambda bi, i: (bi, 0, 0)
    fixed = lambda bi, i: (0, 0)
    return pl.pallas_call(
        _inproj_latent_kernel,
        grid=(b, s // tm),
        in_specs=[pl.BlockSpec((1, tm, d), row),
                  pl.BlockSpec((1, d), fixed),
                  pl.BlockSpec((1, 1, d), mod),
                  pl.BlockSpec((1, 1, d), mod),
                  pl.BlockSpec((d, IN_COLS), fixed),
                  pl.BlockSpec((tm, LANES), lambda bi, i: (i, 0)),
                  pl.BlockSpec((tm, LANES), lambda bi, i: (i, 0))],
        out_specs=[pl.BlockSpec((1, QA_COLS, tm), colt),
                   pl.BlockSpec((1, tm, KA_COLS), row),
                   pl.BlockSpec((1, VA_COLS, tm), colt),
                   pl.BlockSpec((1, QB_COLS, tm), colt),
                   pl.BlockSpec((1, tm, KB_COLS), row),
                   pl.BlockSpec((1, VB_COLS, tm), colt)],
        out_shape=[jax.ShapeDtypeStruct((b, QA_COLS, s), BF16),
                   jax.ShapeDtypeStruct((b, s, KA_COLS), BF16),
                   jax.ShapeDtypeStruct((b, VA_COLS, s), BF16),
                   jax.ShapeDtypeStruct((b, QB_COLS, s), BF16),
                   jax.ShapeDtypeStruct((b, s, KB_COLS), BF16),
                   jax.ShapeDtypeStruct((b, VB_COLS, s), BF16)],
        compiler_params=_cparams(("arbitrary", "arbitrary")),
        name="inproj_latent",
    )(x, g, shift, scale, w_bf16, cos, sin)


def _inproj_ctx(ctx, g, shift, scale, w_ctx_bf16):
    b, c, d = ctx.shape
    n = w_ctx_bf16.shape[1]
    whole = lambda bi: (bi, 0, 0)
    mod = lambda bi: (0, 0, 0)
    fixed = lambda bi: (0, 0)
    return pl.pallas_call(
        _inproj_ctx_kernel,
        grid=(b,),
        in_specs=[pl.BlockSpec((1, c, d), whole),
                  pl.BlockSpec((1, d), fixed),
                  pl.BlockSpec((1, 1, d), mod),
                  pl.BlockSpec((1, 1, d), mod),
                  pl.BlockSpec((d, n), fixed)],
        out_specs=[pl.BlockSpec((1, c, KA_COLS), whole),
                   pl.BlockSpec((1, VA_COLS, c), whole),
                   pl.BlockSpec((1, c, KB_COLS), whole),
                   pl.BlockSpec((1, VB_COLS, c), whole)],
        out_shape=[jax.ShapeDtypeStruct((b, c, KA_COLS), BF16),
                   jax.ShapeDtypeStruct((b, VA_COLS, c), BF16),
                   jax.ShapeDtypeStruct((b, c, KB_COLS), BF16),
                   jax.ShapeDtypeStruct((b, VB_COLS, c), BF16)],
        compiler_params=_cparams(("arbitrary",)),
        name="inproj_ctx",
    )(ctx, g, shift, scale, w_ctx_bf16)


def _diff_attn_kernel(qt_ref, k_ref, vt_ref, lq1_ref, lk1_ref, lq2_ref, lk2_ref, sg_ref, o_ref,
                      m_ref, acc_ref, s_ref, *, tk):
    d = HEAD_DIM
    hw = 2 * d
    qt = qt_ref[0]
    row = lax.broadcasted_iota(jnp.int32, qt.shape, 0)
    zero = jnp.zeros_like(qt)
    rhs = (jnp.where(row < d, qt, zero), jnp.where(row >= d, qt, zero))
    n_chunks = k_ref.shape[1] // tk
    m_ref[...] = jnp.full(m_ref.shape, -jnp.inf, F32)
    acc_ref[...] = jnp.zeros(acc_ref.shape, F32)
    ones = jnp.ones((ONES_ROWS, tk), BF16)

    def scores(c, j):
        off = pl.multiple_of(j * tk, tk)
        s_ref[c] = jnp.dot(k_ref[0, pl.ds(off, tk), :], rhs[c], preferred_element_type=F32)

    def accumulate(c, j):
        off = pl.multiple_of(j * tk, tk)
        vt = jnp.concatenate([vt_ref[0, :, pl.ds(off, tk)], ones], axis=0)
        st = s_ref[c]
        m_old = m_ref[c]
        m_new = jnp.maximum(m_old, jnp.max(st, axis=0, keepdims=True))
        alpha = jnp.exp2(m_old - m_new)
        p = jnp.exp2(st - m_new).astype(BF16)
        acc_ref[c] = alpha * acc_ref[c] + jnp.dot(vt, p, preferred_element_type=F32)
        m_ref[c] = m_new

    scores(0, 0)

    def chunk(j, carry):
        scores(1, j)
        accumulate(0, j)
        scores(0, j + 1)
        accumulate(1, j)
        return carry

    lax.fori_loop(0, n_chunks - 1, chunk, 0)
    scores(1, n_chunks - 1)
    accumulate(0, n_chunks - 1)
    accumulate(1, n_chunks - 1)

    lam = (jnp.exp(jnp.sum(lq1_ref[...] * lk1_ref[...], axis=-1, keepdims=True))
           - jnp.exp(jnp.sum(lq2_ref[...] * lk2_ref[...], axis=-1, keepdims=True)) + LAM_INIT)
    a1 = acc_ref[0]
    a2 = acc_ref[1]
    ot = a1[:hw] / a1[hw:hw + 1] - lam * (a2[:hw] / a2[hw:hw + 1])
    ms = jnp.mean(ot * ot, axis=0, keepdims=True)
    ot = ot * lax.rsqrt(ms + NORM_EPS)
    o_ref[0] = (ot.T * (sg_ref[...] * (1.0 - LAM_INIT))).astype(o_ref.dtype)


def _diff_attn(qat, ka, vat, lq1, lk1, lq2, lk2, subln_g, tq, tk):
    b, _, s = qat.shape
    sk = ka.shape[1]
    hw = 2 * HEAD_DIM
    vec = lambda bi, h, i: (0, 0)
    return pl.pallas_call(
        functools.partial(_diff_attn_kernel, tk=tk),
        grid=(b, DA_HEADS, s // tq),
        in_specs=[pl.BlockSpec((1, hw, tq), lambda bi, h, i: (bi, h, i)),
                  pl.BlockSpec((1, sk, hw), lambda bi, h, i: (bi, 0, h)),
                  pl.BlockSpec((1, hw, sk), lambda bi, h, i: (bi, h, 0)),
                  pl.BlockSpec((1, HEAD_DIM), vec),
                  pl.BlockSpec((1, HEAD_DIM), vec),
                  pl.BlockSpec((1, HEAD_DIM), vec),
                  pl.BlockSpec((1, HEAD_DIM), vec),
                  pl.BlockSpec((1, hw), vec)],
        out_specs=pl.BlockSpec((1, tq, hw), lambda bi, h, i: (bi, i, h)),
        out_shape=jax.ShapeDtypeStruct((b, s, DA_HEADS * hw), BF16),
        scratch_shapes=[pltpu.VMEM((2, 1, tq), F32),
                        pltpu.VMEM((2, hw + ONES_ROWS, tq), F32),
                        pltpu.VMEM((2, tk, tq), F32)],
        compiler_params=_cparams(("arbitrary", "arbitrary", "arbitrary")),
        name="diff_attn",
    )(qat, ka, vat, lq1, lk1, lq2, lk2, subln_g)


def _win_attn_kernel(qt_ref, k_ref, vt_ref, kc_ref, vct_ref, sink_ref, o_ref, *, tq, lk):
    d = HEAD_DIM
    grp = WA_GROUP
    s_len = k_ref.shape[1]
    c_len = kc_ref.shape[1]
    nk = lk + c_len
    i = pl.program_id(1)
    q0 = i * tq
    start = pl.multiple_of(jnp.clip(q0 - WINDOW, 0, s_len - lk), LANES)
    keys = jnp.concatenate([k_ref[0, pl.ds(start, lk), :], kc_ref[0]], axis=0)
    kpos = start + lax.broadcasted_iota(jnp.int32, (nk, tq), 0)
    qpos = q0 + lax.broadcasted_iota(jnp.int32, (nk, tq), 1)
    visible = jnp.logical_or(kpos >= start + lk, jnp.abs(kpos - qpos) <= WINDOW)
    visible = jnp.concatenate([visible] * grp, axis=1)
    ones = jnp.ones((ONES_ROWS, nk), BF16)
    qt = qt_ref[0]
    blank = jnp.zeros((d, grp * tq), BF16)
    outs = []
    for kv in range(WA_KV_HEADS):
        heads = range(kv * grp, (kv + 1) * grp)
        qcat = jnp.concatenate([qt[h * d:(h + 1) * d, :] for h in heads], axis=1)
        rhs = jnp.concatenate([qcat if j == kv else blank for j in range(WA_KV_HEADS)], axis=0)
        st = jnp.dot(keys, rhs, preferred_element_type=F32)
        st = jnp.where(visible, st, MASK_VALUE)
        sink = jnp.concatenate([jnp.broadcast_to(sink_ref[:, h:h + 1] * LOG2E, (1, tq)) for h in heads], axis=1)
        m = jnp.maximum(jnp.max(st, axis=0, keepdims=True), sink)
        p = jnp.exp2(st - m).astype(BF16)
        vt = jnp.concatenate([vt_ref[0, kv * d:(kv + 1) * d, pl.ds(start, lk)],
                              vct_ref[0, kv * d:(kv + 1) * d, :]], axis=1)
        acc = jnp.dot(jnp.concatenate([vt, ones], axis=0), p, preferred_element_type=F32)
        o = acc[:d] / (acc[d:d + 1] + jnp.exp2(sink - m))
        outs.extend(o[:, g * tq:(g + 1) * tq] for g in range(grp))
    o_ref[0] = jnp.concatenate(outs, axis=0).T.astype(o_ref.dtype)


def _win_attn(qbt, kb, vbt, kbc, vbct, sink, tq):
    b, _, s = qbt.shape
    c = kbc.shape[1]
    lk = tq + 2 * WINDOW
    assert s >= lk and tq % LANES == 0
    whole = lambda bi, i: (bi, 0, 0)
    return pl.pallas_call(
        functools.partial(_win_attn_kernel, tq=tq, lk=lk),
        grid=(b, s // tq),
        in_specs=[pl.BlockSpec((1, QB_COLS, tq), lambda bi, i: (bi, 0, i)),
                  pl.BlockSpec((1, s, KB_COLS), whole),
                  pl.BlockSpec((1, VB_COLS, s), whole),
                  pl.BlockSpec((1, c, KB_COLS), whole),
                  pl.BlockSpec((1, VB_COLS, c), whole),
                  pl.BlockSpec((1, WA_HEADS), lambda bi, i: (0, 0))],
        out_specs=pl.BlockSpec((1, tq, QB_COLS), lambda bi, i: (bi, i, 0)),
        out_shape=jax.ShapeDtypeStruct((b, s, QB_COLS), BF16),
        compiler_params=_cparams(("arbitrary", "arbitrary")),
        name="win_attn",
    )(qbt, kb, vbt, kbc, vbct, sink)


def _outproj_router_kernel(ya_ref, yb_ref, woa_ref, wob_ref, x_ref, g1_ref, n2_ref, sh_ref, sc_ref,
                           wr_ref, br_ref, x1_ref, hx_ref, code_ref, gate_ref, cnt_ref, carry_ref):
    first = jnp.logical_and(pl.program_id(0) == 0, pl.program_id(1) == 0)

    @pl.when(first)
    def _():
        carry_ref[...] = jnp.zeros(carry_ref.shape, F32)

    y = (jnp.dot(ya_ref[0], woa_ref[...], preferred_element_type=F32)
         + jnp.dot(yb_ref[0], wob_ref[...], preferred_element_type=F32))
    x1 = x_ref[0] + g1_ref[0] * y
    x1_ref[0] = x1
    hx = _modulated_norm(x1, n2_ref[...], sh_ref[0], sc_ref[0])
    _store_row_tiles(hx_ref, hx)
    hx_hi = hx.astype(BF16)
    hx_lo = (hx - hx_hi.astype(F32)).astype(BF16)
    wr = wr_ref[...]
    part = jnp.dot(hx_hi, wr, preferred_element_type=F32)
    logits = (part[:, :N_EXPERTS] + part[:, N_EXPERTS:]
              + jnp.dot(hx_lo, wr[:, :N_EXPERTS], preferred_element_type=F32) + br_ref[...])

    tm = logits.shape[0]
    lane_e = lax.broadcasted_iota(jnp.int32, (tm, N_EXPERTS), 1).astype(F32)
    work = logits
    tops, idxs, hots = [], [], []
    for _k in range(TOP_K):
        m = jnp.max(work, axis=-1, keepdims=True)
        idx = jnp.min(jnp.where(work == m, lane_e, float(N_EXPERTS)), axis=-1, keepdims=True)
        hot = lane_e == idx
        work = jnp.where(hot, -jnp.inf, work)
        tops.append(m)
        idxs.append(idx)
        hots.append(hot)
    es = [jnp.exp(t - tops[0]) for t in tops]
    den = es[0] + es[1] + es[2] + es[3]

    multi = jnp.zeros((tm, N_EXPERTS), F32)
    for hot in hots:
        multi = multi + hot.astype(F32)
    r_i = lax.broadcasted_iota(jnp.int32, (tm, tm), 0)
    c_i = lax.broadcasted_iota(jnp.int32, (tm, tm), 1)
    tri = (c_i <= r_i).astype(BF16)
    incl = jnp.dot(tri, multi.astype(BF16), preferred_element_type=F32)
    before = carry_ref[...] + incl - 1.0

    lane_k = lax.broadcasted_iota(jnp.int32, (tm, TOP_K), 1)
    code = jnp.zeros((tm, TOP_K), jnp.int32)
    gate = jnp.zeros((tm, TOP_K), F32)
    for k in range(TOP_K):
        rank = jnp.sum(jnp.where(hots[k], before, 0.0), axis=-1, keepdims=True)
        ck = idxs[k].astype(jnp.int32) * 65536 + rank.astype(jnp.int32)
        code = jnp.where(lane_k == k, ck, code)
        gate = jnp.where(lane_k == k, es[k] / den, gate)
    code_ref[...] = code
    gate_ref[...] = gate
    carry_ref[...] = carry_ref[...] + jnp.sum(multi, axis=0, keepdims=True)
    cnt_ref[...] = carry_ref[...]


def _outproj_router(ya, yb, woa, wob, x, g1, n2, sh2, sc2, w_r_parts, b_r, tm):
    b, s, d = x.shape
    nb = s // tm
    row = lambda bi, i: (bi, i, 0)
    mod = lambda bi, i: (bi, 0, 0)
    fixed = lambda bi, i: (0, 0)
    tok = lambda bi, i: (bi * nb + i, 0)
    half = ya.shape[2]
    return pl.pallas_call(
        _outproj_router_kernel,
        grid=(b, nb),
        in_specs=[pl.BlockSpec((1, tm, half), row),
                  pl.BlockSpec((1, tm, half), row),
                  pl.BlockSpec((half, d), fixed),
                  pl.BlockSpec((half, d), fixed),
                  pl.BlockSpec((1, tm, d), row),
                  pl.BlockSpec((1, 1, d), mod),
                  pl.BlockSpec((1, d), fixed),
                  pl.BlockSpec((1, 1, d), mod),
                  pl.BlockSpec((1, 1, d), mod),
                  pl.BlockSpec((d, 2 * N_EXPERTS), fixed),
                  pl.BlockSpec((1, N_EXPERTS), fixed)],
        out_specs=[pl.BlockSpec((1, tm, d), row),
                   pl.BlockSpec((tm * ROW_TILE, LANES), tok),
                   pl.BlockSpec((tm, TOP_K), tok),
                   pl.BlockSpec((tm, TOP_K), tok),
                   pl.BlockSpec((1, N_EXPERTS), fixed)],
        out_shape=[jax.ShapeDtypeStruct((b, s, d), F32),
                   jax.ShapeDtypeStruct((b * s * ROW_TILE, LANES), F32),
                   jax.ShapeDtypeStruct((b * s, TOP_K), jnp.int32),
                   jax.ShapeDtypeStruct((b * s, TOP_K), F32),
                   jax.ShapeDtypeStruct((1, N_EXPERTS), F32)],
        scratch_shapes=[pltpu.VMEM((1, N_EXPERTS), F32)],
        compiler_params=_cparams(("arbitrary", "arbitrary")),
        name="outproj_router",
    )(ya, yb, woa, wob, x, g1, n2, sh2, sc2, w_r_parts, b_r)


def _row_copy(src_hbm, src_row, dst_ref, dst_row, sem):
    return pltpu.make_async_copy(src_hbm.at[pl.ds(src_row, 1)], dst_ref.at[pl.ds(dst_row, 1)], sem)


def _store_row_tiles(ref, val):
    n = val.shape[0]
    for c in range(ROW_TILE):
        ref[pl.ds(c, n, stride=ROW_TILE), :] = val[:, c * LANES:(c + 1) * LANES]


def _load_row_tiles(ref, n):
    return [ref[pl.ds(c, n, stride=ROW_TILE), :] for c in range(ROW_TILE)]


def _tile_rows(row, count=1):
    if isinstance(row, int):
        return pl.ds(row * ROW_TILE, count * ROW_TILE)
    return pl.ds(pl.multiple_of(row * ROW_TILE, ROW_TILE), count * ROW_TILE)


def _tile_copy(src_ref, src_row, dst_ref, dst_row, sem):
    return pltpu.make_async_copy(src_ref.at[_tile_rows(src_row)], dst_ref.at[_tile_rows(dst_row)], sem)


def _dispatch_kernel(pbound_ref, cnt_ref, code_ref, hx_ref, xs_hbm, dest_ref, zeros, sem, zsem, *, chunk):
    n_rows = xs_hbm.shape[0] // ROW_TILE
    blk = zeros.shape[0] // ROW_TILE

    @pl.when(pl.program_id(0) == 0)
    def _():
        zeros[...] = jnp.zeros(zeros.shape, F32)
        tail_blocks = (n_rows - pbound_ref[N_EXPERTS]) // blk

        def pad_row(r, carry):
            _tile_copy(zeros, 0, xs_hbm, r, zsem).start()
            return carry

        def pad_expert(e, carry):
            lax.fori_loop(pbound_ref[e] + cnt_ref[e], pbound_ref[e + 1], pad_row, 0)
            return carry

        def tail_copy(t):
            return pltpu.make_async_copy(zeros, xs_hbm.at[_tile_rows(pbound_ref[N_EXPERTS] + t * blk, blk)], zsem)

        def tail_start(t, carry):
            tail_copy(t).start()
            return carry

        lax.fori_loop(0, N_EXPERTS, pad_expert, 0)
        lax.fori_loop(0, tail_blocks, tail_start, 0)

        def pad_row_wait(r, carry):
            _tile_copy(zeros, 0, xs_hbm, 0, zsem).wait()
            return carry

        def pad_expert_wait(e, carry):
            lax.fori_loop(pbound_ref[e] + cnt_ref[e], pbound_ref[e + 1], pad_row_wait, 0)
            return carry

        def tail_wait(t, carry):
            tail_copy(0).wait()
            return carry

        lax.fori_loop(0, N_EXPERTS, pad_expert_wait, 0)
        lax.fori_loop(0, tail_blocks, tail_wait, 0)

    for j in range(chunk):
        cd = code_ref[j]
        dst = pbound_ref[cd >> 16] + (cd & 0xFFFF)
        dest_ref[j] = dst
        _tile_copy(hx_ref, j // TOP_K, xs_hbm, dst, sem).start(priority=j % 2)
    pltpu.make_async_copy(xs_hbm.at[_tile_rows(0, chunk)], xs_hbm.at[_tile_rows(0, chunk)], sem).wait()


def _dispatch(pbound, counts_i, code_flat, hx_tiles, n_rows, chunk):
    n_assign = code_flat.shape[0]
    return pl.pallas_call(
        functools.partial(_dispatch_kernel, chunk=chunk),
        grid_spec=pltpu.PrefetchScalarGridSpec(
            num_scalar_prefetch=2,
            grid=(n_assign // chunk,),
            in_specs=[pl.BlockSpec((chunk,), lambda i, pb, ct: (i,), memory_space=pltpu.SMEM),
                      pl.BlockSpec((chunk // TOP_K * ROW_TILE, LANES), lambda i, pb, ct: (i, 0))],
            out_specs=[pl.BlockSpec(memory_space=pl.ANY),
                       pl.BlockSpec((chunk,), lambda i, pb, ct: (i,), memory_space=pltpu.SMEM)],
            scratch_shapes=[pltpu.VMEM((EXPERT_BLOCK * ROW_TILE, LANES), F32),
                            pltpu.SemaphoreType.DMA(()),
                            pltpu.SemaphoreType.DMA(())]),
        out_shape=[jax.ShapeDtypeStruct((n_rows * ROW_TILE, LANES), F32),
                   jax.ShapeDtypeStruct((n_assign,), jnp.int32)],
        compiler_params=_cparams(("arbitrary",)),
        name="moe_dispatch",
    )(pbound, counts_i, code_flat, hx_tiles)


def _expert_kernel(bexp_ref, run_ref, nxt_ref, nact_ref, xs_ref, wgu_hbm, wdn_hbm, bg_ref, bu_ref, bd_ref,
                   o_ref, wgu_f, wdn_f, wg_s, wu_s, wd_s, wsem):
    i = pl.program_id(0)
    nact = nact_ref[0]

    def weight_copies(expert, w):
        return (pltpu.make_async_copy(wgu_hbm.at[expert], wgu_f.at[w], wsem.at[w, 0]),
                pltpu.make_async_copy(wdn_hbm.at[expert], wdn_f.at[w], wsem.at[w, 1]))

    @pl.when(i == 0)
    def _():
        for cp in weight_copies(bexp_ref[0], 0):
            cp.start()

    @pl.when(i < nact)
    def _():
        changed = jnp.logical_or(i == 0, bexp_ref[i] != bexp_ref[jnp.maximum(i - 1, 0)])

        @pl.when(changed)
        def _():
            w = run_ref[i] % 2
            for cp in weight_copies(bexp_ref[i], w):
                cp.wait()
            half = MXU_COLS // 2
            src = lax.broadcasted_iota(jnp.int32, (MXU_COLS, MXU_COLS), 0)
            dst = lax.broadcasted_iota(jnp.int32, (MXU_COLS, MXU_COLS), 1)
            perm = (src == jnp.where(dst < half, 2 * dst, 2 * (dst - half) + 1)).astype(BF16)
            for k in range(wgu_f.shape[2] // MXU_COLS):
                wk = wgu_f[w, :, k * MXU_COLS:(k + 1) * MXU_COLS].astype(BF16)
                sep = jnp.dot(wk, perm, preferred_element_type=F32).astype(BF16)
                wg_s[:, k * half:(k + 1) * half] = sep[:, :half]
                wu_s[:, k * half:(k + 1) * half] = sep[:, half:]
            wd_s[...] = wdn_f[w].astype(BF16)

            @pl.when(nxt_ref[i] >= 0)
            def _():
                for cp in weight_copies(nxt_ref[i], 1 - w):
                    cp.start()

        xb = jnp.concatenate([c.astype(BF16) for c in _load_row_tiles(xs_ref, EXPERT_BLOCK)], axis=1)
        g = jnp.dot(xb, wg_s[...], preferred_element_type=F32) + bg_ref[0]
        u = jnp.dot(xb, wu_s[...], preferred_element_type=F32) + bu_ref[0]
        g = jnp.minimum(g, SWIGLU_LIMIT)
        u = jnp.clip(u, -SWIGLU_LIMIT, SWIGLU_LIMIT)
        a = g * (1.0 / (1.0 + jnp.exp(-SWIGLU_ALPHA * g))) * (u + 1.0)
        o_ref[...] = jnp.dot(a.astype(BF16), wd_s[...], preferred_element_type=F32) + bd_ref[0]

    @pl.when(i >= nact)
    def _():
        o_ref[...] = jnp.zeros(o_ref.shape, F32)


def _experts(block_exp, run_id, next_exp, n_active, xs, w_gu, w_dn, bg, bu, bd):
    f, d = w_dn.shape[1:]
    n_rows = xs.shape[0] // ROW_TILE
    nblk = n_rows // EXPERT_BLOCK
    bsel = lambda i, be, ru, nx, na: (be[i], 0, 0)
    blk = lambda i, be, ru, nx, na: (i, 0)
    anyspace = pl.BlockSpec(memory_space=pl.ANY)
    return pl.pallas_call(
        _expert_kernel,
        grid_spec=pltpu.PrefetchScalarGridSpec(
            num_scalar_prefetch=4,
            grid=(nblk,),
            in_specs=[pl.BlockSpec((EXPERT_BLOCK * ROW_TILE, LANES), blk),
                      anyspace, anyspace,
                      pl.BlockSpec((1, 1, f), bsel),
                      pl.BlockSpec((1, 1, f), bsel),
                      pl.BlockSpec((1, 1, d), bsel)],
            out_specs=pl.BlockSpec((EXPERT_BLOCK, d), blk),
            scratch_shapes=[pltpu.VMEM((2, d, 2 * f), F32),
                            pltpu.VMEM((2, f, d), F32),
                            pltpu.VMEM((d, f), BF16), pltpu.VMEM((d, f), BF16), pltpu.VMEM((f, d), BF16),
                            pltpu.SemaphoreType.DMA((2, 2))]),
        out_shape=jax.ShapeDtypeStruct((n_rows, d), F32),
        compiler_params=_cparams(("arbitrary",)),
        name="moe_experts",
    )(block_exp, run_id, next_exp, n_active, xs, w_gu, w_dn, bg, bu, bd)


def _combine_kernel(dest_ref, rows_hbm, gate_ref, x1_ref, g2_ref, fg_ref, o_ref, buf, sems, *, tm):
    n = pl.program_id(0)
    slot = n % 2

    def row_gather(step, to_slot, tok, k):
        return _row_copy(rows_hbm, dest_ref[step * (tm * TOP_K) + tok * TOP_K + k], buf.at[to_slot, k], tok,
                         sems.at[to_slot])

    @pl.when(n == 0)
    def _():
        def issue(tok, carry):
            for k in range(TOP_K):
                row_gather(0, 0, tok, k).start()
            return carry
        lax.fori_loop(0, tm, issue, 0, unroll=4)

    @pl.when(n + 1 < pl.num_programs(0))
    def _():
        for tok in range(tm):
            for k in range(TOP_K):
                row_gather(n + 1, 1 - slot, tok, k).start(priority=k % 2)

    for k in range(TOP_K):
        pltpu.make_async_copy(rows_hbm.at[pl.ds(0, tm)], buf.at[slot, k], sems.at[slot]).wait()

    gate = gate_ref[...]
    y = gate[:, 0:1] * buf[slot, 0]
    for k in range(1, TOP_K):
        y = y + gate[:, k:k + 1] * buf[slot, k]
    xo = x1_ref[...] + g2_ref[0] * y
    ms = jnp.mean(xo * xo, axis=-1, keepdims=True)
    o_ref[...] = xo * lax.rsqrt(ms + NORM_EPS) * fg_ref[...]


def _combine(dest, rows, gates, x1, g2, final_g, tm):
    b, s, d = x1.shape
    nb = s // tm
    out = pl.pallas_call(
        functools.partial(_combine_kernel, tm=tm),
        grid_spec=pltpu.PrefetchScalarGridSpec(
            num_scalar_prefetch=1,
            grid=(b * nb,),
            in_specs=[pl.BlockSpec(memory_space=pl.ANY),
                      pl.BlockSpec((tm, TOP_K), lambda n, ds: (n, 0)),
                      pl.BlockSpec((tm, d), lambda n, ds: (n, 0)),
                      pl.BlockSpec((1, 1, d), lambda n, ds: (n // nb, 0, 0)),
                      pl.BlockSpec((1, d), lambda n, ds: (0, 0))],
            out_specs=pl.BlockSpec((tm, d), lambda n, ds: (n, 0)),
            scratch_shapes=[pltpu.VMEM((2, TOP_K, tm, d), F32), pltpu.SemaphoreType.DMA((2,))]),
        out_shape=jax.ShapeDtypeStruct((b * s, d), F32),
        compiler_params=_cparams(("arbitrary",)),
        name="moe_combine",
    )(dest, rows, gates, x1.reshape(b * s, d), g2, final_g)
    return out.reshape(b, s, d)


def _pick(n, prefs):
    for p in prefs:
        if n % p == 0:
            return p
    raise ValueError(f"no tile for {n}")


def kernel(x, c, ctx, c_ctx, w_mod, b_mod, norm1_g, w_in, lam_q1, lam_k1, lam_q2, lam_k2, subln_g,
           sink, w_out, norm2_g, w_router, b_router, w_gate_up, b_gate_up, w_down, b_down, final_g):
    b, s, d = x.shape
    c_len = ctx.shape[1]
    assert w_mod.shape[0] == 1, "single-layer block"
    assert d == ROW_TILE * LANES, "dispatched token rows are one (8, 128) f32 tile each"
    t = b * s

    pad = (-(b + 1)) % 8
    cvecs = jnp.concatenate([c, c_ctx[None, :], jnp.zeros((pad, d), F32)], axis=0)
    mod = _adaln(cvecs, w_mod[0], b_mod[0])
    sh1, sc1, g1, sh2, sc2, g2 = [mod[:b, k * d:(k + 1) * d].reshape(b, 1, d) for k in range(6)]
    csh1 = mod[b:b + 1, 0:d].reshape(1, 1, d)
    csc1 = mod[b:b + 1, d:2 * d].reshape(1, 1, d)

    w_in_bf = w_in[0].astype(BF16)
    cos, sin = _rope_tables(s)
    n1 = norm1_g[0].reshape(1, d)
    qat, ka, vat, qbt, kb, vbt = _inproj_latent(x, n1, sh1, sc1, w_in_bf, cos, sin, _pick(s, (512, 256, 128)))
    w_ctx_bf = jnp.concatenate([w_in_bf[:, O_KA:O_QB], w_in_bf[:, O_KB:IN_COLS]], axis=1)
    kac, vact, kbc, vbct = _inproj_ctx(ctx, n1, csh1, csc1, w_ctx_bf)

    ka_all = jnp.concatenate([ka, kac], axis=1)
    vat_all = jnp.concatenate([vat, vact], axis=2)
    sk = s + c_len
    ya = _diff_attn(qat, ka_all, vat_all,
                    lam_q1[0].reshape(1, -1), lam_k1[0].reshape(1, -1),
                    lam_q2[0].reshape(1, -1), lam_k2[0].reshape(1, -1),
                    subln_g[0].reshape(1, -1),
                    _pick(s, (1024, 512, 256, 128)), _pick(sk, (768, 512, 384, 256, 128)))

    yb = _win_attn(qbt, kb, vbt, kbc, vbct, sink[0].reshape(1, -1), _pick(s, (256, 128)))

    w_out_bf = w_out[0].astype(BF16)
    w_r_hi = w_router[0].astype(BF16)
    w_r_lo = (w_router[0] - w_r_hi.astype(F32)).astype(BF16)
    x1, hx2, code, gates, counts = _outproj_router(
        ya, yb, w_out_bf[:QA_COLS], w_out_bf[QA_COLS:], x, g1, norm2_g[0].reshape(1, d), sh2, sc2,
        jnp.concatenate([w_r_hi, w_r_lo], axis=1), b_router[0].reshape(1, -1), _pick(s, (256, 128)))

    n_assign = t * TOP_K
    n_rows = n_assign + N_EXPERTS * EXPERT_BLOCK
    counts_i = counts.reshape(-1).astype(jnp.int32)
    padded = ((counts_i + EXPERT_BLOCK - 1) // EXPERT_BLOCK) * EXPERT_BLOCK
    pend = jnp.cumsum(padded).astype(jnp.int32)
    pbound = jnp.concatenate([jnp.zeros((1,), jnp.int32), pend])
    block_start = jnp.arange(n_rows // EXPERT_BLOCK, dtype=jnp.int32) * EXPERT_BLOCK
    block_exp = jnp.minimum(jnp.sum((pend[None, :] <= block_start[:, None]).astype(jnp.int32), axis=1),
                            N_EXPERTS - 1)
    n_active = pend[-1:] // EXPERT_BLOCK
    changed = jnp.concatenate([jnp.ones((1,), jnp.int32), (block_exp[1:] != block_exp[:-1]).astype(jnp.int32)])
    run_id = jnp.cumsum(changed).astype(jnp.int32) - 1
    e_ids = jnp.arange(N_EXPERTS, dtype=jnp.int32)
    later_nonempty = (e_ids[None, :] > e_ids[:, None]) & (padded[None, :] > 0)
    next_of_expert = jnp.min(jnp.where(later_nonempty, e_ids[None, :], N_EXPERTS), axis=1)
    next_of_expert = jnp.where(next_of_expert < N_EXPERTS, next_of_expert, -1).astype(jnp.int32)
    next_exp = next_of_expert[block_exp]

    xs, dest = _dispatch(pbound, counts_i, code.reshape(-1), hx2, n_rows, _pick(n_assign, (1024,)))

    f = w_down.shape[2]
    bg = b_gate_up[0, :, 0::2].reshape(N_EXPERTS, 1, f)
    bu = b_gate_up[0, :, 1::2].reshape(N_EXPERTS, 1, f)
    bd = b_down[0].reshape(N_EXPERTS, 1, d)
    rows = _experts(block_exp, run_id, next_exp, n_active, xs, w_gate_up[0], w_down[0], bg, bu, bd)

    return _combine(dest, rows, gates, x1, g2, final_g.reshape(1, d), _pick(s, (256,)))
```

```python
import functools
import math

import jax
import jax.numpy as jnp
import numpy as np
from jax import lax
from jax.experimental import pallas as pl
from jax.experimental.pallas import tpu as pltpu

F32 = jnp.float32
BF16 = jnp.bfloat16
HIGHEST = lax.Precision.HIGHEST

GRID_W = 64
NORM_EPS = 1e-6
ROPE_BASE = 10000.0
MASK_VALUE = -1e30
DA_HEADS = 4
HEAD_DIM = 64
WA_HEADS = 8
WA_KV_HEADS = 2
WA_GROUP = WA_HEADS // WA_KV_HEADS
WINDOW = 128
N_EXPERTS = 32
TOP_K = 4
SWIGLU_LIMIT = 7.0
SWIGLU_ALPHA = 1.702
EXPERT_BLOCK = 256
LAM_INIT = 0.8 - 0.6 * math.exp(-0.3 * 0)
LOG2E = math.log2(math.e)

QA_COLS = DA_HEADS * 2 * HEAD_DIM
KA_COLS = QA_COLS
VA_COLS = QA_COLS
QB_COLS = WA_HEADS * HEAD_DIM
KB_COLS = WA_KV_HEADS * HEAD_DIM
VB_COLS = KB_COLS
O_QA = 0
O_KA = O_QA + QA_COLS
O_VA = O_KA + KA_COLS
O_QB = O_VA + VA_COLS
O_KB = O_QB + QB_COLS
O_VB = O_KB + KB_COLS
IN_COLS = O_VB + VB_COLS

LANES = 128
ROW_TILE = 8
MXU_COLS = 256
ONES_ROWS = 16
VMEM_LIMIT = 56 * 1024 * 1024


def _cparams(sem):
    return pltpu.CompilerParams(dimension_semantics=sem, vmem_limit_bytes=VMEM_LIMIT)


def _adaln_kernel(c_ref, w_ref, b_ref, o_ref):
    cv = c_ref[...]
    s = cv * (1.0 / (1.0 + jnp.exp(-cv)))
    o_ref[...] = jnp.dot(s, w_ref[...], precision=HIGHEST, preferred_element_type=F32) + b_ref[...]


def _adaln(cvecs, w_mod, b_mod):
    rows, d = cvecs.shape
    n = w_mod.shape[1]
    tn = 1024
    return pl.pallas_call(
        _adaln_kernel,
        grid=(n // tn,),
        in_specs=[pl.BlockSpec((rows, d), lambda j: (0, 0)),
                  pl.BlockSpec((d, tn), lambda j: (0, j)),
                  pl.BlockSpec((1, tn), lambda j: (0, j))],
        out_specs=pl.BlockSpec((rows, tn), lambda j: (0, j)),
        out_shape=jax.ShapeDtypeStruct((rows, n), F32),
        compiler_params=_cparams(("arbitrary",)),
        name="adaln",
    )(cvecs, w_mod, b_mod.reshape(1, n))


def _rope_section(sec, cos, sin):
    tm = sec.shape[0]
    lane = lax.broadcasted_iota(jnp.int32, (tm, LANES), 1)
    low = (lane % 32) < 16
    outs = []
    for j in range(sec.shape[1] // LANES):
        c = sec[:, j * LANES:(j + 1) * LANES]
        partner = jnp.where(low, pltpu.roll(c, LANES - 16, 1), pltpu.roll(c, 16, 1))
        outs.append(c * cos + partner * sin)
    return jnp.concatenate(outs, axis=1)


def _modulated_norm(x, g, shift, scale):
    ms = jnp.mean(x * x, axis=-1, keepdims=True)
    return (x * lax.rsqrt(ms + NORM_EPS) * g) * (1.0 + scale) + shift


def _inproj_latent_kernel(x_ref, g_ref, sh_ref, sc_ref, w_ref, cos_ref, sin_ref,
                          qat_ref, ka_ref, vat_ref, qbt_ref, kb_ref, vbt_ref):
    h = _modulated_norm(x_ref[0], g_ref[...], sh_ref[0], sc_ref[0])
    p = jnp.dot(h.astype(BF16), w_ref[...], preferred_element_type=F32)
    cos = cos_ref[...]
    sin = sin_ref[...]
    qscale = HEAD_DIM ** -0.5 * LOG2E
    qat_ref[0] = (_rope_section(p[:, O_QA:O_KA], cos, sin) * qscale).T.astype(BF16)
    ka_ref[0] = _rope_section(p[:, O_KA:O_VA], cos, sin).astype(BF16)
    vat_ref[0] = p[:, O_VA:O_QB].T.astype(BF16)
    qbt_ref[0] = (_rope_section(p[:, O_QB:O_KB], cos, sin) * qscale).T.astype(BF16)
    kb_ref[0] = _rope_section(p[:, O_KB:O_VB], cos, sin).astype(BF16)
    vbt_ref[0] = p[:, O_VB:IN_COLS].T.astype(BF16)


def _inproj_ctx_kernel(x_ref, g_ref, sh_ref, sc_ref, w_ref, ka_ref, vat_ref, kb_ref, vbt_ref):
    h = _modulated_norm(x_ref[0], g_ref[...], sh_ref[0], sc_ref[0])
    p = jnp.dot(h.astype(BF16), w_ref[...], preferred_element_type=F32)
    ka_ref[0] = p[:, 0:KA_COLS].astype(BF16)
    vat_ref[0] = p[:, KA_COLS:KA_COLS + VA_COLS].T.astype(BF16)
    kb_ref[0] = p[:, KA_COLS + VA_COLS:KA_COLS + VA_COLS + KB_COLS].astype(BF16)
    vbt_ref[0] = p[:, KA_COLS + VA_COLS + KB_COLS:].T.astype(BF16)


def _rope_tables(n_tok):
    pos = np.arange(n_tok)
    nf = HEAD_DIM // 4
    inv = ROPE_BASE ** (-np.arange(nf) / nf)
    ar = (pos // GRID_W)[:, None] * inv
    ac = (pos % GRID_W)[:, None] * inv
    cos = np.concatenate([np.cos(ar), np.cos(ar), np.cos(ac), np.cos(ac)], axis=1)
    sin = np.concatenate([-np.sin(ar), np.sin(ar), -np.sin(ac), np.sin(ac)], axis=1)
    reps = (1, LANES // HEAD_DIM)
    return jnp.asarray(np.tile(cos, reps), F32), jnp.asarray(np.tile(sin, reps), F32)


def _inproj_latent(x, g, shift, scale, w_bf16, cos, sin, tm):
    b, s, d = x.shape
    row = lambda bi, i: (bi, i, 0)
    colt = lambda bi, i: (bi, 0, i)
    mod = lambda bi, i: (bi, 0, 0)
    fixed = lambda bi, i: (0, 0)
    return pl.pallas_call(
        _inproj_latent_kernel,
        grid=(b, s // tm),
        in_specs=[pl.BlockSpec((1, tm, d), row),
                  pl.BlockSpec((1, d), fixed),
                  pl.BlockSpec((1, 1, d), mod),
                  pl.BlockSpec((1, 1, d), mod),
                  pl.BlockSpec((d, IN_COLS), fixed),
                  pl.BlockSpec((tm, LANES), lambda bi, i: (i, 0)),
                  pl.BlockSpec((tm, LANES), lambda bi, i: (i, 0))],
        out_specs=[pl.BlockSpec((1, QA_COLS, tm), colt),
                   pl.BlockSpec((1, tm, KA_COLS), row),
                   pl.BlockSpec((1, VA_COLS, tm), colt),
                   pl.BlockSpec((1, QB_COLS, tm), colt),
                   pl.BlockSpec((1, tm, KB_COLS), row),
                   pl.BlockSpec((1, VB_COLS, tm), colt)],
        out_shape=[jax.ShapeDtypeStruct((b, QA_COLS, s), BF16),
                   jax.ShapeDtypeStruct((b, s, KA_COLS), BF16),
                   jax.ShapeDtypeStruct((b, VA_COLS, s), BF16),
                   jax.ShapeDtypeStruct((b, QB_COLS, s), BF16),
                   jax.ShapeDtypeStruct((b, s, KB_COLS), BF16),
                   jax.ShapeDtypeStruct((b, VB_COLS, s), BF16)],
        compiler_params=_cparams(("arbitrary", "arbitrary")),
        name="inproj_latent",
    )(x, g, shift, scale, w_bf16, cos, sin)


def _inproj_ctx(ctx, g, shift, scale, w_ctx_bf16):
    b, c, d = ctx.shape
    n = w_ctx_bf16.shape[1]
    whole = lambda bi: (bi, 0, 0)
    mod = lambda bi: (0, 0, 0)
    fixed = lambda bi: (0, 0)
    return pl.pallas_call(
        _inproj_ctx_kernel,
        grid=(b,),
        in_specs=[pl.BlockSpec((1, c, d), whole),
                  pl.BlockSpec((1, d), fixed),
                  pl.BlockSpec((1, 1, d), mod),
                  pl.BlockSpec((1, 1, d), mod),
                  pl.BlockSpec((d, n), fixed)],
        out_specs=[pl.BlockSpec((1, c, KA_COLS), whole),
                   pl.BlockSpec((1, VA_COLS, c), whole),
                   pl.BlockSpec((1, c, KB_COLS), whole),
                   pl.BlockSpec((1, VB_COLS, c), whole)],
        out_shape=[jax.ShapeDtypeStruct((b, c, KA_COLS), BF16),
                   jax.ShapeDtypeStruct((b, VA_COLS, c), BF16),
                   jax.ShapeDtypeStruct((b, c, KB_COLS), BF16),
                   jax.ShapeDtypeStruct((b, VB_COLS, c), BF16)],
        compiler_params=_cparams(("arbitrary",)),
        name="inproj_ctx",
    )(ctx, g, shift, scale, w_ctx_bf16)


def _diff_attn_kernel(qt_ref, k_ref, vt_ref, lq1_ref, lk1_ref, lq2_ref, lk2_ref, sg_ref, o_ref,
                      m_ref, acc_ref, s_ref, *, tk):
    d = HEAD_DIM
    hw = 2 * d
    qt = qt_ref[0]
    row = lax.broadcasted_iota(jnp.int32, qt.shape, 0)
    zero = jnp.zeros_like(qt)
    rhs = (jnp.where(row < d, qt, zero), jnp.where(row >= d, qt, zero))
    n_chunks = k_ref.shape[1] // tk
    m_ref[...] = jnp.full(m_ref.shape, -jnp.inf, F32)
    acc_ref[...] = jnp.zeros(acc_ref.shape, F32)
    ones = jnp.ones((ONES_ROWS, tk), BF16)

    def scores(c, j):
        off = pl.multiple_of(j * tk, tk)
        s_ref[c] = jnp.dot(k_ref[0, pl.ds(off, tk), :], rhs[c], preferred_element_type=F32)

    def accumulate(c, j):
        off = pl.multiple_of(j * tk, tk)
        vt = jnp.concatenate([vt_ref[0, :, pl.ds(off, tk)], ones], axis=0)
        st = s_ref[c]
        m_old = m_ref[c]
        m_new = jnp.maximum(m_old, jnp.max(st, axis=0, keepdims=True))
        alpha = jnp.exp2(m_old - m_new)
        p = jnp.exp2(st - m_new).astype(BF16)
        acc_ref[c] = alpha * acc_ref[c] + jnp.dot(vt, p, preferred_element_type=F32)
        m_ref[c] = m_new

    scores(0, 0)

    def chunk(j):
        scores(1, j)
        accumulate(0, j)
        scores(0, j + 1)
        accumulate(1, j)

    group = 5 if (n_chunks - 1) % 5 == 0 else 1

    def chunk_group(jj, carry):
        for r in range(group):
            chunk(group * jj + r)
        return carry

    lax.fori_loop(0, (n_chunks - 1) // group, chunk_group, 0)
    scores(1, n_chunks - 1)
    accumulate(0, n_chunks - 1)
    accumulate(1, n_chunks - 1)

    lam = (jnp.exp(jnp.sum(lq1_ref[...] * lk1_ref[...], axis=-1, keepdims=True))
           - jnp.exp(jnp.sum(lq2_ref[...] * lk2_ref[...], axis=-1, keepdims=True)) + LAM_INIT)
    a1 = acc_ref[0]
    a2 = acc_ref[1]
    ot = a1[:hw] / a1[hw:hw + 1] - lam * (a2[:hw] / a2[hw:hw + 1])
    ms = jnp.mean(ot * ot, axis=0, keepdims=True)
    ot = ot * lax.rsqrt(ms + NORM_EPS)
    o_ref[0] = (ot.T * (sg_ref[...] * (1.0 - LAM_INIT))).astype(o_ref.dtype)


def _diff_attn(qat, ka, vat, lq1, lk1, lq2, lk2, subln_g, tq, tk):
    b, _, s = qat.shape
    sk = ka.shape[1]
    hw = 2 * HEAD_DIM
    vec = lambda bi, h, i: (0, 0)
    return pl.pallas_call(
        functools.partial(_diff_attn_kernel, tk=tk),
        grid=(b, DA_HEADS, s // tq),
        in_specs=[pl.BlockSpec((1, hw, tq), lambda bi, h, i: (bi, h, i)),
                  pl.BlockSpec((1, sk, hw), lambda bi, h, i: (bi, 0, h)),
                  pl.BlockSpec((1, hw, sk), lambda bi, h, i: (bi, h, 0)),
                  pl.BlockSpec((1, HEAD_DIM), vec),
                  pl.BlockSpec((1, HEAD_DIM), vec),
                  pl.BlockSpec((1, HEAD_DIM), vec),
                  pl.BlockSpec((1, HEAD_DIM), vec),
                  pl.BlockSpec((1, hw), vec)],
        out_specs=pl.BlockSpec((1, tq, hw), lambda bi, h, i: (bi, i, h)),
        out_shape=jax.ShapeDtypeStruct((b, s, DA_HEADS * hw), BF16),
        scratch_shapes=[pltpu.VMEM((2, 1, tq), F32),
                        pltpu.VMEM((2, hw + ONES_ROWS, tq), F32),
                        pltpu.VMEM((2, tk, tq), F32)],
        compiler_params=_cparams(("arbitrary", "arbitrary", "arbitrary")),
        name="diff_attn",
    )(qat, ka, vat, lq1, lk1, lq2, lk2, subln_g)


def _win_attn_kernel(qt_ref, k_ref, vt_ref, kc_ref, vct_ref, sink_ref, o_ref, *, tq, lk):
    d = HEAD_DIM
    grp = WA_GROUP
    s_len = k_ref.shape[1]
    c_len = kc_ref.shape[1]
    nk = lk + c_len
    i = pl.program_id(1)
    q0 = i * tq
    start = pl.multiple_of(jnp.clip(q0 - WINDOW, 0, s_len - lk), LANES)
    keys = jnp.concatenate([k_ref[0, pl.ds(start, lk), :], kc_ref[0]], axis=0)
    kpos = start + lax.broadcasted_iota(jnp.int32, (nk, tq), 0)
    qpos = q0 + lax.broadcasted_iota(jnp.int32, (nk, tq), 1)
    visible = jnp.logical_or(kpos >= start + lk, jnp.abs(kpos - qpos) <= WINDOW)
    visible = jnp.concatenate([visible] * grp, axis=1)
    ones = jnp.ones((ONES_ROWS, nk), BF16)
    qt = qt_ref[0]
    blank = jnp.zeros((d, grp * tq), BF16)
    outs = []
    for kv in range(WA_KV_HEADS):
        heads = range(kv * grp, (kv + 1) * grp)
        qcat = jnp.concatenate([qt[h * d:(h + 1) * d, :] for h in heads], axis=1)
        rhs = jnp.concatenate([qcat if j == kv else blank for j in range(WA_KV_HEADS)], axis=0)
        st = jnp.dot(keys, rhs, preferred_element_type=F32)
        st = jnp.where(visible, st, MASK_VALUE)
        sink = jnp.concatenate([jnp.broadcast_to(sink_ref[:, h:h + 1] * LOG2E, (1, tq)) for h in heads], axis=1)
        m = jnp.maximum(jnp.max(st, axis=0, keepdims=True), sink)
        p = jnp.exp2(st - m).astype(BF16)
        vt = jnp.concatenate([vt_ref[0, kv * d:(kv + 1) * d, pl.ds(start, lk)],
                              vct_ref[0, kv * d:(kv + 1) * d, :]], axis=1)
        acc = jnp.dot(jnp.concatenate([vt, ones], axis=0), p, preferred_element_type=F32)
        o = acc[:d] / (acc[d:d + 1] + jnp.exp2(sink - m))
        outs.extend(o[:, g * tq:(g + 1) * tq] for g in range(grp))
    o_ref[0] = jnp.concatenate(outs, axis=0).T.astype(o_ref.dtype)


def _win_attn(qbt, kb, vbt, kbc, vbct, sink, tq):
    b, _, s = qbt.shape
    c = kbc.shape[1]
    lk = tq + 2 * WINDOW
    assert s >= lk and tq % LANES == 0
    whole = lambda bi, i: (bi, 0, 0)
    return pl.pallas_call(
        functools.partial(_win_attn_kernel, tq=tq, lk=lk),
        grid=(b, s // tq),
        in_specs=[pl.BlockSpec((1, QB_COLS, tq), lambda bi, i: (bi, 0, i)),
                  pl.BlockSpec((1, s, KB_COLS), whole),
                  pl.BlockSpec((1, VB_COLS, s), whole),
                  pl.BlockSpec((1, c, KB_COLS), whole),
                  pl.BlockSpec((1, VB_COLS, c), whole),
                  pl.BlockSpec((1, WA_HEADS), lambda bi, i: (0, 0))],
        out_specs=pl.BlockSpec((1, tq, QB_COLS), lambda bi, i: (bi, i, 0)),
        out_shape=jax.ShapeDtypeStruct((b, s, QB_COLS), BF16),
        compiler_params=_cparams(("arbitrary", "arbitrary")),
        name="win_attn",
    )(qbt, kb, vbt, kbc, vbct, sink)


def _outproj_router_kernel(ya_ref, yb_ref, woa_ref, wob_ref, x_ref, g1_ref, n2_ref, sh_ref, sc_ref,
                           wr_ref, br_ref, x1_ref, hx_ref, code_ref, gate_ref, cnt_ref, carry_ref):
    first = jnp.logical_and(pl.program_id(0) == 0, pl.program_id(1) == 0)

    @pl.when(first)
    def _():
        carry_ref[...] = jnp.zeros(carry_ref.shape, F32)

    y = (jnp.dot(ya_ref[0], woa_ref[...], preferred_element_type=F32)
         + jnp.dot(yb_ref[0], wob_ref[...], preferred_element_type=F32))
    x1 = x_ref[0] + g1_ref[0] * y
    x1_ref[0] = x1
    hx = _modulated_norm(x1, n2_ref[...], sh_ref[0], sc_ref[0])
    _store_row_tiles(hx_ref, hx)
    hx_hi = hx.astype(BF16)
    hx_lo = (hx - hx_hi.astype(F32)).astype(BF16)
    wr = wr_ref[...]
    part = jnp.dot(hx_hi, wr, preferred_element_type=F32)
    logits = (part[:, :N_EXPERTS] + part[:, N_EXPERTS:]
              + jnp.dot(hx_lo, wr[:, :N_EXPERTS], preferred_element_type=F32) + br_ref[...])

    tm = logits.shape[0]
    lane_e = lax.broadcasted_iota(jnp.int32, (tm, N_EXPERTS), 1).astype(F32)
    work = logits
    tops, idxs, hots = [], [], []
    for _k in range(TOP_K):
        m = jnp.max(work, axis=-1, keepdims=True)
        idx = jnp.min(jnp.where(work == m, lane_e, float(N_EXPERTS)), axis=-1, keepdims=True)
        hot = lane_e == idx
        work = jnp.where(hot, -jnp.inf, work)
        tops.append(m)
        idxs.append(idx)
        hots.append(hot)
    es = [jnp.exp(t - tops[0]) for t in tops]
    den = es[0] + es[1] + es[2] + es[3]

    multi = jnp.zeros((tm, N_EXPERTS), F32)
    for hot in hots:
        multi = multi + hot.astype(F32)
    r_i = lax.broadcasted_iota(jnp.int32, (tm, tm), 0)
    c_i = lax.broadcasted_iota(jnp.int32, (tm, tm), 1)
    tri = (c_i <= r_i).astype(BF16)
    incl = jnp.dot(tri, multi.astype(BF16), preferred_element_type=F32)
    before = carry_ref[...] + incl - 1.0

    lane_k = lax.broadcasted_iota(jnp.int32, (tm, TOP_K), 1)
    code = jnp.zeros((tm, TOP_K), jnp.int32)
    gate = jnp.zeros((tm, TOP_K), F32)
    for k in range(TOP_K):
        rank = jnp.sum(jnp.where(hots[k], before, 0.0), axis=-1, keepdims=True)
        ck = idxs[k].astype(jnp.int32) * 65536 + rank.astype(jnp.int32)
        code = jnp.where(lane_k == k, ck, code)
        gate = jnp.where(lane_k == k, es[k] / den, gate)
    code_ref[...] = code
    gate_ref[...] = gate
    carry_ref[...] = carry_ref[...] + jnp.sum(multi, axis=0, keepdims=True)
    cnt_ref[...] = carry_ref[...]


def _outproj_router(ya, yb, woa, wob, x, g1, n2, sh2, sc2, w_r_parts, b_r, tm):
    b, s, d = x.shape
    nb = s // tm
    row = lambda bi, i: (bi, i, 0)
    mod = lambda bi, i: (bi, 0, 0)
    fixed = lambda bi, i: (0, 0)
    tok = lambda bi, i: (bi * nb + i, 0)
    half = ya.shape[2]
    return pl.pallas_call(
        _outproj_router_kernel,
        grid=(b, nb),
        in_specs=[pl.BlockSpec((1, tm, half), row),
                  pl.BlockSpec((1, tm, half), row),
                  pl.BlockSpec((half, d), fixed),
                  pl.BlockSpec((half, d), fixed),
                  pl.BlockSpec((1, tm, d), row),
                  pl.BlockSpec((1, 1, d), mod),
                  pl.BlockSpec((1, d), fixed),
                  pl.BlockSpec((1, 1, d), mod),
                  pl.BlockSpec((1, 1, d), mod),
                  pl.BlockSpec((d, 2 * N_EXPERTS), fixed),
                  pl.BlockSpec((1, N_EXPERTS), fixed)],
        out_specs=[pl.BlockSpec((1, tm, d), row),
                   pl.BlockSpec((tm * ROW_TILE, LANES), tok),
                   pl.BlockSpec((tm, TOP_K), tok),
                   pl.BlockSpec((tm, TOP_K), tok),
                   pl.BlockSpec((1, N_EXPERTS), fixed)],
        out_shape=[jax.ShapeDtypeStruct((b, s, d), F32),
                   jax.ShapeDtypeStruct((b * s * ROW_TILE, LANES), F32),
                   jax.ShapeDtypeStruct((b * s, TOP_K), jnp.int32),
                   jax.ShapeDtypeStruct((b * s, TOP_K), F32),
                   jax.ShapeDtypeStruct((1, N_EXPERTS), F32)],
        scratch_shapes=[pltpu.VMEM((1, N_EXPERTS), F32)],
        compiler_params=_cparams(("arbitrary", "arbitrary")),
        name="outproj_router",
    )(ya, yb, woa, wob, x, g1, n2, sh2, sc2, w_r_parts, b_r)


def _row_copy(src_hbm, src_row, dst_ref, dst_row, sem):
    return pltpu.make_async_copy(src_hbm.at[pl.ds(src_row, 1)], dst_ref.at[pl.ds(dst_row, 1)], sem)


def _store_row_tiles(ref, val):
    n = val.shape[0]
    for c in range(ROW_TILE):
        ref[pl.ds(c, n, stride=ROW_TILE), :] = val[:, c * LANES:(c + 1) * LANES]


def _load_row_tiles(ref, n):
    return [ref[pl.ds(c, n, stride=ROW_TILE), :] for c in range(ROW_TILE)]


def _tile_rows(row, count=1):
    if isinstance(row, int):
        return pl.ds(row * ROW_TILE, count * ROW_TILE)
    return pl.ds(pl.multiple_of(row * ROW_TILE, ROW_TILE), count * ROW_TILE)


def _tile_copy(src_ref, src_row, dst_ref, dst_row, sem):
    return pltpu.make_async_copy(src_ref.at[_tile_rows(src_row)], dst_ref.at[_tile_rows(dst_row)], sem)


def _dispatch_kernel(pbound_ref, cnt_ref, code_ref, hx_ref, xs_hbm, dest_ref, zeros, sem, zsem, *, chunk):
    n_rows = xs_hbm.shape[0] // ROW_TILE
    blk = zeros.shape[0] // ROW_TILE

    @pl.when(pl.program_id(0) == 0)
    def _():
        zeros[...] = jnp.zeros(zeros.shape, F32)
        tail_blocks = (n_rows - pbound_ref[N_EXPERTS]) // blk

        def pad_row(r, carry):
            _tile_copy(zeros, 0, xs_hbm, r, zsem).start()
            return carry

        def pad_expert(e, carry):
            lax.fori_loop(pbound_ref[e] + cnt_ref[e], pbound_ref[e + 1], pad_row, 0)
            return carry

        def tail_copy(t):
            return pltpu.make_async_copy(zeros, xs_hbm.at[_tile_rows(pbound_ref[N_EXPERTS] + t * blk, blk)], zsem)

        def tail_start(t, carry):
            tail_copy(t).start()
            return carry

        lax.fori_loop(0, N_EXPERTS, pad_expert, 0)
        lax.fori_loop(0, tail_blocks, tail_start, 0)

        def pad_row_wait(r, carry):
            _tile_copy(zeros, 0, xs_hbm, 0, zsem).wait()
            return carry

        def pad_expert_wait(e, carry):
            lax.fori_loop(pbound_ref[e] + cnt_ref[e], pbound_ref[e + 1], pad_row_wait, 0)
            return carry

        def tail_wait(t, carry):
            tail_copy(0).wait()
            return carry

        lax.fori_loop(0, N_EXPERTS, pad_expert_wait, 0)
        lax.fori_loop(0, tail_blocks, tail_wait, 0)

    for j in range(chunk):
        cd = code_ref[j // TOP_K, j % TOP_K]
        dst = pbound_ref[cd >> 16] + (cd & 0xFFFF)
        dest_ref[j] = dst
        _tile_copy(hx_ref, j // TOP_K, xs_hbm, dst, sem).start(priority=j % 2)
    pltpu.make_async_copy(xs_hbm.at[_tile_rows(0, chunk)], xs_hbm.at[_tile_rows(0, chunk)], sem).wait()


def _dispatch(pbound, counts_i, code, hx_tiles, n_rows, chunk):
    n_assign = code.shape[0] * TOP_K
    return pl.pallas_call(
        functools.partial(_dispatch_kernel, chunk=chunk),
        grid_spec=pltpu.PrefetchScalarGridSpec(
            num_scalar_prefetch=2,
            grid=(n_assign // chunk,),
            in_specs=[pl.BlockSpec((chunk // TOP_K, TOP_K), lambda i, pb, ct: (i, 0), memory_space=pltpu.SMEM),
                      pl.BlockSpec((chunk // TOP_K * ROW_TILE, LANES), lambda i, pb, ct: (i, 0))],
            out_specs=[pl.BlockSpec(memory_space=pl.ANY),
                       pl.BlockSpec((chunk,), lambda i, pb, ct: (i,), memory_space=pltpu.SMEM)],
            scratch_shapes=[pltpu.VMEM((EXPERT_BLOCK * ROW_TILE, LANES), F32),
                            pltpu.SemaphoreType.DMA(()),
                            pltpu.SemaphoreType.DMA(())]),
        out_shape=[jax.ShapeDtypeStruct((n_rows * ROW_TILE, LANES), F32),
                   jax.ShapeDtypeStruct((n_assign,), jnp.int32)],
        compiler_params=_cparams(("arbitrary",)),
        name="moe_dispatch",
    )(pbound, counts_i, code, hx_tiles)


def _expert_kernel(bexp_ref, run_ref, nxt_ref, nact_ref, xs_ref, wgu_hbm, wdn_hbm, bg_ref, bu_ref, bd_ref,
                   o_ref, wgu_f, wdn_f, wg_s, wu_s, wd_s, wsem):
    i = pl.program_id(0)
    nact = nact_ref[0]

    def weight_copies(expert, w):
        return (pltpu.make_async_copy(wgu_hbm.at[expert], wgu_f.at[w], wsem.at[w, 0]),
                pltpu.make_async_copy(wdn_hbm.at[expert], wdn_f.at[w], wsem.at[w, 1]))

    @pl.when(i == 0)
    def _():
        for cp in weight_copies(bexp_ref[0], 0):
            cp.start()

    @pl.when(i < nact)
    def _():
        changed = jnp.logical_or(i == 0, bexp_ref[i] != bexp_ref[jnp.maximum(i - 1, 0)])

        @pl.when(changed)
        def _():
            w = run_ref[i] % 2
            for cp in weight_copies(bexp_ref[i], w):
                cp.wait()
            half = MXU_COLS // 2
            src = lax.broadcasted_iota(jnp.int32, (MXU_COLS, MXU_COLS), 0)
            dst = lax.broadcasted_iota(jnp.int32, (MXU_COLS, MXU_COLS), 1)
            perm = (src == jnp.where(dst < half, 2 * dst, 2 * (dst - half) + 1)).astype(BF16)
            for k in range(wgu_f.shape[2] // MXU_COLS):
                wk = wgu_f[w, :, k * MXU_COLS:(k + 1) * MXU_COLS].astype(BF16)
                sep = jnp.dot(wk, perm, preferred_element_type=F32).astype(BF16)
                wg_s[:, k * half:(k + 1) * half] = sep[:, :half]
                wu_s[:, k * half:(k + 1) * half] = sep[:, half:]
            wd_s[...] = wdn_f[w].astype(BF16)

            @pl.when(nxt_ref[i] >= 0)
            def _():
                for cp in weight_copies(nxt_ref[i], 1 - w):
                    cp.start()

        xb = jnp.concatenate([c.astype(BF16) for c in _load_row_tiles(xs_ref, EXPERT_BLOCK)], axis=1)
        g = jnp.dot(xb, wg_s[...], preferred_element_type=F32) + bg_ref[0]
        u = jnp.dot(xb, wu_s[...], preferred_element_type=F32) + bu_ref[0]
        g = jnp.minimum(g, SWIGLU_LIMIT)
        u = jnp.clip(u, -SWIGLU_LIMIT, SWIGLU_LIMIT)
        a = g * (1.0 / (1.0 + jnp.exp(-SWIGLU_ALPHA * g))) * (u + 1.0)
        o_ref[...] = jnp.dot(a.astype(BF16), wd_s[...], preferred_element_type=F32) + bd_ref[0]

    @pl.when(i >= nact)
    def _():
        o_ref[...] = jnp.zeros(o_ref.shape, F32)


def _experts(block_exp, run_id, next_exp, n_active, xs, w_gu, w_dn, bg, bu, bd):
    f, d = w_dn.shape[1:]
    n_rows = xs.shape[0] // ROW_TILE
    nblk = n_rows // EXPERT_BLOCK
    bsel = lambda i, be, ru, nx, na: (be[i], 0, 0)
    blk = lambda i, be, ru, nx, na: (i, 0)
    anyspace = pl.BlockSpec(memory_space=pl.ANY)
    return pl.pallas_call(
        _expert_kernel,
        grid_spec=pltpu.PrefetchScalarGridSpec(
            num_scalar_prefetch=4,
            grid=(nblk,),
            in_specs=[pl.BlockSpec((EXPERT_BLOCK * ROW_TILE, LANES), blk),
                      anyspace, anyspace,
                      pl.BlockSpec((1, 1, f), bsel),
                      pl.BlockSpec((1, 1, f), bsel),
                      pl.BlockSpec((1, 1, d), bsel)],
            out_specs=pl.BlockSpec((EXPERT_BLOCK, d), blk),
            scratch_shapes=[pltpu.VMEM((2, d, 2 * f), F32),
                            pltpu.VMEM((2, f, d), F32),
                            pltpu.VMEM((d, f), BF16), pltpu.VMEM((d, f), BF16), pltpu.VMEM((f, d), BF16),
                            pltpu.SemaphoreType.DMA((2, 2))]),
        out_shape=jax.ShapeDtypeStruct((n_rows, d), F32),
        compiler_params=_cparams(("arbitrary",)),
        name="moe_experts",
    )(block_exp, run_id, next_exp, n_active, xs, w_gu, w_dn, bg, bu, bd)


def _combine_kernel(dest_ref, rows_hbm, gate_ref, x1_ref, g2_ref, fg_ref, o_ref, buf, sems, *, tm):
    n = pl.program_id(0)
    slot = n % 2

    def row_gather(step, to_slot, tok, k):
        return _row_copy(rows_hbm, dest_ref[step * (tm * TOP_K) + tok * TOP_K + k], buf.at[to_slot, k], tok,
                         sems.at[to_slot])

    @pl.when(n == 0)
    def _():
        def issue(tok, carry):
            for k in range(TOP_K):
                row_gather(0, 0, tok, k).start()
            return carry
        lax.fori_loop(0, tm, issue, 0, unroll=4)

    @pl.when(n + 1 < pl.num_programs(0))
    def _():
        for tok in range(tm):
            for k in range(TOP_K):
                row_gather(n + 1, 1 - slot, tok, k).start(priority=k % 2)

    for k in range(TOP_K):
        pltpu.make_async_copy(rows_hbm.at[pl.ds(0, tm)], buf.at[slot, k], sems.at[slot]).wait()

    gate = gate_ref[...]
    y = gate[:, 0:1] * buf[slot, 0]
    for k in range(1, TOP_K):
        y = y + gate[:, k:k + 1] * buf[slot, k]
    xo = x1_ref[...] + g2_ref[0] * y
    ms = jnp.mean(xo * xo, axis=-1, keepdims=True)
    o_ref[...] = xo * lax.rsqrt(ms + NORM_EPS) * fg_ref[...]


def _combine(dest, rows, gates, x1, g2, final_g, tm):
    b, s, d = x1.shape
    nb = s // tm
    out = pl.pallas_call(
        functools.partial(_combine_kernel, tm=tm),
        grid_spec=pltpu.PrefetchScalarGridSpec(
            num_scalar_prefetch=1,
            grid=(b * nb,),
            in_specs=[pl.BlockSpec(memory_space=pl.ANY),
                      pl.BlockSpec((tm, TOP_K), lambda n, ds: (n, 0)),
                      pl.BlockSpec((tm, d), lambda n, ds: (n, 0)),
                      pl.BlockSpec((1, 1, d), lambda n, ds: (n // nb, 0, 0)),
                      pl.BlockSpec((1, d), lambda n, ds: (0, 0))],
            out_specs=pl.BlockSpec((tm, d), lambda n, ds: (n, 0)),
            scratch_shapes=[pltpu.VMEM((2, TOP_K, tm, d), F32), pltpu.SemaphoreType.DMA((2,))]),
        out_shape=jax.ShapeDtypeStruct((b * s, d), F32),
        compiler_params=_cparams(("arbitrary",)),
        name="moe_combine",
    )(dest, rows, gates, x1.reshape(b * s, d), g2, final_g)
    return out.reshape(b, s, d)


def _pick(n, prefs):
    for p in prefs:
        if n % p == 0:
            return p
    raise ValueError(f"no tile for {n}")


def kernel(x, c, ctx, c_ctx, w_mod, b_mod, norm1_g, w_in, lam_q1, lam_k1, lam_q2, lam_k2, subln_g,
           sink, w_out, norm2_g, w_router, b_router, w_gate_up, b_gate_up, w_down, b_down, final_g):
    b, s, d = x.shape
    c_len = ctx.shape[1]
    assert w_mod.shape[0] == 1, "single-layer block"
    assert d == ROW_TILE * LANES, "dispatched token rows are one (8, 128) f32 tile each"
    t = b * s

    pad = (-(b + 1)) % 8
    cvecs = jnp.concatenate([c, c_ctx[None, :], jnp.zeros((pad, d), F32)], axis=0)
    mod = _adaln(cvecs, w_mod[0], b_mod[0])
    sh1, sc1, g1, sh2, sc2, g2 = [mod[:b, k * d:(k + 1) * d].reshape(b, 1, d) for k in range(6)]
    csh1 = mod[b:b + 1, 0:d].reshape(1, 1, d)
    csc1 = mod[b:b + 1, d:2 * d].reshape(1, 1, d)

    w_in_bf = w_in[0].astype(BF16)
    cos, sin = _rope_tables(s)
    n1 = norm1_g[0].reshape(1, d)
    qat, ka, vat, qbt, kb, vbt = _inproj_latent(x, n1, sh1, sc1, w_in_bf, cos, sin, _pick(s, (512, 256, 128)))
    w_ctx_bf = jnp.concatenate([w_in_bf[:, O_KA:O_QB], w_in_bf[:, O_KB:IN_COLS]], axis=1)
    kac, vact, kbc, vbct = _inproj_ctx(ctx, n1, csh1, csc1, w_ctx_bf)

    ka_all = jnp.concatenate([ka, kac], axis=1)
    vat_all = jnp.concatenate([vat, vact], axis=2)
    sk = s + c_len
    ya = _diff_attn(qat, ka_all, vat_all,
                    lam_q1[0].reshape(1, -1), lam_k1[0].reshape(1, -1),
                    lam_q2[0].reshape(1, -1), lam_k2[0].reshape(1, -1),
                    subln_g[0].reshape(1, -1),
                    _pick(s, (1024, 512, 256, 128)), _pick(sk, (768, 512, 384, 256, 128)))

    yb = _win_attn(qbt, kb, vbt, kbc, vbct, sink[0].reshape(1, -1), _pick(s, (256, 128)))

    w_out_bf = w_out[0].astype(BF16)
    w_r_hi = w_router[0].astype(BF16)
    w_r_lo = (w_router[0] - w_r_hi.astype(F32)).astype(BF16)
    x1, hx2, code, gates, counts = _outproj_router(
        ya, yb, w_out_bf[:QA_COLS], w_out_bf[QA_COLS:], x, g1, norm2_g[0].reshape(1, d), sh2, sc2,
        jnp.concatenate([w_r_hi, w_r_lo], axis=1), b_router[0].reshape(1, -1), _pick(s, (256, 128)))

    n_assign = t * TOP_K
    n_rows = n_assign + N_EXPERTS * EXPERT_BLOCK
    counts_i = counts.reshape(-1).astype(jnp.int32)
    padded = ((counts_i + EXPERT_BLOCK - 1) // EXPERT_BLOCK) * EXPERT_BLOCK
    pend = jnp.cumsum(padded).astype(jnp.int32)
    pbound = jnp.concatenate([jnp.zeros((1,), jnp.int32), pend])
    block_start = jnp.arange(n_rows // EXPERT_BLOCK, dtype=jnp.int32) * EXPERT_BLOCK
    block_exp = jnp.minimum(jnp.sum((pend[None, :] <= block_start[:, None]).astype(jnp.int32), axis=1),
                            N_EXPERTS - 1)
    n_active = pend[-1:] // EXPERT_BLOCK
    changed = jnp.concatenate([jnp.ones((1,), jnp.int32), (block_exp[1:] != block_exp[:-1]).astype(jnp.int32)])
    run_id = jnp.cumsum(changed).astype(jnp.int32) - 1
    e_ids = jnp.arange(N_EXPERTS, dtype=jnp.int32)
    later_nonempty = (e_ids[None, :] > e_ids[:, None]) & (padded[None, :] > 0)
    next_of_expert = jnp.min(jnp.where(later_nonempty, e_ids[None, :], N_EXPERTS), axis=1)
    next_of_expert = jnp.where(next_of_expert < N_EXPERTS, next_of_expert, -1).astype(jnp.int32)
    next_exp = next_of_expert[block_exp]

    xs, dest = _dispatch(pbound, counts_i, code, hx2, n_rows, _pick(n_assign, (1024,)))

    f = w_down.shape[2]
    bg = b_gate_up[0, :, 0::2].reshape(N_EXPERTS, 1, f)
    bu = b_gate_up[0, :, 1::2].reshape(N_EXPERTS, 1, f)
    bd = b_down[0].reshape(N_EXPERTS, 1, d)
    rows = _experts(block_exp, run_id, next_exp, n_active, xs, w_gate_up[0], w_down[0], bg, bu, bd)

    return _combine(dest, rows, gates, x1, g2, final_g.reshape(1, d), _pick(s, (256,)))
```

```python
import functools
import math

import jax
import jax.numpy as jnp
import numpy as np
from jax import lax
from jax.experimental import pallas as pl
from jax.experimental.pallas import tpu as pltpu

F32 = jnp.float32
BF16 = jnp.bfloat16
HIGHEST = lax.Precision.HIGHEST

GRID_W = 64
NORM_EPS = 1e-6
ROPE_BASE = 10000.0
MASK_VALUE = -1e30
DA_HEADS = 4
HEAD_DIM = 64
WA_HEADS = 8
WA_KV_HEADS = 2
WA_GROUP = WA_HEADS // WA_KV_HEADS
WINDOW = 128
N_EXPERTS = 32
TOP_K = 4
SWIGLU_LIMIT = 7.0
SWIGLU_ALPHA = 1.702
EXPERT_BLOCK = 256
LAM_INIT = 0.8 - 0.6 * math.exp(-0.3 * 0)
LOG2E = math.log2(math.e)

QA_COLS = DA_HEADS * 2 * HEAD_DIM
KA_COLS = QA_COLS
VA_COLS = QA_COLS
QB_COLS = WA_HEADS * HEAD_DIM
KB_COLS = WA_KV_HEADS * HEAD_DIM
VB_COLS = KB_COLS
O_QA = 0
O_KA = O_QA + QA_COLS
O_VA = O_KA + KA_COLS
O_QB = O_VA + VA_COLS
O_KB = O_QB + QB_COLS
O_VB = O_KB + KB_COLS
IN_COLS = O_VB + VB_COLS

LANES = 128
ROW_TILE = 8
MXU_COLS = 256
ONES_ROWS = 16
VMEM_LIMIT = 56 * 1024 * 1024


def _cparams(sem):
    return pltpu.CompilerParams(dimension_semantics=sem, vmem_limit_bytes=VMEM_LIMIT)


def _adaln_kernel(c_ref, w_ref, b_ref, o_ref):
    cv = c_ref[...]
    s = cv * (1.0 / (1.0 + jnp.exp(-cv)))
    o_ref[...] = jnp.dot(s, w_ref[...], precision=HIGHEST, preferred_element_type=F32) + b_ref[...]


def _adaln(cvecs, w_mod, b_mod):
    rows, d = cvecs.shape
    n = w_mod.shape[1]
    tn = 1024
    return pl.pallas_call(
        _adaln_kernel,
        grid=(n // tn,),
        in_specs=[pl.BlockSpec((rows, d), lambda j: (0, 0)),
                  pl.BlockSpec((d, tn), lambda j: (0, j)),
                  pl.BlockSpec((1, tn), lambda j: (0, j))],
        out_specs=pl.BlockSpec((rows, tn), lambda j: (0, j)),
        out_shape=jax.ShapeDtypeStruct((rows, n), F32),
        compiler_params=_cparams(("arbitrary",)),
        name="adaln",
    )(cvecs, w_mod, b_mod.reshape(1, n))


def _rope_section(sec, cos, sin):
    tm = sec.shape[0]
    lane = lax.broadcasted_iota(jnp.int32, (tm, LANES), 1)
    low = (lane % 32) < 16
    outs = []
    for j in range(sec.shape[1] // LANES):
        c = sec[:, j * LANES:(j + 1) * LANES]
        partner = jnp.where(low, pltpu.roll(c, LANES - 16, 1), pltpu.roll(c, 16, 1))
        outs.append(c * cos + partner * sin)
    return jnp.concatenate(outs, axis=1)


def _modulated_norm(x, g, shift, scale):
    ms = jnp.mean(x * x, axis=-1, keepdims=True)
    return (x * lax.rsqrt(ms + NORM_EPS) * g) * (1.0 + scale) + shift


def _inproj_latent_kernel(x_ref, g_ref, sh_ref, sc_ref, w_ref, cos_ref, sin_ref,
                          qat_ref, ka_ref, vat_ref, qbt_ref, kb_ref, vbt_ref):
    h = _modulated_norm(x_ref[0], g_ref[...], sh_ref[0], sc_ref[0])
    p = jnp.dot(h.astype(BF16), w_ref[...], preferred_element_type=F32)
    cos = cos_ref[...]
    sin = sin_ref[...]
    qscale = HEAD_DIM ** -0.5 * LOG2E
    qat_ref[0] = (_rope_section(p[:, O_QA:O_KA], cos, sin) * qscale).T.astype(BF16)
    ka_ref[0] = _rope_section(p[:, O_KA:O_VA], cos, sin).astype(BF16)
    vat_ref[0] = p[:, O_VA:O_QB].T.astype(BF16)
    qbt_ref[0] = (_rope_section(p[:, O_QB:O_KB], cos, sin) * qscale).T.astype(BF16)
    kb_ref[0] = _rope_section(p[:, O_KB:O_VB], cos, sin).astype(BF16)
    vbt_ref[0] = p[:, O_VB:IN_COLS].T.astype(BF16)


def _inproj_ctx_kernel(x_ref, g_ref, sh_ref, sc_ref, w_ref, ka_ref, vat_ref, kb_ref, vbt_ref):
    h = _modulated_norm(x_ref[0], g_ref[...], sh_ref[0], sc_ref[0])
    p = jnp.dot(h.astype(BF16), w_ref[...], preferred_element_type=F32)
    ka_ref[0] = p[:, 0:KA_COLS].astype(BF16)
    vat_ref[0] = p[:, KA_COLS:KA_COLS + VA_COLS].T.astype(BF16)
    kb_ref[0] = p[:, KA_COLS + VA_COLS:KA_COLS + VA_COLS + KB_COLS].astype(BF16)
    vbt_ref[0] = p[:, KA_COLS + VA_COLS + KB_COLS:].T.astype(BF16)


def _rope_tables(n_tok):
    pos = np.arange(n_tok)
    nf = HEAD_DIM // 4
    inv = ROPE_BASE ** (-np.arange(nf) / nf)
    ar = (pos // GRID_W)[:, None] * inv
    ac = (pos % GRID_W)[:, None] * inv
    cos = np.concatenate([np.cos(ar), np.cos(ar), np.cos(ac), np.cos(ac)], axis=1)
    sin = np.concatenate([-np.sin(ar), np.sin(ar), -np.sin(ac), np.sin(ac)], axis=1)
    reps = (1, LANES // HEAD_DIM)
    return jnp.asarray(np.tile(cos, reps), F32), jnp.asarray(np.tile(sin, reps), F32)


def _inproj_latent(x, g, shift, scale, w_bf16, cos, sin, tm):
    b, s, d = x.shape
    row = lambda bi, i: (bi, i, 0)
    colt = lambda bi, i: (bi, 0, i)
    mod = lambda bi, i: (bi, 0, 0)
    fixed = lambda bi, i: (0, 0)
    return pl.pallas_call(
        _inproj_latent_kernel,
        grid=(b, s // tm),
        in_specs=[pl.BlockSpec((1, tm, d), row),
                  pl.BlockSpec((1, d), fixed),
                  pl.BlockSpec((1, 1, d), mod),
                  pl.BlockSpec((1, 1, d), mod),
                  pl.BlockSpec((d, IN_COLS), fixed),
                  pl.BlockSpec((tm, LANES), lambda bi, i: (i, 0)),
                  pl.BlockSpec((tm, LANES), lambda bi, i: (i, 0))],
        out_specs=[pl.BlockSpec((1, QA_COLS, tm), colt),
                   pl.BlockSpec((1, tm, KA_COLS), row),
                   pl.BlockSpec((1, VA_COLS, tm), colt),
                   pl.BlockSpec((1, QB_COLS, tm), colt),
                   pl.BlockSpec((1, tm, KB_COLS), row),
                   pl.BlockSpec((1, VB_COLS, tm), colt)],
        out_shape=[jax.ShapeDtypeStruct((b, QA_COLS, s), BF16),
                   jax.ShapeDtypeStruct((b, s, KA_COLS), BF16),
                   jax.ShapeDtypeStruct((b, VA_COLS, s), BF16),
                   jax.ShapeDtypeStruct((b, QB_COLS, s), BF16),
                   jax.ShapeDtypeStruct((b, s, KB_COLS), BF16),
                   jax.ShapeDtypeStruct((b, VB_COLS, s), BF16)],
        compiler_params=_cparams(("arbitrary", "arbitrary")),
        name="inproj_latent",
    )(x, g, shift, scale, w_bf16, cos, sin)


def _inproj_ctx(ctx, g, shift, scale, w_ctx_bf16):
    b, c, d = ctx.shape
    n = w_ctx_bf16.shape[1]
    whole = lambda bi: (bi, 0, 0)
    mod = lambda bi: (0, 0, 0)
    fixed = lambda bi: (0, 0)
    return pl.pallas_call(
        _inproj_ctx_kernel,
        grid=(b,),
        in_specs=[pl.BlockSpec((1, c, d), whole),
                  pl.BlockSpec((1, d), fixed),
                  pl.BlockSpec((1, 1, d), mod),
                  pl.BlockSpec((1, 1, d), mod),
                  pl.BlockSpec((d, n), fixed)],
        out_specs=[pl.BlockSpec((1, c, KA_COLS), whole),
                   pl.BlockSpec((1, VA_COLS, c), whole),
                   pl.BlockSpec((1, c, KB_COLS), whole),
                   pl.BlockSpec((1, VB_COLS, c), whole)],
        out_shape=[jax.ShapeDtypeStruct((b, c, KA_COLS), BF16),
                   jax.ShapeDtypeStruct((b, VA_COLS, c), BF16),
                   jax.ShapeDtypeStruct((b, c, KB_COLS), BF16),
                   jax.ShapeDtypeStruct((b, VB_COLS, c), BF16)],
        compiler_params=_cparams(("arbitrary",)),
        name="inproj_ctx",
    )(ctx, g, shift, scale, w_ctx_bf16)


def _diff_attn_kernel(qt_ref, k_ref, vt_ref, lq1_ref, lk1_ref, lq2_ref, lk2_ref, sg_ref, o_ref,
                      m_ref, acc_ref, s_ref, *, tk):
    d = HEAD_DIM
    hw = 2 * d
    qt = qt_ref[0]
    row = lax.broadcasted_iota(jnp.int32, qt.shape, 0)
    zero = jnp.zeros_like(qt)
    rhs = (jnp.where(row < d, qt, zero), jnp.where(row >= d, qt, zero))
    n_chunks = k_ref.shape[1] // tk
    m_ref[...] = jnp.full(m_ref.shape, -jnp.inf, F32)
    acc_ref[...] = jnp.zeros(acc_ref.shape, F32)
    ones = jnp.ones((ONES_ROWS, tk), BF16)

    def scores(c, j):
        off = pl.multiple_of(j * tk, tk)
        s_ref[c] = jnp.dot(k_ref[0, pl.ds(off, tk), :], rhs[c], preferred_element_type=F32)

    def accumulate(c, j):
        off = pl.multiple_of(j * tk, tk)
        vt = jnp.concatenate([vt_ref[0, :, pl.ds(off, tk)], ones], axis=0)
        st = s_ref[c]
        m_old = m_ref[c]
        m_new = jnp.maximum(m_old, jnp.max(st, axis=0, keepdims=True))
        alpha = jnp.exp2(m_old - m_new)
        p = jnp.exp2(st - m_new).astype(BF16)
        acc_ref[c] = alpha * acc_ref[c] + jnp.dot(vt, p, preferred_element_type=F32)
        m_ref[c] = m_new

    scores(0, 0)

    def chunk(j):
        scores(1, j)
        accumulate(0, j)
        scores(0, j + 1)
        accumulate(1, j)

    group = 5 if (n_chunks - 1) % 5 == 0 else 1

    def chunk_group(jj, carry):
        for r in range(group):
            chunk(group * jj + r)
        return carry

    lax.fori_loop(0, (n_chunks - 1) // group, chunk_group, 0)
    scores(1, n_chunks - 1)
    accumulate(0, n_chunks - 1)
    accumulate(1, n_chunks - 1)

    lam = (jnp.exp(jnp.sum(lq1_ref[...] * lk1_ref[...], axis=-1, keepdims=True))
           - jnp.exp(jnp.sum(lq2_ref[...] * lk2_ref[...], axis=-1, keepdims=True)) + LAM_INIT)
    a1 = acc_ref[0]
    a2 = acc_ref[1]
    ot = a1[:hw] / a1[hw:hw + 1] - lam * (a2[:hw] / a2[hw:hw + 1])
    ms = jnp.mean(ot * ot, axis=0, keepdims=True)
    ot = ot * lax.rsqrt(ms + NORM_EPS)
    o_ref[0] = (ot.T * (sg_ref[...] * (1.0 - LAM_INIT))).astype(o_ref.dtype)


def _diff_attn(qat, ka, vat, lq1, lk1, lq2, lk2, subln_g, tq, tk):
    b, _, s = qat.shape
    sk = ka.shape[1]
    hw = 2 * HEAD_DIM
    vec = lambda bi, h, i: (0, 0)
    return pl.pallas_call(
        functools.partial(_diff_attn_kernel, tk=tk),
        grid=(b, DA_HEADS, s // tq),
        in_specs=[pl.BlockSpec((1, hw, tq), lambda bi, h, i: (bi, h, i)),
                  pl.BlockSpec((1, sk, hw), lambda bi, h, i: (bi, 0, h)),
                  pl.BlockSpec((1, hw, sk), lambda bi, h, i: (bi, h, 0)),
                  pl.BlockSpec((1, HEAD_DIM), vec),
                  pl.BlockSpec((1, HEAD_DIM), vec),
                  pl.BlockSpec((1, HEAD_DIM), vec),
                  pl.BlockSpec((1, HEAD_DIM), vec),
                  pl.BlockSpec((1, hw), vec)],
        out_specs=pl.BlockSpec((1, tq, hw), lambda bi, h, i: (bi, i, h)),
        out_shape=jax.ShapeDtypeStruct((b, s, DA_HEADS * hw), BF16),
        scratch_shapes=[pltpu.VMEM((2, 1, tq), F32),
                        pltpu.VMEM((2, hw + ONES_ROWS, tq), F32),
                        pltpu.VMEM((2, tk, tq), F32)],
        compiler_params=_cparams(("arbitrary", "arbitrary", "arbitrary")),
        name="diff_attn",
    )(qat, ka, vat, lq1, lk1, lq2, lk2, subln_g)


def _win_attn_kernel(qt_ref, k_ref, vt_ref, kc_ref, vct_ref, sink_ref, o_ref, *, tq, lk):
    d = HEAD_DIM
    grp = WA_GROUP
    s_len = k_ref.shape[1]
    c_len = kc_ref.shape[1]
    nk = lk + c_len
    i = pl.program_id(1)
    q0 = i * tq
    start = pl.multiple_of(jnp.clip(q0 - WINDOW, 0, s_len - lk), LANES)
    keys = jnp.concatenate([k_ref[0, pl.ds(start, lk), :], kc_ref[0]], axis=0)
    kpos = start + lax.broadcasted_iota(jnp.int32, (nk, tq), 0)
    qpos = q0 + lax.broadcasted_iota(jnp.int32, (nk, tq), 1)
    visible = jnp.logical_or(kpos >= start + lk, jnp.abs(kpos - qpos) <= WINDOW)
    visible = jnp.concatenate([visible] * grp, axis=1)
    ones = jnp.ones((ONES_ROWS, nk), BF16)
    qt = qt_ref[0]
    blank = jnp.zeros((d, grp * tq), BF16)
    outs = []
    for kv in range(WA_KV_HEADS):
        heads = range(kv * grp, (kv + 1) * grp)
        qcat = jnp.concatenate([qt[h * d:(h + 1) * d, :] for h in heads], axis=1)
        rhs = jnp.concatenate([qcat if j == kv else blank for j in range(WA_KV_HEADS)], axis=0)
        st = jnp.dot(keys, rhs, preferred_element_type=F32)
        st = jnp.where(visible, st, MASK_VALUE)
        sink = jnp.concatenate([jnp.broadcast_to(sink_ref[:, h:h + 1] * LOG2E, (1, tq)) for h in heads], axis=1)
        m = jnp.maximum(jnp.max(st, axis=0, keepdims=True), sink)
        p = jnp.exp2(st - m).astype(BF16)
        vt = jnp.concatenate([vt_ref[0, kv * d:(kv + 1) * d, pl.ds(start, lk)],
                              vct_ref[0, kv * d:(kv + 1) * d, :]], axis=1)
        acc = jnp.dot(jnp.concatenate([vt, ones], axis=0), p, preferred_element_type=F32)
        o = acc[:d] / (acc[d:d + 1] + jnp.exp2(sink - m))
        outs.extend(o[:, g * tq:(g + 1) * tq] for g in range(grp))
    o_ref[0] = jnp.concatenate(outs, axis=0).T.astype(o_ref.dtype)


def _win_attn(qbt, kb, vbt, kbc, vbct, sink, tq):
    b, _, s = qbt.shape
    c = kbc.shape[1]
    lk = tq + 2 * WINDOW
    assert s >= lk and tq % LANES == 0
    whole = lambda bi, i: (bi, 0, 0)
    return pl.pallas_call(
        functools.partial(_win_attn_kernel, tq=tq, lk=lk),
        grid=(b, s // tq),
        in_specs=[pl.BlockSpec((1, QB_COLS, tq), lambda bi, i: (bi, 0, i)),
                  pl.BlockSpec((1, s, KB_COLS), whole),
                  pl.BlockSpec((1, VB_COLS, s), whole),
                  pl.BlockSpec((1, c, KB_COLS), whole),
                  pl.BlockSpec((1, VB_COLS, c), whole),
                  pl.BlockSpec((1, WA_HEADS), lambda bi, i: (0, 0))],
        out_specs=pl.BlockSpec((1, tq, QB_COLS), lambda bi, i: (bi, i, 0)),
        out_shape=jax.ShapeDtypeStruct((b, s, QB_COLS), BF16),
        compiler_params=_cparams(("arbitrary", "arbitrary")),
        name="win_attn",
    )(qbt, kb, vbt, kbc, vbct, sink)


def _outproj_router_kernel(ya_ref, yb_ref, woa_ref, wob_ref, x_ref, g1_ref, n2_ref, sh_ref, sc_ref,
                           wr_ref, br_ref, x1_ref, hx_ref, code_ref, gate_ref, cnt_ref, carry_ref):
    first = jnp.logical_and(pl.program_id(0) == 0, pl.program_id(1) == 0)

    @pl.when(first)
    def _():
        carry_ref[...] = jnp.zeros(carry_ref.shape, F32)

    y = (jnp.dot(ya_ref[0], woa_ref[...], preferred_element_type=F32)
         + jnp.dot(yb_ref[0], wob_ref[...], preferred_element_type=F32))
    x1 = x_ref[0] + g1_ref[0] * y
    x1_ref[0] = x1
    hx = _modulated_norm(x1, n2_ref[...], sh_ref[0], sc_ref[0])
    _store_row_tiles(hx_ref, hx)
    tm = hx.shape[0]
    hx_hi = hx.astype(BF16)
    hx_lo = (hx - hx_hi.astype(F32)).astype(BF16)
    wr = wr_ref[...]
    nt = (((1,), (1,)), ((), ()))
    part = lax.dot_general(wr, hx_hi, nt, preferred_element_type=F32)
    logits = (part[:N_EXPERTS] + part[N_EXPERTS:]
              + lax.dot_general(wr[:N_EXPERTS], hx_lo, nt, preferred_element_type=F32) + br_ref[...])

    row_e = lax.broadcasted_iota(jnp.int32, (N_EXPERTS, tm), 0).astype(F32)
    work = logits
    tops, idxs, hots = [], [], []
    for _k in range(TOP_K):
        m = jnp.max(work, axis=0, keepdims=True)
        idx = jnp.min(jnp.where(work == m, row_e, float(N_EXPERTS)), axis=0, keepdims=True)
        hot = row_e == idx
        work = jnp.where(hot, -jnp.inf, work)
        tops.append(m)
        idxs.append(idx)
        hots.append(hot)
    es = [jnp.exp(t - tops[0]) for t in tops]
    den = es[0] + es[1] + es[2] + es[3]

    multi = jnp.zeros((N_EXPERTS, tm), F32)
    for hot in hots:
        multi = multi + hot.astype(F32)
    r_i = lax.broadcasted_iota(jnp.int32, (tm, tm), 0)
    c_i = lax.broadcasted_iota(jnp.int32, (tm, tm), 1)
    tri = (r_i <= c_i).astype(BF16)
    incl = jnp.dot(multi.astype(BF16), tri, preferred_element_type=F32)
    before = carry_ref[...] + incl - 1.0

    codes, gates = [], []
    for k in range(TOP_K):
        rank = jnp.sum(jnp.where(hots[k], before, 0.0), axis=0, keepdims=True)
        codes.append(idxs[k].astype(jnp.int32) * 65536 + rank.astype(jnp.int32))
        gates.append(es[k] / den)
    fill = ROW_TILE - TOP_K
    code_ref[...] = jnp.concatenate(codes + [jnp.zeros((fill, tm), jnp.int32)], axis=0)
    gate_ref[...] = jnp.concatenate(gates + [jnp.zeros((fill, tm), F32)], axis=0)
    carry_ref[...] = carry_ref[...] + jnp.sum(multi, axis=1, keepdims=True)
    cnt_ref[...] = carry_ref[...]


def _outproj_router(ya, yb, woa, wob, x, g1, n2, sh2, sc2, w_r_parts, b_r, tm):
    b, s, d = x.shape
    nb = s // tm
    row = lambda bi, i: (bi, i, 0)
    mod = lambda bi, i: (bi, 0, 0)
    fixed = lambda bi, i: (0, 0)
    tok = lambda bi, i: (bi * nb + i, 0)
    half = ya.shape[2]
    return pl.pallas_call(
        _outproj_router_kernel,
        grid=(b, nb),
        in_specs=[pl.BlockSpec((1, tm, half), row),
                  pl.BlockSpec((1, tm, half), row),
                  pl.BlockSpec((half, d), fixed),
                  pl.BlockSpec((half, d), fixed),
                  pl.BlockSpec((1, tm, d), row),
                  pl.BlockSpec((1, 1, d), mod),
                  pl.BlockSpec((1, d), fixed),
                  pl.BlockSpec((1, 1, d), mod),
                  pl.BlockSpec((1, 1, d), mod),
                  pl.BlockSpec((2 * N_EXPERTS, d), fixed),
                  pl.BlockSpec((N_EXPERTS, 1), fixed)],
        out_specs=[pl.BlockSpec((1, tm, d), row),
                   pl.BlockSpec((tm * ROW_TILE, LANES), tok),
                   pl.BlockSpec((ROW_TILE, tm), lambda bi, i: (0, bi * nb + i)),
                   pl.BlockSpec((ROW_TILE, tm), lambda bi, i: (0, bi * nb + i)),
                   pl.BlockSpec((N_EXPERTS, 1), fixed)],
        out_shape=[jax.ShapeDtypeStruct((b, s, d), F32),
                   jax.ShapeDtypeStruct((b * s * ROW_TILE, LANES), F32),
                   jax.ShapeDtypeStruct((ROW_TILE, b * s), jnp.int32),
                   jax.ShapeDtypeStruct((ROW_TILE, b * s), F32),
                   jax.ShapeDtypeStruct((N_EXPERTS, 1), F32)],
        scratch_shapes=[pltpu.VMEM((N_EXPERTS, 1), F32)],
        compiler_params=_cparams(("arbitrary", "arbitrary")),
        name="outproj_router",
    )(ya, yb, woa, wob, x, g1, n2, sh2, sc2, w_r_parts, b_r)


def _row_copy(src_hbm, src_row, dst_ref, dst_row, sem):
    return pltpu.make_async_copy(src_hbm.at[pl.ds(src_row, 1)], dst_ref.at[pl.ds(dst_row, 1)], sem)


def _store_row_tiles(ref, val):
    n = val.shape[0]
    for c in range(ROW_TILE):
        ref[pl.ds(c, n, stride=ROW_TILE), :] = val[:, c * LANES:(c + 1) * LANES]


def _load_row_tiles(ref, n):
    return [ref[pl.ds(c, n, stride=ROW_TILE), :] for c in range(ROW_TILE)]


def _tile_rows(row, count=1):
    if isinstance(row, int):
        return pl.ds(row * ROW_TILE, count * ROW_TILE)
    return pl.ds(pl.multiple_of(row * ROW_TILE, ROW_TILE), count * ROW_TILE)


def _tile_copy(src_ref, src_row, dst_ref, dst_row, sem):
    return pltpu.make_async_copy(src_ref.at[_tile_rows(src_row)], dst_ref.at[_tile_rows(dst_row)], sem)


def _dispatch_kernel(pbound_ref, cnt_ref, code_ref, hx_ref, xs_hbm, dest_ref, zeros, sem, zsem, *, chunk):
    n_rows = xs_hbm.shape[0] // ROW_TILE
    blk = zeros.shape[0] // ROW_TILE

    @pl.when(pl.program_id(0) == 0)
    def _():
        zeros[...] = jnp.zeros(zeros.shape, F32)
        tail_blocks = (n_rows - pbound_ref[N_EXPERTS]) // blk

        def pad_row(r, carry):
            _tile_copy(zeros, 0, xs_hbm, r, zsem).start()
            return carry

        def pad_expert(e, carry):
            lax.fori_loop(pbound_ref[e] + cnt_ref[e], pbound_ref[e + 1], pad_row, 0)
            return carry

        def tail_copy(t):
            return pltpu.make_async_copy(zeros, xs_hbm.at[_tile_rows(pbound_ref[N_EXPERTS] + t * blk, blk)], zsem)

        def tail_start(t, carry):
            tail_copy(t).start()
            return carry

        lax.fori_loop(0, N_EXPERTS, pad_expert, 0)
        lax.fori_loop(0, tail_blocks, tail_start, 0)

        def pad_row_wait(r, carry):
            _tile_copy(zeros, 0, xs_hbm, 0, zsem).wait()
            return carry

        def pad_expert_wait(e, carry):
            lax.fori_loop(pbound_ref[e] + cnt_ref[e], pbound_ref[e + 1], pad_row_wait, 0)
            return carry

        def tail_wait(t, carry):
            tail_copy(0).wait()
            return carry

        lax.fori_loop(0, N_EXPERTS, pad_expert_wait, 0)
        lax.fori_loop(0, tail_blocks, tail_wait, 0)

    for j in range(chunk):
        cd = code_ref[j % TOP_K, j // TOP_K]
        dst = pbound_ref[cd >> 16] + (cd & 0xFFFF)
        dest_ref[j] = dst
        _tile_copy(hx_ref, j // TOP_K, xs_hbm, dst, sem).start(priority=j % 2)
    pltpu.make_async_copy(xs_hbm.at[_tile_rows(0, chunk)], xs_hbm.at[_tile_rows(0, chunk)], sem).wait()


def _dispatch(pbound, counts_i, code, hx_tiles, n_rows, chunk):
    n_assign = code.shape[1] * TOP_K
    return pl.pallas_call(
        functools.partial(_dispatch_kernel, chunk=chunk),
        grid_spec=pltpu.PrefetchScalarGridSpec(
            num_scalar_prefetch=2,
            grid=(n_assign // chunk,),
            in_specs=[pl.BlockSpec((ROW_TILE, chunk // TOP_K), lambda i, pb, ct: (0, i), memory_space=pltpu.SMEM),
                      pl.BlockSpec((chunk // TOP_K * ROW_TILE, LANES), lambda i, pb, ct: (i, 0))],
            out_specs=[pl.BlockSpec(memory_space=pl.ANY),
                       pl.BlockSpec((chunk,), lambda i, pb, ct: (i,), memory_space=pltpu.SMEM)],
            scratch_shapes=[pltpu.VMEM((EXPERT_BLOCK * ROW_TILE, LANES), F32),
                            pltpu.SemaphoreType.DMA(()),
                            pltpu.SemaphoreType.DMA(())]),
        out_shape=[jax.ShapeDtypeStruct((n_rows * ROW_TILE, LANES), F32),
                   jax.ShapeDtypeStruct((n_assign,), jnp.int32)],
        compiler_params=_cparams(("arbitrary",)),
        name="moe_dispatch",
    )(pbound, counts_i, code, hx_tiles)


def _expert_kernel(bexp_ref, run_ref, nxt_ref, nact_ref, xs_ref, wgu_hbm, wdn_hbm, bg_ref, bu_ref, bd_ref,
                   o_ref, wgu_f, wdn_f, wg_s, wu_s, wd_s, wsem):
    i = pl.program_id(0)
    nact = nact_ref[0]

    def weight_copies(expert, w):
        return (pltpu.make_async_copy(wgu_hbm.at[expert], wgu_f.at[w], wsem.at[w, 0]),
                pltpu.make_async_copy(wdn_hbm.at[expert], wdn_f.at[w], wsem.at[w, 1]))

    @pl.when(i == 0)
    def _():
        for cp in weight_copies(bexp_ref[0], 0):
            cp.start()

    @pl.when(i < nact)
    def _():
        changed = jnp.logical_or(i == 0, bexp_ref[i] != bexp_ref[jnp.maximum(i - 1, 0)])

        @pl.when(changed)
        def _():
            w = run_ref[i] % 2
            for cp in weight_copies(bexp_ref[i], w):
                cp.wait()
            half = MXU_COLS // 2
            src = lax.broadcasted_iota(jnp.int32, (MXU_COLS, MXU_COLS), 0)
            dst = lax.broadcasted_iota(jnp.int32, (MXU_COLS, MXU_COLS), 1)
            perm = (src == jnp.where(dst < half, 2 * dst, 2 * (dst - half) + 1)).astype(BF16)
            for k in range(wgu_f.shape[2] // MXU_COLS):
                wk = wgu_f[w, :, k * MXU_COLS:(k + 1) * MXU_COLS].astype(BF16)
                sep = jnp.dot(wk, perm, preferred_element_type=F32).astype(BF16)
                wg_s[:, k * half:(k + 1) * half] = sep[:, :half]
                wu_s[:, k * half:(k + 1) * half] = sep[:, half:]
            wd_s[...] = wdn_f[w].astype(BF16)

            @pl.when(nxt_ref[i] >= 0)
            def _():
                for cp in weight_copies(nxt_ref[i], 1 - w):
                    cp.start()

        xb = jnp.concatenate([c.astype(BF16) for c in _load_row_tiles(xs_ref, EXPERT_BLOCK)], axis=1)
        g = jnp.dot(xb, wg_s[...], preferred_element_type=F32) + bg_ref[0]
        u = jnp.dot(xb, wu_s[...], preferred_element_type=F32) + bu_ref[0]
        g = jnp.minimum(g, SWIGLU_LIMIT)
        u = jnp.clip(u, -SWIGLU_LIMIT, SWIGLU_LIMIT)
        a = g * (1.0 / (1.0 + jnp.exp(-SWIGLU_ALPHA * g))) * (u + 1.0)
        o_ref[...] = jnp.dot(a.astype(BF16), wd_s[...], preferred_element_type=F32) + bd_ref[0]

    @pl.when(i >= nact)
    def _():
        o_ref[...] = jnp.zeros(o_ref.shape, F32)


def _experts(block_exp, run_id, next_exp, n_active, xs, w_gu, w_dn, bg, bu, bd):
    f, d = w_dn.shape[1:]
    n_rows = xs.shape[0] // ROW_TILE
    nblk = n_rows // EXPERT_BLOCK
    bsel = lambda i, be, ru, nx, na: (be[i], 0, 0)
    blk = lambda i, be, ru, nx, na: (i, 0)
    anyspace = pl.BlockSpec(memory_space=pl.ANY)
    return pl.pallas_call(
        _expert_kernel,
        grid_spec=pltpu.PrefetchScalarGridSpec(
            num_scalar_prefetch=4,
            grid=(nblk,),
            in_specs=[pl.BlockSpec((EXPERT_BLOCK * ROW_TILE, LANES), blk),
                      anyspace, anyspace,
                      pl.BlockSpec((1, 1, f), bsel),
                      pl.BlockSpec((1, 1, f), bsel),
                      pl.BlockSpec((1, 1, d), bsel)],
            out_specs=pl.BlockSpec((EXPERT_BLOCK, d), blk),
            scratch_shapes=[pltpu.VMEM((2, d, 2 * f), F32),
                            pltpu.VMEM((2, f, d), F32),
                            pltpu.VMEM((d, f), BF16), pltpu.VMEM((d, f), BF16), pltpu.VMEM((f, d), BF16),
                            pltpu.SemaphoreType.DMA((2, 2))]),
        out_shape=jax.ShapeDtypeStruct((n_rows, d), F32),
        compiler_params=_cparams(("arbitrary",)),
        name="moe_experts",
    )(block_exp, run_id, next_exp, n_active, xs, w_gu, w_dn, bg, bu, bd)


def _combine_kernel(dest_ref, rows_hbm, gate_ref, x1_ref, g2_ref, fg_ref, o_ref, buf, sems, *, tm):
    n = pl.program_id(0)
    slot = n % 2

    def row_gather(step, to_slot, tok, k):
        return _row_copy(rows_hbm, dest_ref[step * (tm * TOP_K) + tok * TOP_K + k], buf.at[to_slot, k], tok,
                         sems.at[to_slot])

    @pl.when(n == 0)
    def _():
        def issue(tok, carry):
            for k in range(TOP_K):
                row_gather(0, 0, tok, k).start()
            return carry
        lax.fori_loop(0, tm, issue, 0, unroll=4)

    @pl.when(n + 1 < pl.num_programs(0))
    def _():
        for tok in range(tm):
            for k in range(TOP_K):
                row_gather(n + 1, 1 - slot, tok, k).start(priority=k % 2)

    for k in range(TOP_K):
        pltpu.make_async_copy(rows_hbm.at[pl.ds(0, tm)], buf.at[slot, k], sems.at[slot]).wait()

    gate = gate_ref[...]
    y = gate[:, 0:1] * buf[slot, 0]
    for k in range(1, TOP_K):
        y = y + gate[:, k:k + 1] * buf[slot, k]
    xo = x1_ref[...] + g2_ref[0] * y
    ms = jnp.mean(xo * xo, axis=-1, keepdims=True)
    o_ref[...] = xo * lax.rsqrt(ms + NORM_EPS) * fg_ref[...]


def _combine(dest, rows, gates, x1, g2, final_g, tm):
    b, s, d = x1.shape
    nb = s // tm
    out = pl.pallas_call(
        functools.partial(_combine_kernel, tm=tm),
        grid_spec=pltpu.PrefetchScalarGridSpec(
            num_scalar_prefetch=1,
            grid=(b * nb,),
            in_specs=[pl.BlockSpec(memory_space=pl.ANY),
                      pl.BlockSpec((tm, TOP_K), lambda n, ds: (n, 0)),
                      pl.BlockSpec((tm, d), lambda n, ds: (n, 0)),
                      pl.BlockSpec((1, 1, d), lambda n, ds: (n // nb, 0, 0)),
                      pl.BlockSpec((1, d), lambda n, ds: (0, 0))],
            out_specs=pl.BlockSpec((tm, d), lambda n, ds: (n, 0)),
            scratch_shapes=[pltpu.VMEM((2, TOP_K, tm, d), F32), pltpu.SemaphoreType.DMA((2,))]),
        out_shape=jax.ShapeDtypeStruct((b * s, d), F32),
        compiler_params=_cparams(("arbitrary",)),
        name="moe_combine",
    )(dest, rows, gates, x1.reshape(b * s, d), g2, final_g)
    return out.reshape(b, s, d)


def _pick(n, prefs):
    for p in prefs:
        if n % p == 0:
            return p
    raise ValueError(f"no tile for {n}")


def kernel(x, c, ctx, c_ctx, w_mod, b_mod, norm1_g, w_in, lam_q1, lam_k1, lam_q2, lam_k2, subln_g,
           sink, w_out, norm2_g, w_router, b_router, w_gate_up, b_gate_up, w_down, b_down, final_g):
    b, s, d = x.shape
    c_len = ctx.shape[1]
    assert w_mod.shape[0] == 1, "single-layer block"
    assert d == ROW_TILE * LANES, "dispatched token rows are one (8, 128) f32 tile each"
    t = b * s

    pad = (-(b + 1)) % 8
    cvecs = jnp.concatenate([c, c_ctx[None, :], jnp.zeros((pad, d), F32)], axis=0)
    mod = _adaln(cvecs, w_mod[0], b_mod[0])
    sh1, sc1, g1, sh2, sc2, g2 = [mod[:b, k * d:(k + 1) * d].reshape(b, 1, d) for k in range(6)]
    csh1 = mod[b:b + 1, 0:d].reshape(1, 1, d)
    csc1 = mod[b:b + 1, d:2 * d].reshape(1, 1, d)

    w_in_bf = w_in[0].astype(BF16)
    cos, sin = _rope_tables(s)
    n1 = norm1_g[0].reshape(1, d)
    qat, ka, vat, qbt, kb, vbt = _inproj_latent(x, n1, sh1, sc1, w_in_bf, cos, sin, _pick(s, (512, 256, 128)))
    w_ctx_bf = jnp.concatenate([w_in_bf[:, O_KA:O_QB], w_in_bf[:, O_KB:IN_COLS]], axis=1)
    kac, vact, kbc, vbct = _inproj_ctx(ctx, n1, csh1, csc1, w_ctx_bf)

    ka_all = jnp.concatenate([ka, kac], axis=1)
    vat_all = jnp.concatenate([vat, vact], axis=2)
    sk = s + c_len
    ya = _diff_attn(qat, ka_all, vat_all,
                    lam_q1[0].reshape(1, -1), lam_k1[0].reshape(1, -1),
                    lam_q2[0].reshape(1, -1), lam_k2[0].reshape(1, -1),
                    subln_g[0].reshape(1, -1),
                    _pick(s, (1024, 512, 256, 128)), _pick(sk, (768, 512, 384, 256, 128)))

    yb = _win_attn(qbt, kb, vbt, kbc, vbct, sink[0].reshape(1, -1), _pick(s, (256, 128)))

    w_out_bf = w_out[0].astype(BF16)
    w_r_hi = w_router[0].astype(BF16)
    w_r_lo = (w_router[0] - w_r_hi.astype(F32)).astype(BF16)
    x1, hx2, code, gates, counts = _outproj_router(
        ya, yb, w_out_bf[:QA_COLS], w_out_bf[QA_COLS:], x, g1, norm2_g[0].reshape(1, d), sh2, sc2,
        jnp.concatenate([w_r_hi, w_r_lo], axis=1).T, b_router[0].reshape(-1, 1), _pick(s, (512, 256, 128)))

    n_assign = t * TOP_K
    n_rows = n_assign + N_EXPERTS * EXPERT_BLOCK
    counts_i = counts.reshape(-1).astype(jnp.int32)
    padded = ((counts_i + EXPERT_BLOCK - 1) // EXPERT_BLOCK) * EXPERT_BLOCK
    pend = jnp.cumsum(padded).astype(jnp.int32)
    pbound = jnp.concatenate([jnp.zeros((1,), jnp.int32), pend])
    block_start = jnp.arange(n_rows // EXPERT_BLOCK, dtype=jnp.int32) * EXPERT_BLOCK
    block_exp = jnp.minimum(jnp.sum((pend[None, :] <= block_start[:, None]).astype(jnp.int32), axis=1),
                            N_EXPERTS - 1)
    n_active = pend[-1:] // EXPERT_BLOCK
    changed = jnp.concatenate([jnp.ones((1,), jnp.int32), (block_exp[1:] != block_exp[:-1]).astype(jnp.int32)])
    run_id = jnp.cumsum(changed).astype(jnp.int32) - 1
    e_ids = jnp.arange(N_EXPERTS, dtype=jnp.int32)
    later_nonempty = (e_ids[None, :] > e_ids[:, None]) & (padded[None, :] > 0)
    next_of_expert = jnp.min(jnp.where(later_nonempty, e_ids[None, :], N_EXPERTS), axis=1)
    next_of_expert = jnp.where(next_of_expert < N_EXPERTS, next_of_expert, -1).astype(jnp.int32)
    next_exp = jnp.sum(jnp.where(block_exp[:, None] == e_ids[None, :], next_of_expert[None, :], 0), axis=1)

    xs, dest = _dispatch(pbound, counts_i, code, hx2, n_rows, _pick(n_assign, (1024,)))

    f = w_down.shape[2]
    bg = b_gate_up[0, :, 0::2].reshape(N_EXPERTS, 1, f)
    bu = b_gate_up[0, :, 1::2].reshape(N_EXPERTS, 1, f)
    bd = b_down[0].reshape(N_EXPERTS, 1, d)
    rows = _experts(block_exp, run_id, next_exp, n_active, xs, w_gate_up[0], w_down[0], bg, bu, bd)

    return _combine(dest, rows, gates[:TOP_K].T, x1, g2, final_g.reshape(1, d), _pick(s, (256,)))
```

```python
import functools
import math

import jax
import jax.numpy as jnp
import numpy as np
from jax import lax
from jax.experimental import pallas as pl
from jax.experimental.pallas import tpu as pltpu

F32 = jnp.float32
BF16 = jnp.bfloat16
HIGHEST = lax.Precision.HIGHEST

GRID_W = 64
NORM_EPS = 1e-6
ROPE_BASE = 10000.0
MASK_VALUE = -1e30
DA_HEADS = 4
HEAD_DIM = 64
WA_HEADS = 8
WA_KV_HEADS = 2
WA_GROUP = WA_HEADS // WA_KV_HEADS
WINDOW = 128
N_EXPERTS = 32
TOP_K = 4
SWIGLU_LIMIT = 7.0
SWIGLU_ALPHA = 1.702
EXPERT_BLOCK = 256
LAM_INIT = 0.8 - 0.6 * math.exp(-0.3 * 0)
LOG2E = math.log2(math.e)

QA_COLS = DA_HEADS * 2 * HEAD_DIM
KA_COLS = QA_COLS
VA_COLS = QA_COLS
QB_COLS = WA_HEADS * HEAD_DIM
KB_COLS = WA_KV_HEADS * HEAD_DIM
VB_COLS = KB_COLS
O_QA = 0
O_KA = O_QA + QA_COLS
O_VA = O_KA + KA_COLS
O_QB = O_VA + VA_COLS
O_KB = O_QB + QB_COLS
O_VB = O_KB + KB_COLS
IN_COLS = O_VB + VB_COLS

LANES = 128
ROW_TILE = 8
MXU_COLS = 256
ONES_ROWS = 16
VMEM_LIMIT = 56 * 1024 * 1024


def _cparams(sem):
    return pltpu.CompilerParams(dimension_semantics=sem, vmem_limit_bytes=VMEM_LIMIT)


def _adaln_kernel(c_ref, w_ref, b_ref, o_ref):
    cv = c_ref[...]
    s = cv * (1.0 / (1.0 + jnp.exp(-cv)))
    o_ref[...] = jnp.dot(s, w_ref[...], precision=HIGHEST, preferred_element_type=F32) + b_ref[...]


def _adaln(cvecs, w_mod, b_mod):
    rows, d = cvecs.shape
    n = w_mod.shape[1]
    tn = 1024
    return pl.pallas_call(
        _adaln_kernel,
        grid=(n // tn,),
        in_specs=[pl.BlockSpec((rows, d), lambda j: (0, 0)),
                  pl.BlockSpec((d, tn), lambda j: (0, j)),
                  pl.BlockSpec((1, tn), lambda j: (0, j))],
        out_specs=pl.BlockSpec((rows, tn), lambda j: (0, j)),
        out_shape=jax.ShapeDtypeStruct((rows, n), F32),
        compiler_params=_cparams(("arbitrary",)),
        name="adaln",
    )(cvecs, w_mod, b_mod.reshape(1, n))


def _rope_section(sec, cos, sin):
    tm = sec.shape[0]
    lane = lax.broadcasted_iota(jnp.int32, (tm, LANES), 1)
    low = (lane % 32) < 16
    outs = []
    for j in range(sec.shape[1] // LANES):
        c = sec[:, j * LANES:(j + 1) * LANES]
        partner = jnp.where(low, pltpu.roll(c, LANES - 16, 1), pltpu.roll(c, 16, 1))
        outs.append(c * cos + partner * sin)
    return jnp.concatenate(outs, axis=1)


def _modulated_norm(x, g, shift, scale):
    ms = jnp.mean(x * x, axis=-1, keepdims=True)
    return (x * lax.rsqrt(ms + NORM_EPS) * g) * (1.0 + scale) + shift


def _inproj_latent_kernel(x_ref, g_ref, sh_ref, sc_ref, w_ref, cos_ref, sin_ref,
                          qat_ref, ka_ref, vat_ref, qbt_ref, kb_ref, vbt_ref):
    h = _modulated_norm(x_ref[0], g_ref[...], sh_ref[0], sc_ref[0])
    p = jnp.dot(h.astype(BF16), w_ref[...], preferred_element_type=F32)
    cos = cos_ref[...]
    sin = sin_ref[...]
    qscale = HEAD_DIM ** -0.5 * LOG2E
    qat_ref[0] = (_rope_section(p[:, O_QA:O_KA], cos, sin) * qscale).T.astype(BF16)
    ka_ref[0] = _rope_section(p[:, O_KA:O_VA], cos, sin).astype(BF16)
    vat_ref[0] = p[:, O_VA:O_QB].T.astype(BF16)
    qbt_ref[0] = (_rope_section(p[:, O_QB:O_KB], cos, sin) * qscale).T.astype(BF16)
    kb_ref[0] = _rope_section(p[:, O_KB:O_VB], cos, sin).astype(BF16)
    vbt_ref[0] = p[:, O_VB:IN_COLS].T.astype(BF16)


def _inproj_ctx_kernel(x_ref, g_ref, sh_ref, sc_ref, w_ref, ka_ref, vat_ref, kb_ref, vbt_ref):
    h = _modulated_norm(x_ref[0], g_ref[...], sh_ref[0], sc_ref[0])
    p = jnp.dot(h.astype(BF16), w_ref[...], preferred_element_type=F32)
    ka_ref[0] = p[:, 0:KA_COLS].astype(BF16)
    vat_ref[0] = p[:, KA_COLS:KA_COLS + VA_COLS].T.astype(BF16)
    kb_ref[0] = p[:, KA_COLS + VA_COLS:KA_COLS + VA_COLS + KB_COLS].astype(BF16)
    vbt_ref[0] = p[:, KA_COLS + VA_COLS + KB_COLS:].T.astype(BF16)


def _rope_tables(n_tok):
    pos = np.arange(n_tok)
    nf = HEAD_DIM // 4
    inv = ROPE_BASE ** (-np.arange(nf) / nf)
    ar = (pos // GRID_W)[:, None] * inv
    ac = (pos % GRID_W)[:, None] * inv
    cos = np.concatenate([np.cos(ar), np.cos(ar), np.cos(ac), np.cos(ac)], axis=1)
    sin = np.concatenate([-np.sin(ar), np.sin(ar), -np.sin(ac), np.sin(ac)], axis=1)
    reps = (1, LANES // HEAD_DIM)
    return jnp.asarray(np.tile(cos, reps), F32), jnp.asarray(np.tile(sin, reps), F32)


def _inproj_latent(x, g, shift, scale, w_bf16, cos, sin, tm):
    b, s, d = x.shape
    row = lambda bi, i: (bi, i, 0)
    colt = lambda bi, i: (bi, 0, i)
    mod = lambda bi, i: (bi, 0, 0)
    fixed = lambda bi, i: (0, 0)
    return pl.pallas_call(
        _inproj_latent_kernel,
        grid=(b, s // tm),
        in_specs=[pl.BlockSpec((1, tm, d), row),
                  pl.BlockSpec((1, d), fixed),
                  pl.BlockSpec((1, 1, d), mod),
                  pl.BlockSpec((1, 1, d), mod),
                  pl.BlockSpec((d, IN_COLS), fixed),
                  pl.BlockSpec((tm, LANES), lambda bi, i: (i, 0)),
                  pl.BlockSpec((tm, LANES), lambda bi, i: (i, 0))],
        out_specs=[pl.BlockSpec((1, QA_COLS, tm), colt),
                   pl.BlockSpec((1, tm, KA_COLS), row),
                   pl.BlockSpec((1, VA_COLS, tm), colt),
                   pl.BlockSpec((1, QB_COLS, tm), colt),
                   pl.BlockSpec((1, tm, KB_COLS), row),
                   pl.BlockSpec((1, VB_COLS, tm), colt)],
        out_shape=[jax.ShapeDtypeStruct((b, QA_COLS, s), BF16),
                   jax.ShapeDtypeStruct((b, s, KA_COLS), BF16),
                   jax.ShapeDtypeStruct((b, VA_COLS, s), BF16),
                   jax.ShapeDtypeStruct((b, QB_COLS, s), BF16),
                   jax.ShapeDtypeStruct((b, s, KB_COLS), BF16),
                   jax.ShapeDtypeStruct((b, VB_COLS, s), BF16)],
        compiler_params=_cparams(("arbitrary", "arbitrary")),
        name="inproj_latent",
    )(x, g, shift, scale, w_bf16, cos, sin)


def _inproj_ctx(ctx, g, shift, scale, w_ctx_bf16):
    b, c, d = ctx.shape
    n = w_ctx_bf16.shape[1]
    whole = lambda bi: (bi, 0, 0)
    mod = lambda bi: (0, 0, 0)
    fixed = lambda bi: (0, 0)
    return pl.pallas_call(
        _inproj_ctx_kernel,
        grid=(b,),
        in_specs=[pl.BlockSpec((1, c, d), whole),
                  pl.BlockSpec((1, d), fixed),
                  pl.BlockSpec((1, 1, d), mod),
                  pl.BlockSpec((1, 1, d), mod),
                  pl.BlockSpec((d, n), fixed)],
        out_specs=[pl.BlockSpec((1, c, KA_COLS), whole),
                   pl.BlockSpec((1, VA_COLS, c), whole),
                   pl.BlockSpec((1, c, KB_COLS), whole),
                   pl.BlockSpec((1, VB_COLS, c), whole)],
        out_shape=[jax.ShapeDtypeStruct((b, c, KA_COLS), BF16),
                   jax.ShapeDtypeStruct((b, VA_COLS, c), BF16),
                   jax.ShapeDtypeStruct((b, c, KB_COLS), BF16),
                   jax.ShapeDtypeStruct((b, VB_COLS, c), BF16)],
        compiler_params=_cparams(("arbitrary",)),
        name="inproj_ctx",
    )(ctx, g, shift, scale, w_ctx_bf16)


def _diff_attn_kernel(qt_ref, k_ref, vt_ref, lq1_ref, lk1_ref, lq2_ref, lk2_ref, sg_ref, o_ref,
                      m_ref, acc_ref, s_ref, *, tq, tk):
    d = HEAD_DIM
    hw = 2 * d
    n_chunks = k_ref.shape[1] // tk
    n_tiles = qt_ref.shape[2] // tq
    ones = jnp.ones((ONES_ROWS, tk), BF16)
    lam = (jnp.exp(jnp.sum(lq1_ref[...] * lk1_ref[...], axis=-1, keepdims=True))
           - jnp.exp(jnp.sum(lq2_ref[...] * lk2_ref[...], axis=-1, keepdims=True)) + LAM_INIT)

    def query_rhs(t):
        qt = qt_ref[0, :, pl.ds(pl.multiple_of(t * tq, tq), tq)]
        row = lax.broadcasted_iota(jnp.int32, qt.shape, 0)
        zero = jnp.zeros_like(qt)
        return jnp.where(row < d, qt, zero), jnp.where(row >= d, qt, zero)

    def scores(c, j, rhs):
        off = pl.multiple_of(j * tk, tk)
        s_ref[c] = jnp.dot(k_ref[0, pl.ds(off, tk), :], rhs[c], preferred_element_type=F32)

    def accumulate(c, j):
        off = pl.multiple_of(j * tk, tk)
        vt = jnp.concatenate([vt_ref[0, :, pl.ds(off, tk)], ones], axis=0)
        st = s_ref[c]
        m_old = m_ref[c]
        m_new = jnp.maximum(m_old, jnp.max(st, axis=0, keepdims=True))
        alpha = jnp.exp2(m_old - m_new)
        p = jnp.exp2(st - m_new).astype(BF16)
        acc_ref[c] = alpha * acc_ref[c] + jnp.dot(vt, p, preferred_element_type=F32)
        m_ref[c] = m_new

    group = 5 if (n_chunks - 1) % 5 == 0 else 1

    scores(0, 0, query_rhs(0))

    def tile(t, carry):
        rhs = query_rhs(t)
        m_ref[...] = jnp.full(m_ref.shape, -jnp.inf, F32)
        acc_ref[...] = jnp.zeros(acc_ref.shape, F32)

        def chunk_group(jj, carry2):
            for r in range(group):
                j = group * jj + r
                scores(1, j, rhs)
                accumulate(0, j)
                scores(0, j + 1, rhs)
                accumulate(1, j)
            return carry2

        lax.fori_loop(0, (n_chunks - 1) // group, chunk_group, 0)
        scores(1, n_chunks - 1, rhs)
        accumulate(0, n_chunks - 1)
        scores(0, 0, query_rhs(jnp.minimum(t + 1, n_tiles - 1)))
        accumulate(1, n_chunks - 1)

        a1 = acc_ref[0]
        a2 = acc_ref[1]
        ot = a1[:hw] / a1[hw:hw + 1] - lam * (a2[:hw] / a2[hw:hw + 1])
        ms = jnp.mean(ot * ot, axis=0, keepdims=True)
        ot = ot * lax.rsqrt(ms + NORM_EPS)
        o_ref[0, pl.ds(pl.multiple_of(t * tq, tq), tq), :] = (ot.T * (sg_ref[...] * (1.0 - LAM_INIT))).astype(o_ref.dtype)
        return carry

    lax.fori_loop(0, n_tiles, tile, 0)


def _diff_attn(qat, ka, vat, lq1, lk1, lq2, lk2, subln_g, tq, tk):
    b, _, s = qat.shape
    sk = ka.shape[1]
    hw = 2 * HEAD_DIM
    vec = lambda bi, h: (0, 0)
    return pl.pallas_call(
        functools.partial(_diff_attn_kernel, tq=tq, tk=tk),
        grid=(b, DA_HEADS),
        in_specs=[pl.BlockSpec((1, hw, s), lambda bi, h: (bi, h, 0)),
                  pl.BlockSpec((1, sk, hw), lambda bi, h: (bi, 0, h)),
                  pl.BlockSpec((1, hw, sk), lambda bi, h: (bi, h, 0)),
                  pl.BlockSpec((1, HEAD_DIM), vec),
                  pl.BlockSpec((1, HEAD_DIM), vec),
                  pl.BlockSpec((1, HEAD_DIM), vec),
                  pl.BlockSpec((1, HEAD_DIM), vec),
                  pl.BlockSpec((1, hw), vec)],
        out_specs=pl.BlockSpec((1, s, hw), lambda bi, h: (bi, 0, h)),
        out_shape=jax.ShapeDtypeStruct((b, s, DA_HEADS * hw), BF16),
        scratch_shapes=[pltpu.VMEM((2, 1, tq), F32),
                        pltpu.VMEM((2, hw + ONES_ROWS, tq), F32),
                        pltpu.VMEM((2, tk, tq), F32)],
        compiler_params=_cparams(("arbitrary", "arbitrary")),
        name="diff_attn",
    )(qat, ka, vat, lq1, lk1, lq2, lk2, subln_g)


def _win_attn_kernel(qt_ref, k_ref, vt_ref, kc_ref, vct_ref, sink_ref, o_ref, *, tq, lk):
    d = HEAD_DIM
    grp = WA_GROUP
    s_len = k_ref.shape[1]
    c_len = kc_ref.shape[1]
    nk = lk + c_len
    i = pl.program_id(1)
    q0 = i * tq
    start = pl.multiple_of(jnp.clip(q0 - WINDOW, 0, s_len - lk), LANES)
    keys = jnp.concatenate([k_ref[0, pl.ds(start, lk), :], kc_ref[0]], axis=0)
    kpos = start + lax.broadcasted_iota(jnp.int32, (nk, tq), 0)
    qpos = q0 + lax.broadcasted_iota(jnp.int32, (nk, tq), 1)
    visible = jnp.logical_or(kpos >= start + lk, jnp.abs(kpos - qpos) <= WINDOW)
    visible = jnp.concatenate([visible] * grp, axis=1)
    ones = jnp.ones((ONES_ROWS, nk), BF16)
    qt = qt_ref[0]
    blank = jnp.zeros((d, grp * tq), BF16)
    outs = []
    for kv in range(WA_KV_HEADS):
        heads = range(kv * grp, (kv + 1) * grp)
        qcat = jnp.concatenate([qt[h * d:(h + 1) * d, :] for h in heads], axis=1)
        rhs = jnp.concatenate([qcat if j == kv else blank for j in range(WA_KV_HEADS)], axis=0)
        st = jnp.dot(keys, rhs, preferred_element_type=F32)
        st = jnp.where(visible, st, MASK_VALUE)
        sink = jnp.concatenate([jnp.broadcast_to(sink_ref[:, h:h + 1] * LOG2E, (1, tq)) for h in heads], axis=1)
        m = jnp.maximum(jnp.max(st, axis=0, keepdims=True), sink)
        p = jnp.exp2(st - m).astype(BF16)
        vt = jnp.concatenate([vt_ref[0, kv * d:(kv + 1) * d, pl.ds(start, lk)],
                              vct_ref[0, kv * d:(kv + 1) * d, :]], axis=1)
        acc = jnp.dot(jnp.concatenate([vt, ones], axis=0), p, preferred_element_type=F32)
        o = acc[:d] / (acc[d:d + 1] + jnp.exp2(sink - m))
        outs.extend(o[:, g * tq:(g + 1) * tq] for g in range(grp))
    o_ref[0] = jnp.concatenate(outs, axis=0).T.astype(o_ref.dtype)


def _win_attn(qbt, kb, vbt, kbc, vbct, sink, tq):
    b, _, s = qbt.shape
    c = kbc.shape[1]
    lk = tq + 2 * WINDOW
    assert s >= lk and tq % LANES == 0
    whole = lambda bi, i: (bi, 0, 0)
    return pl.pallas_call(
        functools.partial(_win_attn_kernel, tq=tq, lk=lk),
        grid=(b, s // tq),
        in_specs=[pl.BlockSpec((1, QB_COLS, tq), lambda bi, i: (bi, 0, i)),
                  pl.BlockSpec((1, s, KB_COLS), whole),
                  pl.BlockSpec((1, VB_COLS, s), whole),
                  pl.BlockSpec((1, c, KB_COLS), whole),
                  pl.BlockSpec((1, VB_COLS, c), whole),
                  pl.BlockSpec((1, WA_HEADS), lambda bi, i: (0, 0))],
        out_specs=pl.BlockSpec((1, tq, QB_COLS), lambda bi, i: (bi, i, 0)),
        out_shape=jax.ShapeDtypeStruct((b, s, QB_COLS), BF16),
        compiler_params=_cparams(("arbitrary", "arbitrary")),
        name="win_attn",
    )(qbt, kb, vbt, kbc, vbct, sink)


def _outproj_router_kernel(ya_ref, yb_ref, woa_ref, wob_ref, x_ref, g1_ref, n2_ref, sh_ref, sc_ref,
                           wr_ref, br_ref, x1_ref, hx_ref, code_ref, gate_ref, cnt_ref, carry_ref):
    first = jnp.logical_and(pl.program_id(0) == 0, pl.program_id(1) == 0)

    @pl.when(first)
    def _():
        carry_ref[...] = jnp.zeros(carry_ref.shape, F32)

    y = (jnp.dot(ya_ref[0], woa_ref[...], preferred_element_type=F32)
         + jnp.dot(yb_ref[0], wob_ref[...], preferred_element_type=F32))
    x1 = x_ref[0] + g1_ref[0] * y
    x1_ref[0] = x1
    hx = _modulated_norm(x1, n2_ref[...], sh_ref[0], sc_ref[0])
    _store_row_tiles(hx_ref, hx)
    tm = hx.shape[0]
    hx_hi = hx.astype(BF16)
    hx_lo = (hx - hx_hi.astype(F32)).astype(BF16)
    wr = wr_ref[...]
    nt = (((1,), (1,)), ((), ()))
    part = lax.dot_general(wr, hx_hi, nt, preferred_element_type=F32)
    logits = (part[:N_EXPERTS] + part[N_EXPERTS:]
              + lax.dot_general(wr[:N_EXPERTS], hx_lo, nt, preferred_element_type=F32) + br_ref[...])

    row_e = lax.broadcasted_iota(jnp.int32, (N_EXPERTS, tm), 0).astype(F32)
    work = logits
    tops, idxs, hots = [], [], []
    for _k in range(TOP_K):
        m = jnp.max(work, axis=0, keepdims=True)
        idx = jnp.min(jnp.where(work == m, row_e, float(N_EXPERTS)), axis=0, keepdims=True)
        hot = row_e == idx
        work = jnp.where(hot, -jnp.inf, work)
        tops.append(m)
        idxs.append(idx)
        hots.append(hot)
    es = [jnp.exp(t - tops[0]) for t in tops]
    den = es[0] + es[1] + es[2] + es[3]

    multi = jnp.zeros((N_EXPERTS, tm), F32)
    for hot in hots:
        multi = multi + hot.astype(F32)
    r_i = lax.broadcasted_iota(jnp.int32, (tm, tm), 0)
    c_i = lax.broadcasted_iota(jnp.int32, (tm, tm), 1)
    tri = (r_i <= c_i).astype(BF16)
    incl = jnp.dot(multi.astype(BF16), tri, preferred_element_type=F32)
    before = carry_ref[...] + incl - 1.0

    codes, gates = [], []
    for k in range(TOP_K):
        rank = jnp.sum(jnp.where(hots[k], before, 0.0), axis=0, keepdims=True)
        codes.append(idxs[k].astype(jnp.int32) * 65536 + rank.astype(jnp.int32))
        gates.append(es[k] / den)
    fill = ROW_TILE - TOP_K
    code_ref[...] = jnp.concatenate(codes + [jnp.zeros((fill, tm), jnp.int32)], axis=0)
    gate_ref[...] = jnp.concatenate(gates + [jnp.zeros((fill, tm), F32)], axis=0)
    carry_ref[...] = carry_ref[...] + jnp.sum(multi, axis=1, keepdims=True)
    cnt_ref[...] = carry_ref[...]


def _outproj_router(ya, yb, woa, wob, x, g1, n2, sh2, sc2, w_r_parts, b_r, tm):
    b, s, d = x.shape
    nb = s // tm
    row = lambda bi, i: (bi, i, 0)
    mod = lambda bi, i: (bi, 0, 0)
    fixed = lambda bi, i: (0, 0)
    tok = lambda bi, i: (bi * nb + i, 0)
    half = ya.shape[2]
    return pl.pallas_call(
        _outproj_router_kernel,
        grid=(b, nb),
        in_specs=[pl.BlockSpec((1, tm, half), row),
                  pl.BlockSpec((1, tm, half), row),
                  pl.BlockSpec((half, d), fixed),
                  pl.BlockSpec((half, d), fixed),
                  pl.BlockSpec((1, tm, d), row),
                  pl.BlockSpec((1, 1, d), mod),
                  pl.BlockSpec((1, d), fixed),
                  pl.BlockSpec((1, 1, d), mod),
                  pl.BlockSpec((1, 1, d), mod),
                  pl.BlockSpec((2 * N_EXPERTS, d), fixed),
                  pl.BlockSpec((N_EXPERTS, 1), fixed)],
        out_specs=[pl.BlockSpec((1, tm, d), row),
                   pl.BlockSpec((tm * ROW_TILE, LANES), tok),
                   pl.BlockSpec((ROW_TILE, tm), lambda bi, i: (0, bi * nb + i)),
                   pl.BlockSpec((ROW_TILE, tm), lambda bi, i: (0, bi * nb + i)),
                   pl.BlockSpec((N_EXPERTS, 1), fixed)],
        out_shape=[jax.ShapeDtypeStruct((b, s, d), F32),
                   jax.ShapeDtypeStruct((b * s * ROW_TILE, LANES), F32),
                   jax.ShapeDtypeStruct((ROW_TILE, b * s), jnp.int32),
                   jax.ShapeDtypeStruct((ROW_TILE, b * s), F32),
                   jax.ShapeDtypeStruct((N_EXPERTS, 1), F32)],
        scratch_shapes=[pltpu.VMEM((N_EXPERTS, 1), F32)],
        compiler_params=_cparams(("arbitrary", "arbitrary")),
        name="outproj_router",
    )(ya, yb, woa, wob, x, g1, n2, sh2, sc2, w_r_parts, b_r)


def _sorted_rows_kernel(pbound_ref, code_ref, dest_ref):
    code = code_ref[...]
    expert = code >> 16
    base = jnp.zeros_like(code)
    for e in range(N_EXPERTS):
        base = jnp.where(expert == e, pbound_ref[e], base)
    dest_ref[...] = base + (code & 0xFFFF)


def _sorted_rows(pbound, code, tn):
    rows, n_tok = code.shape
    return pl.pallas_call(
        _sorted_rows_kernel,
        grid_spec=pltpu.PrefetchScalarGridSpec(
            num_scalar_prefetch=1,
            grid=(n_tok // tn,),
            in_specs=[pl.BlockSpec((rows, tn), lambda i, pb: (0, i))],
            out_specs=pl.BlockSpec((rows, tn), lambda i, pb: (0, i))),
        out_shape=jax.ShapeDtypeStruct((rows, n_tok), jnp.int32),
        compiler_params=_cparams(("arbitrary",)),
        name="moe_sorted_rows",
    )(pbound, code)


def _row_copy(src_hbm, src_row, dst_ref, dst_row, sem):
    return pltpu.make_async_copy(src_hbm.at[pl.ds(src_row, 1)], dst_ref.at[pl.ds(dst_row, 1)], sem)


def _store_row_tiles(ref, val):
    n = val.shape[0]
    for c in range(ROW_TILE):
        ref[pl.ds(c, n, stride=ROW_TILE), :] = val[:, c * LANES:(c + 1) * LANES]


def _load_row_tiles(ref, n):
    return [ref[pl.ds(c, n, stride=ROW_TILE), :] for c in range(ROW_TILE)]


def _tile_rows(row, count=1):
    if isinstance(row, int):
        return pl.ds(row * ROW_TILE, count * ROW_TILE)
    return pl.ds(pl.multiple_of(row * ROW_TILE, ROW_TILE), count * ROW_TILE)


def _tile_copy(src_ref, src_row, dst_ref, dst_row, sem):
    return pltpu.make_async_copy(src_ref.at[_tile_rows(src_row)], dst_ref.at[_tile_rows(dst_row)], sem)


def _dispatch_kernel(pbound_ref, cnt_ref, dest_ref, hx_ref, xs_hbm, zeros, sem, zsem, *, chunk):
    n_rows = xs_hbm.shape[0] // ROW_TILE
    blk = zeros.shape[0] // ROW_TILE

    @pl.when(pl.program_id(0) == 0)
    def _():
        zeros[...] = jnp.zeros(zeros.shape, F32)
        tail_blocks = (n_rows - pbound_ref[N_EXPERTS]) // blk

        def pad_row(r, carry):
            _tile_copy(zeros, 0, xs_hbm, r, zsem).start()
            return carry

        def pad_expert(e, carry):
            lax.fori_loop(pbound_ref[e] + cnt_ref[e], pbound_ref[e + 1], pad_row, 0)
            return carry

        def tail_copy(t):
            return pltpu.make_async_copy(zeros, xs_hbm.at[_tile_rows(pbound_ref[N_EXPERTS] + t * blk, blk)], zsem)

        def tail_start(t, carry):
            tail_copy(t).start()
            return carry

        lax.fori_loop(0, N_EXPERTS, pad_expert, 0)
        lax.fori_loop(0, tail_blocks, tail_start, 0)

        def pad_row_wait(r, carry):
            _tile_copy(zeros, 0, xs_hbm, 0, zsem).wait()
            return carry

        def pad_expert_wait(e, carry):
            lax.fori_loop(pbound_ref[e] + cnt_ref[e], pbound_ref[e + 1], pad_row_wait, 0)
            return carry

        def tail_wait(t, carry):
            tail_copy(0).wait()
            return carry

        lax.fori_loop(0, N_EXPERTS, pad_expert_wait, 0)
        lax.fori_loop(0, tail_blocks, tail_wait, 0)

    for j in range(chunk):
        _tile_copy(hx_ref, j // TOP_K, xs_hbm, dest_ref[j % TOP_K, j // TOP_K], sem).start(priority=j % 2)
    pltpu.make_async_copy(xs_hbm.at[_tile_rows(0, chunk)], xs_hbm.at[_tile_rows(0, chunk)], sem).wait()


def _dispatch(pbound, counts_i, dest, hx_tiles, n_rows, chunk):
    n_assign = dest.shape[1] * TOP_K
    return pl.pallas_call(
        functools.partial(_dispatch_kernel, chunk=chunk),
        grid_spec=pltpu.PrefetchScalarGridSpec(
            num_scalar_prefetch=2,
            grid=(n_assign // chunk,),
            in_specs=[pl.BlockSpec((ROW_TILE, chunk // TOP_K), lambda i, pb, ct: (0, i), memory_space=pltpu.SMEM),
                      pl.BlockSpec((chunk // TOP_K * ROW_TILE, LANES), lambda i, pb, ct: (i, 0))],
            out_specs=pl.BlockSpec(memory_space=pl.ANY),
            scratch_shapes=[pltpu.VMEM((EXPERT_BLOCK * ROW_TILE, LANES), F32),
                            pltpu.SemaphoreType.DMA(()),
                            pltpu.SemaphoreType.DMA(())]),
        out_shape=jax.ShapeDtypeStruct((n_rows * ROW_TILE, LANES), F32),
        compiler_params=_cparams(("arbitrary",)),
        name="moe_dispatch",
    )(pbound, counts_i, dest, hx_tiles)


def _expert_kernel(bexp_ref, run_ref, nxt_ref, nact_ref, xs_ref, wgu_hbm, wdn_hbm, bg_ref, bu_ref, bd_ref,
                   o_ref, wgu_f, wdn_f, wg_s, wu_s, wd_s, a_ref, wsem):
    i = pl.program_id(0)
    nact = nact_ref[0]
    prev = jnp.maximum(i - 1, 0)
    cur_a = i % 2
    prev_a = 1 - cur_a

    def weight_copies(expert, w):
        return (pltpu.make_async_copy(wgu_hbm.at[expert], wgu_f.at[w], wsem.at[w, 0]),
                pltpu.make_async_copy(wdn_hbm.at[expert], wdn_f.at[w], wsem.at[w, 1]))

    @pl.when(i == 0)
    def _():
        for cp in weight_copies(bexp_ref[0], 0):
            cp.start()

    @pl.when(i < nact)
    def _():
        changed = jnp.logical_or(i == 0, bexp_ref[i] != bexp_ref[prev])

        @pl.when(changed)
        def _():
            w = run_ref[i] % 2
            for cp in weight_copies(bexp_ref[i], w):
                cp.wait()
            half = MXU_COLS // 2
            src = lax.broadcasted_iota(jnp.int32, (MXU_COLS, MXU_COLS), 0)
            dst = lax.broadcasted_iota(jnp.int32, (MXU_COLS, MXU_COLS), 1)
            perm = (src == jnp.where(dst < half, 2 * dst, 2 * (dst - half) + 1)).astype(BF16)
            for k in range(wgu_f.shape[2] // MXU_COLS):
                wk = wgu_f[w, :, k * MXU_COLS:(k + 1) * MXU_COLS].astype(BF16)
                sep = jnp.dot(wk, perm, preferred_element_type=F32).astype(BF16)
                wg_s[:, k * half:(k + 1) * half] = sep[:, :half]
                wu_s[:, k * half:(k + 1) * half] = sep[:, half:]
            wd_s[w] = wdn_f[w].astype(BF16)

            @pl.when(nxt_ref[i] >= 0)
            def _():
                for cp in weight_copies(nxt_ref[i], 1 - w):
                    cp.start()

    def gate_up():
        xb = jnp.concatenate([c.astype(BF16) for c in _load_row_tiles(xs_ref, EXPERT_BLOCK)], axis=1)
        g = jnp.dot(xb, wg_s[...], preferred_element_type=F32) + bg_ref[0]
        u = jnp.dot(xb, wu_s[...], preferred_element_type=F32) + bu_ref[0]
        return g, u

    def swiglu(g, u):
        g = jnp.minimum(g, SWIGLU_LIMIT)
        u = jnp.clip(u, -SWIGLU_LIMIT, SWIGLU_LIMIT)
        return (g * (1.0 / (1.0 + jnp.exp(-SWIGLU_ALPHA * g))) * (u + 1.0)).astype(BF16)

    def down():
        return jnp.dot(a_ref[prev_a], wd_s[run_ref[prev] % 2], preferred_element_type=F32) + bd_ref[0]

    @pl.when(i == 0)
    def _():
        a_ref[cur_a] = swiglu(*gate_up())
        o_ref[...] = jnp.zeros(o_ref.shape, F32)

    @pl.when(jnp.logical_and(i > 0, i < nact))
    def _():
        g, u = gate_up()
        out = down()
        a_ref[cur_a] = swiglu(g, u)
        _store_row_tiles(o_ref, out)

    @pl.when(jnp.logical_and(i > 0, i == nact))
    def _():
        _store_row_tiles(o_ref, down())

    @pl.when(i > nact)
    def _():
        o_ref[...] = jnp.zeros(o_ref.shape, F32)


def _experts(block_exp, run_id, next_exp, n_active, xs, w_gu, w_dn, bg, bu, bd):
    f, d = w_dn.shape[1:]
    n_rows = xs.shape[0] // ROW_TILE
    nblk = n_rows // EXPERT_BLOCK
    cur = lambda i: jnp.minimum(i, nblk - 1)
    prev = lambda i: jnp.maximum(i - 1, 0)
    anyspace = pl.BlockSpec(memory_space=pl.ANY)
    return pl.pallas_call(
        _expert_kernel,
        grid_spec=pltpu.PrefetchScalarGridSpec(
            num_scalar_prefetch=4,
            grid=(nblk + 1,),
            in_specs=[pl.BlockSpec((EXPERT_BLOCK * ROW_TILE, LANES), lambda i, be, ru, nx, na: (cur(i), 0)),
                      anyspace, anyspace,
                      pl.BlockSpec((1, 1, f), lambda i, be, ru, nx, na: (be[cur(i)], 0, 0)),
                      pl.BlockSpec((1, 1, f), lambda i, be, ru, nx, na: (be[cur(i)], 0, 0)),
                      pl.BlockSpec((1, 1, d), lambda i, be, ru, nx, na: (be[prev(i)], 0, 0))],
            out_specs=pl.BlockSpec((EXPERT_BLOCK * ROW_TILE, LANES), lambda i, be, ru, nx, na: (prev(i), 0)),
            scratch_shapes=[pltpu.VMEM((2, d, 2 * f), F32),
                            pltpu.VMEM((2, f, d), F32),
                            pltpu.VMEM((d, f), BF16), pltpu.VMEM((d, f), BF16), pltpu.VMEM((2, f, d), BF16),
                            pltpu.VMEM((2, EXPERT_BLOCK, f), BF16),
                            pltpu.SemaphoreType.DMA((2, 2))]),
        out_shape=jax.ShapeDtypeStruct((n_rows * ROW_TILE, LANES), F32),
        compiler_params=_cparams(("arbitrary",)),
        name="moe_experts",
    )(block_exp, run_id, next_exp, n_active, xs, w_gu, w_dn, bg, bu, bd)


def _combine_kernel(dest_ref, rows_hbm, gate_ref, x1_ref, g2_ref, fg_ref, o_ref, buf, sems, *, tm):
    n = pl.program_id(0)
    slot = n % 2

    def row_gather(step, to_slot, tok, k):
        n_tok = tm * pl.num_programs(0)
        return _tile_copy(rows_hbm, dest_ref[k * n_tok + step * tm + tok], buf.at[to_slot, k], tok, sems.at[to_slot])

    @pl.when(n == 0)
    def _():
        def issue(tok, carry):
            for k in range(TOP_K):
                row_gather(0, 0, tok, k).start()
            return carry
        lax.fori_loop(0, tm, issue, 0, unroll=4)

    @pl.when(n + 1 < pl.num_programs(0))
    def _():
        for tok in range(tm):
            for k in range(TOP_K):
                row_gather(n + 1, 1 - slot, tok, k).start(priority=k % 2)

    for k in range(TOP_K):
        pltpu.make_async_copy(rows_hbm.at[_tile_rows(0, tm)], buf.at[slot, k], sems.at[slot]).wait()

    gate = gate_ref[...]
    parts = None
    for k in range(TOP_K):
        gk = gate[:, k:k + 1]
        tiles = _load_row_tiles(buf.at[slot, k], tm)
        parts = [gk * r for r in tiles] if parts is None else [p + gk * r for p, r in zip(parts, tiles)]
    y = jnp.concatenate(parts, axis=1)
    xo = x1_ref[...] + g2_ref[0] * y
    ms = jnp.mean(xo * xo, axis=-1, keepdims=True)
    o_ref[...] = xo * lax.rsqrt(ms + NORM_EPS) * fg_ref[...]


def _combine(dest, rows, gates, x1, g2, final_g, tm):
    b, s, d = x1.shape
    nb = s // tm
    out = pl.pallas_call(
        functools.partial(_combine_kernel, tm=tm),
        grid_spec=pltpu.PrefetchScalarGridSpec(
            num_scalar_prefetch=1,
            grid=(b * nb,),
            in_specs=[pl.BlockSpec(memory_space=pl.ANY),
                      pl.BlockSpec((tm, TOP_K), lambda n, ds: (n, 0)),
                      pl.BlockSpec((tm, d), lambda n, ds: (n, 0)),
                      pl.BlockSpec((1, 1, d), lambda n, ds: (n // nb, 0, 0)),
                      pl.BlockSpec((1, d), lambda n, ds: (0, 0))],
            out_specs=pl.BlockSpec((tm, d), lambda n, ds: (n, 0)),
            scratch_shapes=[pltpu.VMEM((2, TOP_K, tm * ROW_TILE, LANES), F32), pltpu.SemaphoreType.DMA((2,))]),
        out_shape=jax.ShapeDtypeStruct((b * s, d), F32),
        compiler_params=_cparams(("arbitrary",)),
        name="moe_combine",
    )(dest, rows, gates, x1.reshape(b * s, d), g2, final_g)
    return out.reshape(b, s, d)


def _pick(n, prefs):
    for p in prefs:
        if n % p == 0:
            return p
    raise ValueError(f"no tile for {n}")


def kernel(x, c, ctx, c_ctx, w_mod, b_mod, norm1_g, w_in, lam_q1, lam_k1, lam_q2, lam_k2, subln_g,
           sink, w_out, norm2_g, w_router, b_router, w_gate_up, b_gate_up, w_down, b_down, final_g):
    b, s, d = x.shape
    c_len = ctx.shape[1]
    assert w_mod.shape[0] == 1, "single-layer block"
    assert d == ROW_TILE * LANES, "dispatched token rows are one (8, 128) f32 tile each"
    t = b * s

    pad = (-(b + 1)) % 8
    cvecs = jnp.concatenate([c, c_ctx[None, :], jnp.zeros((pad, d), F32)], axis=0)
    mod = _adaln(cvecs, w_mod[0], b_mod[0])
    sh1, sc1, g1, sh2, sc2, g2 = [mod[:b, k * d:(k + 1) * d].reshape(b, 1, d) for k in range(6)]
    csh1 = mod[b:b + 1, 0:d].reshape(1, 1, d)
    csc1 = mod[b:b + 1, d:2 * d].reshape(1, 1, d)

    w_in_bf = w_in[0].astype(BF16)
    cos, sin = _rope_tables(s)
    n1 = norm1_g[0].reshape(1, d)
    qat, ka, vat, qbt, kb, vbt = _inproj_latent(x, n1, sh1, sc1, w_in_bf, cos, sin, _pick(s, (512, 256, 128)))
    w_ctx_bf = jnp.concatenate([w_in_bf[:, O_KA:O_QB], w_in_bf[:, O_KB:IN_COLS]], axis=1)
    kac, vact, kbc, vbct = _inproj_ctx(ctx, n1, csh1, csc1, w_ctx_bf)

    ka_all = jnp.concatenate([ka, kac], axis=1)
    vat_all = jnp.concatenate([vat, vact], axis=2)
    sk = s + c_len
    ya = _diff_attn(qat, ka_all, vat_all,
                    lam_q1[0].reshape(1, -1), lam_k1[0].reshape(1, -1),
                    lam_q2[0].reshape(1, -1), lam_k2[0].reshape(1, -1),
                    subln_g[0].reshape(1, -1),
                    _pick(s, (1024, 512, 256, 128)), _pick(sk, (768, 512, 384, 256, 128)))

    yb = _win_attn(qbt, kb, vbt, kbc, vbct, sink[0].reshape(1, -1), _pick(s, (256, 128)))

    w_out_bf = w_out[0].astype(BF16)
    w_r_hi = w_router[0].astype(BF16)
    w_r_lo = (w_router[0] - w_r_hi.astype(F32)).astype(BF16)
    x1, hx2, code, gates, counts = _outproj_router(
        ya, yb, w_out_bf[:QA_COLS], w_out_bf[QA_COLS:], x, g1, norm2_g[0].reshape(1, d), sh2, sc2,
        jnp.concatenate([w_r_hi, w_r_lo], axis=1).T, b_router[0].reshape(-1, 1), _pick(s, (512, 256, 128)))

    n_assign = t * TOP_K
    n_rows = n_assign + N_EXPERTS * EXPERT_BLOCK
    counts_i = counts.reshape(-1).astype(jnp.int32)
    padded = ((counts_i + EXPERT_BLOCK - 1) // EXPERT_BLOCK) * EXPERT_BLOCK
    pend = jnp.cumsum(padded).astype(jnp.int32)
    pbound = jnp.concatenate([jnp.zeros((1,), jnp.int32), pend])
    block_start = jnp.arange(n_rows // EXPERT_BLOCK, dtype=jnp.int32) * EXPERT_BLOCK
    block_exp = jnp.minimum(jnp.sum((pend[None, :] <= block_start[:, None]).astype(jnp.int32), axis=1),
                            N_EXPERTS - 1)
    n_active = pend[-1:] // EXPERT_BLOCK
    changed = jnp.concatenate([jnp.ones((1,), jnp.int32), (block_exp[1:] != block_exp[:-1]).astype(jnp.int32)])
    run_id = jnp.cumsum(changed).astype(jnp.int32) - 1
    e_ids = jnp.arange(N_EXPERTS, dtype=jnp.int32)
    later_nonempty = (e_ids[None, :] > e_ids[:, None]) & (padded[None, :] > 0)
    next_of_expert = jnp.min(jnp.where(later_nonempty, e_ids[None, :], N_EXPERTS), axis=1)
    next_of_expert = jnp.where(next_of_expert < N_EXPERTS, next_of_expert, -1).astype(jnp.int32)
    next_exp = jnp.sum(jnp.where(block_exp[:, None] == e_ids[None, :], next_of_expert[None, :], 0), axis=1)

    dest = _sorted_rows(pbound, code, _pick(t, (2048, 1024, 512, 256)))
    xs = _dispatch(pbound, counts_i, dest, hx2, n_rows, _pick(n_assign, (1024,)))

    f = w_down.shape[2]
    bg = b_gate_up[0, :, 0::2].reshape(N_EXPERTS, 1, f)
    bu = b_gate_up[0, :, 1::2].reshape(N_EXPERTS, 1, f)
    bd = b_down[0].reshape(N_EXPERTS, 1, d)
    rows = _experts(block_exp, run_id, next_exp, n_active, xs, w_gate_up[0], w_down[0], bg, bu, bd)

    return _combine(dest[:TOP_K].reshape(-1), rows, gates[:TOP_K].T, x1, g2, final_g.reshape(1, d), _pick(s, (256,)))
```

```python
import functools
import math

import jax
import jax.numpy as jnp
import numpy as np
from jax import lax
from jax.experimental import pallas as pl
from jax.experimental.pallas import tpu as pltpu

F32 = jnp.float32
BF16 = jnp.bfloat16
HIGHEST = lax.Precision.HIGHEST

GRID_W = 64
NORM_EPS = 1e-6
ROPE_BASE = 10000.0
MASK_VALUE = -1e30
DA_HEADS = 4
HEAD_DIM = 64
WA_HEADS = 8
WA_KV_HEADS = 2
WA_GROUP = WA_HEADS // WA_KV_HEADS
WINDOW = 128
N_EXPERTS = 32
TOP_K = 4
SWIGLU_LIMIT = 7.0
SWIGLU_ALPHA = 1.702
EXPERT_BLOCK = 256
LAM_INIT = 0.8 - 0.6 * math.exp(-0.3 * 0)
LOG2E = math.log2(math.e)

QA_COLS = DA_HEADS * 2 * HEAD_DIM
KA_COLS = QA_COLS
VA_COLS = QA_COLS
QB_COLS = WA_HEADS * HEAD_DIM
KB_COLS = WA_KV_HEADS * HEAD_DIM
VB_COLS = KB_COLS
O_QA = 0
O_KA = O_QA + QA_COLS
O_VA = O_KA + KA_COLS
O_QB = O_VA + VA_COLS
O_KB = O_QB + QB_COLS
O_VB = O_KB + KB_COLS
IN_COLS = O_VB + VB_COLS

LANES = 128
ROW_TILE = 8
MXU_COLS = 256
ONES_ROWS = 16
VMEM_LIMIT = 56 * 1024 * 1024


def _cparams(sem):
    return pltpu.CompilerParams(dimension_semantics=sem, vmem_limit_bytes=VMEM_LIMIT)


def _adaln_kernel(c_ref, w_ref, b_ref, o_ref):
    cv = c_ref[...]
    s = cv * (1.0 / (1.0 + jnp.exp(-cv)))
    o_ref[...] = jnp.dot(s, w_ref[...], precision=HIGHEST, preferred_element_type=F32) + b_ref[...]


def _adaln(cvecs, w_mod, b_mod):
    rows, d = cvecs.shape
    n = w_mod.shape[1]
    tn = 1024
    return pl.pallas_call(
        _adaln_kernel,
        grid=(n // tn,),
        in_specs=[pl.BlockSpec((rows, d), lambda j: (0, 0)),
                  pl.BlockSpec((d, tn), lambda j: (0, j)),
                  pl.BlockSpec((1, tn), lambda j: (0, j))],
        out_specs=pl.BlockSpec((rows, tn), lambda j: (0, j)),
        out_shape=jax.ShapeDtypeStruct((rows, n), F32),
        compiler_params=_cparams(("arbitrary",)),
        name="adaln",
    )(cvecs, w_mod, b_mod.reshape(1, n))


def _rope_section(sec, cos, sin):
    tm = sec.shape[0]
    lane = lax.broadcasted_iota(jnp.int32, (tm, LANES), 1)
    low = (lane % 32) < 16
    outs = []
    for j in range(sec.shape[1] // LANES):
        c = sec[:, j * LANES:(j + 1) * LANES]
        partner = jnp.where(low, pltpu.roll(c, LANES - 16, 1), pltpu.roll(c, 16, 1))
        outs.append(c * cos + partner * sin)
    return jnp.concatenate(outs, axis=1)


def _modulated_norm(x, g, shift, scale):
    ms = jnp.mean(x * x, axis=-1, keepdims=True)
    return (x * lax.rsqrt(ms + NORM_EPS) * g) * (1.0 + scale) + shift


def _inproj_latent_kernel(x_ref, g_ref, sh_ref, sc_ref, w_ref, cos_ref, sin_ref,
                          qat_ref, ka_ref, vat_ref, qbt_ref, kb_ref, vbt_ref):
    h = _modulated_norm(x_ref[0], g_ref[...], sh_ref[0], sc_ref[0])
    p = jnp.dot(h.astype(BF16), w_ref[...], preferred_element_type=F32)
    cos = cos_ref[...]
    sin = sin_ref[...]
    qscale = HEAD_DIM ** -0.5 * LOG2E
    qat_ref[0] = (_rope_section(p[:, O_QA:O_KA], cos, sin) * qscale).T.astype(BF16)
    ka_ref[0] = _rope_section(p[:, O_KA:O_VA], cos, sin).astype(BF16)
    vat_ref[0] = p[:, O_VA:O_QB].T.astype(BF16)
    qbt_ref[0] = (_rope_section(p[:, O_QB:O_KB], cos, sin) * qscale).T.astype(BF16)
    kb_ref[0] = _rope_section(p[:, O_KB:O_VB], cos, sin).astype(BF16)
    vbt_ref[0] = p[:, O_VB:IN_COLS].T.astype(BF16)


def _inproj_ctx_kernel(x_ref, g_ref, sh_ref, sc_ref, w_ref, ka_ref, vat_ref, kb_ref, vbt_ref):
    h = _modulated_norm(x_ref[0], g_ref[...], sh_ref[0], sc_ref[0])
    p = jnp.dot(h.astype(BF16), w_ref[...], preferred_element_type=F32)
    ka_ref[0] = p[:, 0:KA_COLS].astype(BF16)
    vat_ref[0] = p[:, KA_COLS:KA_COLS + VA_COLS].T.astype(BF16)
    kb_ref[0] = p[:, KA_COLS + VA_COLS:KA_COLS + VA_COLS + KB_COLS].astype(BF16)
    vbt_ref[0] = p[:, KA_COLS + VA_COLS + KB_COLS:].T.astype(BF16)


def _rope_tables(n_tok):
    pos = np.arange(n_tok)
    nf = HEAD_DIM // 4
    inv = ROPE_BASE ** (-np.arange(nf) / nf)
    ar = (pos // GRID_W)[:, None] * inv
    ac = (pos % GRID_W)[:, None] * inv
    cos = np.concatenate([np.cos(ar), np.cos(ar), np.cos(ac), np.cos(ac)], axis=1)
    sin = np.concatenate([-np.sin(ar), np.sin(ar), -np.sin(ac), np.sin(ac)], axis=1)
    reps = (1, LANES // HEAD_DIM)
    return jnp.asarray(np.tile(cos, reps), F32), jnp.asarray(np.tile(sin, reps), F32)


def _inproj_latent(x, g, shift, scale, w_bf16, cos, sin, tm):
    b, s, d = x.shape
    row = lambda bi, i: (bi, i, 0)
    colt = lambda bi, i: (bi, 0, i)
    mod = lambda bi, i: (bi, 0, 0)
    fixed = lambda bi, i: (0, 0)
    return pl.pallas_call(
        _inproj_latent_kernel,
        grid=(b, s // tm),
        in_specs=[pl.BlockSpec((1, tm, d), row),
                  pl.BlockSpec((1, d), fixed),
                  pl.BlockSpec((1, 1, d), mod),
                  pl.BlockSpec((1, 1, d), mod),
                  pl.BlockSpec((d, IN_COLS), fixed),
                  pl.BlockSpec((tm, LANES), lambda bi, i: (i, 0)),
                  pl.BlockSpec((tm, LANES), lambda bi, i: (i, 0))],
        out_specs=[pl.BlockSpec((1, QA_COLS, tm), colt),
                   pl.BlockSpec((1, tm, KA_COLS), row),
                   pl.BlockSpec((1, VA_COLS, tm), colt),
                   pl.BlockSpec((1, QB_COLS, tm), colt),
                   pl.BlockSpec((1, tm, KB_COLS), row),
                   pl.BlockSpec((1, VB_COLS, tm), colt)],
        out_shape=[jax.ShapeDtypeStruct((b, QA_COLS, s), BF16),
                   jax.ShapeDtypeStruct((b, s, KA_COLS), BF16),
                   jax.ShapeDtypeStruct((b, VA_COLS, s), BF16),
                   jax.ShapeDtypeStruct((b, QB_COLS, s), BF16),
                   jax.ShapeDtypeStruct((b, s, KB_COLS), BF16),
                   jax.ShapeDtypeStruct((b, VB_COLS, s), BF16)],
        compiler_params=_cparams(("arbitrary", "arbitrary")),
        name="inproj_latent",
    )(x, g, shift, scale, w_bf16, cos, sin)


def _inproj_ctx(ctx, g, shift, scale, w_ctx_bf16):
    b, c, d = ctx.shape
    n = w_ctx_bf16.shape[1]
    whole = lambda bi: (bi, 0, 0)
    mod = lambda bi: (0, 0, 0)
    fixed = lambda bi: (0, 0)
    return pl.pallas_call(
        _inproj_ctx_kernel,
        grid=(b,),
        in_specs=[pl.BlockSpec((1, c, d), whole),
                  pl.BlockSpec((1, d), fixed),
                  pl.BlockSpec((1, 1, d), mod),
                  pl.BlockSpec((1, 1, d), mod),
                  pl.BlockSpec((d, n), fixed)],
        out_specs=[pl.BlockSpec((1, c, KA_COLS), whole),
                   pl.BlockSpec((1, VA_COLS, c), whole),
                   pl.BlockSpec((1, c, KB_COLS), whole),
                   pl.BlockSpec((1, VB_COLS, c), whole)],
        out_shape=[jax.ShapeDtypeStruct((b, c, KA_COLS), BF16),
                   jax.ShapeDtypeStruct((b, VA_COLS, c), BF16),
                   jax.ShapeDtypeStruct((b, c, KB_COLS), BF16),
                   jax.ShapeDtypeStruct((b, VB_COLS, c), BF16)],
        compiler_params=_cparams(("arbitrary",)),
        name="inproj_ctx",
    )(ctx, g, shift, scale, w_ctx_bf16)


def _diff_attn_kernel(qt_ref, k_ref, vt_ref, lq1_ref, lk1_ref, lq2_ref, lk2_ref, sg_ref, o_ref,
                      m_ref, acc_ref, s_ref, *, tq, tk):
    d = HEAD_DIM
    hw = 2 * d
    n_chunks = k_ref.shape[1] // tk
    n_tiles = qt_ref.shape[2] // tq
    ones = jnp.ones((ONES_ROWS, tk), BF16)
    lam = (jnp.exp(jnp.sum(lq1_ref[...] * lk1_ref[...], axis=-1, keepdims=True))
           - jnp.exp(jnp.sum(lq2_ref[...] * lk2_ref[...], axis=-1, keepdims=True)) + LAM_INIT)

    def query_rhs(t):
        qt = qt_ref[0, :, pl.ds(pl.multiple_of(t * tq, tq), tq)]
        row = lax.broadcasted_iota(jnp.int32, qt.shape, 0)
        zero = jnp.zeros_like(qt)
        return jnp.where(row < d, qt, zero), jnp.where(row >= d, qt, zero)

    def scores(c, j, rhs):
        off = pl.multiple_of(j * tk, tk)
        s_ref[c] = jnp.dot(k_ref[0, pl.ds(off, tk), :], rhs[c], preferred_element_type=F32)

    def accumulate(c, j):
        off = pl.multiple_of(j * tk, tk)
        vt = jnp.concatenate([vt_ref[0, :, pl.ds(off, tk)], ones], axis=0)
        st = s_ref[c]
        m_old = m_ref[c]
        m_new = jnp.maximum(m_old, jnp.max(st, axis=0, keepdims=True))
        alpha = jnp.exp2(m_old - m_new)
        p = jnp.exp2(st - m_new).astype(BF16)
        acc_ref[c] = alpha * acc_ref[c] + jnp.dot(vt, p, preferred_element_type=F32)
        m_ref[c] = m_new

    group = 5 if (n_chunks - 1) % 5 == 0 else 1

    scores(0, 0, query_rhs(0))

    def tile(t, carry):
        rhs = query_rhs(t)
        m_ref[...] = jnp.full(m_ref.shape, -jnp.inf, F32)
        acc_ref[...] = jnp.zeros(acc_ref.shape, F32)

        def chunk_group(jj, carry2):
            for r in range(group):
                j = group * jj + r
                scores(1, j, rhs)
                accumulate(0, j)
                scores(0, j + 1, rhs)
                accumulate(1, j)
            return carry2

        lax.fori_loop(0, (n_chunks - 1) // group, chunk_group, 0)
        scores(1, n_chunks - 1, rhs)
        accumulate(0, n_chunks - 1)
        scores(0, 0, query_rhs(jnp.minimum(t + 1, n_tiles - 1)))
        accumulate(1, n_chunks - 1)

        a1 = acc_ref[0]
        a2 = acc_ref[1]
        ot = a1[:hw] / a1[hw:hw + 1] - lam * (a2[:hw] / a2[hw:hw + 1])
        ms = jnp.mean(ot * ot, axis=0, keepdims=True)
        ot = ot * lax.rsqrt(ms + NORM_EPS)
        o_ref[0, pl.ds(pl.multiple_of(t * tq, tq), tq), :] = (ot.T * (sg_ref[...] * (1.0 - LAM_INIT))).astype(o_ref.dtype)
        return carry

    lax.fori_loop(0, n_tiles, tile, 0)


def _diff_attn(qat, ka, vat, lq1, lk1, lq2, lk2, subln_g, tq, tk):
    b, _, s = qat.shape
    sk = ka.shape[1]
    hw = 2 * HEAD_DIM
    vec = lambda bi, h: (0, 0)
    return pl.pallas_call(
        functools.partial(_diff_attn_kernel, tq=tq, tk=tk),
        grid=(b, DA_HEADS),
        in_specs=[pl.BlockSpec((1, hw, s), lambda bi, h: (bi, h, 0)),
                  pl.BlockSpec((1, sk, hw), lambda bi, h: (bi, 0, h)),
                  pl.BlockSpec((1, hw, sk), lambda bi, h: (bi, h, 0)),
                  pl.BlockSpec((1, HEAD_DIM), vec),
                  pl.BlockSpec((1, HEAD_DIM), vec),
                  pl.BlockSpec((1, HEAD_DIM), vec),
                  pl.BlockSpec((1, HEAD_DIM), vec),
                  pl.BlockSpec((1, hw), vec)],
        out_specs=pl.BlockSpec((1, s, hw), lambda bi, h: (bi, 0, h)),
        out_shape=jax.ShapeDtypeStruct((b, s, DA_HEADS * hw), BF16),
        scratch_shapes=[pltpu.VMEM((2, 1, tq), F32),
                        pltpu.VMEM((2, hw + ONES_ROWS, tq), F32),
                        pltpu.VMEM((2, tk, tq), F32)],
        compiler_params=_cparams(("arbitrary", "arbitrary")),
        name="diff_attn",
    )(qat, ka, vat, lq1, lk1, lq2, lk2, subln_g)


def _win_attn_kernel(qt_ref, k_ref, vt_ref, kc_ref, vct_ref, sink_ref, o_ref, *, tq, lk):
    d = HEAD_DIM
    grp = WA_GROUP
    s_len = k_ref.shape[1]
    c_len = kc_ref.shape[1]
    nk = lk + c_len
    i = pl.program_id(1)
    q0 = i * tq
    start = pl.multiple_of(jnp.clip(q0 - WINDOW, 0, s_len - lk), LANES)
    keys = jnp.concatenate([k_ref[0, pl.ds(start, lk), :], kc_ref[0]], axis=0)
    kpos = start + lax.broadcasted_iota(jnp.int32, (lk, tq), 0)
    qpos = q0 + lax.broadcasted_iota(jnp.int32, (lk, tq), 1)
    visible = jnp.abs(kpos - qpos) <= WINDOW
    visible = jnp.concatenate([visible] * grp, axis=1)
    ones = jnp.ones((ONES_ROWS, nk), BF16)
    qt = qt_ref[0]
    blank = jnp.zeros((d, grp * tq), BF16)
    outs = []
    for kv in range(WA_KV_HEADS):
        heads = range(kv * grp, (kv + 1) * grp)
        qcat = jnp.concatenate([qt[h * d:(h + 1) * d, :] for h in heads], axis=1)
        rhs = jnp.concatenate([qcat if j == kv else blank for j in range(WA_KV_HEADS)], axis=0)
        st = jnp.dot(keys, rhs, preferred_element_type=F32)
        st = jnp.concatenate([jnp.where(visible, st[:lk], MASK_VALUE), st[lk:]], axis=0)
        sink = jnp.concatenate([jnp.broadcast_to(sink_ref[:, h:h + 1] * LOG2E, (1, tq)) for h in heads], axis=1)
        m = jnp.maximum(jnp.max(st, axis=0, keepdims=True), sink)
        p = jnp.exp2(st - m).astype(BF16)
        vt = jnp.concatenate([vt_ref[0, kv * d:(kv + 1) * d, pl.ds(start, lk)],
                              vct_ref[0, kv * d:(kv + 1) * d, :]], axis=1)
        acc = jnp.dot(jnp.concatenate([vt, ones], axis=0), p, preferred_element_type=F32)
        o = acc[:d] / (acc[d:d + 1] + jnp.exp2(sink - m))
        outs.extend(o[:, g * tq:(g + 1) * tq] for g in range(grp))
    o_ref[0] = jnp.concatenate(outs, axis=0).T.astype(o_ref.dtype)


def _win_attn(qbt, kb, vbt, kbc, vbct, sink, tq):
    b, _, s = qbt.shape
    c = kbc.shape[1]
    lk = tq + 2 * WINDOW
    assert s >= lk and tq % LANES == 0
    whole = lambda bi, i: (bi, 0, 0)
    return pl.pallas_call(
        functools.partial(_win_attn_kernel, tq=tq, lk=lk),
        grid=(b, s // tq),
        in_specs=[pl.BlockSpec((1, QB_COLS, tq), lambda bi, i: (bi, 0, i)),
                  pl.BlockSpec((1, s, KB_COLS), whole),
                  pl.BlockSpec((1, VB_COLS, s), whole),
                  pl.BlockSpec((1, c, KB_COLS), whole),
                  pl.BlockSpec((1, VB_COLS, c), whole),
                  pl.BlockSpec((1, WA_HEADS), lambda bi, i: (0, 0))],
        out_specs=pl.BlockSpec((1, tq, QB_COLS), lambda bi, i: (bi, i, 0)),
        out_shape=jax.ShapeDtypeStruct((b, s, QB_COLS), BF16),
        compiler_params=_cparams(("arbitrary", "arbitrary")),
        name="win_attn",
    )(qbt, kb, vbt, kbc, vbct, sink)


def _outproj_router_kernel(ya_ref, yb_ref, woa_ref, wob_ref, x_ref, g1_ref, n2_ref, sh_ref, sc_ref,
                           wr_ref, br_ref, x1_ref, hx_ref, code_ref, gate_ref, cnt_ref, carry_ref):
    first = jnp.logical_and(pl.program_id(0) == 0, pl.program_id(1) == 0)

    @pl.when(first)
    def _():
        carry_ref[...] = jnp.zeros(carry_ref.shape, F32)

    y = (jnp.dot(ya_ref[0], woa_ref[...], preferred_element_type=F32)
         + jnp.dot(yb_ref[0], wob_ref[...], preferred_element_type=F32))
    x1 = x_ref[0] + g1_ref[0] * y
    x1_ref[0] = x1
    hx = _modulated_norm(x1, n2_ref[...], sh_ref[0], sc_ref[0])
    _store_row_tiles(hx_ref, hx)
    tm = hx.shape[0]
    hx_hi = hx.astype(BF16)
    hx_lo = (hx - hx_hi.astype(F32)).astype(BF16)
    wr = wr_ref[...]
    nt = (((1,), (1,)), ((), ()))
    part = lax.dot_general(wr, hx_hi, nt, preferred_element_type=F32)
    logits = (part[:N_EXPERTS] + part[N_EXPERTS:]
              + lax.dot_general(wr[:N_EXPERTS], hx_lo, nt, preferred_element_type=F32) + br_ref[...])

    row_e = lax.broadcasted_iota(jnp.int32, (N_EXPERTS, tm), 0).astype(F32)
    work = logits
    tops, idxs, hots = [], [], []
    for _k in range(TOP_K):
        m = jnp.max(work, axis=0, keepdims=True)
        idx = jnp.min(jnp.where(work == m, row_e, float(N_EXPERTS)), axis=0, keepdims=True)
        hot = row_e == idx
        work = jnp.where(hot, -jnp.inf, work)
        tops.append(m)
        idxs.append(idx)
        hots.append(hot)
    es = [jnp.exp(t - tops[0]) for t in tops]
    den = es[0] + es[1] + es[2] + es[3]

    multi = jnp.zeros((N_EXPERTS, tm), F32)
    for hot in hots:
        multi = multi + hot.astype(F32)
    r_i = lax.broadcasted_iota(jnp.int32, (tm, tm), 0)
    c_i = lax.broadcasted_iota(jnp.int32, (tm, tm), 1)
    tri = (r_i <= c_i).astype(BF16)
    incl = jnp.dot(multi.astype(BF16), tri, preferred_element_type=F32)
    before = carry_ref[...] + incl - 1.0

    codes, gates = [], []
    for k in range(TOP_K):
        rank = jnp.sum(jnp.where(hots[k], before, 0.0), axis=0, keepdims=True)
        codes.append(idxs[k].astype(jnp.int32) * 65536 + rank.astype(jnp.int32))
        gates.append(es[k] / den)
    fill = ROW_TILE - TOP_K
    code_ref[...] = jnp.concatenate(codes + [jnp.zeros((fill, tm), jnp.int32)], axis=0)
    gate_ref[...] = jnp.concatenate(gates + [jnp.zeros((fill, tm), F32)], axis=0)
    carry_ref[...] = carry_ref[...] + jnp.sum(multi, axis=1, keepdims=True)
    cnt_ref[...] = carry_ref[...]


def _outproj_router(ya, yb, woa, wob, x, g1, n2, sh2, sc2, w_r_parts, b_r, tm):
    b, s, d = x.shape
    nb = s // tm
    row = lambda bi, i: (bi, i, 0)
    mod = lambda bi, i: (bi, 0, 0)
    fixed = lambda bi, i: (0, 0)
    tok = lambda bi, i: (bi * nb + i, 0)
    half = ya.shape[2]
    return pl.pallas_call(
        _outproj_router_kernel,
        grid=(b, nb),
        in_specs=[pl.BlockSpec((1, tm, half), row),
                  pl.BlockSpec((1, tm, half), row),
                  pl.BlockSpec((half, d), fixed),
                  pl.BlockSpec((half, d), fixed),
                  pl.BlockSpec((1, tm, d), row),
                  pl.BlockSpec((1, 1, d), mod),
                  pl.BlockSpec((1, d), fixed),
                  pl.BlockSpec((1, 1, d), mod),
                  pl.BlockSpec((1, 1, d), mod),
                  pl.BlockSpec((2 * N_EXPERTS, d), fixed),
                  pl.BlockSpec((N_EXPERTS, 1), fixed)],
        out_specs=[pl.BlockSpec((1, tm, d), row),
                   pl.BlockSpec((tm * ROW_TILE, LANES), tok),
                   pl.BlockSpec((ROW_TILE, tm), lambda bi, i: (0, bi * nb + i)),
                   pl.BlockSpec((ROW_TILE, tm), lambda bi, i: (0, bi * nb + i)),
                   pl.BlockSpec((N_EXPERTS, 1), fixed)],
        out_shape=[jax.ShapeDtypeStruct((b, s, d), F32),
                   jax.ShapeDtypeStruct((b * s * ROW_TILE, LANES), F32),
                   jax.ShapeDtypeStruct((ROW_TILE, b * s), jnp.int32),
                   jax.ShapeDtypeStruct((ROW_TILE, b * s), F32),
                   jax.ShapeDtypeStruct((N_EXPERTS, 1), F32)],
        scratch_shapes=[pltpu.VMEM((N_EXPERTS, 1), F32)],
        compiler_params=_cparams(("arbitrary", "arbitrary")),
        name="outproj_router",
    )(ya, yb, woa, wob, x, g1, n2, sh2, sc2, w_r_parts, b_r)


def _sorted_rows_kernel(pbound_ref, code_ref, dest_ref):
    code = code_ref[...]
    expert = code >> 16
    base = jnp.zeros_like(code)
    for e in range(N_EXPERTS):
        base = jnp.where(expert == e, pbound_ref[e], base)
    dest_ref[...] = base + (code & 0xFFFF)


def _sorted_rows(pbound, code, tn):
    rows, n_tok = code.shape
    return pl.pallas_call(
        _sorted_rows_kernel,
        grid_spec=pltpu.PrefetchScalarGridSpec(
            num_scalar_prefetch=1,
            grid=(n_tok // tn,),
            in_specs=[pl.BlockSpec((rows, tn), lambda i, pb: (0, i))],
            out_specs=pl.BlockSpec((rows, tn), lambda i, pb: (0, i))),
        out_shape=jax.ShapeDtypeStruct((rows, n_tok), jnp.int32),
        compiler_params=_cparams(("arbitrary",)),
        name="moe_sorted_rows",
    )(pbound, code)


def _row_copy(src_hbm, src_row, dst_ref, dst_row, sem):
    return pltpu.make_async_copy(src_hbm.at[pl.ds(src_row, 1)], dst_ref.at[pl.ds(dst_row, 1)], sem)


def _store_row_tiles(ref, val):
    n = val.shape[0]
    for c in range(ROW_TILE):
        ref[pl.ds(c, n, stride=ROW_TILE), :] = val[:, c * LANES:(c + 1) * LANES]


def _load_row_tiles(ref, n):
    return [ref[pl.ds(c, n, stride=ROW_TILE), :] for c in range(ROW_TILE)]


def _tile_rows(row, count=1):
    if isinstance(row, int):
        return pl.ds(row * ROW_TILE, count * ROW_TILE)
    return pl.ds(pl.multiple_of(row * ROW_TILE, ROW_TILE), count * ROW_TILE)


def _tile_copy(src_ref, src_row, dst_ref, dst_row, sem):
    return pltpu.make_async_copy(src_ref.at[_tile_rows(src_row)], dst_ref.at[_tile_rows(dst_row)], sem)


def _dispatch_kernel(pbound_ref, cnt_ref, dest_ref, hx_ref, xs_hbm, zeros, sem, zsem, *, chunk):
    n_rows = xs_hbm.shape[0] // ROW_TILE
    blk = zeros.shape[0] // ROW_TILE

    @pl.when(pl.program_id(0) == 0)
    def _():
        zeros[...] = jnp.zeros(zeros.shape, F32)
        tail_blocks = (n_rows - pbound_ref[N_EXPERTS]) // blk

        def pad_row(r, carry):
            _tile_copy(zeros, 0, xs_hbm, r, zsem).start()
            return carry

        def pad_expert(e, carry):
            lax.fori_loop(pbound_ref[e] + cnt_ref[e], pbound_ref[e + 1], pad_row, 0)
            return carry

        def tail_copy(t):
            return pltpu.make_async_copy(zeros, xs_hbm.at[_tile_rows(pbound_ref[N_EXPERTS] + t * blk, blk)], zsem)

        def tail_start(t, carry):
            tail_copy(t).start()
            return carry

        lax.fori_loop(0, N_EXPERTS, pad_expert, 0)
        lax.fori_loop(0, tail_blocks, tail_start, 0)

        def pad_row_wait(r, carry):
            _tile_copy(zeros, 0, xs_hbm, 0, zsem).wait()
            return carry

        def pad_expert_wait(e, carry):
            lax.fori_loop(pbound_ref[e] + cnt_ref[e], pbound_ref[e + 1], pad_row_wait, 0)
            return carry

        def tail_wait(t, carry):
            tail_copy(0).wait()
            return carry

        lax.fori_loop(0, N_EXPERTS, pad_expert_wait, 0)
        lax.fori_loop(0, tail_blocks, tail_wait, 0)

    for j in range(chunk):
        _tile_copy(hx_ref, j // TOP_K, xs_hbm, dest_ref[j % TOP_K, j // TOP_K], sem).start(priority=j % 2)
    pltpu.make_async_copy(xs_hbm.at[_tile_rows(0, chunk)], xs_hbm.at[_tile_rows(0, chunk)], sem).wait()


def _dispatch(pbound, counts_i, dest, hx_tiles, n_rows, chunk):
    n_assign = dest.shape[1] * TOP_K
    return pl.pallas_call(
        functools.partial(_dispatch_kernel, chunk=chunk),
        grid_spec=pltpu.PrefetchScalarGridSpec(
            num_scalar_prefetch=2,
            grid=(n_assign // chunk,),
            in_specs=[pl.BlockSpec((ROW_TILE, chunk // TOP_K), lambda i, pb, ct: (0, i), memory_space=pltpu.SMEM),
                      pl.BlockSpec((chunk // TOP_K * ROW_TILE, LANES), lambda i, pb, ct: (i, 0))],
            out_specs=pl.BlockSpec(memory_space=pl.ANY),
            scratch_shapes=[pltpu.VMEM((EXPERT_BLOCK * ROW_TILE, LANES), F32),
                            pltpu.SemaphoreType.DMA(()),
                            pltpu.SemaphoreType.DMA(())]),
        out_shape=jax.ShapeDtypeStruct((n_rows * ROW_TILE, LANES), F32),
        compiler_params=_cparams(("arbitrary",)),
        name="moe_dispatch",
    )(pbound, counts_i, dest, hx_tiles)


def _expert_kernel(bexp_ref, run_ref, nxt_ref, nact_ref, xs_ref, wgu_hbm, wdn_hbm, bg_ref, bu_ref, bd_ref,
                   o_ref, wgu_f, wdn_f, wg_s, wu_s, wd_s, wsem):
    i = pl.program_id(0)
    nact = nact_ref[0]

    def weight_copies(expert, w):
        return (pltpu.make_async_copy(wgu_hbm.at[expert], wgu_f.at[w], wsem.at[w, 0]),
                pltpu.make_async_copy(wdn_hbm.at[expert], wdn_f.at[w], wsem.at[w, 1]))

    @pl.when(i == 0)
    def _():
        for cp in weight_copies(bexp_ref[0], 0):
            cp.start()

    @pl.when(i < nact)
    def _():
        changed = jnp.logical_or(i == 0, bexp_ref[i] != bexp_ref[jnp.maximum(i - 1, 0)])

        @pl.when(changed)
        def _():
            w = run_ref[i] % 2
            for cp in weight_copies(bexp_ref[i], w):
                cp.wait()
            half = MXU_COLS // 2
            src = lax.broadcasted_iota(jnp.int32, (MXU_COLS, MXU_COLS), 0)
            dst = lax.broadcasted_iota(jnp.int32, (MXU_COLS, MXU_COLS), 1)
            perm = (src == jnp.where(dst < half, 2 * dst, 2 * (dst - half) + 1)).astype(BF16)
            for k in range(wgu_f.shape[2] // MXU_COLS):
                wk = wgu_f[w, :, k * MXU_COLS:(k + 1) * MXU_COLS].astype(BF16)
                sep = jnp.dot(wk, perm, preferred_element_type=F32).astype(BF16)
                wg_s[:, k * half:(k + 1) * half] = sep[:, :half]
                wu_s[:, k * half:(k + 1) * half] = sep[:, half:]
            wd_s[...] = wdn_f[w].astype(BF16)

            @pl.when(nxt_ref[i] >= 0)
            def _():
                for cp in weight_copies(nxt_ref[i], 1 - w):
                    cp.start()

        xb = jnp.concatenate([c.astype(BF16) for c in _load_row_tiles(xs_ref, EXPERT_BLOCK)], axis=1)
        g = jnp.dot(xb, wg_s[...], preferred_element_type=F32) + bg_ref[0]
        u = jnp.dot(xb, wu_s[...], preferred_element_type=F32) + bu_ref[0]
        g = jnp.minimum(g, SWIGLU_LIMIT)
        u = jnp.clip(u, -SWIGLU_LIMIT, SWIGLU_LIMIT)
        a = g * (1.0 / (1.0 + jnp.exp(-SWIGLU_ALPHA * g))) * (u + 1.0)
        _store_row_tiles(o_ref, jnp.dot(a.astype(BF16), wd_s[...], preferred_element_type=F32) + bd_ref[0])

    @pl.when(i >= nact)
    def _():
        o_ref[...] = jnp.zeros(o_ref.shape, F32)


def _experts(block_exp, run_id, next_exp, n_active, xs, w_gu, w_dn, bg, bu, bd):
    f, d = w_dn.shape[1:]
    n_rows = xs.shape[0] // ROW_TILE
    nblk = n_rows // EXPERT_BLOCK
    bsel = lambda i, be, ru, nx, na: (be[i], 0, 0)
    blk = lambda i, be, ru, nx, na: (i, 0)
    anyspace = pl.BlockSpec(memory_space=pl.ANY)
    return pl.pallas_call(
        _expert_kernel,
        grid_spec=pltpu.PrefetchScalarGridSpec(
            num_scalar_prefetch=4,
            grid=(nblk,),
            in_specs=[pl.BlockSpec((EXPERT_BLOCK * ROW_TILE, LANES), blk),
                      anyspace, anyspace,
                      pl.BlockSpec((1, 1, f), bsel),
                      pl.BlockSpec((1, 1, f), bsel),
                      pl.BlockSpec((1, 1, d), bsel)],
            out_specs=pl.BlockSpec((EXPERT_BLOCK * ROW_TILE, LANES), blk),
            scratch_shapes=[pltpu.VMEM((2, d, 2 * f), F32),
                            pltpu.VMEM((2, f, d), F32),
                            pltpu.VMEM((d, f), BF16), pltpu.VMEM((d, f), BF16), pltpu.VMEM((f, d), BF16),
                            pltpu.SemaphoreType.DMA((2, 2))]),
        out_shape=jax.ShapeDtypeStruct((n_rows * ROW_TILE, LANES), F32),
        compiler_params=_cparams(("arbitrary",)),
        name="moe_experts",
    )(block_exp, run_id, next_exp, n_active, xs, w_gu, w_dn, bg, bu, bd)


def _combine_kernel(dest_ref, rows_hbm, gate_ref, x1_ref, g2_ref, fg_ref, o_ref, buf, sems, *, tm):
    n = pl.program_id(0)
    slot = n % 2

    def row_gather(step, to_slot, tok, k):
        n_tok = tm * pl.num_programs(0)
        return _tile_copy(rows_hbm, dest_ref[k * n_tok + step * tm + tok], buf.at[to_slot, k], tok, sems.at[to_slot])

    @pl.when(n == 0)
    def _():
        def issue(tok, carry):
            for k in range(TOP_K):
                row_gather(0, 0, tok, k).start()
            return carry
        lax.fori_loop(0, tm, issue, 0, unroll=4)

    @pl.when(n + 1 < pl.num_programs(0))
    def _():
        for tok in range(tm):
            for k in range(TOP_K):
                row_gather(n + 1, 1 - slot, tok, k).start(priority=k % 2)

    for k in range(TOP_K):
        pltpu.make_async_copy(rows_hbm.at[_tile_rows(0, tm)], buf.at[slot, k], sems.at[slot]).wait()

    gate = gate_ref[...]
    parts = None
    for k in range(TOP_K):
        gk = gate[:, k:k + 1]
        tiles = _load_row_tiles(buf.at[slot, k], tm)
        parts = [gk * r for r in tiles] if parts is None else [p + gk * r for p, r in zip(parts, tiles)]
    y = jnp.concatenate(parts, axis=1)
    xo = x1_ref[...] + g2_ref[0] * y
    ms = jnp.mean(xo * xo, axis=-1, keepdims=True)
    o_ref[...] = xo * lax.rsqrt(ms + NORM_EPS) * fg_ref[...]


def _combine(dest, rows, gates, x1, g2, final_g, tm):
    b, s, d = x1.shape
    nb = s // tm
    out = pl.pallas_call(
        functools.partial(_combine_kernel, tm=tm),
        grid_spec=pltpu.PrefetchScalarGridSpec(
            num_scalar_prefetch=1,
            grid=(b * nb,),
            in_specs=[pl.BlockSpec(memory_space=pl.ANY),
                      pl.BlockSpec((tm, TOP_K), lambda n, ds: (n, 0)),
                      pl.BlockSpec((tm, d), lambda n, ds: (n, 0)),
                      pl.BlockSpec((1, 1, d), lambda n, ds: (n // nb, 0, 0)),
                      pl.BlockSpec((1, d), lambda n, ds: (0, 0))],
            out_specs=pl.BlockSpec((tm, d), lambda n, ds: (n, 0)),
            scratch_shapes=[pltpu.VMEM((2, TOP_K, tm * ROW_TILE, LANES), F32), pltpu.SemaphoreType.DMA((2,))]),
        out_shape=jax.ShapeDtypeStruct((b * s, d), F32),
        compiler_params=_cparams(("arbitrary",)),
        name="moe_combine",
    )(dest, rows, gates, x1.reshape(b * s, d), g2, final_g)
    return out.reshape(b, s, d)


def _pick(n, prefs):
    for p in prefs:
        if n % p == 0:
            return p
    raise ValueError(f"no tile for {n}")


def kernel(x, c, ctx, c_ctx, w_mod, b_mod, norm1_g, w_in, lam_q1, lam_k1, lam_q2, lam_k2, subln_g,
           sink, w_out, norm2_g, w_router, b_router, w_gate_up, b_gate_up, w_down, b_down, final_g):
    b, s, d = x.shape
    c_len = ctx.shape[1]
    assert w_mod.shape[0] == 1, "single-layer block"
    assert d == ROW_TILE * LANES, "dispatched token rows are one (8, 128) f32 tile each"
    t = b * s

    pad = (-(b + 1)) % 8
    cvecs = jnp.concatenate([c, c_ctx[None, :], jnp.zeros((pad, d), F32)], axis=0)
    mod = _adaln(cvecs, w_mod[0], b_mod[0])
    sh1, sc1, g1, sh2, sc2, g2 = [mod[:b, k * d:(k + 1) * d].reshape(b, 1, d) for k in range(6)]
    csh1 = mod[b:b + 1, 0:d].reshape(1, 1, d)
    csc1 = mod[b:b + 1, d:2 * d].reshape(1, 1, d)

    w_in_bf = w_in[0].astype(BF16)
    cos, sin = _rope_tables(s)
    n1 = norm1_g[0].reshape(1, d)
    qat, ka, vat, qbt, kb, vbt = _inproj_latent(x, n1, sh1, sc1, w_in_bf, cos, sin, _pick(s, (512, 256, 128)))
    w_ctx_bf = jnp.concatenate([w_in_bf[:, O_KA:O_QB], w_in_bf[:, O_KB:IN_COLS]], axis=1)
    kac, vact, kbc, vbct = _inproj_ctx(ctx, n1, csh1, csc1, w_ctx_bf)

    ka_all = jnp.concatenate([ka, kac], axis=1)
    vat_all = jnp.concatenate([vat, vact], axis=2)
    sk = s + c_len
    ya = _diff_attn(qat, ka_all, vat_all,
                    lam_q1[0].reshape(1, -1), lam_k1[0].reshape(1, -1),
                    lam_q2[0].reshape(1, -1), lam_k2[0].reshape(1, -1),
                    subln_g[0].reshape(1, -1),
                    _pick(s, (1024, 512, 256, 128)), _pick(sk, (768, 512, 384, 256, 128)))

    yb = _win_attn(qbt, kb, vbt, kbc, vbct, sink[0].reshape(1, -1), _pick(s, (256, 128)))

    w_out_bf = w_out[0].astype(BF16)
    w_r_hi = w_router[0].astype(BF16)
    w_r_lo = (w_router[0] - w_r_hi.astype(F32)).astype(BF16)
    x1, hx2, code, gates, counts = _outproj_router(
        ya, yb, w_out_bf[:QA_COLS], w_out_bf[QA_COLS:], x, g1, norm2_g[0].reshape(1, d), sh2, sc2,
        jnp.concatenate([w_r_hi, w_r_lo], axis=1).T, b_router[0].reshape(-1, 1), _pick(s, (512, 256, 128)))

    n_assign = t * TOP_K
    n_rows = n_assign + N_EXPERTS * EXPERT_BLOCK
    counts_i = counts.reshape(-1).astype(jnp.int32)
    padded = ((counts_i + EXPERT_BLOCK - 1) // EXPERT_BLOCK) * EXPERT_BLOCK
    pend = jnp.cumsum(padded).astype(jnp.int32)
    pbound = jnp.concatenate([jnp.zeros((1,), jnp.int32), pend])
    block_start = jnp.arange(n_rows // EXPERT_BLOCK, dtype=jnp.int32) * EXPERT_BLOCK
    block_exp = jnp.minimum(jnp.sum((pend[None, :] <= block_start[:, None]).astype(jnp.int32), axis=1),
                            N_EXPERTS - 1)
    n_active = pend[-1:] // EXPERT_BLOCK
    changed = jnp.concatenate([jnp.ones((1,), jnp.int32), (block_exp[1:] != block_exp[:-1]).astype(jnp.int32)])
    run_id = jnp.cumsum(changed).astype(jnp.int32) - 1
    e_ids = jnp.arange(N_EXPERTS, dtype=jnp.int32)
    later_nonempty = (e_ids[None, :] > e_ids[:, None]) & (padded[None, :] > 0)
    next_of_expert = jnp.min(jnp.where(later_nonempty, e_ids[None, :], N_EXPERTS), axis=1)
    next_of_expert = jnp.where(next_of_expert < N_EXPERTS, next_of_expert, -1).astype(jnp.int32)
    next_exp = jnp.sum(jnp.where(block_exp[:, None] == e_ids[None, :], next_of_expert[None, :], 0), axis=1)

    dest = _sorted_rows(pbound, code, _pick(t, (2048, 1024, 512, 256)))
    xs = _dispatch(pbound, counts_i, dest, hx2, n_rows, _pick(n_assign, (1024,)))

    f = w_down.shape[2]
    bg = b_gate_up[0, :, 0::2].reshape(N_EXPERTS, 1, f)
    bu = b_gate_up[0, :, 1::2].reshape(N_EXPERTS, 1, f)
    bd = b_down[0].reshape(N_EXPERTS, 1, d)
    rows = _experts(block_exp, run_id, next_exp, n_active, xs, w_gate_up[0], w_down[0], bg, bu, bd)

    return _combine(dest[:TOP_K].reshape(-1), rows, gates[:TOP_K].T, x1, g2, final_g.reshape(1, d), _pick(s, (256,)))
```

```python
import functools
import math

import jax
import jax.numpy as jnp
import numpy as np
from jax import lax
from jax.experimental import pallas as pl
from jax.experimental.pallas import tpu as pltpu

F32 = jnp.float32
BF16 = jnp.bfloat16

GRID_W = 64
NORM_EPS = 1e-6
ROPE_BASE = 10000.0
MASK_VALUE = -1e30
DA_HEADS = 4
HEAD_DIM = 64
WA_HEADS = 8
WA_KV_HEADS = 2
WA_GROUP = WA_HEADS // WA_KV_HEADS
WINDOW = 128
N_EXPERTS = 32
TOP_K = 4
SWIGLU_LIMIT = 7.0
SWIGLU_ALPHA = 1.702
EXPERT_BLOCK = 256
LAM_INIT = 0.8 - 0.6 * math.exp(-0.3 * 0)
LOG2E = math.log2(math.e)

QA_COLS = DA_HEADS * 2 * HEAD_DIM
KA_COLS = QA_COLS
VA_COLS = QA_COLS
QB_COLS = WA_HEADS * HEAD_DIM
KB_COLS = WA_KV_HEADS * HEAD_DIM
VB_COLS = KB_COLS
O_QA = 0
O_KA = O_QA + QA_COLS
O_VA = O_KA + KA_COLS
O_QB = O_VA + VA_COLS
O_KB = O_QB + QB_COLS
O_VB = O_KB + KB_COLS
IN_COLS = O_VB + VB_COLS

LANES = 128
ROW_TILE = 8
MXU_COLS = 256
ONES_ROWS = 16
VMEM_LIMIT = 56 * 1024 * 1024


def _cparams(sem):
    return pltpu.CompilerParams(dimension_semantics=sem, vmem_limit_bytes=VMEM_LIMIT)


def _adaln_kernel(c_ref, w_ref, b_ref, o_ref):
    cv = c_ref[...]
    s = cv * (1.0 / (1.0 + jnp.exp(-cv)))
    rows = s.shape[0]
    w = w_ref[...]
    s_hi = s.astype(BF16)
    s_lo = (s - s_hi.astype(F32)).astype(BF16)
    w_hi = w.astype(BF16)
    w_lo = (w - w_hi.astype(F32)).astype(BF16)
    part = jnp.dot(jnp.concatenate([s_hi, s_lo], axis=0), w_hi, preferred_element_type=F32)
    o_ref[...] = part[:rows] + part[rows:] + jnp.dot(s_hi, w_lo, preferred_element_type=F32) + b_ref[...]


def _adaln(cvecs, w_mod, b_mod):
    rows, d = cvecs.shape
    n = w_mod.shape[1]
    tn = 1024
    return pl.pallas_call(
        _adaln_kernel,
        grid=(n // tn,),
        in_specs=[pl.BlockSpec((rows, d), lambda j: (0, 0)),
                  pl.BlockSpec((d, tn), lambda j: (0, j)),
                  pl.BlockSpec((1, tn), lambda j: (0, j))],
        out_specs=pl.BlockSpec((rows, tn), lambda j: (0, j)),
        out_shape=jax.ShapeDtypeStruct((rows, n), F32),
        compiler_params=_cparams(("arbitrary",)),
        name="adaln",
    )(cvecs, w_mod, b_mod.reshape(1, n))


def _rope_section(sec, cos, sin):
    tm = sec.shape[0]
    lane = lax.broadcasted_iota(jnp.int32, (tm, LANES), 1)
    low = (lane % 32) < 16
    outs = []
    for j in range(sec.shape[1] // LANES):
        c = sec[:, j * LANES:(j + 1) * LANES]
        partner = jnp.where(low, pltpu.roll(c, LANES - 16, 1), pltpu.roll(c, 16, 1))
        outs.append(c * cos + partner * sin)
    return jnp.concatenate(outs, axis=1)


def _modulated_norm(x, g, shift, scale):
    ms = jnp.mean(x * x, axis=-1, keepdims=True)
    return (x * lax.rsqrt(ms + NORM_EPS) * g) * (1.0 + scale) + shift


def _inproj_latent_kernel(x_ref, g_ref, sh_ref, sc_ref, w_ref, cos_ref, sin_ref,
                          qat_ref, ka_ref, vat_ref, qbt_ref, kb_ref, vbt_ref):
    h = _modulated_norm(x_ref[0], g_ref[...], sh_ref[0], sc_ref[0])
    p = jnp.dot(h.astype(BF16), w_ref[...], preferred_element_type=F32)
    cos = cos_ref[...]
    sin = sin_ref[...]
    qscale = HEAD_DIM ** -0.5 * LOG2E
    qat_ref[0] = (_rope_section(p[:, O_QA:O_KA], cos, sin) * qscale).T.astype(BF16)
    ka_ref[0] = _rope_section(p[:, O_KA:O_VA], cos, sin).astype(BF16)
    vat_ref[0] = p[:, O_VA:O_QB].T.astype(BF16)
    qbt_ref[0] = (_rope_section(p[:, O_QB:O_KB], cos, sin) * qscale).T.astype(BF16)
    kb_ref[0] = _rope_section(p[:, O_KB:O_VB], cos, sin).astype(BF16)
    vbt_ref[0] = p[:, O_VB:IN_COLS].T.astype(BF16)


def _inproj_ctx_kernel(x_ref, g_ref, sh_ref, sc_ref, w_ref, ka_ref, vat_ref, kb_ref, vbt_ref):
    h = _modulated_norm(x_ref[0], g_ref[...], sh_ref[0], sc_ref[0])
    p = jnp.dot(h.astype(BF16), w_ref[...], preferred_element_type=F32)
    ka_ref[0] = p[:, 0:KA_COLS].astype(BF16)
    vat_ref[0] = p[:, KA_COLS:KA_COLS + VA_COLS].T.astype(BF16)
    kb_ref[0] = p[:, KA_COLS + VA_COLS:KA_COLS + VA_COLS + KB_COLS].astype(BF16)
    vbt_ref[0] = p[:, KA_COLS + VA_COLS + KB_COLS:].T.astype(BF16)


def _rope_tables(n_tok):
    pos = np.arange(n_tok)
    nf = HEAD_DIM // 4
    inv = ROPE_BASE ** (-np.arange(nf) / nf)
    ar = (pos // GRID_W)[:, None] * inv
    ac = (pos % GRID_W)[:, None] * inv
    cos = np.concatenate([np.cos(ar), np.cos(ar), np.cos(ac), np.cos(ac)], axis=1)
    sin = np.concatenate([-np.sin(ar), np.sin(ar), -np.sin(ac), np.sin(ac)], axis=1)
    reps = (1, LANES // HEAD_DIM)
    return jnp.asarray(np.tile(cos, reps), F32), jnp.asarray(np.tile(sin, reps), F32)


def _inproj_latent(x, g, shift, scale, w_bf16, cos, sin, tm):
    b, s, d = x.shape
    row = lambda bi, i: (bi, i, 0)
    colt = lambda bi, i: (bi, 0, i)
    mod = lambda bi, i: (bi, 0, 0)
    fixed = lambda bi, i: (0, 0)
    return pl.pallas_call(
        _inproj_latent_kernel,
        grid=(b, s // tm),
        in_specs=[pl.BlockSpec((1, tm, d), row),
                  pl.BlockSpec((1, d), fixed),
                  pl.BlockSpec((1, 1, d), mod),
                  pl.BlockSpec((1, 1, d), mod),
                  pl.BlockSpec((d, IN_COLS), fixed),
                  pl.BlockSpec((tm, LANES), lambda bi, i: (i, 0)),
                  pl.BlockSpec((tm, LANES), lambda bi, i: (i, 0))],
        out_specs=[pl.BlockSpec((1, QA_COLS, tm), colt),
                   pl.BlockSpec((1, tm, KA_COLS), row),
                   pl.BlockSpec((1, VA_COLS, tm), colt),
                   pl.BlockSpec((1, QB_COLS, tm), colt),
                   pl.BlockSpec((1, tm, KB_COLS), row),
                   pl.BlockSpec((1, VB_COLS, tm), colt)],
        out_shape=[jax.ShapeDtypeStruct((b, QA_COLS, s), BF16),
                   jax.ShapeDtypeStruct((b, s, KA_COLS), BF16),
                   jax.ShapeDtypeStruct((b, VA_COLS, s), BF16),
                   jax.ShapeDtypeStruct((b, QB_COLS, s), BF16),
                   jax.ShapeDtypeStruct((b, s, KB_COLS), BF16),
                   jax.ShapeDtypeStruct((b, VB_COLS, s), BF16)],
        compiler_params=_cparams(("arbitrary", "arbitrary")),
        name="inproj_latent",
    )(x, g, shift, scale, w_bf16, cos, sin)


def _inproj_ctx(ctx, g, shift, scale, w_ctx_bf16):
    b, c, d = ctx.shape
    n = w_ctx_bf16.shape[1]
    whole = lambda bi: (bi, 0, 0)
    mod = lambda bi: (0, 0, 0)
    fixed = lambda bi: (0, 0)
    return pl.pallas_call(
        _inproj_ctx_kernel,
        grid=(b,),
        in_specs=[pl.BlockSpec((1, c, d), whole),
                  pl.BlockSpec((1, d), fixed),
                  pl.BlockSpec((1, 1, d), mod),
                  pl.BlockSpec((1, 1, d), mod),
                  pl.BlockSpec((d, n), fixed)],
        out_specs=[pl.BlockSpec((1, c, KA_COLS), whole),
                   pl.BlockSpec((1, VA_COLS, c), whole),
                   pl.BlockSpec((1, c, KB_COLS), whole),
                   pl.BlockSpec((1, VB_COLS, c), whole)],
        out_shape=[jax.ShapeDtypeStruct((b, c, KA_COLS), BF16),
                   jax.ShapeDtypeStruct((b, VA_COLS, c), BF16),
                   jax.ShapeDtypeStruct((b, c, KB_COLS), BF16),
                   jax.ShapeDtypeStruct((b, VB_COLS, c), BF16)],
        compiler_params=_cparams(("arbitrary",)),
        name="inproj_ctx",
    )(ctx, g, shift, scale, w_ctx_bf16)


def _diff_attn_kernel(qt_ref, k_ref, vt_ref, lq1_ref, lk1_ref, lq2_ref, lk2_ref, sg_ref, o_ref,
                      m_ref, acc_ref, s_ref, *, tq, tk):
    d = HEAD_DIM
    hw = 2 * d
    n_chunks = k_ref.shape[1] // tk
    n_tiles = qt_ref.shape[2] // tq
    ones = jnp.ones((ONES_ROWS, tk), BF16)
    lam = (jnp.exp(jnp.sum(lq1_ref[...] * lk1_ref[...], axis=-1, keepdims=True))
           - jnp.exp(jnp.sum(lq2_ref[...] * lk2_ref[...], axis=-1, keepdims=True)) + LAM_INIT)

    def query_rhs(t):
        qt = qt_ref[0, :, pl.ds(pl.multiple_of(t * tq, tq), tq)]
        row = lax.broadcasted_iota(jnp.int32, qt.shape, 0)
        zero = jnp.zeros_like(qt)
        return jnp.where(row < d, qt, zero), jnp.where(row >= d, qt, zero)

    def scores(c, j, rhs):
        off = pl.multiple_of(j * tk, tk)
        s_ref[c] = jnp.dot(k_ref[0, pl.ds(off, tk), :], rhs[c], preferred_element_type=F32)

    def accumulate(c, j):
        off = pl.multiple_of(j * tk, tk)
        vt = jnp.concatenate([vt_ref[0, :, pl.ds(off, tk)], ones], axis=0)
        st = s_ref[c]
        m_old = m_ref[c]
        m_new = jnp.maximum(m_old, jnp.max(st, axis=0, keepdims=True))
        alpha = jnp.exp2(m_old - m_new)
        p = jnp.exp2(st - m_new).astype(BF16)
        acc_ref[c] = alpha * acc_ref[c] + jnp.dot(vt, p, preferred_element_type=F32)
        m_ref[c] = m_new

    group = 5 if (n_chunks - 1) % 5 == 0 else 1

    scores(0, 0, query_rhs(0))

    def tile(t, carry):
        rhs = query_rhs(t)
        m_ref[...] = jnp.full(m_ref.shape, -jnp.inf, F32)
        acc_ref[...] = jnp.zeros(acc_ref.shape, F32)

        def chunk_group(jj, carry2):
            for r in range(group):
                j = group * jj + r
                scores(1, j, rhs)
                accumulate(0, j)
                scores(0, j + 1, rhs)
                accumulate(1, j)
            return carry2

        lax.fori_loop(0, (n_chunks - 1) // group, chunk_group, 0)
        scores(1, n_chunks - 1, rhs)
        accumulate(0, n_chunks - 1)
        scores(0, 0, query_rhs(jnp.minimum(t + 1, n_tiles - 1)))
        accumulate(1, n_chunks - 1)

        a1 = acc_ref[0]
        a2 = acc_ref[1]
        ot = a1[:hw] / a1[hw:hw + 1] - lam * (a2[:hw] / a2[hw:hw + 1])
        ms = jnp.mean(ot * ot, axis=0, keepdims=True)
        ot = ot * lax.rsqrt(ms + NORM_EPS)
        o_ref[0, pl.ds(pl.multiple_of(t * tq, tq), tq), :] = (ot.T * (sg_ref[...] * (1.0 - LAM_INIT))).astype(o_ref.dtype)
        return carry

    lax.fori_loop(0, n_tiles, tile, 0)


def _diff_attn(qat, ka, vat, lq1, lk1, lq2, lk2, subln_g, tq, tk):
    b, _, s = qat.shape
    sk = ka.shape[1]
    hw = 2 * HEAD_DIM
    vec = lambda bi, h: (0, 0)
    return pl.pallas_call(
        functools.partial(_diff_attn_kernel, tq=tq, tk=tk),
        grid=(b, DA_HEADS),
        in_specs=[pl.BlockSpec((1, hw, s), lambda bi, h: (bi, h, 0)),
                  pl.BlockSpec((1, sk, hw), lambda bi, h: (bi, 0, h)),
                  pl.BlockSpec((1, hw, sk), lambda bi, h: (bi, h, 0)),
                  pl.BlockSpec((1, HEAD_DIM), vec),
                  pl.BlockSpec((1, HEAD_DIM), vec),
                  pl.BlockSpec((1, HEAD_DIM), vec),
                  pl.BlockSpec((1, HEAD_DIM), vec),
                  pl.BlockSpec((1, hw), vec)],
        out_specs=pl.BlockSpec((1, s, hw), lambda bi, h: (bi, 0, h)),
        out_shape=jax.ShapeDtypeStruct((b, s, DA_HEADS * hw), BF16),
        scratch_shapes=[pltpu.VMEM((2, 1, tq), F32),
                        pltpu.VMEM((2, hw + ONES_ROWS, tq), F32),
                        pltpu.VMEM((2, tk, tq), F32)],
        compiler_params=_cparams(("arbitrary", "arbitrary")),
        name="diff_attn",
    )(qat, ka, vat, lq1, lk1, lq2, lk2, subln_g)


def _win_attn_kernel(qt_ref, k_ref, vt_ref, kc_ref, vct_ref, sink_ref, o_ref, *, tq, lk):
    d = HEAD_DIM
    grp = WA_GROUP
    s_len = k_ref.shape[1]
    c_len = kc_ref.shape[1]
    nk = lk + c_len
    i = pl.program_id(1)
    q0 = i * tq
    start = pl.multiple_of(jnp.clip(q0 - WINDOW, 0, s_len - lk), LANES)
    keys = jnp.concatenate([k_ref[0, pl.ds(start, lk), :], kc_ref[0]], axis=0)
    kpos = start + lax.broadcasted_iota(jnp.int32, (lk, tq), 0)
    qpos = q0 + lax.broadcasted_iota(jnp.int32, (lk, tq), 1)
    visible = jnp.abs(kpos - qpos) <= WINDOW
    visible = jnp.concatenate([visible] * grp, axis=1)
    ones = jnp.ones((ONES_ROWS, nk), BF16)
    qt = qt_ref[0]
    blank = jnp.zeros((d, grp * tq), BF16)
    outs = []
    for kv in range(WA_KV_HEADS):
        heads = range(kv * grp, (kv + 1) * grp)
        qcat = jnp.concatenate([qt[h * d:(h + 1) * d, :] for h in heads], axis=1)
        rhs = jnp.concatenate([qcat if j == kv else blank for j in range(WA_KV_HEADS)], axis=0)
        st = jnp.dot(keys, rhs, preferred_element_type=F32)
        st = jnp.concatenate([jnp.where(visible, st[:lk], MASK_VALUE), st[lk:]], axis=0)
        sink = jnp.concatenate([jnp.broadcast_to(sink_ref[:, h:h + 1] * LOG2E, (1, tq)) for h in heads], axis=1)
        m = jnp.maximum(jnp.max(st, axis=0, keepdims=True), sink)
        p = jnp.exp2(st - m).astype(BF16)
        vt = jnp.concatenate([vt_ref[0, kv * d:(kv + 1) * d, pl.ds(start, lk)],
                              vct_ref[0, kv * d:(kv + 1) * d, :]], axis=1)
        acc = jnp.dot(jnp.concatenate([vt, ones], axis=0), p, preferred_element_type=F32)
        o = acc[:d] / (acc[d:d + 1] + jnp.exp2(sink - m))
        outs.extend(o[:, g * tq:(g + 1) * tq] for g in range(grp))
    o_ref[0] = jnp.concatenate(outs, axis=0).T.astype(o_ref.dtype)


def _win_attn(qbt, kb, vbt, kbc, vbct, sink, tq):
    b, _, s = qbt.shape
    c = kbc.shape[1]
    lk = tq + 2 * WINDOW
    assert s >= lk and tq % LANES == 0
    whole = lambda bi, i: (bi, 0, 0)
    return pl.pallas_call(
        functools.partial(_win_attn_kernel, tq=tq, lk=lk),
        grid=(b, s // tq),
        in_specs=[pl.BlockSpec((1, QB_COLS, tq), lambda bi, i: (bi, 0, i)),
                  pl.BlockSpec((1, s, KB_COLS), whole),
                  pl.BlockSpec((1, VB_COLS, s), whole),
                  pl.BlockSpec((1, c, KB_COLS), whole),
                  pl.BlockSpec((1, VB_COLS, c), whole),
                  pl.BlockSpec((1, WA_HEADS), lambda bi, i: (0, 0))],
        out_specs=pl.BlockSpec((1, tq, QB_COLS), lambda bi, i: (bi, i, 0)),
        out_shape=jax.ShapeDtypeStruct((b, s, QB_COLS), BF16),
        compiler_params=_cparams(("arbitrary", "arbitrary")),
        name="win_attn",
    )(qbt, kb, vbt, kbc, vbct, sink)


def _outproj_router_kernel(ya_ref, yb_ref, woa_ref, wob_ref, x_ref, g1_ref, n2_ref, sh_ref, sc_ref,
                           wr_ref, br_ref, x1_ref, hx_ref, code_ref, gate_ref, cnt_ref, carry_ref):
    first = jnp.logical_and(pl.program_id(0) == 0, pl.program_id(1) == 0)

    @pl.when(first)
    def _():
        carry_ref[...] = jnp.zeros(carry_ref.shape, F32)

    y = (jnp.dot(ya_ref[0], woa_ref[...], preferred_element_type=F32)
         + jnp.dot(yb_ref[0], wob_ref[...], preferred_element_type=F32))
    x1 = x_ref[0] + g1_ref[0] * y
    x1_ref[0] = x1
    hx = _modulated_norm(x1, n2_ref[...], sh_ref[0], sc_ref[0])
    _store_row_tiles(hx_ref, hx)
    tm = hx.shape[0]
    hx_hi = hx.astype(BF16)
    hx_lo = (hx - hx_hi.astype(F32)).astype(BF16)
    wr = wr_ref[...]
    nt = (((1,), (1,)), ((), ()))
    part = lax.dot_general(wr, hx_hi, nt, preferred_element_type=F32)
    logits = (part[:N_EXPERTS] + part[N_EXPERTS:]
              + lax.dot_general(wr[:N_EXPERTS], hx_lo, nt, preferred_element_type=F32) + br_ref[...])

    row_e = lax.broadcasted_iota(jnp.int32, (N_EXPERTS, tm), 0).astype(F32)
    work = logits
    tops, idxs, hots = [], [], []
    for _k in range(TOP_K):
        m = jnp.max(work, axis=0, keepdims=True)
        idx = jnp.min(jnp.where(work == m, row_e, float(N_EXPERTS)), axis=0, keepdims=True)
        hot = row_e == idx
        work = jnp.where(hot, -jnp.inf, work)
        tops.append(m)
        idxs.append(idx)
        hots.append(hot)
    es = [jnp.exp(t - tops[0]) for t in tops]
    den = es[0] + es[1] + es[2] + es[3]

    multi = jnp.zeros((N_EXPERTS, tm), F32)
    for hot in hots:
        multi = multi + hot.astype(F32)
    r_i = lax.broadcasted_iota(jnp.int32, (tm, tm), 0)
    c_i = lax.broadcasted_iota(jnp.int32, (tm, tm), 1)
    tri = (r_i <= c_i).astype(BF16)
    incl = jnp.dot(multi.astype(BF16), tri, preferred_element_type=F32)
    before = carry_ref[...] + incl - 1.0

    codes, gates = [], []
    for k in range(TOP_K):
        rank = jnp.sum(jnp.where(hots[k], before, 0.0), axis=0, keepdims=True)
        codes.append(idxs[k].astype(jnp.int32) * 65536 + rank.astype(jnp.int32))
        gates.append(es[k] / den)
    fill = ROW_TILE - TOP_K
    code_ref[...] = jnp.concatenate(codes + [jnp.zeros((fill, tm), jnp.int32)], axis=0)
    gate_ref[...] = jnp.concatenate(gates + [jnp.zeros((fill, tm), F32)], axis=0)
    carry_ref[...] = carry_ref[...] + jnp.sum(multi, axis=1, keepdims=True)
    cnt_ref[...] = carry_ref[...]


def _outproj_router(ya, yb, woa, wob, x, g1, n2, sh2, sc2, w_r_parts, b_r, tm):
    b, s, d = x.shape
    nb = s // tm
    row = lambda bi, i: (bi, i, 0)
    mod = lambda bi, i: (bi, 0, 0)
    fixed = lambda bi, i: (0, 0)
    tok = lambda bi, i: (bi * nb + i, 0)
    half = ya.shape[2]
    return pl.pallas_call(
        _outproj_router_kernel,
        grid=(b, nb),
        in_specs=[pl.BlockSpec((1, tm, half), row),
                  pl.BlockSpec((1, tm, half), row),
                  pl.BlockSpec((half, d), fixed),
                  pl.BlockSpec((half, d), fixed),
                  pl.BlockSpec((1, tm, d), row),
                  pl.BlockSpec((1, 1, d), mod),
                  pl.BlockSpec((1, d), fixed),
                  pl.BlockSpec((1, 1, d), mod),
                  pl.BlockSpec((1, 1, d), mod),
                  pl.BlockSpec((2 * N_EXPERTS, d), fixed),
                  pl.BlockSpec((N_EXPERTS, 1), fixed)],
        out_specs=[pl.BlockSpec((1, tm, d), row),
                   pl.BlockSpec((tm * ROW_TILE, LANES), tok),
                   pl.BlockSpec((ROW_TILE, tm), lambda bi, i: (0, bi * nb + i)),
                   pl.BlockSpec((ROW_TILE, tm), lambda bi, i: (0, bi * nb + i)),
                   pl.BlockSpec((N_EXPERTS, 1), fixed)],
        out_shape=[jax.ShapeDtypeStruct((b, s, d), F32),
                   jax.ShapeDtypeStruct((b * s * ROW_TILE, LANES), F32),
                   jax.ShapeDtypeStruct((ROW_TILE, b * s), jnp.int32),
                   jax.ShapeDtypeStruct((ROW_TILE, b * s), F32),
                   jax.ShapeDtypeStruct((N_EXPERTS, 1), F32)],
        scratch_shapes=[pltpu.VMEM((N_EXPERTS, 1), F32)],
        compiler_params=_cparams(("arbitrary", "arbitrary")),
        name="outproj_router",
    )(ya, yb, woa, wob, x, g1, n2, sh2, sc2, w_r_parts, b_r)


def _sorted_rows_kernel(pbound_ref, code_ref, dest_ref):
    code = code_ref[...]
    expert = code >> 16
    base = jnp.zeros_like(code)
    for e in range(N_EXPERTS):
        base = jnp.where(expert == e, pbound_ref[e], base)
    dest_ref[...] = base + (code & 0xFFFF)


def _sorted_rows(pbound, code, tn):
    rows, n_tok = code.shape
    return pl.pallas_call(
        _sorted_rows_kernel,
        grid_spec=pltpu.PrefetchScalarGridSpec(
            num_scalar_prefetch=1,
            grid=(n_tok // tn,),
            in_specs=[pl.BlockSpec((rows, tn), lambda i, pb: (0, i))],
            out_specs=pl.BlockSpec((rows, tn), lambda i, pb: (0, i))),
        out_shape=jax.ShapeDtypeStruct((rows, n_tok), jnp.int32),
        compiler_params=_cparams(("arbitrary",)),
        name="moe_sorted_rows",
    )(pbound, code)


def _row_copy(src_hbm, src_row, dst_ref, dst_row, sem):
    return pltpu.make_async_copy(src_hbm.at[pl.ds(src_row, 1)], dst_ref.at[pl.ds(dst_row, 1)], sem)


def _store_row_tiles(ref, val):
    n = val.shape[0]
    for c in range(ROW_TILE):
        ref[pl.ds(c, n, stride=ROW_TILE), :] = val[:, c * LANES:(c + 1) * LANES]


def _load_row_tiles(ref, n):
    return [ref[pl.ds(c, n, stride=ROW_TILE), :] for c in range(ROW_TILE)]


def _tile_rows(row, count=1):
    if isinstance(row, int):
        return pl.ds(row * ROW_TILE, count * ROW_TILE)
    return pl.ds(pl.multiple_of(row * ROW_TILE, ROW_TILE), count * ROW_TILE)


def _tile_copy(src_ref, src_row, dst_ref, dst_row, sem):
    return pltpu.make_async_copy(src_ref.at[_tile_rows(src_row)], dst_ref.at[_tile_rows(dst_row)], sem)


def _dispatch_kernel(pbound_ref, cnt_ref, dest_ref, hx_ref, xs_hbm, zeros, sem, zsem, *, chunk):
    n_rows = xs_hbm.shape[0] // ROW_TILE
    blk = zeros.shape[0] // ROW_TILE
    pad_sizes = [blk >> (k + 1) for k in range(blk.bit_length() - 1)]

    @pl.when(pl.program_id(0) == 0)
    def _():
        zeros[...] = jnp.zeros(zeros.shape, F32)
        tail_blocks = (n_rows - pbound_ref[N_EXPERTS]) // blk

        def pad_copy(row, size):
            return pltpu.make_async_copy(zeros.at[_tile_rows(0, size)], xs_hbm.at[_tile_rows(row, size)], zsem)

        def pad_expert(e, carry):
            row = pbound_ref[e] + cnt_ref[e]
            n_pad = pbound_ref[e + 1] - row
            for size in pad_sizes:
                take = (n_pad & size) != 0

                @pl.when(take)
                def _():
                    pad_copy(row, size).start()
                row = row + jnp.where(take, size, 0)
            return carry

        def tail_copy(t):
            return pltpu.make_async_copy(zeros, xs_hbm.at[_tile_rows(pbound_ref[N_EXPERTS] + t * blk, blk)], zsem)

        def tail_start(t, carry):
            tail_copy(t).start()
            return carry

        lax.fori_loop(0, N_EXPERTS, pad_expert, 0)
        lax.fori_loop(0, tail_blocks, tail_start, 0)

        def pad_expert_wait(e, carry):
            n_pad = pbound_ref[e + 1] - pbound_ref[e] - cnt_ref[e]
            for size in pad_sizes:
                @pl.when((n_pad & size) != 0)
                def _():
                    pad_copy(0, size).wait()
            return carry

        def tail_wait(t, carry):
            tail_copy(0).wait()
            return carry

        lax.fori_loop(0, N_EXPERTS, pad_expert_wait, 0)
        lax.fori_loop(0, tail_blocks, tail_wait, 0)

    for j in range(chunk):
        _tile_copy(hx_ref, j // TOP_K, xs_hbm, dest_ref[j % TOP_K, j // TOP_K], sem).start(priority=j % 2)
    pltpu.make_async_copy(xs_hbm.at[_tile_rows(0, chunk)], xs_hbm.at[_tile_rows(0, chunk)], sem).wait()


def _dispatch(pbound, counts_i, dest, hx_tiles, n_rows, chunk):
    n_assign = dest.shape[1] * TOP_K
    return pl.pallas_call(
        functools.partial(_dispatch_kernel, chunk=chunk),
        grid_spec=pltpu.PrefetchScalarGridSpec(
            num_scalar_prefetch=2,
            grid=(n_assign // chunk,),
            in_specs=[pl.BlockSpec((ROW_TILE, chunk // TOP_K), lambda i, pb, ct: (0, i), memory_space=pltpu.SMEM),
                      pl.BlockSpec((chunk // TOP_K * ROW_TILE, LANES), lambda i, pb, ct: (i, 0))],
            out_specs=pl.BlockSpec(memory_space=pl.ANY),
            scratch_shapes=[pltpu.VMEM((EXPERT_BLOCK * ROW_TILE, LANES), F32),
                            pltpu.SemaphoreType.DMA(()),
                            pltpu.SemaphoreType.DMA(())]),
        out_shape=jax.ShapeDtypeStruct((n_rows * ROW_TILE, LANES), F32),
        compiler_params=_cparams(("arbitrary",)),
        name="moe_dispatch",
    )(pbound, counts_i, dest, hx_tiles)


def _expert_kernel(bexp_ref, run_ref, nxt_ref, nact_ref, xs_ref, wgu_hbm, wdn_hbm, bg_ref, bu_ref, bd_ref,
                   o_ref, wgu_f, wdn_f, wg_s, wu_s, wd_s, wsem):
    i = pl.program_id(0)
    nact = nact_ref[0]

    def weight_copies(expert, w):
        return (pltpu.make_async_copy(wgu_hbm.at[expert], wgu_f.at[w], wsem.at[w, 0]),
                pltpu.make_async_copy(wdn_hbm.at[expert], wdn_f.at[w], wsem.at[w, 1]))

    @pl.when(i == 0)
    def _():
        for cp in weight_copies(bexp_ref[0], 0):
            cp.start()

    @pl.when(i < nact)
    def _():
        changed = jnp.logical_or(i == 0, bexp_ref[i] != bexp_ref[jnp.maximum(i - 1, 0)])

        @pl.when(changed)
        def _():
            w = run_ref[i] % 2
            for cp in weight_copies(bexp_ref[i], w):
                cp.wait()
            half = MXU_COLS // 2
            src = lax.broadcasted_iota(jnp.int32, (MXU_COLS, MXU_COLS), 0)
            dst = lax.broadcasted_iota(jnp.int32, (MXU_COLS, MXU_COLS), 1)
            perm = (src == jnp.where(dst < half, 2 * dst, 2 * (dst - half) + 1)).astype(BF16)
            for k in range(wgu_f.shape[2] // MXU_COLS):
                wk = wgu_f[w, :, k * MXU_COLS:(k + 1) * MXU_COLS].astype(BF16)
                sep = jnp.dot(wk, perm, preferred_element_type=F32).astype(BF16)
                wg_s[:, k * half:(k + 1) * half] = sep[:, :half]
                wu_s[:, k * half:(k + 1) * half] = sep[:, half:]
            wd_s[...] = wdn_f[w].astype(BF16)

            @pl.when(nxt_ref[i] >= 0)
            def _():
                for cp in weight_copies(nxt_ref[i], 1 - w):
                    cp.start()

        xb = jnp.concatenate([c.astype(BF16) for c in _load_row_tiles(xs_ref, EXPERT_BLOCK)], axis=1)
        g = jnp.dot(xb, wg_s[...], preferred_element_type=F32) + bg_ref[0]
        u = jnp.dot(xb, wu_s[...], preferred_element_type=F32) + bu_ref[0]
        g = jnp.minimum(g, SWIGLU_LIMIT)
        u = jnp.clip(u, -SWIGLU_LIMIT, SWIGLU_LIMIT)
        a = g * (1.0 / (1.0 + jnp.exp(-SWIGLU_ALPHA * g))) * (u + 1.0)
        _store_row_tiles(o_ref, jnp.dot(a.astype(BF16), wd_s[...], preferred_element_type=F32) + bd_ref[0])

    @pl.when(i >= nact)
    def _():
        o_ref[...] = jnp.zeros(o_ref.shape, F32)


def _experts(block_exp, run_id, next_exp, n_active, xs, w_gu, w_dn, bg, bu, bd):
    f, d = w_dn.shape[1:]
    n_rows = xs.shape[0] // ROW_TILE
    nblk = n_rows // EXPERT_BLOCK
    bsel = lambda i, be, ru, nx, na: (be[i], 0, 0)
    blk = lambda i, be, ru, nx, na: (i, 0)
    anyspace = pl.BlockSpec(memory_space=pl.ANY)
    return pl.pallas_call(
        _expert_kernel,
        grid_spec=pltpu.PrefetchScalarGridSpec(
            num_scalar_prefetch=4,
            grid=(nblk,),
            in_specs=[pl.BlockSpec((EXPERT_BLOCK * ROW_TILE, LANES), blk),
                      anyspace, anyspace,
                      pl.BlockSpec((1, 1, f), bsel),
                      pl.BlockSpec((1, 1, f), bsel),
                      pl.BlockSpec((1, 1, d), bsel)],
            out_specs=pl.BlockSpec((EXPERT_BLOCK * ROW_TILE, LANES), blk),
            scratch_shapes=[pltpu.VMEM((2, d, 2 * f), F32),
                            pltpu.VMEM((2, f, d), F32),
                            pltpu.VMEM((d, f), BF16), pltpu.VMEM((d, f), BF16), pltpu.VMEM((f, d), BF16),
                            pltpu.SemaphoreType.DMA((2, 2))]),
        out_shape=jax.ShapeDtypeStruct((n_rows * ROW_TILE, LANES), F32),
        compiler_params=_cparams(("arbitrary",)),
        name="moe_experts",
    )(block_exp, run_id, next_exp, n_active, xs, w_gu, w_dn, bg, bu, bd)


def _combine_kernel(dest_ref, rows_hbm, gate_ref, x1_ref, g2_ref, fg_ref, o_ref, buf, sems, *, tm):
    n = pl.program_id(0)
    slot = n % 2

    def row_gather(step, to_slot, tok, k):
        n_tok = tm * pl.num_programs(0)
        return _tile_copy(rows_hbm, dest_ref[k * n_tok + step * tm + tok], buf.at[to_slot, k], tok, sems.at[to_slot])

    @pl.when(n == 0)
    def _():
        def issue(tok, carry):
            for k in range(TOP_K):
                row_gather(0, 0, tok, k).start()
            return carry
        lax.fori_loop(0, tm, issue, 0, unroll=4)

    @pl.when(n + 1 < pl.num_programs(0))
    def _():
        for tok in range(tm):
            for k in range(TOP_K):
                row_gather(n + 1, 1 - slot, tok, k).start(priority=k % 2)

    for k in range(TOP_K):
        pltpu.make_async_copy(rows_hbm.at[_tile_rows(0, tm)], buf.at[slot, k], sems.at[slot]).wait()

    gate = gate_ref[...]
    parts = None
    for k in range(TOP_K):
        gk = gate[:, k:k + 1]
        tiles = _load_row_tiles(buf.at[slot, k], tm)
        parts = [gk * r for r in tiles] if parts is None else [p + gk * r for p, r in zip(parts, tiles)]
    y = jnp.concatenate(parts, axis=1)
    xo = x1_ref[...] + g2_ref[0] * y
    ms = jnp.mean(xo * xo, axis=-1, keepdims=True)
    o_ref[...] = xo * lax.rsqrt(ms + NORM_EPS) * fg_ref[...]


def _combine(dest, rows, gates, x1, g2, final_g, tm):
    b, s, d = x1.shape
    nb = s // tm
    out = pl.pallas_call(
        functools.partial(_combine_kernel, tm=tm),
        grid_spec=pltpu.PrefetchScalarGridSpec(
            num_scalar_prefetch=1,
            grid=(b * nb,),
            in_specs=[pl.BlockSpec(memory_space=pl.ANY),
                      pl.BlockSpec((tm, TOP_K), lambda n, ds: (n, 0)),
                      pl.BlockSpec((tm, d), lambda n, ds: (n, 0)),
                      pl.BlockSpec((1, 1, d), lambda n, ds: (n // nb, 0, 0)),
                      pl.BlockSpec((1, d), lambda n, ds: (0, 0))],
            out_specs=pl.BlockSpec((tm, d), lambda n, ds: (n, 0)),
            scratch_shapes=[pltpu.VMEM((2, TOP_K, tm * ROW_TILE, LANES), F32), pltpu.SemaphoreType.DMA((2,))]),
        out_shape=jax.ShapeDtypeStruct((b * s, d), F32),
        compiler_params=_cparams(("arbitrary",)),
        name="moe_combine",
    )(dest, rows, gates, x1.reshape(b * s, d), g2, final_g)
    return out.reshape(b, s, d)


def _pick(n, prefs):
    for p in prefs:
        if n % p == 0:
            return p
    raise ValueError(f"no tile for {n}")


def kernel(x, c, ctx, c_ctx, w_mod, b_mod, norm1_g, w_in, lam_q1, lam_k1, lam_q2, lam_k2, subln_g,
           sink, w_out, norm2_g, w_router, b_router, w_gate_up, b_gate_up, w_down, b_down, final_g):
    b, s, d = x.shape
    c_len = ctx.shape[1]
    assert w_mod.shape[0] == 1, "single-layer block"
    assert d == ROW_TILE * LANES, "dispatched token rows are one (8, 128) f32 tile each"
    t = b * s

    pad = (-(b + 1)) % 8
    cvecs = jnp.concatenate([c, c_ctx[None, :], jnp.zeros((pad, d), F32)], axis=0)
    mod = _adaln(cvecs, w_mod[0], b_mod[0])
    sh1, sc1, g1, sh2, sc2, g2 = [mod[:b, k * d:(k + 1) * d].reshape(b, 1, d) for k in range(6)]
    csh1 = mod[b:b + 1, 0:d].reshape(1, 1, d)
    csc1 = mod[b:b + 1, d:2 * d].reshape(1, 1, d)

    w_in_bf = w_in[0].astype(BF16)
    cos, sin = _rope_tables(s)
    n1 = norm1_g[0].reshape(1, d)
    qat, ka, vat, qbt, kb, vbt = _inproj_latent(x, n1, sh1, sc1, w_in_bf, cos, sin, _pick(s, (512, 256, 128)))
    w_ctx_bf = jnp.concatenate([w_in_bf[:, O_KA:O_QB], w_in_bf[:, O_KB:IN_COLS]], axis=1)
    kac, vact, kbc, vbct = _inproj_ctx(ctx, n1, csh1, csc1, w_ctx_bf)

    ka_all = jnp.concatenate([ka, kac], axis=1)
    vat_all = jnp.concatenate([vat, vact], axis=2)
    sk = s + c_len
    ya = _diff_attn(qat, ka_all, vat_all,
                    lam_q1[0].reshape(1, -1), lam_k1[0].reshape(1, -1),
                    lam_q2[0].reshape(1, -1), lam_k2[0].reshape(1, -1),
                    subln_g[0].reshape(1, -1),
                    _pick(s, (1024, 512, 256, 128)), _pick(sk, (768, 512, 384, 256, 128)))

    yb = _win_attn(qbt, kb, vbt, kbc, vbct, sink[0].reshape(1, -1), _pick(s, (256, 128)))

    w_out_bf = w_out[0].astype(BF16)
    w_r_hi = w_router[0].astype(BF16)
    w_r_lo = (w_router[0] - w_r_hi.astype(F32)).astype(BF16)
    x1, hx2, code, gates, counts = _outproj_router(
        ya, yb, w_out_bf[:QA_COLS], w_out_bf[QA_COLS:], x, g1, norm2_g[0].reshape(1, d), sh2, sc2,
        jnp.concatenate([w_r_hi, w_r_lo], axis=1).T, b_router[0].reshape(-1, 1), _pick(s, (512, 256, 128)))

    n_assign = t * TOP_K
    n_rows = n_assign + N_EXPERTS * EXPERT_BLOCK
    counts_i = counts.reshape(-1).astype(jnp.int32)
    padded = ((counts_i + EXPERT_BLOCK - 1) // EXPERT_BLOCK) * EXPERT_BLOCK
    pend = jnp.cumsum(padded).astype(jnp.int32)
    pbound = jnp.concatenate([jnp.zeros((1,), jnp.int32), pend])
    block_start = jnp.arange(n_rows // EXPERT_BLOCK, dtype=jnp.int32) * EXPERT_BLOCK
    block_exp = jnp.minimum(jnp.sum((pend[None, :] <= block_start[:, None]).astype(jnp.int32), axis=1),
                            N_EXPERTS - 1)
    n_active = pend[-1:] // EXPERT_BLOCK
    changed = jnp.concatenate([jnp.ones((1,), jnp.int32), (block_exp[1:] != block_exp[:-1]).astype(jnp.int32)])
    run_id = jnp.cumsum(changed).astype(jnp.int32) - 1
    e_ids = jnp.arange(N_EXPERTS, dtype=jnp.int32)
    later_nonempty = (e_ids[None, :] > e_ids[:, None]) & (padded[None, :] > 0)
    next_of_expert = jnp.min(jnp.where(later_nonempty, e_ids[None, :], N_EXPERTS), axis=1)
    next_of_expert = jnp.where(next_of_expert < N_EXPERTS, next_of_expert, -1).astype(jnp.int32)
    next_exp = jnp.sum(jnp.where(block_exp[:, None] == e_ids[None, :], next_of_expert[None, :], 0), axis=1)

    dest = _sorted_rows(pbound, code, _pick(t, (2048, 1024, 512, 256)))
    xs = _dispatch(pbound, counts_i, dest, hx2, n_rows, _pick(n_assign, (1024,)))

    f = w_down.shape[2]
    bg = b_gate_up[0, :, 0::2].reshape(N_EXPERTS, 1, f)
    bu = b_gate_up[0, :, 1::2].reshape(N_EXPERTS, 1, f)
    bd = b_down[0].reshape(N_EXPERTS, 1, d)
    rows = _experts(block_exp, run_id, next_exp, n_active, xs, w_gate_up[0], w_down[0], bg, bu, bd)

    return _combine(dest[:TOP_K].reshape(-1), rows, gates[:TOP_K].T, x1, g2, final_g.reshape(1, d), _pick(s, (256,)))
```

```python
import functools
import math

import jax
import jax.numpy as jnp
import numpy as np
from jax import lax
from jax.experimental import pallas as pl
from jax.experimental.pallas import tpu as pltpu

F32 = jnp.float32
BF16 = jnp.bfloat16

GRID_W = 64
NORM_EPS = 1e-6
ROPE_BASE = 10000.0
MASK_VALUE = -1e30
DA_HEADS = 4
HEAD_DIM = 64
WA_HEADS = 8
WA_KV_HEADS = 2
WA_GROUP = WA_HEADS // WA_KV_HEADS
WINDOW = 128
N_EXPERTS = 32
TOP_K = 4
SWIGLU_LIMIT = 7.0
SWIGLU_ALPHA = 1.702
EXPERT_BLOCK = 256
LAM_INIT = 0.8 - 0.6 * math.exp(-0.3 * 0)
LOG2E = math.log2(math.e)

QA_COLS = DA_HEADS * 2 * HEAD_DIM
KA_COLS = QA_COLS
VA_COLS = QA_COLS
QB_COLS = WA_HEADS * HEAD_DIM
KB_COLS = WA_KV_HEADS * HEAD_DIM
VB_COLS = KB_COLS
O_QA = 0
O_KA = O_QA + QA_COLS
O_VA = O_KA + KA_COLS
O_QB = O_VA + VA_COLS
O_KB = O_QB + QB_COLS
O_VB = O_KB + KB_COLS
IN_COLS = O_VB + VB_COLS

LANES = 128
ROW_TILE = 8
INPROJ_SUBTILES = 2
MXU_COLS = 256
ONES_ROWS = 16
VMEM_LIMIT = 56 * 1024 * 1024


def _cparams(sem):
    return pltpu.CompilerParams(dimension_semantics=sem, vmem_limit_bytes=VMEM_LIMIT)


def _adaln_kernel(c_ref, w_ref, b_ref, o_ref):
    cv = c_ref[...]
    s = cv * (1.0 / (1.0 + jnp.exp(-cv)))
    rows = s.shape[0]
    w = w_ref[...]
    s_hi = s.astype(BF16)
    s_lo = (s - s_hi.astype(F32)).astype(BF16)
    w_hi = w.astype(BF16)
    w_lo = (w - w_hi.astype(F32)).astype(BF16)
    part = jnp.dot(jnp.concatenate([s_hi, s_lo], axis=0), w_hi, preferred_element_type=F32)
    o_ref[...] = part[:rows] + part[rows:] + jnp.dot(s_hi, w_lo, preferred_element_type=F32) + b_ref[...]


def _adaln(cvecs, w_mod, b_mod):
    rows, d = cvecs.shape
    n = w_mod.shape[1]
    tn = 1024
    return pl.pallas_call(
        _adaln_kernel,
        grid=(n // tn,),
        in_specs=[pl.BlockSpec((rows, d), lambda j: (0, 0)),
                  pl.BlockSpec((d, tn), lambda j: (0, j)),
                  pl.BlockSpec((1, tn), lambda j: (0, j))],
        out_specs=pl.BlockSpec((rows, tn), lambda j: (0, j)),
        out_shape=jax.ShapeDtypeStruct((rows, n), F32),
        compiler_params=_cparams(("arbitrary",)),
        name="adaln",
    )(cvecs, w_mod, b_mod.reshape(1, n))


def _rope_section(sec, cos, sin):
    tm = sec.shape[0]
    lane = lax.broadcasted_iota(jnp.int32, (tm, LANES), 1)
    low = (lane % 32) < 16
    outs = []
    for j in range(sec.shape[1] // LANES):
        c = sec[:, j * LANES:(j + 1) * LANES]
        partner = jnp.where(low, pltpu.roll(c, LANES - 16, 1), pltpu.roll(c, 16, 1))
        outs.append(c * cos + partner * sin)
    return jnp.concatenate(outs, axis=1)


def _modulated_norm(x, g, shift, scale):
    ms = jnp.mean(x * x, axis=-1, keepdims=True)
    return (x * lax.rsqrt(ms + NORM_EPS) * g) * (1.0 + scale) + shift


def _inproj_latent_kernel(x_ref, g_ref, sh_ref, sc_ref, w_ref, cos_ref, sin_ref,
                          qat_ref, ka_ref, vat_ref, qbt_ref, kb_ref, vbt_ref, p_ref):
    tm = x_ref.shape[1]
    th = tm // INPROJ_SUBTILES
    qscale = HEAD_DIM ** -0.5 * LOG2E

    def project(sub):
        rows = slice(sub * th, (sub + 1) * th)
        h = _modulated_norm(x_ref[0, rows], g_ref[...], sh_ref[0], sc_ref[0])
        p_ref[sub] = jnp.dot(h.astype(BF16), w_ref[...], preferred_element_type=F32)

    def emit(sub):
        rows = slice(sub * th, (sub + 1) * th)
        p = p_ref[sub]
        cos = cos_ref[rows]
        sin = sin_ref[rows]
        qat_ref[0, :, rows] = (_rope_section(p[:, O_QA:O_KA], cos, sin) * qscale).T.astype(BF16)
        ka_ref[0, rows] = _rope_section(p[:, O_KA:O_VA], cos, sin).astype(BF16)
        vat_ref[0, :, rows] = p[:, O_VA:O_QB].T.astype(BF16)
        qbt_ref[0, :, rows] = (_rope_section(p[:, O_QB:O_KB], cos, sin) * qscale).T.astype(BF16)
        kb_ref[0, rows] = _rope_section(p[:, O_KB:O_VB], cos, sin).astype(BF16)
        vbt_ref[0, :, rows] = p[:, O_VB:IN_COLS].T.astype(BF16)

    project(0)
    for sub in range(1, INPROJ_SUBTILES):
        project(sub)
        emit(sub - 1)
    emit(INPROJ_SUBTILES - 1)


def _inproj_ctx_kernel(x_ref, g_ref, sh_ref, sc_ref, w_ref, ka_ref, vat_ref, kb_ref, vbt_ref):
    h = _modulated_norm(x_ref[0], g_ref[...], sh_ref[0], sc_ref[0])
    p = jnp.dot(h.astype(BF16), w_ref[...], preferred_element_type=F32)
    ka_ref[0] = p[:, 0:KA_COLS].astype(BF16)
    vat_ref[0] = p[:, KA_COLS:KA_COLS + VA_COLS].T.astype(BF16)
    kb_ref[0] = p[:, KA_COLS + VA_COLS:KA_COLS + VA_COLS + KB_COLS].astype(BF16)
    vbt_ref[0] = p[:, KA_COLS + VA_COLS + KB_COLS:].T.astype(BF16)


def _rope_tables(n_tok):
    pos = np.arange(n_tok)
    nf = HEAD_DIM // 4
    inv = ROPE_BASE ** (-np.arange(nf) / nf)
    ar = (pos // GRID_W)[:, None] * inv
    ac = (pos % GRID_W)[:, None] * inv
    cos = np.concatenate([np.cos(ar), np.cos(ar), np.cos(ac), np.cos(ac)], axis=1)
    sin = np.concatenate([-np.sin(ar), np.sin(ar), -np.sin(ac), np.sin(ac)], axis=1)
    reps = (1, LANES // HEAD_DIM)
    return jnp.asarray(np.tile(cos, reps), F32), jnp.asarray(np.tile(sin, reps), F32)


def _inproj_latent(x, g, shift, scale, w_bf16, cos, sin, tm):
    b, s, d = x.shape
    row = lambda bi, i: (bi, i, 0)
    colt = lambda bi, i: (bi, 0, i)
    mod = lambda bi, i: (bi, 0, 0)
    fixed = lambda bi, i: (0, 0)
    return pl.pallas_call(
        _inproj_latent_kernel,
        grid=(b, s // tm),
        in_specs=[pl.BlockSpec((1, tm, d), row),
                  pl.BlockSpec((1, d), fixed),
                  pl.BlockSpec((1, 1, d), mod),
                  pl.BlockSpec((1, 1, d), mod),
                  pl.BlockSpec((d, IN_COLS), fixed),
                  pl.BlockSpec((tm, LANES), lambda bi, i: (i, 0)),
                  pl.BlockSpec((tm, LANES), lambda bi, i: (i, 0))],
        out_specs=[pl.BlockSpec((1, QA_COLS, tm), colt),
                   pl.BlockSpec((1, tm, KA_COLS), row),
                   pl.BlockSpec((1, VA_COLS, tm), colt),
                   pl.BlockSpec((1, QB_COLS, tm), colt),
                   pl.BlockSpec((1, tm, KB_COLS), row),
                   pl.BlockSpec((1, VB_COLS, tm), colt)],
        out_shape=[jax.ShapeDtypeStruct((b, QA_COLS, s), BF16),
                   jax.ShapeDtypeStruct((b, s, KA_COLS), BF16),
                   jax.ShapeDtypeStruct((b, VA_COLS, s), BF16),
                   jax.ShapeDtypeStruct((b, QB_COLS, s), BF16),
                   jax.ShapeDtypeStruct((b, s, KB_COLS), BF16),
                   jax.ShapeDtypeStruct((b, VB_COLS, s), BF16)],
        scratch_shapes=[pltpu.VMEM((INPROJ_SUBTILES, tm // INPROJ_SUBTILES, IN_COLS), F32)],
        compiler_params=_cparams(("arbitrary", "arbitrary")),
        name="inproj_latent",
    )(x, g, shift, scale, w_bf16, cos, sin)


def _inproj_ctx(ctx, g, shift, scale, w_ctx_bf16):
    b, c, d = ctx.shape
    n = w_ctx_bf16.shape[1]
    whole = lambda bi: (bi, 0, 0)
    mod = lambda bi: (0, 0, 0)
    fixed = lambda bi: (0, 0)
    return pl.pallas_call(
        _inproj_ctx_kernel,
        grid=(b,),
        in_specs=[pl.BlockSpec((1, c, d), whole),
                  pl.BlockSpec((1, d), fixed),
                  pl.BlockSpec((1, 1, d), mod),
                  pl.BlockSpec((1, 1, d), mod),
                  pl.BlockSpec((d, n), fixed)],
        out_specs=[pl.BlockSpec((1, c, KA_COLS), whole),
                   pl.BlockSpec((1, VA_COLS, c), whole),
                   pl.BlockSpec((1, c, KB_COLS), whole),
                   pl.BlockSpec((1, VB_COLS, c), whole)],
        out_shape=[jax.ShapeDtypeStruct((b, c, KA_COLS), BF16),
                   jax.ShapeDtypeStruct((b, VA_COLS, c), BF16),
                   jax.ShapeDtypeStruct((b, c, KB_COLS), BF16),
                   jax.ShapeDtypeStruct((b, VB_COLS, c), BF16)],
        compiler_params=_cparams(("arbitrary",)),
        name="inproj_ctx",
    )(ctx, g, shift, scale, w_ctx_bf16)


def _diff_attn_kernel(qt_ref, k_ref, vt_ref, lq1_ref, lk1_ref, lq2_ref, lk2_ref, sg_ref, o_ref,
                      m_ref, acc_ref, s_ref, *, tq, tk):
    d = HEAD_DIM
    hw = 2 * d
    n_chunks = k_ref.shape[1] // tk
    n_tiles = qt_ref.shape[2] // tq
    ones = jnp.ones((ONES_ROWS, tk), BF16)
    lam = (jnp.exp(jnp.sum(lq1_ref[...] * lk1_ref[...], axis=-1, keepdims=True))
           - jnp.exp(jnp.sum(lq2_ref[...] * lk2_ref[...], axis=-1, keepdims=True)) + LAM_INIT)

    def query_rhs(t):
        qt = qt_ref[0, :, pl.ds(pl.multiple_of(t * tq, tq), tq)]
        row = lax.broadcasted_iota(jnp.int32, qt.shape, 0)
        zero = jnp.zeros_like(qt)
        return jnp.where(row < d, qt, zero), jnp.where(row >= d, qt, zero)

    def scores(c, j, rhs):
        off = pl.multiple_of(j * tk, tk)
        s_ref[c] = jnp.dot(k_ref[0, pl.ds(off, tk), :], rhs[c], preferred_element_type=F32)

    def accumulate(c, j):
        off = pl.multiple_of(j * tk, tk)
        vt = jnp.concatenate([vt_ref[0, :, pl.ds(off, tk)], ones], axis=0)
        st = s_ref[c]
        m_old = m_ref[c]
        m_new = jnp.maximum(m_old, jnp.max(st, axis=0, keepdims=True))
        alpha = jnp.exp2(m_old - m_new)
        p = jnp.exp2(st - m_new).astype(BF16)
        acc_ref[c] = alpha * acc_ref[c] + jnp.dot(vt, p, preferred_element_type=F32)
        m_ref[c] = m_new

    group = 5 if (n_chunks - 1) % 5 == 0 else 1

    scores(0, 0, query_rhs(0))

    def tile(t, carry):
        rhs = query_rhs(t)
        m_ref[...] = jnp.full(m_ref.shape, -jnp.inf, F32)
        acc_ref[...] = jnp.zeros(acc_ref.shape, F32)

        def chunk_group(jj, carry2):
            for r in range(group):
                j = group * jj + r
                scores(1, j, rhs)
                accumulate(0, j)
                scores(0, j + 1, rhs)
                accumulate(1, j)
            return carry2

        lax.fori_loop(0, (n_chunks - 1) // group, chunk_group, 0)
        scores(1, n_chunks - 1, rhs)
        accumulate(0, n_chunks - 1)
        scores(0, 0, query_rhs(jnp.minimum(t + 1, n_tiles - 1)))
        accumulate(1, n_chunks - 1)

        a1 = acc_ref[0]
        a2 = acc_ref[1]
        ot = a1[:hw] / a1[hw:hw + 1] - lam * (a2[:hw] / a2[hw:hw + 1])
        ms = jnp.mean(ot * ot, axis=0, keepdims=True)
        ot = ot * lax.rsqrt(ms + NORM_EPS)
        o_ref[0, pl.ds(pl.multiple_of(t * tq, tq), tq), :] = (ot.T * (sg_ref[...] * (1.0 - LAM_INIT))).astype(o_ref.dtype)
        return carry

    lax.fori_loop(0, n_tiles, tile, 0)


def _diff_attn(qat, ka, vat, lq1, lk1, lq2, lk2, subln_g, tq, tk):
    b, _, s = qat.shape
    sk = ka.shape[1]
    hw = 2 * HEAD_DIM
    vec = lambda bi, h: (0, 0)
    return pl.pallas_call(
        functools.partial(_diff_attn_kernel, tq=tq, tk=tk),
        grid=(b, DA_HEADS),
        in_specs=[pl.BlockSpec((1, hw, s), lambda bi, h: (bi, h, 0)),
                  pl.BlockSpec((1, sk, hw), lambda bi, h: (bi, 0, h)),
                  pl.BlockSpec((1, hw, sk), lambda bi, h: (bi, h, 0)),
                  pl.BlockSpec((1, HEAD_DIM), vec),
                  pl.BlockSpec((1, HEAD_DIM), vec),
                  pl.BlockSpec((1, HEAD_DIM), vec),
                  pl.BlockSpec((1, HEAD_DIM), vec),
                  pl.BlockSpec((1, hw), vec)],
        out_specs=pl.BlockSpec((1, s, hw), lambda bi, h: (bi, 0, h)),
        out_shape=jax.ShapeDtypeStruct((b, s, DA_HEADS * hw), BF16),
        scratch_shapes=[pltpu.VMEM((2, 1, tq), F32),
                        pltpu.VMEM((2, hw + ONES_ROWS, tq), F32),
                        pltpu.VMEM((2, tk, tq), F32)],
        compiler_params=_cparams(("arbitrary", "arbitrary")),
        name="diff_attn",
    )(qat, ka, vat, lq1, lk1, lq2, lk2, subln_g)


def _win_attn_kernel(qt_ref, k_ref, vt_ref, kc_ref, vct_ref, sink_ref, o_ref, *, tq, lk):
    d = HEAD_DIM
    grp = WA_GROUP
    s_len = k_ref.shape[1]
    c_len = kc_ref.shape[1]
    nk = lk + c_len
    i = pl.program_id(1)
    q0 = i * tq
    start = pl.multiple_of(jnp.clip(q0 - WINDOW, 0, s_len - lk), LANES)
    keys = jnp.concatenate([k_ref[0, pl.ds(start, lk), :], kc_ref[0]], axis=0)
    kpos = start + lax.broadcasted_iota(jnp.int32, (lk, tq), 0)
    qpos = q0 + lax.broadcasted_iota(jnp.int32, (lk, tq), 1)
    visible = jnp.abs(kpos - qpos) <= WINDOW
    visible = jnp.concatenate([visible] * grp, axis=1)
    ones = jnp.ones((ONES_ROWS, nk), BF16)
    qt = qt_ref[0]
    blank = jnp.zeros((d, grp * tq), BF16)
    outs = []
    for kv in range(WA_KV_HEADS):
        heads = range(kv * grp, (kv + 1) * grp)
        qcat = jnp.concatenate([qt[h * d:(h + 1) * d, :] for h in heads], axis=1)
        rhs = jnp.concatenate([qcat if j == kv else blank for j in range(WA_KV_HEADS)], axis=0)
        st = jnp.dot(keys, rhs, preferred_element_type=F32)
        st = jnp.concatenate([jnp.where(visible, st[:lk], MASK_VALUE), st[lk:]], axis=0)
        sink = jnp.concatenate([jnp.broadcast_to(sink_ref[:, h:h + 1] * LOG2E, (1, tq)) for h in heads], axis=1)
        m = jnp.maximum(jnp.max(st, axis=0, keepdims=True), sink)
        p = jnp.exp2(st - m).astype(BF16)
        vt = jnp.concatenate([vt_ref[0, kv * d:(kv + 1) * d, pl.ds(start, lk)],
                              vct_ref[0, kv * d:(kv + 1) * d, :]], axis=1)
        acc = jnp.dot(jnp.concatenate([vt, ones], axis=0), p, preferred_element_type=F32)
        o = acc[:d] / (acc[d:d + 1] + jnp.exp2(sink - m))
        outs.extend(o[:, g * tq:(g + 1) * tq] for g in range(grp))
    o_ref[0] = jnp.concatenate(outs, axis=0).T.astype(o_ref.dtype)


def _win_attn(qbt, kb, vbt, kbc, vbct, sink, tq):
    b, _, s = qbt.shape
    c = kbc.shape[1]
    lk = tq + 2 * WINDOW
    assert s >= lk and tq % LANES == 0
    whole = lambda bi, i: (bi, 0, 0)
    return pl.pallas_call(
        functools.partial(_win_attn_kernel, tq=tq, lk=lk),
        grid=(b, s // tq),
        in_specs=[pl.BlockSpec((1, QB_COLS, tq), lambda bi, i: (bi, 0, i)),
                  pl.BlockSpec((1, s, KB_COLS), whole),
                  pl.BlockSpec((1, VB_COLS, s), whole),
                  pl.BlockSpec((1, c, KB_COLS), whole),
                  pl.BlockSpec((1, VB_COLS, c), whole),
                  pl.BlockSpec((1, WA_HEADS), lambda bi, i: (0, 0))],
        out_specs=pl.BlockSpec((1, tq, QB_COLS), lambda bi, i: (bi, i, 0)),
        out_shape=jax.ShapeDtypeStruct((b, s, QB_COLS), BF16),
        compiler_params=_cparams(("arbitrary", "arbitrary")),
        name="win_attn",
    )(qbt, kb, vbt, kbc, vbct, sink)


def _outproj_router_kernel(ya_ref, yb_ref, woa_ref, wob_ref, x_ref, g1_ref, n2_ref, sh_ref, sc_ref,
                           wr_ref, br_ref, x1_ref, hx_ref, code_ref, gate_ref, cnt_ref, carry_ref):
    first = jnp.logical_and(pl.program_id(0) == 0, pl.program_id(1) == 0)

    @pl.when(first)
    def _():
        carry_ref[...] = jnp.zeros(carry_ref.shape, F32)

    y = (jnp.dot(ya_ref[0], woa_ref[...], preferred_element_type=F32)
         + jnp.dot(yb_ref[0], wob_ref[...], preferred_element_type=F32))
    x1 = x_ref[0] + g1_ref[0] * y
    x1_ref[0] = x1
    hx = _modulated_norm(x1, n2_ref[...], sh_ref[0], sc_ref[0])
    _store_row_tiles(hx_ref, hx)
    tm = hx.shape[0]
    hx_hi = hx.astype(BF16)
    hx_lo = (hx - hx_hi.astype(F32)).astype(BF16)
    wr = wr_ref[...]
    nt = (((1,), (1,)), ((), ()))
    part = lax.dot_general(wr, hx_hi, nt, preferred_element_type=F32)
    logits = (part[:N_EXPERTS] + part[N_EXPERTS:]
              + lax.dot_general(wr[:N_EXPERTS], hx_lo, nt, preferred_element_type=F32) + br_ref[...])

    row_e = lax.broadcasted_iota(jnp.int32, (N_EXPERTS, tm), 0).astype(F32)
    work = logits
    tops, idxs, hots = [], [], []
    for _k in range(TOP_K):
        m = jnp.max(work, axis=0, keepdims=True)
        idx = jnp.min(jnp.where(work == m, row_e, float(N_EXPERTS)), axis=0, keepdims=True)
        hot = row_e == idx
        work = jnp.where(hot, -jnp.inf, work)
        tops.append(m)
        idxs.append(idx)
        hots.append(hot)
    es = [jnp.exp(t - tops[0]) for t in tops]
    den = es[0] + es[1] + es[2] + es[3]

    multi = jnp.zeros((N_EXPERTS, tm), F32)
    for hot in hots:
        multi = multi + hot.astype(F32)
    r_i = lax.broadcasted_iota(jnp.int32, (tm, tm), 0)
    c_i = lax.broadcasted_iota(jnp.int32, (tm, tm), 1)
    tri = (r_i <= c_i).astype(BF16)
    incl = jnp.dot(multi.astype(BF16), tri, preferred_element_type=F32)
    before = carry_ref[...] + incl - 1.0

    codes, gates = [], []
    for k in range(TOP_K):
        rank = jnp.sum(jnp.where(hots[k], before, 0.0), axis=0, keepdims=True)
        codes.append(idxs[k].astype(jnp.int32) * 65536 + rank.astype(jnp.int32))
        gates.append(es[k] / den)
    fill = ROW_TILE - TOP_K
    code_ref[...] = jnp.concatenate(codes + [jnp.zeros((fill, tm), jnp.int32)], axis=0)
    gate_ref[...] = jnp.concatenate(gates + [jnp.zeros((fill, tm), F32)], axis=0)
    carry_ref[...] = carry_ref[...] + jnp.sum(multi, axis=1, keepdims=True)
    cnt_ref[...] = carry_ref[...]


def _outproj_router(ya, yb, woa, wob, x, g1, n2, sh2, sc2, w_r_parts, b_r, tm):
    b, s, d = x.shape
    nb = s // tm
    row = lambda bi, i: (bi, i, 0)
    mod = lambda bi, i: (bi, 0, 0)
    fixed = lambda bi, i: (0, 0)
    tok = lambda bi, i: (bi * nb + i, 0)
    half = ya.shape[2]
    return pl.pallas_call(
        _outproj_router_kernel,
        grid=(b, nb),
        in_specs=[pl.BlockSpec((1, tm, half), row),
                  pl.BlockSpec((1, tm, half), row),
                  pl.BlockSpec((half, d), fixed),
                  pl.BlockSpec((half, d), fixed),
                  pl.BlockSpec((1, tm, d), row),
                  pl.BlockSpec((1, 1, d), mod),
                  pl.BlockSpec((1, d), fixed),
                  pl.BlockSpec((1, 1, d), mod),
                  pl.BlockSpec((1, 1, d), mod),
                  pl.BlockSpec((2 * N_EXPERTS, d), fixed),
                  pl.BlockSpec((N_EXPERTS, 1), fixed)],
        out_specs=[pl.BlockSpec((1, tm, d), row),
                   pl.BlockSpec((tm * ROW_TILE, LANES), tok),
                   pl.BlockSpec((ROW_TILE, tm), lambda bi, i: (0, bi * nb + i)),
                   pl.BlockSpec((ROW_TILE, tm), lambda bi, i: (0, bi * nb + i)),
                   pl.BlockSpec((N_EXPERTS, 1), fixed)],
        out_shape=[jax.ShapeDtypeStruct((b, s, d), F32),
                   jax.ShapeDtypeStruct((b * s * ROW_TILE, LANES), F32),
                   jax.ShapeDtypeStruct((ROW_TILE, b * s), jnp.int32),
                   jax.ShapeDtypeStruct((ROW_TILE, b * s), F32),
                   jax.ShapeDtypeStruct((N_EXPERTS, 1), F32)],
        scratch_shapes=[pltpu.VMEM((N_EXPERTS, 1), F32)],
        compiler_params=_cparams(("arbitrary", "arbitrary")),
        name="outproj_router",
    )(ya, yb, woa, wob, x, g1, n2, sh2, sc2, w_r_parts, b_r)


def _sorted_rows_kernel(pbound_ref, code_ref, dest_ref):
    code = code_ref[...]
    expert = code >> 16
    base = jnp.zeros_like(code)
    for e in range(N_EXPERTS):
        base = jnp.where(expert == e, pbound_ref[e], base)
    dest_ref[...] = base + (code & 0xFFFF)


def _sorted_rows(pbound, code, tn):
    rows, n_tok = code.shape
    return pl.pallas_call(
        _sorted_rows_kernel,
        grid_spec=pltpu.PrefetchScalarGridSpec(
            num_scalar_prefetch=1,
            grid=(n_tok // tn,),
            in_specs=[pl.BlockSpec((rows, tn), lambda i, pb: (0, i))],
            out_specs=pl.BlockSpec((rows, tn), lambda i, pb: (0, i))),
        out_shape=jax.ShapeDtypeStruct((rows, n_tok), jnp.int32),
        compiler_params=_cparams(("arbitrary",)),
        name="moe_sorted_rows",
    )(pbound, code)


def _store_row_tiles(ref, val):
    n = val.shape[0]
    for c in range(ROW_TILE):
        ref[pl.ds(c, n, stride=ROW_TILE), :] = val[:, c * LANES:(c + 1) * LANES]


def _load_row_tiles(ref, n):
    return [ref[pl.ds(c, n, stride=ROW_TILE), :] for c in range(ROW_TILE)]


def _tile_rows(row, count=1):
    if isinstance(row, int):
        return pl.ds(row * ROW_TILE, count * ROW_TILE)
    return pl.ds(pl.multiple_of(row * ROW_TILE, ROW_TILE), count * ROW_TILE)


def _tile_copy(src_ref, src_row, dst_ref, dst_row, sem):
    return pltpu.make_async_copy(src_ref.at[_tile_rows(src_row)], dst_ref.at[_tile_rows(dst_row)], sem)


def _dispatch_kernel(pbound_ref, cnt_ref, dest_ref, hx_ref, xs_hbm, zeros, sem, zsem, *, chunk):
    n_rows = xs_hbm.shape[0] // ROW_TILE
    blk = zeros.shape[0] // ROW_TILE
    pad_sizes = [blk >> (k + 1) for k in range(blk.bit_length() - 1)]

    @pl.when(pl.program_id(0) == 0)
    def _():
        zeros[...] = jnp.zeros(zeros.shape, F32)
        tail_blocks = (n_rows - pbound_ref[N_EXPERTS]) // blk

        def pad_copy(row, size):
            return pltpu.make_async_copy(zeros.at[_tile_rows(0, size)], xs_hbm.at[_tile_rows(row, size)], zsem)

        def pad_expert(e, carry):
            row = pbound_ref[e] + cnt_ref[e]
            n_pad = pbound_ref[e + 1] - row
            for size in pad_sizes:
                take = (n_pad & size) != 0

                @pl.when(take)
                def _():
                    pad_copy(row, size).start()
                row = row + jnp.where(take, size, 0)
            return carry

        def tail_copy(t):
            return pltpu.make_async_copy(zeros, xs_hbm.at[_tile_rows(pbound_ref[N_EXPERTS] + t * blk, blk)], zsem)

        def tail_start(t, carry):
            tail_copy(t).start()
            return carry

        lax.fori_loop(0, N_EXPERTS, pad_expert, 0)
        lax.fori_loop(0, tail_blocks, tail_start, 0)

        def pad_expert_wait(e, carry):
            n_pad = pbound_ref[e + 1] - pbound_ref[e] - cnt_ref[e]
            for size in pad_sizes:
                @pl.when((n_pad & size) != 0)
                def _():
                    pad_copy(0, size).wait()
            return carry

        def tail_wait(t, carry):
            tail_copy(0).wait()
            return carry

        lax.fori_loop(0, N_EXPERTS, pad_expert_wait, 0)
        lax.fori_loop(0, tail_blocks, tail_wait, 0)

    for j in range(chunk):
        _tile_copy(hx_ref, j // TOP_K, xs_hbm, dest_ref[j % TOP_K, j // TOP_K], sem).start(priority=j % 2)
    pltpu.make_async_copy(xs_hbm.at[_tile_rows(0, chunk)], xs_hbm.at[_tile_rows(0, chunk)], sem).wait()


def _dispatch(pbound, counts_i, dest, hx_tiles, n_rows, chunk):
    n_assign = dest.shape[1] * TOP_K
    return pl.pallas_call(
        functools.partial(_dispatch_kernel, chunk=chunk),
        grid_spec=pltpu.PrefetchScalarGridSpec(
            num_scalar_prefetch=2,
            grid=(n_assign // chunk,),
            in_specs=[pl.BlockSpec((ROW_TILE, chunk // TOP_K), lambda i, pb, ct: (0, i), memory_space=pltpu.SMEM),
                      pl.BlockSpec((chunk // TOP_K * ROW_TILE, LANES), lambda i, pb, ct: (i, 0))],
            out_specs=pl.BlockSpec(memory_space=pl.ANY),
            scratch_shapes=[pltpu.VMEM((EXPERT_BLOCK * ROW_TILE, LANES), F32),
                            pltpu.SemaphoreType.DMA(()),
                            pltpu.SemaphoreType.DMA(())]),
        out_shape=jax.ShapeDtypeStruct((n_rows * ROW_TILE, LANES), F32),
        compiler_params=_cparams(("arbitrary",)),
        name="moe_dispatch",
    )(pbound, counts_i, dest, hx_tiles)


def _expert_kernel(bexp_ref, run_ref, nxt_ref, nact_ref, xs_ref, wgu_hbm, wdn_hbm, bg_ref, bu_ref, bd_ref,
                   o_ref, wgu_f, wdn_f, wg_s, wu_s, wd_s, wsem):
    i = pl.program_id(0)
    nact = nact_ref[0]

    def weight_copies(expert, w):
        return (pltpu.make_async_copy(wgu_hbm.at[expert], wgu_f.at[w], wsem.at[w, 0]),
                pltpu.make_async_copy(wdn_hbm.at[expert], wdn_f.at[w], wsem.at[w, 1]))

    @pl.when(i == 0)
    def _():
        for cp in weight_copies(bexp_ref[0], 0):
            cp.start()

    @pl.when(i < nact)
    def _():
        changed = jnp.logical_or(i == 0, bexp_ref[i] != bexp_ref[jnp.maximum(i - 1, 0)])

        @pl.when(changed)
        def _():
            w = run_ref[i] % 2
            for cp in weight_copies(bexp_ref[i], w):
                cp.wait()
            half = MXU_COLS // 2
            src = lax.broadcasted_iota(jnp.int32, (MXU_COLS, MXU_COLS), 0)
            dst = lax.broadcasted_iota(jnp.int32, (MXU_COLS, MXU_COLS), 1)
            perm = (src == jnp.where(dst < half, 2 * dst, 2 * (dst - half) + 1)).astype(BF16)
            for k in range(wgu_f.shape[2] // MXU_COLS):
                wk = wgu_f[w, :, k * MXU_COLS:(k + 1) * MXU_COLS].astype(BF16)
                sep = jnp.dot(wk, perm, preferred_element_type=F32).astype(BF16)
                wg_s[:, k * half:(k + 1) * half] = sep[:, :half]
                wu_s[:, k * half:(k + 1) * half] = sep[:, half:]
            wd_s[...] = wdn_f[w].astype(BF16)

            @pl.when(nxt_ref[i] >= 0)
            def _():
                for cp in weight_copies(nxt_ref[i], 1 - w):
                    cp.start()

        xb = jnp.concatenate([c.astype(BF16) for c in _load_row_tiles(xs_ref, EXPERT_BLOCK)], axis=1)
        g = jnp.dot(xb, wg_s[...], preferred_element_type=F32) + bg_ref[0]
        u = jnp.dot(xb, wu_s[...], preferred_element_type=F32) + bu_ref[0]
        g = jnp.minimum(g, SWIGLU_LIMIT)
        u = jnp.clip(u, -SWIGLU_LIMIT, SWIGLU_LIMIT)
        a = g * (1.0 / (1.0 + jnp.exp(-SWIGLU_ALPHA * g))) * (u + 1.0)
        _store_row_tiles(o_ref, jnp.dot(a.astype(BF16), wd_s[...], preferred_element_type=F32) + bd_ref[0])

    @pl.when(i >= nact)
    def _():
        o_ref[...] = jnp.zeros(o_ref.shape, F32)


def _experts(block_exp, run_id, next_exp, n_active, xs, w_gu, w_dn, bg, bu, bd):
    f, d = w_dn.shape[1:]
    n_rows = xs.shape[0] // ROW_TILE
    nblk = n_rows // EXPERT_BLOCK
    bsel = lambda i, be, ru, nx, na: (be[i], 0, 0)
    blk = lambda i, be, ru, nx, na: (i, 0)
    anyspace = pl.BlockSpec(memory_space=pl.ANY)
    return pl.pallas_call(
        _expert_kernel,
        grid_spec=pltpu.PrefetchScalarGridSpec(
            num_scalar_prefetch=4,
            grid=(nblk,),
            in_specs=[pl.BlockSpec((EXPERT_BLOCK * ROW_TILE, LANES), blk),
                      anyspace, anyspace,
                      pl.BlockSpec((1, 1, f), bsel),
                      pl.BlockSpec((1, 1, f), bsel),
                      pl.BlockSpec((1, 1, d), bsel)],
            out_specs=pl.BlockSpec((EXPERT_BLOCK * ROW_TILE, LANES), blk),
            scratch_shapes=[pltpu.VMEM((2, d, 2 * f), F32),
                            pltpu.VMEM((2, f, d), F32),
                            pltpu.VMEM((d, f), BF16), pltpu.VMEM((d, f), BF16), pltpu.VMEM((f, d), BF16),
                            pltpu.SemaphoreType.DMA((2, 2))]),
        out_shape=jax.ShapeDtypeStruct((n_rows * ROW_TILE, LANES), F32),
        compiler_params=_cparams(("arbitrary",)),
        name="moe_experts",
    )(block_exp, run_id, next_exp, n_active, xs, w_gu, w_dn, bg, bu, bd)


def _combine_kernel(dest_ref, rows_hbm, gate_ref, x1_ref, g2_ref, fg_ref, o_ref, buf, sems, *, tm):
    n = pl.program_id(0)
    slot = n % 2

    def row_gather(step, to_slot, tok, k):
        n_tok = tm * pl.num_programs(0)
        return _tile_copy(rows_hbm, dest_ref[k * n_tok + step * tm + tok], buf.at[to_slot, k], tok, sems.at[to_slot])

    @pl.when(n == 0)
    def _():
        def issue(tok, carry):
            for k in range(TOP_K):
                row_gather(0, 0, tok, k).start()
            return carry
        lax.fori_loop(0, tm, issue, 0, unroll=4)

    @pl.when(n + 1 < pl.num_programs(0))
    def _():
        for tok in range(tm):
            for k in range(TOP_K):
                row_gather(n + 1, 1 - slot, tok, k).start(priority=k % 2)

    for k in range(TOP_K):
        pltpu.make_async_copy(rows_hbm.at[_tile_rows(0, tm)], buf.at[slot, k], sems.at[slot]).wait()

    gate = gate_ref[...]
    parts = None
    for k in range(TOP_K):
        gk = gate[:, k:k + 1]
        tiles = _load_row_tiles(buf.at[slot, k], tm)
        parts = [gk * r for r in tiles] if parts is None else [p + gk * r for p, r in zip(parts, tiles)]
    y = jnp.concatenate(parts, axis=1)
    xo = x1_ref[...] + g2_ref[0] * y
    ms = jnp.mean(xo * xo, axis=-1, keepdims=True)
    o_ref[...] = xo * lax.rsqrt(ms + NORM_EPS) * fg_ref[...]


def _combine(dest, rows, gates, x1, g2, final_g, tm):
    b, s, d = x1.shape
    nb = s // tm
    out = pl.pallas_call(
        functools.partial(_combine_kernel, tm=tm),
        grid_spec=pltpu.PrefetchScalarGridSpec(
            num_scalar_prefetch=1,
            grid=(b * nb,),
            in_specs=[pl.BlockSpec(memory_space=pl.ANY),
                      pl.BlockSpec((tm, TOP_K), lambda n, ds: (n, 0)),
                      pl.BlockSpec((tm, d), lambda n, ds: (n, 0)),
                      pl.BlockSpec((1, 1, d), lambda n, ds: (n // nb, 0, 0)),
                      pl.BlockSpec((1, d), lambda n, ds: (0, 0))],
            out_specs=pl.BlockSpec((tm, d), lambda n, ds: (n, 0)),
            scratch_shapes=[pltpu.VMEM((2, TOP_K, tm * ROW_TILE, LANES), F32), pltpu.SemaphoreType.DMA((2,))]),
        out_shape=jax.ShapeDtypeStruct((b * s, d), F32),
        compiler_params=_cparams(("arbitrary",)),
        name="moe_combine",
    )(dest, rows, gates, x1.reshape(b * s, d), g2, final_g)
    return out.reshape(b, s, d)


def _pick(n, prefs):
    for p in prefs:
        if n % p == 0:
            return p
    raise ValueError(f"no tile for {n}")


def kernel(x, c, ctx, c_ctx, w_mod, b_mod, norm1_g, w_in, lam_q1, lam_k1, lam_q2, lam_k2, subln_g,
           sink, w_out, norm2_g, w_router, b_router, w_gate_up, b_gate_up, w_down, b_down, final_g):
    b, s, d = x.shape
    c_len = ctx.shape[1]
    assert w_mod.shape[0] == 1, "single-layer block"
    assert d == ROW_TILE * LANES, "dispatched token rows are one (8, 128) f32 tile each"
    t = b * s

    pad = (-(b + 1)) % 8
    cvecs = jnp.concatenate([c, c_ctx[None, :], jnp.zeros((pad, d), F32)], axis=0)
    mod = _adaln(cvecs, w_mod[0], b_mod[0])
    sh1, sc1, g1, sh2, sc2, g2 = [mod[:b, k * d:(k + 1) * d].reshape(b, 1, d) for k in range(6)]
    csh1 = mod[b:b + 1, 0:d].reshape(1, 1, d)
    csc1 = mod[b:b + 1, d:2 * d].reshape(1, 1, d)

    w_in_bf = w_in[0].astype(BF16)
    cos, sin = _rope_tables(s)
    n1 = norm1_g[0].reshape(1, d)
    qat, ka, vat, qbt, kb, vbt = _inproj_latent(x, n1, sh1, sc1, w_in_bf, cos, sin, _pick(s, (1024, 512, 256)))
    w_ctx_bf = jnp.concatenate([w_in_bf[:, O_KA:O_QB], w_in_bf[:, O_KB:IN_COLS]], axis=1)
    kac, vact, kbc, vbct = _inproj_ctx(ctx, n1, csh1, csc1, w_ctx_bf)

    ka_all = jnp.concatenate([ka, kac], axis=1)
    vat_all = jnp.concatenate([vat, vact], axis=2)
    sk = s + c_len
    ya = _diff_attn(qat, ka_all, vat_all,
                    lam_q1[0].reshape(1, -1), lam_k1[0].reshape(1, -1),
                    lam_q2[0].reshape(1, -1), lam_k2[0].reshape(1, -1),
                    subln_g[0].reshape(1, -1),
                    _pick(s, (1024, 512, 256, 128)), _pick(sk, (768, 512, 384, 256, 128)))

    yb = _win_attn(qbt, kb, vbt, kbc, vbct, sink[0].reshape(1, -1), _pick(s, (256, 128)))

    w_out_bf = w_out[0].astype(BF16)
    w_r_hi = w_router[0].astype(BF16)
    w_r_lo = (w_router[0] - w_r_hi.astype(F32)).astype(BF16)
    x1, hx2, code, gates, counts = _outproj_router(
        ya, yb, w_out_bf[:QA_COLS], w_out_bf[QA_COLS:], x, g1, norm2_g[0].reshape(1, d), sh2, sc2,
        jnp.concatenate([w_r_hi, w_r_lo], axis=1).T, b_router[0].reshape(-1, 1), _pick(s, (512, 256, 128)))

    n_assign = t * TOP_K
    n_rows = n_assign + N_EXPERTS * EXPERT_BLOCK
    counts_i = counts.reshape(-1).astype(jnp.int32)
    padded = ((counts_i + EXPERT_BLOCK - 1) // EXPERT_BLOCK) * EXPERT_BLOCK
    pend = jnp.cumsum(padded).astype(jnp.int32)
    pbound = jnp.concatenate([jnp.zeros((1,), jnp.int32), pend])
    block_start = jnp.arange(n_rows // EXPERT_BLOCK, dtype=jnp.int32) * EXPERT_BLOCK
    block_exp = jnp.minimum(jnp.sum((pend[None, :] <= block_start[:, None]).astype(jnp.int32), axis=1),
                            N_EXPERTS - 1)
    n_active = pend[-1:] // EXPERT_BLOCK
    changed = jnp.concatenate([jnp.ones((1,), jnp.int32), (block_exp[1:] != block_exp[:-1]).astype(jnp.int32)])
    run_id = jnp.cumsum(changed).astype(jnp.int32) - 1
    e_ids = jnp.arange(N_EXPERTS, dtype=jnp.int32)
    later_nonempty = (e_ids[None, :] > e_ids[:, None]) & (padded[None, :] > 0)
    next_of_expert = jnp.min(jnp.where(later_nonempty, e_ids[None, :], N_EXPERTS), axis=1)
    next_of_expert = jnp.where(next_of_expert < N_EXPERTS, next_of_expert, -1).astype(jnp.int32)
    next_exp = jnp.sum(jnp.where(block_exp[:, None] == e_ids[None, :], next_of_expert[None, :], 0), axis=1)

    dest = _sorted_rows(pbound, code, _pick(t, (2048, 1024, 512, 256)))
    xs = _dispatch(pbound, counts_i, dest, hx2, n_rows, _pick(n_assign, (1024,)))

    f = w_down.shape[2]
    bg = b_gate_up[0, :, 0::2].reshape(N_EXPERTS, 1, f)
    bu = b_gate_up[0, :, 1::2].reshape(N_EXPERTS, 1, f)
    bd = b_down[0].reshape(N_EXPERTS, 1, d)
    rows = _experts(block_exp, run_id, next_exp, n_active, xs, w_gate_up[0], w_down[0], bg, bu, bd)

    return _combine(dest[:TOP_K].reshape(-1), rows, gates[:TOP_K].T, x1, g2, final_g.reshape(1, d), _pick(s, (256,)))
```

```python
import functools
import math

import jax
import jax.numpy as jnp
import numpy as np
from jax import lax
from jax.experimental import pallas as pl
from jax.experimental.pallas import tpu as pltpu

F32 = jnp.float32
BF16 = jnp.bfloat16

GRID_W = 64
NORM_EPS = 1e-6
ROPE_BASE = 10000.0
MASK_VALUE = -1e30
DA_HEADS = 4
HEAD_DIM = 64
WA_HEADS = 8
WA_KV_HEADS = 2
WA_GROUP = WA_HEADS // WA_KV_HEADS
WINDOW = 128
N_EXPERTS = 32
TOP_K = 4
SWIGLU_LIMIT = 7.0
SWIGLU_ALPHA = 1.702
EXPERT_BLOCK = 256
LAM_INIT = 0.8 - 0.6 * math.exp(-0.3 * 0)
LOG2E = math.log2(math.e)

QA_COLS = DA_HEADS * 2 * HEAD_DIM
KA_COLS = QA_COLS
VA_COLS = QA_COLS
QB_COLS = WA_HEADS * HEAD_DIM
KB_COLS = WA_KV_HEADS * HEAD_DIM
VB_COLS = KB_COLS
O_QA = 0
O_KA = O_QA + QA_COLS
O_VA = O_KA + KA_COLS
O_QB = O_VA + VA_COLS
O_KB = O_QB + QB_COLS
O_VB = O_KB + KB_COLS
IN_COLS = O_VB + VB_COLS

LANES = 128
ROW_TILE = 8
INPROJ_SUBTILES = 2
MXU_COLS = 256
ONES_ROWS = 16
VMEM_LIMIT = 56 * 1024 * 1024


def _cparams(sem):
    return pltpu.CompilerParams(dimension_semantics=sem, vmem_limit_bytes=VMEM_LIMIT)


def _adaln_kernel(c_ref, w_ref, b_ref, o_ref):
    cv = c_ref[...]
    s = cv * (1.0 / (1.0 + jnp.exp(-cv)))
    rows = s.shape[0]
    w = w_ref[...]
    s_hi = s.astype(BF16)
    s_lo = (s - s_hi.astype(F32)).astype(BF16)
    w_hi = w.astype(BF16)
    w_lo = (w - w_hi.astype(F32)).astype(BF16)
    part = jnp.dot(jnp.concatenate([s_hi, s_lo], axis=0), w_hi, preferred_element_type=F32)
    o_ref[...] = part[:rows] + part[rows:] + jnp.dot(s_hi, w_lo, preferred_element_type=F32) + b_ref[...]


def _adaln(cvecs, w_mod, b_mod):
    rows, d = cvecs.shape
    n = w_mod.shape[1]
    tn = 1024
    return pl.pallas_call(
        _adaln_kernel,
        grid=(n // tn,),
        in_specs=[pl.BlockSpec((rows, d), lambda j: (0, 0)),
                  pl.BlockSpec((d, tn), lambda j: (0, j)),
                  pl.BlockSpec((1, tn), lambda j: (0, j))],
        out_specs=pl.BlockSpec((rows, tn), lambda j: (0, j)),
        out_shape=jax.ShapeDtypeStruct((rows, n), F32),
        compiler_params=_cparams(("arbitrary",)),
        name="adaln",
    )(cvecs, w_mod, b_mod.reshape(1, n))


def _rope_section(sec, cos, sin):
    tm = sec.shape[0]
    lane = lax.broadcasted_iota(jnp.int32, (tm, LANES), 1)
    low = (lane % 32) < 16
    outs = []
    for j in range(sec.shape[1] // LANES):
        c = sec[:, j * LANES:(j + 1) * LANES]
        partner = jnp.where(low, pltpu.roll(c, LANES - 16, 1), pltpu.roll(c, 16, 1))
        outs.append(c * cos + partner * sin)
    return jnp.concatenate(outs, axis=1)


def _modulated_norm(x, g, shift, scale):
    ms = jnp.mean(x * x, axis=-1, keepdims=True)
    return (x * lax.rsqrt(ms + NORM_EPS) * g) * (1.0 + scale) + shift


def _inproj_latent_kernel(x_ref, g_ref, sh_ref, sc_ref, w_ref, cos_ref, sin_ref,
                          qat_ref, ka_ref, vat_ref, qbt_ref, kb_ref, vbt_ref, p_ref):
    tm = x_ref.shape[1]
    th = tm // INPROJ_SUBTILES
    qscale = HEAD_DIM ** -0.5 * LOG2E

    def project(sub):
        rows = slice(sub * th, (sub + 1) * th)
        h = _modulated_norm(x_ref[0, rows], g_ref[...], sh_ref[0], sc_ref[0])
        p_ref[sub] = jnp.dot(h.astype(BF16), w_ref[...], preferred_element_type=F32)

    def emit(sub):
        rows = slice(sub * th, (sub + 1) * th)
        p = p_ref[sub]
        cos = cos_ref[rows]
        sin = sin_ref[rows]
        qat_ref[0, :, rows] = (_rope_section(p[:, O_QA:O_KA], cos, sin) * qscale).T.astype(BF16)
        ka_ref[0, rows] = _rope_section(p[:, O_KA:O_VA], cos, sin).astype(BF16)
        vat_ref[0, :, rows] = p[:, O_VA:O_QB].T.astype(BF16)
        qbt_ref[0, :, rows] = (_rope_section(p[:, O_QB:O_KB], cos, sin) * qscale).T.astype(BF16)
        kb_ref[0, rows] = _rope_section(p[:, O_KB:O_VB], cos, sin).astype(BF16)
        vbt_ref[0, :, rows] = p[:, O_VB:IN_COLS].T.astype(BF16)

    project(0)
    for sub in range(1, INPROJ_SUBTILES):
        project(sub)
        emit(sub - 1)
    emit(INPROJ_SUBTILES - 1)


def _inproj_ctx_kernel(x_ref, g_ref, sh_ref, sc_ref, w_ref, ka_ref, vat_ref, kb_ref, vbt_ref):
    h = _modulated_norm(x_ref[0], g_ref[...], sh_ref[0], sc_ref[0])
    p = jnp.dot(h.astype(BF16), w_ref[...], preferred_element_type=F32)
    ka_ref[0] = p[:, 0:KA_COLS].astype(BF16)
    vat_ref[0] = p[:, KA_COLS:KA_COLS + VA_COLS].T.astype(BF16)
    kb_ref[0] = p[:, KA_COLS + VA_COLS:KA_COLS + VA_COLS + KB_COLS].astype(BF16)
    vbt_ref[0] = p[:, KA_COLS + VA_COLS + KB_COLS:].T.astype(BF16)


def _rope_tables(n_tok):
    pos = np.arange(n_tok)
    nf = HEAD_DIM // 4
    inv = ROPE_BASE ** (-np.arange(nf) / nf)
    ar = (pos // GRID_W)[:, None] * inv
    ac = (pos % GRID_W)[:, None] * inv
    cos = np.concatenate([np.cos(ar), np.cos(ar), np.cos(ac), np.cos(ac)], axis=1)
    sin = np.concatenate([-np.sin(ar), np.sin(ar), -np.sin(ac), np.sin(ac)], axis=1)
    reps = (1, LANES // HEAD_DIM)
    return jnp.asarray(np.tile(cos, reps), F32), jnp.asarray(np.tile(sin, reps), F32)


def _inproj_latent(x, g, shift, scale, w_bf16, cos, sin, tm):
    b, s, d = x.shape
    row = lambda bi, i: (bi, i, 0)
    colt = lambda bi, i: (bi, 0, i)
    mod = lambda bi, i: (bi, 0, 0)
    fixed = lambda bi, i: (0, 0)
    return pl.pallas_call(
        _inproj_latent_kernel,
        grid=(b, s // tm),
        in_specs=[pl.BlockSpec((1, tm, d), row),
                  pl.BlockSpec((1, d), fixed),
                  pl.BlockSpec((1, 1, d), mod),
                  pl.BlockSpec((1, 1, d), mod),
                  pl.BlockSpec((d, IN_COLS), fixed),
                  pl.BlockSpec((tm, LANES), lambda bi, i: (i, 0)),
                  pl.BlockSpec((tm, LANES), lambda bi, i: (i, 0))],
        out_specs=[pl.BlockSpec((1, QA_COLS, tm), colt),
                   pl.BlockSpec((1, tm, KA_COLS), row),
                   pl.BlockSpec((1, VA_COLS, tm), colt),
                   pl.BlockSpec((1, QB_COLS, tm), colt),
                   pl.BlockSpec((1, tm, KB_COLS), row),
                   pl.BlockSpec((1, VB_COLS, tm), colt)],
        out_shape=[jax.ShapeDtypeStruct((b, QA_COLS, s), BF16),
                   jax.ShapeDtypeStruct((b, s, KA_COLS), BF16),
                   jax.ShapeDtypeStruct((b, VA_COLS, s), BF16),
                   jax.ShapeDtypeStruct((b, QB_COLS, s), BF16),
                   jax.ShapeDtypeStruct((b, s, KB_COLS), BF16),
                   jax.ShapeDtypeStruct((b, VB_COLS, s), BF16)],
        scratch_shapes=[pltpu.VMEM((INPROJ_SUBTILES, tm // INPROJ_SUBTILES, IN_COLS), F32)],
        compiler_params=_cparams(("arbitrary", "arbitrary")),
        name="inproj_latent",
    )(x, g, shift, scale, w_bf16, cos, sin)


def _inproj_ctx(ctx, g, shift, scale, w_ctx_bf16):
    b, c, d = ctx.shape
    n = w_ctx_bf16.shape[1]
    whole = lambda bi: (bi, 0, 0)
    mod = lambda bi: (0, 0, 0)
    fixed = lambda bi: (0, 0)
    return pl.pallas_call(
        _inproj_ctx_kernel,
        grid=(b,),
        in_specs=[pl.BlockSpec((1, c, d), whole),
                  pl.BlockSpec((1, d), fixed),
                  pl.BlockSpec((1, 1, d), mod),
                  pl.BlockSpec((1, 1, d), mod),
                  pl.BlockSpec((d, n), fixed)],
        out_specs=[pl.BlockSpec((1, c, KA_COLS), whole),
                   pl.BlockSpec((1, VA_COLS, c), whole),
                   pl.BlockSpec((1, c, KB_COLS), whole),
                   pl.BlockSpec((1, VB_COLS, c), whole)],
        out_shape=[jax.ShapeDtypeStruct((b, c, KA_COLS), BF16),
                   jax.ShapeDtypeStruct((b, VA_COLS, c), BF16),
                   jax.ShapeDtypeStruct((b, c, KB_COLS), BF16),
                   jax.ShapeDtypeStruct((b, VB_COLS, c), BF16)],
        compiler_params=_cparams(("arbitrary",)),
        name="inproj_ctx",
    )(ctx, g, shift, scale, w_ctx_bf16)


def _diff_attn_kernel(qt_ref, k_ref, vt_ref, lq1_ref, lk1_ref, lq2_ref, lk2_ref, sg_ref, o_ref,
                      m_ref, acc_ref, s_ref, *, tq, tk):
    d = HEAD_DIM
    hw = 2 * d
    n_chunks = k_ref.shape[1] // tk
    n_tiles = qt_ref.shape[2] // tq
    ones = jnp.ones((ONES_ROWS, tk), BF16)
    lam = (jnp.exp(jnp.sum(lq1_ref[...] * lk1_ref[...], axis=-1, keepdims=True))
           - jnp.exp(jnp.sum(lq2_ref[...] * lk2_ref[...], axis=-1, keepdims=True)) + LAM_INIT)

    def query_rhs(t):
        qt = qt_ref[0, :, pl.ds(pl.multiple_of(t * tq, tq), tq)]
        row = lax.broadcasted_iota(jnp.int32, qt.shape, 0)
        zero = jnp.zeros_like(qt)
        return jnp.where(row < d, qt, zero), jnp.where(row >= d, qt, zero)

    def scores(c, j, rhs):
        off = pl.multiple_of(j * tk, tk)
        s_ref[c] = jnp.dot(k_ref[0, pl.ds(off, tk), :], rhs[c], preferred_element_type=F32)

    def accumulate(c, j):
        off = pl.multiple_of(j * tk, tk)
        vt = jnp.concatenate([vt_ref[0, :, pl.ds(off, tk)], ones], axis=0)
        st = s_ref[c]
        m_old = m_ref[c]
        m_new = jnp.maximum(m_old, jnp.max(st, axis=0, keepdims=True))
        alpha = jnp.exp2(m_old - m_new)
        p = jnp.exp2(st - m_new).astype(BF16)
        acc_ref[c] = alpha * acc_ref[c] + jnp.dot(vt, p, preferred_element_type=F32)
        m_ref[c] = m_new

    group = 5 if (n_chunks - 1) % 5 == 0 else 1

    scores(0, 0, query_rhs(0))

    def tile(t, carry):
        rhs = query_rhs(t)
        m_ref[...] = jnp.full(m_ref.shape, -jnp.inf, F32)
        acc_ref[...] = jnp.zeros(acc_ref.shape, F32)

        def chunk_group(jj, carry2):
            for r in range(group):
                j = group * jj + r
                scores(1, j, rhs)
                accumulate(0, j)
                scores(0, j + 1, rhs)
                accumulate(1, j)
            return carry2

        lax.fori_loop(0, (n_chunks - 1) // group, chunk_group, 0)
        scores(1, n_chunks - 1, rhs)
        accumulate(0, n_chunks - 1)
        scores(0, 0, query_rhs(jnp.minimum(t + 1, n_tiles - 1)))
        accumulate(1, n_chunks - 1)

        a1 = acc_ref[0]
        a2 = acc_ref[1]
        ot = a1[:hw] / a1[hw:hw + 1] - lam * (a2[:hw] / a2[hw:hw + 1])
        ms = jnp.mean(ot * ot, axis=0, keepdims=True)
        ot = ot * lax.rsqrt(ms + NORM_EPS)
        o_ref[0, pl.ds(pl.multiple_of(t * tq, tq), tq), :] = (ot.T * (sg_ref[...] * (1.0 - LAM_INIT))).astype(o_ref.dtype)
        return carry

    lax.fori_loop(0, n_tiles, tile, 0)


def _diff_attn(qat, ka, vat, lq1, lk1, lq2, lk2, subln_g, tq, tk):
    b, _, s = qat.shape
    sk = ka.shape[1]
    hw = 2 * HEAD_DIM
    vec = lambda bi, h: (0, 0)
    return pl.pallas_call(
        functools.partial(_diff_attn_kernel, tq=tq, tk=tk),
        grid=(b, DA_HEADS),
        in_specs=[pl.BlockSpec((1, hw, s), lambda bi, h: (bi, h, 0)),
                  pl.BlockSpec((1, sk, hw), lambda bi, h: (bi, 0, h)),
                  pl.BlockSpec((1, hw, sk), lambda bi, h: (bi, h, 0)),
                  pl.BlockSpec((1, HEAD_DIM), vec),
                  pl.BlockSpec((1, HEAD_DIM), vec),
                  pl.BlockSpec((1, HEAD_DIM), vec),
                  pl.BlockSpec((1, HEAD_DIM), vec),
                  pl.BlockSpec((1, hw), vec)],
        out_specs=pl.BlockSpec((1, s, hw), lambda bi, h: (bi, 0, h)),
        out_shape=jax.ShapeDtypeStruct((b, s, DA_HEADS * hw), BF16),
        scratch_shapes=[pltpu.VMEM((2, 1, tq), F32),
                        pltpu.VMEM((2, hw + ONES_ROWS, tq), F32),
                        pltpu.VMEM((2, tk, tq), F32)],
        compiler_params=_cparams(("arbitrary", "arbitrary")),
        name="diff_attn",
    )(qat, ka, vat, lq1, lk1, lq2, lk2, subln_g)


def _win_attn_kernel(qt_ref, k_ref, vt_ref, kc_ref, vct_ref, sink_ref, o_ref, s_ref, *, tq, lk):
    d = HEAD_DIM
    grp = WA_GROUP
    s_len = k_ref.shape[1]
    c_len = kc_ref.shape[1]
    nk = lk + c_len
    i = pl.program_id(1)
    q0 = i * tq
    start = pl.multiple_of(jnp.clip(q0 - WINDOW, 0, s_len - lk), LANES)
    keys = jnp.concatenate([k_ref[0, pl.ds(start, lk), :], kc_ref[0]], axis=0)
    kpos = start + lax.broadcasted_iota(jnp.int32, (lk, tq), 0)
    qpos = q0 + lax.broadcasted_iota(jnp.int32, (lk, tq), 1)
    visible = jnp.abs(kpos - qpos) <= WINDOW
    visible = jnp.concatenate([visible] * grp, axis=1)
    ones = jnp.ones((ONES_ROWS, nk), BF16)
    qt = qt_ref[0]
    blank = jnp.zeros((d, grp * tq), BF16)
    outs = []
    for kv in range(WA_KV_HEADS):
        heads = range(kv * grp, (kv + 1) * grp)
        qcat = jnp.concatenate([qt[h * d:(h + 1) * d, :] for h in heads], axis=1)
        rhs = jnp.concatenate([qcat if j == kv else blank for j in range(WA_KV_HEADS)], axis=0)
        s_ref[kv] = jnp.dot(keys, rhs, preferred_element_type=F32)
    for kv in range(WA_KV_HEADS):
        heads = range(kv * grp, (kv + 1) * grp)
        st = s_ref[kv]
        st = jnp.concatenate([jnp.where(visible, st[:lk], MASK_VALUE), st[lk:]], axis=0)
        sink = jnp.concatenate([jnp.broadcast_to(sink_ref[:, h:h + 1] * LOG2E, (1, tq)) for h in heads], axis=1)
        m = jnp.maximum(jnp.max(st, axis=0, keepdims=True), sink)
        p = jnp.exp2(st - m).astype(BF16)
        vt = jnp.concatenate([vt_ref[0, kv * d:(kv + 1) * d, pl.ds(start, lk)],
                              vct_ref[0, kv * d:(kv + 1) * d, :]], axis=1)
        acc = jnp.dot(jnp.concatenate([vt, ones], axis=0), p, preferred_element_type=F32)
        o = acc[:d] / (acc[d:d + 1] + jnp.exp2(sink - m))
        outs.extend(o[:, g * tq:(g + 1) * tq] for g in range(grp))
    o_ref[0] = jnp.concatenate(outs, axis=0).T.astype(o_ref.dtype)


def _win_attn(qbt, kb, vbt, kbc, vbct, sink, tq):
    b, _, s = qbt.shape
    c = kbc.shape[1]
    lk = tq + 2 * WINDOW
    assert s >= lk and tq % LANES == 0
    whole = lambda bi, i: (bi, 0, 0)
    return pl.pallas_call(
        functools.partial(_win_attn_kernel, tq=tq, lk=lk),
        grid=(b, s // tq),
        in_specs=[pl.BlockSpec((1, QB_COLS, tq), lambda bi, i: (bi, 0, i)),
                  pl.BlockSpec((1, s, KB_COLS), whole),
                  pl.BlockSpec((1, VB_COLS, s), whole),
                  pl.BlockSpec((1, c, KB_COLS), whole),
                  pl.BlockSpec((1, VB_COLS, c), whole),
                  pl.BlockSpec((1, WA_HEADS), lambda bi, i: (0, 0))],
        out_specs=pl.BlockSpec((1, tq, QB_COLS), lambda bi, i: (bi, i, 0)),
        out_shape=jax.ShapeDtypeStruct((b, s, QB_COLS), BF16),
        scratch_shapes=[pltpu.VMEM((WA_KV_HEADS, lk + c, WA_GROUP * tq), F32)],
        compiler_params=_cparams(("arbitrary", "arbitrary")),
        name="win_attn",
    )(qbt, kb, vbt, kbc, vbct, sink)


def _outproj_router_kernel(ya_ref, yb_ref, woa_ref, wob_ref, x_ref, g1_ref, n2_ref, sh_ref, sc_ref,
                           wr_ref, br_ref, x1_ref, hx_ref, code_ref, gate_ref, cnt_ref, carry_ref):
    first = jnp.logical_and(pl.program_id(0) == 0, pl.program_id(1) == 0)

    @pl.when(first)
    def _():
        carry_ref[...] = jnp.zeros(carry_ref.shape, F32)

    y = (jnp.dot(ya_ref[0], woa_ref[...], preferred_element_type=F32)
         + jnp.dot(yb_ref[0], wob_ref[...], preferred_element_type=F32))
    x1 = x_ref[0] + g1_ref[0] * y
    x1_ref[0] = x1
    hx = _modulated_norm(x1, n2_ref[...], sh_ref[0], sc_ref[0])
    _store_row_tiles(hx_ref, hx)
    tm = hx.shape[0]
    hx_hi = hx.astype(BF16)
    hx_lo = (hx - hx_hi.astype(F32)).astype(BF16)
    wr = wr_ref[...]
    nt = (((1,), (1,)), ((), ()))
    part = lax.dot_general(wr, hx_hi, nt, preferred_element_type=F32)
    logits = (part[:N_EXPERTS] + part[N_EXPERTS:]
              + lax.dot_general(wr[:N_EXPERTS], hx_lo, nt, preferred_element_type=F32) + br_ref[...])

    row_e = lax.broadcasted_iota(jnp.int32, (N_EXPERTS, tm), 0).astype(F32)
    work = logits
    tops, idxs, hots = [], [], []
    for _k in range(TOP_K):
        m = jnp.max(work, axis=0, keepdims=True)
        idx = jnp.min(jnp.where(work == m, row_e, float(N_EXPERTS)), axis=0, keepdims=True)
        hot = row_e == idx
        work = jnp.where(hot, -jnp.inf, work)
        tops.append(m)
        idxs.append(idx)
        hots.append(hot)
    es = [jnp.exp(t - tops[0]) for t in tops]
    den = es[0] + es[1] + es[2] + es[3]

    multi = jnp.zeros((N_EXPERTS, tm), F32)
    for hot in hots:
        multi = multi + hot.astype(F32)
    r_i = lax.broadcasted_iota(jnp.int32, (tm, tm), 0)
    c_i = lax.broadcasted_iota(jnp.int32, (tm, tm), 1)
    tri = (r_i <= c_i).astype(BF16)
    incl = jnp.dot(multi.astype(BF16), tri, preferred_element_type=F32)
    before = carry_ref[...] + incl - 1.0

    codes, gates = [], []
    for k in range(TOP_K):
        rank = jnp.sum(jnp.where(hots[k], before, 0.0), axis=0, keepdims=True)
        codes.append(idxs[k].astype(jnp.int32) * 65536 + rank.astype(jnp.int32))
        gates.append(es[k] / den)
    fill = ROW_TILE - TOP_K
    code_ref[...] = jnp.concatenate(codes + [jnp.zeros((fill, tm), jnp.int32)], axis=0)
    gate_ref[...] = jnp.concatenate(gates + [jnp.zeros((fill, tm), F32)], axis=0)
    carry_ref[...] = carry_ref[...] + jnp.sum(multi, axis=1, keepdims=True)
    cnt_ref[...] = carry_ref[...]


def _outproj_router(ya, yb, woa, wob, x, g1, n2, sh2, sc2, w_r_parts, b_r, tm):
    b, s, d = x.shape
    nb = s // tm
    row = lambda bi, i: (bi, i, 0)
    mod = lambda bi, i: (bi, 0, 0)
    fixed = lambda bi, i: (0, 0)
    tok = lambda bi, i: (bi * nb + i, 0)
    half = ya.shape[2]
    return pl.pallas_call(
        _outproj_router_kernel,
        grid=(b, nb),
        in_specs=[pl.BlockSpec((1, tm, half), row),
                  pl.BlockSpec((1, tm, half), row),
                  pl.BlockSpec((half, d), fixed),
                  pl.BlockSpec((half, d), fixed),
                  pl.BlockSpec((1, tm, d), row),
                  pl.BlockSpec((1, 1, d), mod),
                  pl.BlockSpec((1, d), fixed),
                  pl.BlockSpec((1, 1, d), mod),
                  pl.BlockSpec((1, 1, d), mod),
                  pl.BlockSpec((2 * N_EXPERTS, d), fixed),
                  pl.BlockSpec((N_EXPERTS, 1), fixed)],
        out_specs=[pl.BlockSpec((1, tm, d), row),
                   pl.BlockSpec((tm * ROW_TILE, LANES), tok),
                   pl.BlockSpec((ROW_TILE, tm), lambda bi, i: (0, bi * nb + i)),
                   pl.BlockSpec((ROW_TILE, tm), lambda bi, i: (0, bi * nb + i)),
                   pl.BlockSpec((N_EXPERTS, 1), fixed)],
        out_shape=[jax.ShapeDtypeStruct((b, s, d), F32),
                   jax.ShapeDtypeStruct((b * s * ROW_TILE, LANES), F32),
                   jax.ShapeDtypeStruct((ROW_TILE, b * s), jnp.int32),
                   jax.ShapeDtypeStruct((ROW_TILE, b * s), F32),
                   jax.ShapeDtypeStruct((N_EXPERTS, 1), F32)],
        scratch_shapes=[pltpu.VMEM((N_EXPERTS, 1), F32)],
        compiler_params=_cparams(("arbitrary", "arbitrary")),
        name="outproj_router",
    )(ya, yb, woa, wob, x, g1, n2, sh2, sc2, w_r_parts, b_r)


def _sorted_rows_kernel(pbound_ref, code_ref, dest_ref):
    code = code_ref[...]
    expert = code >> 16
    base = jnp.zeros_like(code)
    for e in range(N_EXPERTS):
        base = jnp.where(expert == e, pbound_ref[e], base)
    dest_ref[...] = base + (code & 0xFFFF)


def _sorted_rows(pbound, code, tn):
    rows, n_tok = code.shape
    return pl.pallas_call(
        _sorted_rows_kernel,
        grid_spec=pltpu.PrefetchScalarGridSpec(
            num_scalar_prefetch=1,
            grid=(n_tok // tn,),
            in_specs=[pl.BlockSpec((rows, tn), lambda i, pb: (0, i))],
            out_specs=pl.BlockSpec((rows, tn), lambda i, pb: (0, i))),
        out_shape=jax.ShapeDtypeStruct((rows, n_tok), jnp.int32),
        compiler_params=_cparams(("arbitrary",)),
        name="moe_sorted_rows",
    )(pbound, code)


def _store_row_tiles(ref, val):
    n = val.shape[0]
    for c in range(ROW_TILE):
        ref[pl.ds(c, n, stride=ROW_TILE), :] = val[:, c * LANES:(c + 1) * LANES]


def _load_row_tiles(ref, n):
    return [ref[pl.ds(c, n, stride=ROW_TILE), :] for c in range(ROW_TILE)]


def _tile_rows(row, count=1):
    if isinstance(row, int):
        return pl.ds(row * ROW_TILE, count * ROW_TILE)
    return pl.ds(pl.multiple_of(row * ROW_TILE, ROW_TILE), count * ROW_TILE)


def _tile_copy(src_ref, src_row, dst_ref, dst_row, sem):
    return pltpu.make_async_copy(src_ref.at[_tile_rows(src_row)], dst_ref.at[_tile_rows(dst_row)], sem)


def _dispatch_kernel(pbound_ref, cnt_ref, dest_ref, hx_ref, xs_hbm, zeros, sem, zsem, *, chunk):
    n_rows = xs_hbm.shape[0] // ROW_TILE
    blk = zeros.shape[0] // ROW_TILE
    pad_sizes = [blk >> (k + 1) for k in range(blk.bit_length() - 1)]

    @pl.when(pl.program_id(0) == 0)
    def _():
        zeros[...] = jnp.zeros(zeros.shape, F32)
        tail_blocks = (n_rows - pbound_ref[N_EXPERTS]) // blk

        def pad_copy(row, size):
            return pltpu.make_async_copy(zeros.at[_tile_rows(0, size)], xs_hbm.at[_tile_rows(row, size)], zsem)

        def pad_expert(e, carry):
            row = pbound_ref[e] + cnt_ref[e]
            n_pad = pbound_ref[e + 1] - row
            for size in pad_sizes:
                take = (n_pad & size) != 0

                @pl.when(take)
                def _():
                    pad_copy(row, size).start()
                row = row + jnp.where(take, size, 0)
            return carry

        def tail_copy(t):
            return pltpu.make_async_copy(zeros, xs_hbm.at[_tile_rows(pbound_ref[N_EXPERTS] + t * blk, blk)], zsem)

        def tail_start(t, carry):
            tail_copy(t).start()
            return carry

        lax.fori_loop(0, N_EXPERTS, pad_expert, 0)
        lax.fori_loop(0, tail_blocks, tail_start, 0)

        def pad_expert_wait(e, carry):
            n_pad = pbound_ref[e + 1] - pbound_ref[e] - cnt_ref[e]
            for size in pad_sizes:
                @pl.when((n_pad & size) != 0)
                def _():
                    pad_copy(0, size).wait()
            return carry

        def tail_wait(t, carry):
            tail_copy(0).wait()
            return carry

        lax.fori_loop(0, N_EXPERTS, pad_expert_wait, 0)
        lax.fori_loop(0, tail_blocks, tail_wait, 0)

    for j in range(chunk):
        _tile_copy(hx_ref, j // TOP_K, xs_hbm, dest_ref[j % TOP_K, j // TOP_K], sem).start(priority=j % 2)
    pltpu.make_async_copy(xs_hbm.at[_tile_rows(0, chunk)], xs_hbm.at[_tile_rows(0, chunk)], sem).wait()


def _dispatch(pbound, counts_i, dest, hx_tiles, n_rows, chunk):
    n_assign = dest.shape[1] * TOP_K
    return pl.pallas_call(
        functools.partial(_dispatch_kernel, chunk=chunk),
        grid_spec=pltpu.PrefetchScalarGridSpec(
            num_scalar_prefetch=2,
            grid=(n_assign // chunk,),
            in_specs=[pl.BlockSpec((ROW_TILE, chunk // TOP_K), lambda i, pb, ct: (0, i), memory_space=pltpu.SMEM),
                      pl.BlockSpec((chunk // TOP_K * ROW_TILE, LANES), lambda i, pb, ct: (i, 0))],
            out_specs=pl.BlockSpec(memory_space=pl.ANY),
            scratch_shapes=[pltpu.VMEM((EXPERT_BLOCK * ROW_TILE, LANES), F32),
                            pltpu.SemaphoreType.DMA(()),
                            pltpu.SemaphoreType.DMA(())]),
        out_shape=jax.ShapeDtypeStruct((n_rows * ROW_TILE, LANES), F32),
        compiler_params=_cparams(("arbitrary",)),
        name="moe_dispatch",
    )(pbound, counts_i, dest, hx_tiles)


def _expert_kernel(bexp_ref, run_ref, nxt_ref, nact_ref, xs_ref, wgu_hbm, wdn_hbm, bg_ref, bu_ref, bd_ref,
                   o_ref, wgu_f, wdn_f, wg_s, wu_s, wd_s, wsem):
    i = pl.program_id(0)
    nact = nact_ref[0]

    def weight_copies(expert, w):
        return (pltpu.make_async_copy(wgu_hbm.at[expert], wgu_f.at[w], wsem.at[w, 0]),
                pltpu.make_async_copy(wdn_hbm.at[expert], wdn_f.at[w], wsem.at[w, 1]))

    @pl.when(i == 0)
    def _():
        for cp in weight_copies(bexp_ref[0], 0):
            cp.start()

    @pl.when(i < nact)
    def _():
        changed = jnp.logical_or(i == 0, bexp_ref[i] != bexp_ref[jnp.maximum(i - 1, 0)])

        @pl.when(changed)
        def _():
            w = run_ref[i] % 2
            for cp in weight_copies(bexp_ref[i], w):
                cp.wait()
            half = MXU_COLS // 2
            src = lax.broadcasted_iota(jnp.int32, (MXU_COLS, MXU_COLS), 0)
            dst = lax.broadcasted_iota(jnp.int32, (MXU_COLS, MXU_COLS), 1)
            perm = (src == jnp.where(dst < half, 2 * dst, 2 * (dst - half) + 1)).astype(BF16)
            for k in range(wgu_f.shape[2] // MXU_COLS):
                wk = wgu_f[w, :, k * MXU_COLS:(k + 1) * MXU_COLS].astype(BF16)
                sep = jnp.dot(wk, perm, preferred_element_type=F32).astype(BF16)
                wg_s[:, k * half:(k + 1) * half] = sep[:, :half]
                wu_s[:, k * half:(k + 1) * half] = sep[:, half:]
            wd_s[...] = wdn_f[w].astype(BF16)

            @pl.when(nxt_ref[i] >= 0)
            def _():
                for cp in weight_copies(nxt_ref[i], 1 - w):
                    cp.start()

        xb = jnp.concatenate([c.astype(BF16) for c in _load_row_tiles(xs_ref, EXPERT_BLOCK)], axis=1)
        g = jnp.dot(xb, wg_s[...], preferred_element_type=F32) + bg_ref[0]
        u = jnp.dot(xb, wu_s[...], preferred_element_type=F32) + bu_ref[0]
        g = jnp.minimum(g, SWIGLU_LIMIT)
        u = jnp.clip(u, -SWIGLU_LIMIT, SWIGLU_LIMIT)
        a = g * (1.0 / (1.0 + jnp.exp(-SWIGLU_ALPHA * g))) * (u + 1.0)
        _store_row_tiles(o_ref, jnp.dot(a.astype(BF16), wd_s[...], preferred_element_type=F32) + bd_ref[0])

    @pl.when(i >= nact)
    def _():
        o_ref[...] = jnp.zeros(o_ref.shape, F32)


def _experts(block_exp, run_id, next_exp, n_active, xs, w_gu, w_dn, bg, bu, bd):
    f, d = w_dn.shape[1:]
    n_rows = xs.shape[0] // ROW_TILE
    nblk = n_rows // EXPERT_BLOCK
    bsel = lambda i, be, ru, nx, na: (be[i], 0, 0)
    blk = lambda i, be, ru, nx, na: (i, 0)
    anyspace = pl.BlockSpec(memory_space=pl.ANY)
    return pl.pallas_call(
        _expert_kernel,
        grid_spec=pltpu.PrefetchScalarGridSpec(
            num_scalar_prefetch=4,
            grid=(nblk,),
            in_specs=[pl.BlockSpec((EXPERT_BLOCK * ROW_TILE, LANES), blk),
                      anyspace, anyspace,
                      pl.BlockSpec((1, 1, f), bsel),
                      pl.BlockSpec((1, 1, f), bsel),
                      pl.BlockSpec((1, 1, d), bsel)],
            out_specs=pl.BlockSpec((EXPERT_BLOCK * ROW_TILE, LANES), blk),
            scratch_shapes=[pltpu.VMEM((2, d, 2 * f), F32),
                            pltpu.VMEM((2, f, d), F32),
                            pltpu.VMEM((d, f), BF16), pltpu.VMEM((d, f), BF16), pltpu.VMEM((f, d), BF16),
                            pltpu.SemaphoreType.DMA((2, 2))]),
        out_shape=jax.ShapeDtypeStruct((n_rows * ROW_TILE, LANES), F32),
        compiler_params=_cparams(("arbitrary",)),
        name="moe_experts",
    )(block_exp, run_id, next_exp, n_active, xs, w_gu, w_dn, bg, bu, bd)


def _combine_kernel(dest_ref, rows_hbm, gate_ref, x1_ref, g2_ref, fg_ref, o_ref, buf, sems, *, tm):
    n = pl.program_id(0)
    slot = n % 2

    def row_gather(step, to_slot, tok, k):
        n_tok = tm * pl.num_programs(0)
        return _tile_copy(rows_hbm, dest_ref[k * n_tok + step * tm + tok], buf.at[to_slot, k], tok, sems.at[to_slot])

    @pl.when(n == 0)
    def _():
        def issue(tok, carry):
            for k in range(TOP_K):
                row_gather(0, 0, tok, k).start()
            return carry
        lax.fori_loop(0, tm, issue, 0, unroll=4)

    @pl.when(n + 1 < pl.num_programs(0))
    def _():
        for tok in range(tm):
            for k in range(TOP_K):
                row_gather(n + 1, 1 - slot, tok, k).start(priority=k % 2)

    for k in range(TOP_K):
        pltpu.make_async_copy(rows_hbm.at[_tile_rows(0, tm)], buf.at[slot, k], sems.at[slot]).wait()

    gate = gate_ref[...]
    parts = None
    for k in range(TOP_K):
        gk = gate[:, k:k + 1]
        tiles = _load_row_tiles(buf.at[slot, k], tm)
        parts = [gk * r for r in tiles] if parts is None else [p + gk * r for p, r in zip(parts, tiles)]
    y = jnp.concatenate(parts, axis=1)
    xo = x1_ref[...] + g2_ref[0] * y
    ms = jnp.mean(xo * xo, axis=-1, keepdims=True)
    o_ref[...] = xo * lax.rsqrt(ms + NORM_EPS) * fg_ref[...]


def _combine(dest, rows, gates, x1, g2, final_g, tm):
    b, s, d = x1.shape
    nb = s // tm
    out = pl.pallas_call(
        functools.partial(_combine_kernel, tm=tm),
        grid_spec=pltpu.PrefetchScalarGridSpec(
            num_scalar_prefetch=1,
            grid=(b * nb,),
            in_specs=[pl.BlockSpec(memory_space=pl.ANY),
                      pl.BlockSpec((tm, TOP_K), lambda n, ds: (n, 0)),
                      pl.BlockSpec((tm, d), lambda n, ds: (n, 0)),
                      pl.BlockSpec((1, 1, d), lambda n, ds: (n // nb, 0, 0)),
                      pl.BlockSpec((1, d), lambda n, ds: (0, 0))],
            out_specs=pl.BlockSpec((tm, d), lambda n, ds: (n, 0)),
            scratch_shapes=[pltpu.VMEM((2, TOP_K, tm * ROW_TILE, LANES), F32), pltpu.SemaphoreType.DMA((2,))]),
        out_shape=jax.ShapeDtypeStruct((b * s, d), F32),
        compiler_params=_cparams(("arbitrary",)),
        name="moe_combine",
    )(dest, rows, gates, x1.reshape(b * s, d), g2, final_g)
    return out.reshape(b, s, d)


def _pick(n, prefs):
    for p in prefs:
        if n % p == 0:
            return p
    raise ValueError(f"no tile for {n}")


def kernel(x, c, ctx, c_ctx, w_mod, b_mod, norm1_g, w_in, lam_q1, lam_k1, lam_q2, lam_k2, subln_g,
           sink, w_out, norm2_g, w_router, b_router, w_gate_up, b_gate_up, w_down, b_down, final_g):
    b, s, d = x.shape
    c_len = ctx.shape[1]
    assert w_mod.shape[0] == 1, "single-layer block"
    assert d == ROW_TILE * LANES, "dispatched token rows are one (8, 128) f32 tile each"
    t = b * s

    pad = (-(b + 1)) % 8
    cvecs = jnp.concatenate([c, c_ctx[None, :], jnp.zeros((pad, d), F32)], axis=0)
    mod = _adaln(cvecs, w_mod[0], b_mod[0])
    sh1, sc1, g1, sh2, sc2, g2 = [mod[:b, k * d:(k + 1) * d].reshape(b, 1, d) for k in range(6)]
    csh1 = mod[b:b + 1, 0:d].reshape(1, 1, d)
    csc1 = mod[b:b + 1, d:2 * d].reshape(1, 1, d)

    w_in_bf = w_in[0].astype(BF16)
    cos, sin = _rope_tables(s)
    n1 = norm1_g[0].reshape(1, d)
    qat, ka, vat, qbt, kb, vbt = _inproj_latent(x, n1, sh1, sc1, w_in_bf, cos, sin, _pick(s, (1024, 512, 256)))
    w_ctx_bf = jnp.concatenate([w_in_bf[:, O_KA:O_QB], w_in_bf[:, O_KB:IN_COLS]], axis=1)
    kac, vact, kbc, vbct = _inproj_ctx(ctx, n1, csh1, csc1, w_ctx_bf)

    ka_all = jnp.concatenate([ka, kac], axis=1)
    vat_all = jnp.concatenate([vat, vact], axis=2)
    sk = s + c_len
    ya = _diff_attn(qat, ka_all, vat_all,
                    lam_q1[0].reshape(1, -1), lam_k1[0].reshape(1, -1),
                    lam_q2[0].reshape(1, -1), lam_k2[0].reshape(1, -1),
                    subln_g[0].reshape(1, -1),
                    _pick(s, (1024, 512, 256, 128)), _pick(sk, (768, 512, 384, 256, 128)))

    yb = _win_attn(qbt, kb, vbt, kbc, vbct, sink[0].reshape(1, -1), _pick(s, (256, 128)))

    w_out_bf = w_out[0].astype(BF16)
    w_r_hi = w_router[0].astype(BF16)
    w_r_lo = (w_router[0] - w_r_hi.astype(F32)).astype(BF16)
    x1, hx2, code, gates, counts = _outproj_router(
        ya, yb, w_out_bf[:QA_COLS], w_out_bf[QA_COLS:], x, g1, norm2_g[0].reshape(1, d), sh2, sc2,
        jnp.concatenate([w_r_hi, w_r_lo], axis=1).T, b_router[0].reshape(-1, 1), _pick(s, (512, 256, 128)))

    n_assign = t * TOP_K
    n_rows = n_assign + N_EXPERTS * EXPERT_BLOCK
    counts_i = counts.reshape(-1).astype(jnp.int32)
    padded = ((counts_i + EXPERT_BLOCK - 1) // EXPERT_BLOCK) * EXPERT_BLOCK
    pend = jnp.cumsum(padded).astype(jnp.int32)
    pbound = jnp.concatenate([jnp.zeros((1,), jnp.int32), pend])
    block_start = jnp.arange(n_rows // EXPERT_BLOCK, dtype=jnp.int32) * EXPERT_BLOCK
    block_exp = jnp.minimum(jnp.sum((pend[None, :] <= block_start[:, None]).astype(jnp.int32), axis=1),
                            N_EXPERTS - 1)
    n_active = pend[-1:] // EXPERT_BLOCK
    changed = jnp.concatenate([jnp.ones((1,), jnp.int32), (block_exp[1:] != block_exp[:-1]).astype(jnp.int32)])
    run_id = jnp.cumsum(changed).astype(jnp.int32) - 1
    e_ids = jnp.arange(N_EXPERTS, dtype=jnp.int32)
    later_nonempty = (e_ids[None, :] > e_ids[:, None]) & (padded[None, :] > 0)
    next_of_expert = jnp.min(jnp.where(later_nonempty, e_ids[None, :], N_EXPERTS), axis=1)
    next_of_expert = jnp.where(next_of_expert < N_EXPERTS, next_of_expert, -1).astype(jnp.int32)
    next_exp = jnp.sum(jnp.where(block_exp[:, None] == e_ids[None, :], next_of_expert[None, :], 0), axis=1)

    dest = _sorted_rows(pbound, code, _pick(t, (2048, 1024, 512, 256)))
    xs = _dispatch(pbound, counts_i, dest, hx2, n_rows, _pick(n_assign, (1024,)))

    f = w_down.shape[2]
    bg = b_gate_up[0, :, 0::2].reshape(N_EXPERTS, 1, f)
    bu = b_gate_up[0, :, 1::2].reshape(N_EXPERTS, 1, f)
    bd = b_down[0].reshape(N_EXPERTS, 1, d)
    rows = _experts(block_exp, run_id, next_exp, n_active, xs, w_gate_up[0], w_down[0], bg, bu, bd)

    return _combine(dest[:TOP_K].reshape(-1), rows, gates[:TOP_K].T, x1, g2, final_g.reshape(1, d), _pick(s, (256,)))
```

```python
import functools
import math

import jax
import jax.numpy as jnp
import numpy as np
from jax import lax
from jax.experimental import pallas as pl
from jax.experimental.pallas import tpu as pltpu

F32 = jnp.float32
BF16 = jnp.bfloat16

GRID_W = 64
NORM_EPS = 1e-6
ROPE_BASE = 10000.0
MASK_VALUE = -1e30
DA_HEADS = 4
HEAD_DIM = 64
WA_HEADS = 8
WA_KV_HEADS = 2
WA_GROUP = WA_HEADS // WA_KV_HEADS
WINDOW = 128
N_EXPERTS = 32
TOP_K = 4
SWIGLU_LIMIT = 7.0
SWIGLU_ALPHA = 1.702
EXPERT_BLOCK = 256
LAM_INIT = 0.8 - 0.6 * math.exp(-0.3 * 0)
LOG2E = math.log2(math.e)

QA_COLS = DA_HEADS * 2 * HEAD_DIM
KA_COLS = QA_COLS
VA_COLS = QA_COLS
QB_COLS = WA_HEADS * HEAD_DIM
KB_COLS = WA_KV_HEADS * HEAD_DIM
VB_COLS = KB_COLS
O_QA = 0
O_KA = O_QA + QA_COLS
O_VA = O_KA + KA_COLS
O_QB = O_VA + VA_COLS
O_KB = O_QB + QB_COLS
O_VB = O_KB + KB_COLS
IN_COLS = O_VB + VB_COLS

LANES = 128
ROW_TILE = 8
INPROJ_SUBTILES = 2
MXU_COLS = 256
ONES_ROWS = 16
VMEM_LIMIT = 56 * 1024 * 1024


def _cparams(sem):
    return pltpu.CompilerParams(dimension_semantics=sem, vmem_limit_bytes=VMEM_LIMIT)


def _adaln_kernel(c_ref, w_ref, b_ref, o_ref):
    cv = c_ref[...]
    s = cv * (1.0 / (1.0 + jnp.exp(-cv)))
    rows = s.shape[0]
    w = w_ref[...]
    s_hi = s.astype(BF16)
    s_lo = (s - s_hi.astype(F32)).astype(BF16)
    w_hi = w.astype(BF16)
    w_lo = (w - w_hi.astype(F32)).astype(BF16)
    part = jnp.dot(jnp.concatenate([s_hi, s_lo], axis=0), w_hi, preferred_element_type=F32)
    o_ref[...] = part[:rows] + part[rows:] + jnp.dot(s_hi, w_lo, preferred_element_type=F32) + b_ref[...]


def _adaln(cvecs, w_mod, b_mod):
    rows, d = cvecs.shape
    n = w_mod.shape[1]
    tn = 1024
    return pl.pallas_call(
        _adaln_kernel,
        grid=(n // tn,),
        in_specs=[pl.BlockSpec((rows, d), lambda j: (0, 0)),
                  pl.BlockSpec((d, tn), lambda j: (0, j)),
                  pl.BlockSpec((1, tn), lambda j: (0, j))],
        out_specs=pl.BlockSpec((rows, tn), lambda j: (0, j)),
        out_shape=jax.ShapeDtypeStruct((rows, n), F32),
        compiler_params=_cparams(("arbitrary",)),
        name="adaln",
    )(cvecs, w_mod, b_mod.reshape(1, n))


def _rope_section(sec, cos, sin):
    tm = sec.shape[0]
    lane = lax.broadcasted_iota(jnp.int32, (tm, LANES), 1)
    low = (lane % 32) < 16
    outs = []
    for j in range(sec.shape[1] // LANES):
        c = sec[:, j * LANES:(j + 1) * LANES]
        partner = jnp.where(low, pltpu.roll(c, LANES - 16, 1), pltpu.roll(c, 16, 1))
        outs.append(c * cos + partner * sin)
    return jnp.concatenate(outs, axis=1)


def _modulated_norm(x, g, shift, scale):
    ms = jnp.mean(x * x, axis=-1, keepdims=True)
    return (x * lax.rsqrt(ms + NORM_EPS) * g) * (1.0 + scale) + shift


def _inproj_latent_kernel(x_ref, g_ref, sh_ref, sc_ref, w_ref, cos_ref, sin_ref,
                          qat_ref, ka_ref, vat_ref, qbt_ref, kb_ref, vbt_ref, p_ref):
    tm = x_ref.shape[1]
    th = tm // INPROJ_SUBTILES
    qscale = HEAD_DIM ** -0.5 * LOG2E

    def project(sub):
        rows = slice(sub * th, (sub + 1) * th)
        h = _modulated_norm(x_ref[0, rows], g_ref[...], sh_ref[0], sc_ref[0])
        p_ref[sub] = jnp.dot(h.astype(BF16), w_ref[...], preferred_element_type=F32)

    def emit(sub):
        rows = slice(sub * th, (sub + 1) * th)
        p = p_ref[sub]
        cos = cos_ref[rows]
        sin = sin_ref[rows]
        qat_ref[0, :, rows] = (_rope_section(p[:, O_QA:O_KA], cos, sin) * qscale).T.astype(BF16)
        ka_ref[0, rows] = _rope_section(p[:, O_KA:O_VA], cos, sin).astype(BF16)
        vat_ref[0, :, rows] = p[:, O_VA:O_QB].T.astype(BF16)
        qbt_ref[0, :, rows] = (_rope_section(p[:, O_QB:O_KB], cos, sin) * qscale).T.astype(BF16)
        kb_ref[0, rows] = _rope_section(p[:, O_KB:O_VB], cos, sin).astype(BF16)
        vbt_ref[0, :, rows] = p[:, O_VB:IN_COLS].T.astype(BF16)

    project(0)
    for sub in range(1, INPROJ_SUBTILES):
        project(sub)
        emit(sub - 1)
    emit(INPROJ_SUBTILES - 1)


def _inproj_ctx_kernel(x_ref, g_ref, sh_ref, sc_ref, w_ref, ka_ref, vat_ref, kb_ref, vbt_ref):
    h = _modulated_norm(x_ref[0], g_ref[...], sh_ref[0], sc_ref[0])
    p = jnp.dot(h.astype(BF16), w_ref[...], preferred_element_type=F32)
    ka_ref[0] = p[:, 0:KA_COLS].astype(BF16)
    vat_ref[0] = p[:, KA_COLS:KA_COLS + VA_COLS].T.astype(BF16)
    kb_ref[0] = p[:, KA_COLS + VA_COLS:KA_COLS + VA_COLS + KB_COLS].astype(BF16)
    vbt_ref[0] = p[:, KA_COLS + VA_COLS + KB_COLS:].T.astype(BF16)


def _rope_tables(n_tok):
    pos = np.arange(n_tok)
    nf = HEAD_DIM // 4
    inv = ROPE_BASE ** (-np.arange(nf) / nf)
    ar = (pos // GRID_W)[:, None] * inv
    ac = (pos % GRID_W)[:, None] * inv
    cos = np.concatenate([np.cos(ar), np.cos(ar), np.cos(ac), np.cos(ac)], axis=1)
    sin = np.concatenate([-np.sin(ar), np.sin(ar), -np.sin(ac), np.sin(ac)], axis=1)
    reps = (1, LANES // HEAD_DIM)
    return jnp.asarray(np.tile(cos, reps), F32), jnp.asarray(np.tile(sin, reps), F32)


def _inproj_latent(x, g, shift, scale, w_bf16, cos, sin, tm):
    b, s, d = x.shape
    row = lambda bi, i: (bi, i, 0)
    colt = lambda bi, i: (bi, 0, i)
    mod = lambda bi, i: (bi, 0, 0)
    fixed = lambda bi, i: (0, 0)
    return pl.pallas_call(
        _inproj_latent_kernel,
        grid=(b, s // tm),
        in_specs=[pl.BlockSpec((1, tm, d), row),
                  pl.BlockSpec((1, d), fixed),
                  pl.BlockSpec((1, 1, d), mod),
                  pl.BlockSpec((1, 1, d), mod),
                  pl.BlockSpec((d, IN_COLS), fixed),
                  pl.BlockSpec((tm, LANES), lambda bi, i: (i, 0)),
                  pl.BlockSpec((tm, LANES), lambda bi, i: (i, 0))],
        out_specs=[pl.BlockSpec((1, QA_COLS, tm), colt),
                   pl.BlockSpec((1, tm, KA_COLS), row),
                   pl.BlockSpec((1, VA_COLS, tm), colt),
                   pl.BlockSpec((1, QB_COLS, tm), colt),
                   pl.BlockSpec((1, tm, KB_COLS), row),
                   pl.BlockSpec((1, VB_COLS, tm), colt)],
        out_shape=[jax.ShapeDtypeStruct((b, QA_COLS, s), BF16),
                   jax.ShapeDtypeStruct((b, s, KA_COLS), BF16),
                   jax.ShapeDtypeStruct((b, VA_COLS, s), BF16),
                   jax.ShapeDtypeStruct((b, QB_COLS, s), BF16),
                   jax.ShapeDtypeStruct((b, s, KB_COLS), BF16),
                   jax.ShapeDtypeStruct((b, VB_COLS, s), BF16)],
        scratch_shapes=[pltpu.VMEM((INPROJ_SUBTILES, tm // INPROJ_SUBTILES, IN_COLS), F32)],
        compiler_params=_cparams(("arbitrary", "arbitrary")),
        name="inproj_latent",
    )(x, g, shift, scale, w_bf16, cos, sin)


def _inproj_ctx(ctx, g, shift, scale, w_ctx_bf16):
    b, c, d = ctx.shape
    n = w_ctx_bf16.shape[1]
    whole = lambda bi: (bi, 0, 0)
    mod = lambda bi: (0, 0, 0)
    fixed = lambda bi: (0, 0)
    return pl.pallas_call(
        _inproj_ctx_kernel,
        grid=(b,),
        in_specs=[pl.BlockSpec((1, c, d), whole),
                  pl.BlockSpec((1, d), fixed),
                  pl.BlockSpec((1, 1, d), mod),
                  pl.BlockSpec((1, 1, d), mod),
                  pl.BlockSpec((d, n), fixed)],
        out_specs=[pl.BlockSpec((1, c, KA_COLS), whole),
                   pl.BlockSpec((1, VA_COLS, c), whole),
                   pl.BlockSpec((1, c, KB_COLS), whole),
                   pl.BlockSpec((1, VB_COLS, c), whole)],
        out_shape=[jax.ShapeDtypeStruct((b, c, KA_COLS), BF16),
                   jax.ShapeDtypeStruct((b, VA_COLS, c), BF16),
                   jax.ShapeDtypeStruct((b, c, KB_COLS), BF16),
                   jax.ShapeDtypeStruct((b, VB_COLS, c), BF16)],
        compiler_params=_cparams(("arbitrary",)),
        name="inproj_ctx",
    )(ctx, g, shift, scale, w_ctx_bf16)


def _diff_attn_kernel(qt_ref, k_ref, vt_ref, lq1_ref, lk1_ref, lq2_ref, lk2_ref, sg_ref, o_ref,
                      m_ref, acc_ref, s_ref, *, tq, tk):
    d = HEAD_DIM
    hw = 2 * d
    n_chunks = k_ref.shape[1] // tk
    n_tiles = qt_ref.shape[2] // tq
    ones = jnp.ones((ONES_ROWS, tk), BF16)
    lam = (jnp.exp(jnp.sum(lq1_ref[...] * lk1_ref[...], axis=-1, keepdims=True))
           - jnp.exp(jnp.sum(lq2_ref[...] * lk2_ref[...], axis=-1, keepdims=True)) + LAM_INIT)

    def query_rhs(t):
        qt = qt_ref[0, :, pl.ds(pl.multiple_of(t * tq, tq), tq)]
        row = lax.broadcasted_iota(jnp.int32, qt.shape, 0)
        zero = jnp.zeros_like(qt)
        return jnp.where(row < d, qt, zero), jnp.where(row >= d, qt, zero)

    def scores(c, j, rhs):
        off = pl.multiple_of(j * tk, tk)
        s_ref[c] = jnp.dot(k_ref[0, pl.ds(off, tk), :], rhs[c], preferred_element_type=F32)

    def accumulate(c, j):
        off = pl.multiple_of(j * tk, tk)
        vt = jnp.concatenate([vt_ref[0, :, pl.ds(off, tk)], ones], axis=0)
        st = s_ref[c]
        m_old = m_ref[c]
        m_new = jnp.maximum(m_old, jnp.max(st, axis=0, keepdims=True))
        alpha = jnp.exp2(m_old - m_new)
        p = jnp.exp2(st - m_new).astype(BF16)
        acc_ref[c] = alpha * acc_ref[c] + jnp.dot(vt, p, preferred_element_type=F32)
        m_ref[c] = m_new

    group = 5 if (n_chunks - 1) % 5 == 0 else 1

    scores(0, 0, query_rhs(0))

    def tile(t, carry):
        rhs = query_rhs(t)
        m_ref[...] = jnp.full(m_ref.shape, -jnp.inf, F32)
        acc_ref[...] = jnp.zeros(acc_ref.shape, F32)

        def chunk_group(jj, carry2):
            for r in range(group):
                j = group * jj + r
                scores(1, j, rhs)
                accumulate(0, j)
                scores(0, j + 1, rhs)
                accumulate(1, j)
            return carry2

        lax.fori_loop(0, (n_chunks - 1) // group, chunk_group, 0)
        scores(1, n_chunks - 1, rhs)
        accumulate(0, n_chunks - 1)
        scores(0, 0, query_rhs(jnp.minimum(t + 1, n_tiles - 1)))
        accumulate(1, n_chunks - 1)

        a1 = acc_ref[0]
        a2 = acc_ref[1]
        ot = a1[:hw] / a1[hw:hw + 1] - lam * (a2[:hw] / a2[hw:hw + 1])
        ms = jnp.mean(ot * ot, axis=0, keepdims=True)
        ot = ot * lax.rsqrt(ms + NORM_EPS)
        o_ref[0, pl.ds(pl.multiple_of(t * tq, tq), tq), :] = (ot.T * (sg_ref[...] * (1.0 - LAM_INIT))).astype(o_ref.dtype)
        return carry

    lax.fori_loop(0, n_tiles, tile, 0)


def _diff_attn(qat, ka, vat, lq1, lk1, lq2, lk2, subln_g, tq, tk):
    b, _, s = qat.shape
    sk = ka.shape[1]
    hw = 2 * HEAD_DIM
    vec = lambda bi, h: (0, 0)
    return pl.pallas_call(
        functools.partial(_diff_attn_kernel, tq=tq, tk=tk),
        grid=(b, DA_HEADS),
        in_specs=[pl.BlockSpec((1, hw, s), lambda bi, h: (bi, h, 0)),
                  pl.BlockSpec((1, sk, hw), lambda bi, h: (bi, 0, h)),
                  pl.BlockSpec((1, hw, sk), lambda bi, h: (bi, h, 0)),
                  pl.BlockSpec((1, HEAD_DIM), vec),
                  pl.BlockSpec((1, HEAD_DIM), vec),
                  pl.BlockSpec((1, HEAD_DIM), vec),
                  pl.BlockSpec((1, HEAD_DIM), vec),
                  pl.BlockSpec((1, hw), vec)],
        out_specs=pl.BlockSpec((1, s, hw), lambda bi, h: (bi, 0, h)),
        out_shape=jax.ShapeDtypeStruct((b, s, DA_HEADS * hw), BF16),
        scratch_shapes=[pltpu.VMEM((2, 1, tq), F32),
                        pltpu.VMEM((2, hw + ONES_ROWS, tq), F32),
                        pltpu.VMEM((2, tk, tq), F32)],
        compiler_params=_cparams(("arbitrary", "arbitrary")),
        name="diff_attn",
    )(qat, ka, vat, lq1, lk1, lq2, lk2, subln_g)


def _win_attn_kernel(qt_ref, k_ref, vt_ref, kc_ref, vct_ref, sink_ref, o_ref, s_ref, *, tq, lk):
    d = HEAD_DIM
    grp = WA_GROUP
    s_len = k_ref.shape[1]
    c_len = kc_ref.shape[1]
    nk = lk + c_len
    i = pl.program_id(1)
    q0 = i * tq
    start = pl.multiple_of(jnp.clip(q0 - WINDOW, 0, s_len - lk), LANES)
    keys = jnp.concatenate([k_ref[0, pl.ds(start, lk), :], kc_ref[0]], axis=0)
    kpos = start + lax.broadcasted_iota(jnp.int32, (lk, tq), 0)
    qpos = q0 + lax.broadcasted_iota(jnp.int32, (lk, tq), 1)
    visible = jnp.abs(kpos - qpos) <= WINDOW
    visible = jnp.concatenate([visible] * grp, axis=1)
    ones = jnp.ones((ONES_ROWS, nk), BF16)
    qt = qt_ref[0]
    blank = jnp.zeros((d, grp * tq), BF16)
    outs = []
    for kv in range(WA_KV_HEADS):
        heads = range(kv * grp, (kv + 1) * grp)
        qcat = jnp.concatenate([qt[h * d:(h + 1) * d, :] for h in heads], axis=1)
        rhs = jnp.concatenate([qcat if j == kv else blank for j in range(WA_KV_HEADS)], axis=0)
        s_ref[kv] = jnp.dot(keys, rhs, preferred_element_type=F32)
    for kv in range(WA_KV_HEADS):
        heads = range(kv * grp, (kv + 1) * grp)
        st = s_ref[kv]
        st = jnp.concatenate([jnp.where(visible, st[:lk], MASK_VALUE), st[lk:]], axis=0)
        sink = jnp.concatenate([jnp.broadcast_to(sink_ref[:, h:h + 1] * LOG2E, (1, tq)) for h in heads], axis=1)
        m = jnp.maximum(jnp.max(st, axis=0, keepdims=True), sink)
        p = jnp.exp2(st - m).astype(BF16)
        vt = jnp.concatenate([vt_ref[0, kv * d:(kv + 1) * d, pl.ds(start, lk)],
                              vct_ref[0, kv * d:(kv + 1) * d, :]], axis=1)
        acc = jnp.dot(jnp.concatenate([vt, ones], axis=0), p, preferred_element_type=F32)
        o = acc[:d] / (acc[d:d + 1] + jnp.exp2(sink - m))
        outs.extend(o[:, g * tq:(g + 1) * tq] for g in range(grp))
    o_ref[0] = jnp.concatenate(outs, axis=0).T.astype(o_ref.dtype)


def _win_attn(qbt, kb, vbt, kbc, vbct, sink, tq):
    b, _, s = qbt.shape
    c = kbc.shape[1]
    lk = tq + 2 * WINDOW
    assert s >= lk and tq % LANES == 0
    whole = lambda bi, i: (bi, 0, 0)
    return pl.pallas_call(
        functools.partial(_win_attn_kernel, tq=tq, lk=lk),
        grid=(b, s // tq),
        in_specs=[pl.BlockSpec((1, QB_COLS, tq), lambda bi, i: (bi, 0, i)),
                  pl.BlockSpec((1, s, KB_COLS), whole),
                  pl.BlockSpec((1, VB_COLS, s), whole),
                  pl.BlockSpec((1, c, KB_COLS), whole),
                  pl.BlockSpec((1, VB_COLS, c), whole),
                  pl.BlockSpec((1, WA_HEADS), lambda bi, i: (0, 0))],
        out_specs=pl.BlockSpec((1, tq, QB_COLS), lambda bi, i: (bi, i, 0)),
        out_shape=jax.ShapeDtypeStruct((b, s, QB_COLS), BF16),
        scratch_shapes=[pltpu.VMEM((WA_KV_HEADS, lk + c, WA_GROUP * tq), F32)],
        compiler_params=_cparams(("arbitrary", "arbitrary")),
        name="win_attn",
    )(qbt, kb, vbt, kbc, vbct, sink)


def _outproj_router_kernel(ya_ref, yb_ref, woa_ref, wob_ref, x_ref, g1_ref, n2_ref, sh_ref, sc_ref,
                           wr_ref, br_ref, x1_ref, hx_ref, code_ref, gate_ref, cnt_ref, carry_ref):
    first = jnp.logical_and(pl.program_id(0) == 0, pl.program_id(1) == 0)

    @pl.when(first)
    def _():
        carry_ref[...] = jnp.zeros(carry_ref.shape, F32)

    y = (jnp.dot(ya_ref[0], woa_ref[...], preferred_element_type=F32)
         + jnp.dot(yb_ref[0], wob_ref[...], preferred_element_type=F32))
    x1 = x_ref[0] + g1_ref[0] * y
    x1_ref[0] = x1
    hx = _modulated_norm(x1, n2_ref[...], sh_ref[0], sc_ref[0])
    _store_row_tiles(hx_ref, hx)
    tm = hx.shape[0]
    hx_hi = hx.astype(BF16)
    hx_lo = (hx - hx_hi.astype(F32)).astype(BF16)
    wr = wr_ref[...]
    nt = (((1,), (1,)), ((), ()))
    part = lax.dot_general(wr, hx_hi, nt, preferred_element_type=F32)
    logits = (part[:N_EXPERTS] + part[N_EXPERTS:]
              + lax.dot_general(wr[:N_EXPERTS], hx_lo, nt, preferred_element_type=F32) + br_ref[...])

    row_e = lax.broadcasted_iota(jnp.int32, (N_EXPERTS, tm), 0).astype(F32)
    work = logits
    tops, idxs, hots = [], [], []
    for _k in range(TOP_K):
        m = jnp.max(work, axis=0, keepdims=True)
        idx = jnp.min(jnp.where(work == m, row_e, float(N_EXPERTS)), axis=0, keepdims=True)
        hot = row_e == idx
        work = jnp.where(hot, -jnp.inf, work)
        tops.append(m)
        idxs.append(idx)
        hots.append(hot)
    es = [jnp.exp(t - tops[0]) for t in tops]
    den = es[0] + es[1] + es[2] + es[3]

    multi = jnp.zeros((N_EXPERTS, tm), F32)
    for hot in hots:
        multi = multi + hot.astype(F32)
    r_i = lax.broadcasted_iota(jnp.int32, (tm, tm), 0)
    c_i = lax.broadcasted_iota(jnp.int32, (tm, tm), 1)
    tri = (r_i <= c_i).astype(BF16)
    incl = jnp.dot(multi.astype(BF16), tri, preferred_element_type=F32)
    before = carry_ref[...] + incl - 1.0

    codes, gates = [], []
    for k in range(TOP_K):
        rank = jnp.sum(jnp.where(hots[k], before, 0.0), axis=0, keepdims=True)
        codes.append(idxs[k].astype(jnp.int32) * 65536 + rank.astype(jnp.int32))
        gates.append(es[k] / den)
    fill = ROW_TILE - TOP_K
    code_ref[...] = jnp.concatenate(codes + [jnp.zeros((fill, tm), jnp.int32)], axis=0)
    gate_ref[...] = jnp.concatenate(gates + [jnp.zeros((fill, tm), F32)], axis=0)
    carry_ref[...] = carry_ref[...] + jnp.sum(multi, axis=1, keepdims=True)
    cnt_ref[...] = carry_ref[...]


def _outproj_router(ya, yb, woa, wob, x, g1, n2, sh2, sc2, w_r_parts, b_r, tm):
    b, s, d = x.shape
    nb = s // tm
    row = lambda bi, i: (bi, i, 0)
    mod = lambda bi, i: (bi, 0, 0)
    fixed = lambda bi, i: (0, 0)
    tok = lambda bi, i: (bi * nb + i, 0)
    half = ya.shape[2]
    return pl.pallas_call(
        _outproj_router_kernel,
        grid=(b, nb),
        in_specs=[pl.BlockSpec((1, tm, half), row),
                  pl.BlockSpec((1, tm, half), row),
                  pl.BlockSpec((half, d), fixed),
                  pl.BlockSpec((half, d), fixed),
                  pl.BlockSpec((1, tm, d), row),
                  pl.BlockSpec((1, 1, d), mod),
                  pl.BlockSpec((1, d), fixed),
                  pl.BlockSpec((1, 1, d), mod),
                  pl.BlockSpec((1, 1, d), mod),
                  pl.BlockSpec((2 * N_EXPERTS, d), fixed),
                  pl.BlockSpec((N_EXPERTS, 1), fixed)],
        out_specs=[pl.BlockSpec((1, tm, d), row),
                   pl.BlockSpec((tm * ROW_TILE, LANES), tok),
                   pl.BlockSpec((ROW_TILE, tm), lambda bi, i: (0, bi * nb + i)),
                   pl.BlockSpec((ROW_TILE, tm), lambda bi, i: (0, bi * nb + i)),
                   pl.BlockSpec((N_EXPERTS, 1), fixed)],
        out_shape=[jax.ShapeDtypeStruct((b, s, d), F32),
                   jax.ShapeDtypeStruct((b * s * ROW_TILE, LANES), F32),
                   jax.ShapeDtypeStruct((ROW_TILE, b * s), jnp.int32),
                   jax.ShapeDtypeStruct((ROW_TILE, b * s), F32),
                   jax.ShapeDtypeStruct((N_EXPERTS, 1), F32)],
        scratch_shapes=[pltpu.VMEM((N_EXPERTS, 1), F32)],
        compiler_params=_cparams(("arbitrary", "arbitrary")),
        name="outproj_router",
    )(ya, yb, woa, wob, x, g1, n2, sh2, sc2, w_r_parts, b_r)


def _sorted_rows_kernel(pbound_ref, code_ref, dest_ref):
    code = code_ref[...]
    expert = code >> 16
    base = jnp.zeros_like(code)
    for e in range(N_EXPERTS):
        base = jnp.where(expert == e, pbound_ref[e], base)
    dest_ref[...] = base + (code & 0xFFFF)


def _sorted_rows(pbound, code, tn):
    rows, n_tok = code.shape
    return pl.pallas_call(
        _sorted_rows_kernel,
        grid_spec=pltpu.PrefetchScalarGridSpec(
            num_scalar_prefetch=1,
            grid=(n_tok // tn,),
            in_specs=[pl.BlockSpec((rows, tn), lambda i, pb: (0, i))],
            out_specs=pl.BlockSpec((rows, tn), lambda i, pb: (0, i))),
        out_shape=jax.ShapeDtypeStruct((rows, n_tok), jnp.int32),
        compiler_params=_cparams(("arbitrary",)),
        name="moe_sorted_rows",
    )(pbound, code)


def _store_row_tiles(ref, val):
    n = val.shape[0]
    for c in range(ROW_TILE):
        ref[pl.ds(c, n, stride=ROW_TILE), :] = val[:, c * LANES:(c + 1) * LANES]


def _load_row_tiles(ref, n):
    return [ref[pl.ds(c, n, stride=ROW_TILE), :] for c in range(ROW_TILE)]


def _tile_rows(row, count=1):
    if isinstance(row, int):
        return pl.ds(row * ROW_TILE, count * ROW_TILE)
    return pl.ds(pl.multiple_of(row * ROW_TILE, ROW_TILE), count * ROW_TILE)


def _tile_copy(src_ref, src_row, dst_ref, dst_row, sem):
    return pltpu.make_async_copy(src_ref.at[_tile_rows(src_row)], dst_ref.at[_tile_rows(dst_row)], sem)


def _dispatch_kernel(pbound_ref, cnt_ref, dest_ref, hx_ref, xs_hbm, zeros, sem, zsem, *, chunk):
    n_rows = xs_hbm.shape[0] // ROW_TILE
    blk = zeros.shape[0] // ROW_TILE
    pad_sizes = [blk >> (k + 1) for k in range(blk.bit_length() - 1)]

    @pl.when(pl.program_id(0) == 0)
    def _():
        zeros[...] = jnp.zeros(zeros.shape, F32)
        tail_blocks = (n_rows - pbound_ref[N_EXPERTS]) // blk

        def pad_copy(row, size):
            return pltpu.make_async_copy(zeros.at[_tile_rows(0, size)], xs_hbm.at[_tile_rows(row, size)], zsem)

        def pad_expert(e, carry):
            row = pbound_ref[e] + cnt_ref[e]
            n_pad = pbound_ref[e + 1] - row
            for size in pad_sizes:
                take = (n_pad & size) != 0

                @pl.when(take)
                def _():
                    pad_copy(row, size).start()
                row = row + jnp.where(take, size, 0)
            return carry

        def tail_copy(t):
            return pltpu.make_async_copy(zeros, xs_hbm.at[_tile_rows(pbound_ref[N_EXPERTS] + t * blk, blk)], zsem)

        def tail_start(t, carry):
            tail_copy(t).start()
            return carry

        lax.fori_loop(0, N_EXPERTS, pad_expert, 0)
        lax.fori_loop(0, tail_blocks, tail_start, 0)

        def pad_expert_wait(e, carry):
            n_pad = pbound_ref[e + 1] - pbound_ref[e] - cnt_ref[e]
            for size in pad_sizes:
                @pl.when((n_pad & size) != 0)
                def _():
                    pad_copy(0, size).wait()
            return carry

        def tail_wait(t, carry):
            tail_copy(0).wait()
            return carry

        lax.fori_loop(0, N_EXPERTS, pad_expert_wait, 0)
        lax.fori_loop(0, tail_blocks, tail_wait, 0)

    for j in range(chunk):
        _tile_copy(hx_ref, j // TOP_K, xs_hbm, dest_ref[j % TOP_K, j // TOP_K], sem).start(priority=j % 2)
    pltpu.make_async_copy(xs_hbm.at[_tile_rows(0, chunk)], xs_hbm.at[_tile_rows(0, chunk)], sem).wait()


def _dispatch(pbound, counts_i, dest, hx_tiles, n_rows, chunk):
    n_assign = dest.shape[1] * TOP_K
    return pl.pallas_call(
        functools.partial(_dispatch_kernel, chunk=chunk),
        grid_spec=pltpu.PrefetchScalarGridSpec(
            num_scalar_prefetch=2,
            grid=(n_assign // chunk,),
            in_specs=[pl.BlockSpec((ROW_TILE, chunk // TOP_K), lambda i, pb, ct: (0, i), memory_space=pltpu.SMEM),
                      pl.BlockSpec((chunk // TOP_K * ROW_TILE, LANES), lambda i, pb, ct: (i, 0))],
            out_specs=pl.BlockSpec(memory_space=pl.ANY),
            scratch_shapes=[pltpu.VMEM((EXPERT_BLOCK * ROW_TILE, LANES), F32),
                            pltpu.SemaphoreType.DMA(()),
                            pltpu.SemaphoreType.DMA(())]),
        out_shape=jax.ShapeDtypeStruct((n_rows * ROW_TILE, LANES), F32),
        compiler_params=_cparams(("arbitrary",)),
        name="moe_dispatch",
    )(pbound, counts_i, dest, hx_tiles)


def _expert_kernel(bexp_ref, run_ref, nxt_ref, nact_ref, xs_ref, wgu_hbm, wdn_hbm, bg_ref, bu_ref, bd_ref,
                   o_ref, wgu_f, wdn_f, wg_s, wu_s, wd_s, wsem):
    i = pl.program_id(0)
    nact = nact_ref[0]

    def weight_copies(expert, w):
        return (pltpu.make_async_copy(wgu_hbm.at[expert], wgu_f.at[w], wsem.at[w, 0]),
                pltpu.make_async_copy(wdn_hbm.at[expert], wdn_f.at[w], wsem.at[w, 1]))

    @pl.when(i == 0)
    def _():
        for cp in weight_copies(bexp_ref[0], 0):
            cp.start()

    @pl.when(i < nact)
    def _():
        changed = jnp.logical_or(i == 0, bexp_ref[i] != bexp_ref[jnp.maximum(i - 1, 0)])

        @pl.when(changed)
        def _():
            w = run_ref[i] % 2
            for cp in weight_copies(bexp_ref[i], w):
                cp.wait()
            half = MXU_COLS // 2
            src = lax.broadcasted_iota(jnp.int32, (MXU_COLS, MXU_COLS), 0)
            dst = lax.broadcasted_iota(jnp.int32, (MXU_COLS, MXU_COLS), 1)
            perm = (src == jnp.where(dst < half, 2 * dst, 2 * (dst - half) + 1)).astype(BF16)
            for k in range(wgu_f.shape[2] // MXU_COLS):
                wk = wgu_f[w, :, k * MXU_COLS:(k + 1) * MXU_COLS].astype(BF16)
                sep = jnp.dot(wk, perm, preferred_element_type=F32).astype(BF16)
                wg_s[:, k * half:(k + 1) * half] = sep[:, :half]
                wu_s[:, k * half:(k + 1) * half] = sep[:, half:]
            wd_s[...] = wdn_f[w].astype(BF16)

            @pl.when(nxt_ref[i] >= 0)
            def _():
                for cp in weight_copies(nxt_ref[i], 1 - w):
                    cp.start()

        xb = jnp.concatenate([c.astype(BF16) for c in _load_row_tiles(xs_ref, EXPERT_BLOCK)], axis=1)
        g = jnp.dot(xb, wg_s[...], preferred_element_type=F32) + bg_ref[0]
        u = jnp.dot(xb, wu_s[...], preferred_element_type=F32) + bu_ref[0]
        g = jnp.minimum(g, SWIGLU_LIMIT)
        u = jnp.clip(u, -SWIGLU_LIMIT, SWIGLU_LIMIT)
        a = g * (1.0 / (1.0 + jnp.exp(-SWIGLU_ALPHA * g))) * (u + 1.0)
        _store_row_tiles(o_ref, jnp.dot(a.astype(BF16), wd_s[...], preferred_element_type=F32) + bd_ref[0])

    @pl.when(i >= nact)
    def _():
        o_ref[...] = jnp.zeros(o_ref.shape, F32)


def _experts(block_exp, run_id, next_exp, n_active, xs, w_gu, w_dn, bg, bu, bd):
    f, d = w_dn.shape[1:]
    n_rows = xs.shape[0] // ROW_TILE
    nblk = n_rows // EXPERT_BLOCK
    bsel = lambda i, be, ru, nx, na: (be[i], 0, 0)
    blk = lambda i, be, ru, nx, na: (i, 0)
    anyspace = pl.BlockSpec(memory_space=pl.ANY)
    return pl.pallas_call(
        _expert_kernel,
        grid_spec=pltpu.PrefetchScalarGridSpec(
            num_scalar_prefetch=4,
            grid=(nblk,),
            in_specs=[pl.BlockSpec((EXPERT_BLOCK * ROW_TILE, LANES), blk),
                      anyspace, anyspace,
                      pl.BlockSpec((1, 1, f), bsel),
                      pl.BlockSpec((1, 1, f), bsel),
                      pl.BlockSpec((1, 1, d), bsel)],
            out_specs=pl.BlockSpec((EXPERT_BLOCK * ROW_TILE, LANES), blk),
            scratch_shapes=[pltpu.VMEM((2, d, 2 * f), F32),
                            pltpu.VMEM((2, f, d), F32),
                            pltpu.VMEM((d, f), BF16), pltpu.VMEM((d, f), BF16), pltpu.VMEM((f, d), BF16),
                            pltpu.SemaphoreType.DMA((2, 2))]),
        out_shape=jax.ShapeDtypeStruct((n_rows * ROW_TILE, LANES), F32),
        compiler_params=_cparams(("arbitrary",)),
        name="moe_experts",
    )(block_exp, run_id, next_exp, n_active, xs, w_gu, w_dn, bg, bu, bd)


def _combine_kernel(dest_ref, rows_hbm, gate_ref, x1_ref, g2_ref, fg_ref, o_ref, buf, sems, *, tm):
    n = pl.program_id(0)
    slot = n % 2

    def row_gather(step, to_slot, tok, k):
        n_tok = tm * pl.num_programs(0)
        return _tile_copy(rows_hbm, dest_ref[k * n_tok + step * tm + tok], buf.at[to_slot, k], tok, sems.at[to_slot])

    @pl.when(n == 0)
    def _():
        def issue(tok, carry):
            for k in range(TOP_K):
                row_gather(0, 0, tok, k).start()
            return carry
        lax.fori_loop(0, tm, issue, 0, unroll=4)

    @pl.when(n + 1 < pl.num_programs(0))
    def _():
        for tok in range(tm):
            for k in range(TOP_K):
                row_gather(n + 1, 1 - slot, tok, k).start(priority=k % 2)

    for k in range(TOP_K):
        pltpu.make_async_copy(rows_hbm.at[_tile_rows(0, tm)], buf.at[slot, k], sems.at[slot]).wait()

    gate = gate_ref[...]
    parts = None
    for k in range(TOP_K):
        gk = gate[:, k:k + 1]
        tiles = _load_row_tiles(buf.at[slot, k], tm)
        parts = [gk * r for r in tiles] if parts is None else [p + gk * r for p, r in zip(parts, tiles)]
    y = jnp.concatenate(parts, axis=1)
    xo = x1_ref[...] + g2_ref[0] * y
    ms = jnp.mean(xo * xo, axis=-1, keepdims=True)
    o_ref[...] = xo * lax.rsqrt(ms + NORM_EPS) * fg_ref[...]


def _combine(dest, rows, gates, x1, g2, final_g, tm):
    b, s, d = x1.shape
    nb = s // tm
    out = pl.pallas_call(
        functools.partial(_combine_kernel, tm=tm),
        grid_spec=pltpu.PrefetchScalarGridSpec(
            num_scalar_prefetch=1,
            grid=(b * nb,),
            in_specs=[pl.BlockSpec(memory_space=pl.ANY),
                      pl.BlockSpec((tm, TOP_K), lambda n, ds: (n, 0)),
                      pl.BlockSpec((tm, d), lambda n, ds: (n, 0)),
                      pl.BlockSpec((1, 1, d), lambda n, ds: (n // nb, 0, 0)),
                      pl.BlockSpec((1, d), lambda n, ds: (0, 0))],
            out_specs=pl.BlockSpec((tm, d), lambda n, ds: (n, 0)),
            scratch_shapes=[pltpu.VMEM((2, TOP_K, tm * ROW_TILE, LANES), F32), pltpu.SemaphoreType.DMA((2,))]),
        out_shape=jax.ShapeDtypeStruct((b * s, d), F32),
        compiler_params=_cparams(("arbitrary",)),
        name="moe_combine",
    )(dest, rows, gates, x1.reshape(b * s, d), g2, final_g)
    return out.reshape(b, s, d)


def _pick(n, prefs):
    for p in prefs:
        if n % p == 0:
            return p
    raise ValueError(f"no tile for {n}")


def kernel(x, c, ctx, c_ctx, w_mod, b_mod, norm1_g, w_in, lam_q1, lam_k1, lam_q2, lam_k2, subln_g,
           sink, w_out, norm2_g, w_router, b_router, w_gate_up, b_gate_up, w_down, b_down, final_g):
    b, s, d = x.shape
    c_len = ctx.shape[1]
    assert w_mod.shape[0] == 1, "single-layer block"
    assert d == ROW_TILE * LANES, "dispatched token rows are one (8, 128) f32 tile each"
    t = b * s

    pad = (-(b + 1)) % 8
    cvecs = jnp.concatenate([c, c_ctx[None, :], jnp.zeros((pad, d), F32)], axis=0)
    mod = _adaln(cvecs, w_mod[0], b_mod[0])
    sh1, sc1, g1, sh2, sc2, g2 = [mod[:b, k * d:(k + 1) * d].reshape(b, 1, d) for k in range(6)]
    csh1 = mod[b:b + 1, 0:d].reshape(1, 1, d)
    csc1 = mod[b:b + 1, d:2 * d].reshape(1, 1, d)

    w_in_bf = w_in[0].astype(BF16)
    cos, sin = _rope_tables(s)
    n1 = norm1_g[0].reshape(1, d)
    qat, ka, vat, qbt, kb, vbt = _inproj_latent(x, n1, sh1, sc1, w_in_bf, cos, sin, _pick(s, (1024, 512, 256)))
    w_ctx_bf = jnp.concatenate([w_in_bf[:, O_KA:O_QB], w_in_bf[:, O_KB:IN_COLS]], axis=1)
    kac, vact, kbc, vbct = _inproj_ctx(ctx, n1, csh1, csc1, w_ctx_bf)

    ka_all = jnp.concatenate([ka, kac], axis=1)
    vat_all = jnp.concatenate([vat, vact], axis=2)
    sk = s + c_len
    ya = _diff_attn(qat, ka_all, vat_all,
                    lam_q1[0].reshape(1, -1), lam_k1[0].reshape(1, -1),
                    lam_q2[0].reshape(1, -1), lam_k2[0].reshape(1, -1),
                    subln_g[0].reshape(1, -1),
                    _pick(s, (1024, 512, 256, 128)), _pick(sk, (768, 512, 384, 256, 128)))

    yb = _win_attn(qbt, kb, vbt, kbc, vbct, sink[0].reshape(1, -1), _pick(s, (256, 128)))

    w_out_bf = w_out[0].astype(BF16)
    w_r_hi = w_router[0].astype(BF16)
    w_r_lo = (w_router[0] - w_r_hi.astype(F32)).astype(BF16)
    x1, hx2, code, gates, counts = _outproj_router(
        ya, yb, w_out_bf[:QA_COLS], w_out_bf[QA_COLS:], x, g1, norm2_g[0].reshape(1, d), sh2, sc2,
        jnp.concatenate([w_r_hi, w_r_lo], axis=1).T, b_router[0].reshape(-1, 1), _pick(s, (1024, 512, 256, 128)))

    n_assign = t * TOP_K
    n_rows = n_assign + N_EXPERTS * EXPERT_BLOCK
    counts_i = counts.reshape(-1).astype(jnp.int32)
    padded = ((counts_i + EXPERT_BLOCK - 1) // EXPERT_BLOCK) * EXPERT_BLOCK
    pend = jnp.cumsum(padded).astype(jnp.int32)
    pbound = jnp.concatenate([jnp.zeros((1,), jnp.int32), pend])
    block_start = jnp.arange(n_rows // EXPERT_BLOCK, dtype=jnp.int32) * EXPERT_BLOCK
    block_exp = jnp.minimum(jnp.sum((pend[None, :] <= block_start[:, None]).astype(jnp.int32), axis=1),
                            N_EXPERTS - 1)
    n_active = pend[-1:] // EXPERT_BLOCK
    changed = jnp.concatenate([jnp.ones((1,), jnp.int32), (block_exp[1:] != block_exp[:-1]).astype(jnp.int32)])
    run_id = jnp.cumsum(changed).astype(jnp.int32) - 1
    e_ids = jnp.arange(N_EXPERTS, dtype=jnp.int32)
    later_nonempty = (e_ids[None, :] > e_ids[:, None]) & (padded[None, :] > 0)
    next_of_expert = jnp.min(jnp.where(later_nonempty, e_ids[None, :], N_EXPERTS), axis=1)
    next_of_expert = jnp.where(next_of_expert < N_EXPERTS, next_of_expert, -1).astype(jnp.int32)
    next_exp = jnp.sum(jnp.where(block_exp[:, None] == e_ids[None, :], next_of_expert[None, :], 0), axis=1)

    dest = _sorted_rows(pbound, code, _pick(t, (2048, 1024, 512, 256)))
    xs = _dispatch(pbound, counts_i, dest, hx2, n_rows, _pick(n_assign, (1024,)))

    f = w_down.shape[2]
    bg = b_gate_up[0, :, 0::2].reshape(N_EXPERTS, 1, f)
    bu = b_gate_up[0, :, 1::2].reshape(N_EXPERTS, 1, f)
    bd = b_down[0].reshape(N_EXPERTS, 1, d)
    rows = _experts(block_exp, run_id, next_exp, n_active, xs, w_gate_up[0], w_down[0], bg, bu, bd)

    return _combine(dest[:TOP_K].reshape(-1), rows, gates[:TOP_K].T, x1, g2, final_g.reshape(1, d), _pick(s, (256,)))
```

```python
import functools
import math

import jax
import jax.numpy as jnp
import numpy as np
from jax import lax
from jax.experimental import pallas as pl
from jax.experimental.pallas import tpu as pltpu

F32 = jnp.float32
BF16 = jnp.bfloat16

GRID_W = 64
NORM_EPS = 1e-6
ROPE_BASE = 10000.0
MASK_VALUE = -1e30
DA_HEADS = 4
HEAD_DIM = 64
WA_HEADS = 8
WA_KV_HEADS = 2
WA_GROUP = WA_HEADS // WA_KV_HEADS
WINDOW = 128
N_EXPERTS = 32
TOP_K = 4
SWIGLU_LIMIT = 7.0
SWIGLU_ALPHA = 1.702
EXPERT_BLOCK = 256
LAM_INIT = 0.8 - 0.6 * math.exp(-0.3 * 0)
LOG2E = math.log2(math.e)

QA_COLS = DA_HEADS * 2 * HEAD_DIM
KA_COLS = QA_COLS
VA_COLS = QA_COLS
QB_COLS = WA_HEADS * HEAD_DIM
KB_COLS = WA_KV_HEADS * HEAD_DIM
VB_COLS = KB_COLS
O_QA = 0
O_KA = O_QA + QA_COLS
O_VA = O_KA + KA_COLS
O_QB = O_VA + VA_COLS
O_KB = O_QB + QB_COLS
O_VB = O_KB + KB_COLS
IN_COLS = O_VB + VB_COLS

LANES = 128
ROW_TILE = 8
INPROJ_SUBTILES = 2
MXU_COLS = 256
ONES_ROWS = 16
VMEM_LIMIT = 56 * 1024 * 1024


def _cparams(sem):
    return pltpu.CompilerParams(dimension_semantics=sem, vmem_limit_bytes=VMEM_LIMIT)


def _adaln_kernel(c_ref, w_ref, b_ref, o_ref):
    cv = c_ref[...]
    s = cv * (1.0 / (1.0 + jnp.exp(-cv)))
    rows = s.shape[0]
    w = w_ref[...]
    s_hi = s.astype(BF16)
    s_lo = (s - s_hi.astype(F32)).astype(BF16)
    w_hi = w.astype(BF16)
    w_lo = (w - w_hi.astype(F32)).astype(BF16)
    part = jnp.dot(jnp.concatenate([s_hi, s_lo], axis=0), w_hi, preferred_element_type=F32)
    o_ref[...] = part[:rows] + part[rows:] + jnp.dot(s_hi, w_lo, preferred_element_type=F32) + b_ref[...]


def _adaln(cvecs, w_mod, b_mod):
    rows, d = cvecs.shape
    n = w_mod.shape[1]
    tn = 1024
    return pl.pallas_call(
        _adaln_kernel,
        grid=(n // tn,),
        in_specs=[pl.BlockSpec((rows, d), lambda j: (0, 0)),
                  pl.BlockSpec((d, tn), lambda j: (0, j)),
                  pl.BlockSpec((1, tn), lambda j: (0, j))],
        out_specs=pl.BlockSpec((rows, tn), lambda j: (0, j)),
        out_shape=jax.ShapeDtypeStruct((rows, n), F32),
        compiler_params=_cparams(("arbitrary",)),
        name="adaln",
    )(cvecs, w_mod, b_mod.reshape(1, n))


def _rope_section(sec, cos, sin):
    tm = sec.shape[0]
    lane = lax.broadcasted_iota(jnp.int32, (tm, LANES), 1)
    low = (lane % 32) < 16
    outs = []
    for j in range(sec.shape[1] // LANES):
        c = sec[:, j * LANES:(j + 1) * LANES]
        partner = jnp.where(low, pltpu.roll(c, LANES - 16, 1), pltpu.roll(c, 16, 1))
        outs.append(c * cos + partner * sin)
    return jnp.concatenate(outs, axis=1)


def _modulated_norm(x, g, shift, scale):
    ms = jnp.mean(x * x, axis=-1, keepdims=True)
    return (x * lax.rsqrt(ms + NORM_EPS) * g) * (1.0 + scale) + shift


def _inproj_latent_kernel(x_ref, ctx_ref, g_ref, sh_ref, sc_ref, csh_ref, csc_ref, w_ref, cos_ref, sin_ref,
                          qat_ref, ka_ref, vat_ref, qbt_ref, kb_ref, vbt_ref, kbc_ref, vbct_ref, p_ref):
    tm = x_ref.shape[1]
    th = tm // INPROJ_SUBTILES
    qscale = HEAD_DIM ** -0.5 * LOG2E

    def project(sub):
        rows = slice(sub * th, (sub + 1) * th)
        h = _modulated_norm(x_ref[0, rows], g_ref[...], sh_ref[0], sc_ref[0])
        p_ref[sub] = jnp.dot(h.astype(BF16), w_ref[...], preferred_element_type=F32)

    def emit(sub):
        rows = slice(sub * th, (sub + 1) * th)
        p = p_ref[sub]
        cos = cos_ref[rows]
        sin = sin_ref[rows]
        qat_ref[0, :, rows] = (_rope_section(p[:, O_QA:O_KA], cos, sin) * qscale).T.astype(BF16)
        ka_ref[0, rows] = _rope_section(p[:, O_KA:O_VA], cos, sin).astype(BF16)
        vat_ref[0, :, rows] = p[:, O_VA:O_QB].T.astype(BF16)
        qbt_ref[0, :, rows] = (_rope_section(p[:, O_QB:O_KB], cos, sin) * qscale).T.astype(BF16)
        kb_ref[0, rows] = _rope_section(p[:, O_KB:O_VB], cos, sin).astype(BF16)
        vbt_ref[0, :, rows] = p[:, O_VB:IN_COLS].T.astype(BF16)

    latent_steps = pl.num_programs(1) - 1

    @pl.when(pl.program_id(1) < latent_steps)
    def _():
        project(0)
        for sub in range(1, INPROJ_SUBTILES):
            project(sub)
            emit(sub - 1)
        emit(INPROJ_SUBTILES - 1)

    @pl.when(pl.program_id(1) == latent_steps)
    def _():
        c = ctx_ref.shape[1]
        h = _modulated_norm(ctx_ref[0], g_ref[...], csh_ref[0], csc_ref[0])
        p = jnp.dot(h.astype(BF16), w_ref[...], preferred_element_type=F32)
        ka_ref[0, :c] = p[:, O_KA:O_VA].astype(BF16)
        vat_ref[0, :, :c] = p[:, O_VA:O_QB].T.astype(BF16)
        kbc_ref[0] = p[:, O_KB:O_VB].astype(BF16)
        vbct_ref[0] = p[:, O_VB:IN_COLS].T.astype(BF16)


def _rope_tables(n_tok):
    pos = np.arange(n_tok)
    nf = HEAD_DIM // 4
    inv = ROPE_BASE ** (-np.arange(nf) / nf)
    ar = (pos // GRID_W)[:, None] * inv
    ac = (pos % GRID_W)[:, None] * inv
    cos = np.concatenate([np.cos(ar), np.cos(ar), np.cos(ac), np.cos(ac)], axis=1)
    sin = np.concatenate([-np.sin(ar), np.sin(ar), -np.sin(ac), np.sin(ac)], axis=1)
    reps = (1, LANES // HEAD_DIM)
    return jnp.asarray(np.tile(cos, reps), F32), jnp.asarray(np.tile(sin, reps), F32)


def _inproj_latent(x, ctx, g, shift, scale, cshift, cscale, w_bf16, cos, sin, tm):
    b, s, d = x.shape
    c = ctx.shape[1]
    nb = s // tm
    assert c <= tm and c % LANES == 0
    lat = lambda i: jnp.minimum(i, nb - 1)
    row = lambda bi, i: (bi, lat(i), 0)
    colt = lambda bi, i: (bi, 0, lat(i))
    mod = lambda bi, i: (bi, 0, 0)
    cmod = lambda bi, i: (0, 0, 0)
    fixed = lambda bi, i: (0, 0)
    whole = lambda bi, i: (bi, 0, 0)
    return pl.pallas_call(
        _inproj_latent_kernel,
        grid=(b, nb + 1),
        in_specs=[pl.BlockSpec((1, tm, d), row),
                  pl.BlockSpec((1, c, d), whole),
                  pl.BlockSpec((1, d), fixed),
                  pl.BlockSpec((1, 1, d), mod),
                  pl.BlockSpec((1, 1, d), mod),
                  pl.BlockSpec((1, 1, d), cmod),
                  pl.BlockSpec((1, 1, d), cmod),
                  pl.BlockSpec((d, IN_COLS), fixed),
                  pl.BlockSpec((tm, LANES), lambda bi, i: (lat(i), 0)),
                  pl.BlockSpec((tm, LANES), lambda bi, i: (lat(i), 0))],
        out_specs=[pl.BlockSpec((1, QA_COLS, tm), colt),
                   pl.BlockSpec((1, tm, KA_COLS), lambda bi, i: (bi, i, 0)),
                   pl.BlockSpec((1, VA_COLS, tm), lambda bi, i: (bi, 0, i)),
                   pl.BlockSpec((1, QB_COLS, tm), colt),
                   pl.BlockSpec((1, tm, KB_COLS), row),
                   pl.BlockSpec((1, VB_COLS, tm), colt),
                   pl.BlockSpec((1, c, KB_COLS), whole),
                   pl.BlockSpec((1, VB_COLS, c), whole)],
        out_shape=[jax.ShapeDtypeStruct((b, QA_COLS, s), BF16),
                   jax.ShapeDtypeStruct((b, s + c, KA_COLS), BF16),
                   jax.ShapeDtypeStruct((b, VA_COLS, s + c), BF16),
                   jax.ShapeDtypeStruct((b, QB_COLS, s), BF16),
                   jax.ShapeDtypeStruct((b, s, KB_COLS), BF16),
                   jax.ShapeDtypeStruct((b, VB_COLS, s), BF16),
                   jax.ShapeDtypeStruct((b, c, KB_COLS), BF16),
                   jax.ShapeDtypeStruct((b, VB_COLS, c), BF16)],
        scratch_shapes=[pltpu.VMEM((INPROJ_SUBTILES, tm // INPROJ_SUBTILES, IN_COLS), F32)],
        compiler_params=_cparams(("arbitrary", "arbitrary")),
        name="inproj_latent",
    )(x, ctx, g, shift, scale, cshift, cscale, w_bf16, cos, sin)


def _diff_attn_kernel(qt_ref, k_ref, vt_ref, lq1_ref, lk1_ref, lq2_ref, lk2_ref, sg_ref, o_ref,
                      m_ref, acc_ref, s_ref, *, tq, tk):
    d = HEAD_DIM
    hw = 2 * d
    n_chunks = k_ref.shape[1] // tk
    n_tiles = qt_ref.shape[2] // tq
    ones = jnp.ones((ONES_ROWS, tk), BF16)
    lam = (jnp.exp(jnp.sum(lq1_ref[...] * lk1_ref[...], axis=-1, keepdims=True))
           - jnp.exp(jnp.sum(lq2_ref[...] * lk2_ref[...], axis=-1, keepdims=True)) + LAM_INIT)

    def query_rhs(t):
        qt = qt_ref[0, :, pl.ds(pl.multiple_of(t * tq, tq), tq)]
        row = lax.broadcasted_iota(jnp.int32, qt.shape, 0)
        zero = jnp.zeros_like(qt)
        return jnp.where(row < d, qt, zero), jnp.where(row >= d, qt, zero)

    def scores(c, j, rhs):
        off = pl.multiple_of(j * tk, tk)
        s_ref[c] = jnp.dot(k_ref[0, pl.ds(off, tk), :], rhs[c], preferred_element_type=F32)

    def accumulate(c, j):
        off = pl.multiple_of(j * tk, tk)
        vt = jnp.concatenate([vt_ref[0, :, pl.ds(off, tk)], ones], axis=0)
        st = s_ref[c]
        m_old = m_ref[c]
        m_new = jnp.maximum(m_old, jnp.max(st, axis=0, keepdims=True))
        alpha = jnp.exp2(m_old - m_new)
        p = jnp.exp2(st - m_new).astype(BF16)
        acc_ref[c] = alpha * acc_ref[c] + jnp.dot(vt, p, preferred_element_type=F32)
        m_ref[c] = m_new

    group = 5 if (n_chunks - 1) % 5 == 0 else 1

    scores(0, 0, query_rhs(0))

    def tile(t, carry):
        rhs = query_rhs(t)
        m_ref[...] = jnp.full(m_ref.shape, -jnp.inf, F32)
        acc_ref[...] = jnp.zeros(acc_ref.shape, F32)

        def chunk_group(jj, carry2):
            for r in range(group):
                j = group * jj + r
                scores(1, j, rhs)
                accumulate(0, j)
                scores(0, j + 1, rhs)
                accumulate(1, j)
            return carry2

        lax.fori_loop(0, (n_chunks - 1) // group, chunk_group, 0)
        scores(1, n_chunks - 1, rhs)
        accumulate(0, n_chunks - 1)
        scores(0, 0, query_rhs(jnp.minimum(t + 1, n_tiles - 1)))
        accumulate(1, n_chunks - 1)

        a1 = acc_ref[0]
        a2 = acc_ref[1]
        ot = a1[:hw] / a1[hw:hw + 1] - lam * (a2[:hw] / a2[hw:hw + 1])
        ms = jnp.mean(ot * ot, axis=0, keepdims=True)
        ot = ot * lax.rsqrt(ms + NORM_EPS)
        o_ref[0, pl.ds(pl.multiple_of(t * tq, tq), tq), :] = (ot.T * (sg_ref[...] * (1.0 - LAM_INIT))).astype(o_ref.dtype)
        return carry

    lax.fori_loop(0, n_tiles, tile, 0)


def _diff_attn(qat, ka, vat, lq1, lk1, lq2, lk2, subln_g, tq, tk):
    b, _, s = qat.shape
    sk = ka.shape[1]
    hw = 2 * HEAD_DIM
    vec = lambda bi, h: (0, 0)
    return pl.pallas_call(
        functools.partial(_diff_attn_kernel, tq=tq, tk=tk),
        grid=(b, DA_HEADS),
        in_specs=[pl.BlockSpec((1, hw, s), lambda bi, h: (bi, h, 0)),
                  pl.BlockSpec((1, sk, hw), lambda bi, h: (bi, 0, h)),
                  pl.BlockSpec((1, hw, sk), lambda bi, h: (bi, h, 0)),
                  pl.BlockSpec((1, HEAD_DIM), vec),
                  pl.BlockSpec((1, HEAD_DIM), vec),
                  pl.BlockSpec((1, HEAD_DIM), vec),
                  pl.BlockSpec((1, HEAD_DIM), vec),
                  pl.BlockSpec((1, hw), vec)],
        out_specs=pl.BlockSpec((1, s, hw), lambda bi, h: (bi, 0, h)),
        out_shape=jax.ShapeDtypeStruct((b, s, DA_HEADS * hw), BF16),
        scratch_shapes=[pltpu.VMEM((2, 1, tq), F32),
                        pltpu.VMEM((2, hw + ONES_ROWS, tq), F32),
                        pltpu.VMEM((2, tk, tq), F32)],
        compiler_params=_cparams(("arbitrary", "arbitrary")),
        name="diff_attn",
    )(qat, ka, vat, lq1, lk1, lq2, lk2, subln_g)


def _win_attn_kernel(qt_ref, k_ref, vt_ref, kc_ref, vct_ref, sink_ref, o_ref, s_ref, *, tq, lk):
    d = HEAD_DIM
    grp = WA_GROUP
    s_len = k_ref.shape[1]
    c_len = kc_ref.shape[1]
    nk = lk + c_len
    i = pl.program_id(1)
    q0 = i * tq
    start = pl.multiple_of(jnp.clip(q0 - WINDOW, 0, s_len - lk), LANES)
    keys = jnp.concatenate([k_ref[0, pl.ds(start, lk), :], kc_ref[0]], axis=0)
    kpos = start + lax.broadcasted_iota(jnp.int32, (lk, tq), 0)
    qpos = q0 + lax.broadcasted_iota(jnp.int32, (lk, tq), 1)
    visible = jnp.abs(kpos - qpos) <= WINDOW
    visible = jnp.concatenate([visible] * grp, axis=1)
    ones = jnp.ones((ONES_ROWS, nk), BF16)
    qt = qt_ref[0]
    blank = jnp.zeros((d, grp * tq), BF16)
    outs = []
    for kv in range(WA_KV_HEADS):
        heads = range(kv * grp, (kv + 1) * grp)
        qcat = jnp.concatenate([qt[h * d:(h + 1) * d, :] for h in heads], axis=1)
        rhs = jnp.concatenate([qcat if j == kv else blank for j in range(WA_KV_HEADS)], axis=0)
        s_ref[kv] = jnp.dot(keys, rhs, preferred_element_type=F32)
    for kv in range(WA_KV_HEADS):
        heads = range(kv * grp, (kv + 1) * grp)
        st = s_ref[kv]
        st = jnp.concatenate([jnp.where(visible, st[:lk], MASK_VALUE), st[lk:]], axis=0)
        sink = jnp.concatenate([jnp.broadcast_to(sink_ref[:, h:h + 1] * LOG2E, (1, tq)) for h in heads], axis=1)
        m = jnp.maximum(jnp.max(st, axis=0, keepdims=True), sink)
        p = jnp.exp2(st - m).astype(BF16)
        vt = jnp.concatenate([vt_ref[0, kv * d:(kv + 1) * d, pl.ds(start, lk)],
                              vct_ref[0, kv * d:(kv + 1) * d, :]], axis=1)
        acc = jnp.dot(jnp.concatenate([vt, ones], axis=0), p, preferred_element_type=F32)
        o = acc[:d] / (acc[d:d + 1] + jnp.exp2(sink - m))
        outs.extend(o[:, g * tq:(g + 1) * tq] for g in range(grp))
    o_ref[0] = jnp.concatenate(outs, axis=0).T.astype(o_ref.dtype)


def _win_attn(qbt, kb, vbt, kbc, vbct, sink, tq):
    b, _, s = qbt.shape
    c = kbc.shape[1]
    lk = tq + 2 * WINDOW
    assert s >= lk and tq % LANES == 0
    whole = lambda bi, i: (bi, 0, 0)
    return pl.pallas_call(
        functools.partial(_win_attn_kernel, tq=tq, lk=lk),
        grid=(b, s // tq),
        in_specs=[pl.BlockSpec((1, QB_COLS, tq), lambda bi, i: (bi, 0, i)),
                  pl.BlockSpec((1, s, KB_COLS), whole),
                  pl.BlockSpec((1, VB_COLS, s), whole),
                  pl.BlockSpec((1, c, KB_COLS), whole),
                  pl.BlockSpec((1, VB_COLS, c), whole),
                  pl.BlockSpec((1, WA_HEADS), lambda bi, i: (0, 0))],
        out_specs=pl.BlockSpec((1, tq, QB_COLS), lambda bi, i: (bi, i, 0)),
        out_shape=jax.ShapeDtypeStruct((b, s, QB_COLS), BF16),
        scratch_shapes=[pltpu.VMEM((WA_KV_HEADS, lk + c, WA_GROUP * tq), F32)],
        compiler_params=_cparams(("arbitrary", "arbitrary")),
        name="win_attn",
    )(qbt, kb, vbt, kbc, vbct, sink)


def _outproj_router_kernel(ya_ref, yb_ref, woa_ref, wob_ref, x_ref, g1_ref, n2_ref, sh_ref, sc_ref,
                           wr_ref, br_ref, x1_ref, hx_ref, code_ref, gate_ref, cnt_ref, carry_ref):
    first = jnp.logical_and(pl.program_id(0) == 0, pl.program_id(1) == 0)

    @pl.when(first)
    def _():
        carry_ref[...] = jnp.zeros(carry_ref.shape, F32)

    y = (jnp.dot(ya_ref[0], woa_ref[...], preferred_element_type=F32)
         + jnp.dot(yb_ref[0], wob_ref[...], preferred_element_type=F32))
    x1 = x_ref[0] + g1_ref[0] * y
    x1_ref[0] = x1
    hx = _modulated_norm(x1, n2_ref[...], sh_ref[0], sc_ref[0])
    _store_row_tiles(hx_ref, hx)
    tm = hx.shape[0]
    hx_hi = hx.astype(BF16)
    hx_lo = (hx - hx_hi.astype(F32)).astype(BF16)
    wr = wr_ref[...]
    nt = (((1,), (1,)), ((), ()))
    part = lax.dot_general(wr, hx_hi, nt, preferred_element_type=F32)
    logits = (part[:N_EXPERTS] + part[N_EXPERTS:]
              + lax.dot_general(wr[:N_EXPERTS], hx_lo, nt, preferred_element_type=F32) + br_ref[...])

    row_e = lax.broadcasted_iota(jnp.int32, (N_EXPERTS, tm), 0).astype(F32)
    work = logits
    tops, idxs, hots = [], [], []
    for _k in range(TOP_K):
        m = jnp.max(work, axis=0, keepdims=True)
        idx = jnp.min(jnp.where(work == m, row_e, float(N_EXPERTS)), axis=0, keepdims=True)
        hot = row_e == idx
        work = jnp.where(hot, -jnp.inf, work)
        tops.append(m)
        idxs.append(idx)
        hots.append(hot)
    es = [jnp.exp(t - tops[0]) for t in tops]
    den = es[0] + es[1] + es[2] + es[3]

    multi = jnp.zeros((N_EXPERTS, tm), F32)
    for hot in hots:
        multi = multi + hot.astype(F32)
    r_i = lax.broadcasted_iota(jnp.int32, (tm, tm), 0)
    c_i = lax.broadcasted_iota(jnp.int32, (tm, tm), 1)
    tri = (r_i <= c_i).astype(BF16)
    incl = jnp.dot(multi.astype(BF16), tri, preferred_element_type=F32)
    before = carry_ref[...] + incl - 1.0

    codes, gates = [], []
    for k in range(TOP_K):
        rank = jnp.sum(jnp.where(hots[k], before, 0.0), axis=0, keepdims=True)
        codes.append(idxs[k].astype(jnp.int32) * 65536 + rank.astype(jnp.int32))
        gates.append(es[k] / den)
    fill = ROW_TILE - TOP_K
    code_ref[...] = jnp.concatenate(codes + [jnp.zeros((fill, tm), jnp.int32)], axis=0)
    gate_ref[...] = jnp.concatenate(gates + [jnp.zeros((fill, tm), F32)], axis=0)
    carry_ref[...] = carry_ref[...] + jnp.sum(multi, axis=1, keepdims=True)
    cnt_ref[...] = carry_ref[...]


def _outproj_router(ya, yb, woa, wob, x, g1, n2, sh2, sc2, w_r_parts, b_r, tm):
    b, s, d = x.shape
    nb = s // tm
    row = lambda bi, i: (bi, i, 0)
    mod = lambda bi, i: (bi, 0, 0)
    fixed = lambda bi, i: (0, 0)
    tok = lambda bi, i: (bi * nb + i, 0)
    half = ya.shape[2]
    return pl.pallas_call(
        _outproj_router_kernel,
        grid=(b, nb),
        in_specs=[pl.BlockSpec((1, tm, half), row),
                  pl.BlockSpec((1, tm, half), row),
                  pl.BlockSpec((half, d), fixed),
                  pl.BlockSpec((half, d), fixed),
                  pl.BlockSpec((1, tm, d), row),
                  pl.BlockSpec((1, 1, d), mod),
                  pl.BlockSpec((1, d), fixed),
                  pl.BlockSpec((1, 1, d), mod),
                  pl.BlockSpec((1, 1, d), mod),
                  pl.BlockSpec((2 * N_EXPERTS, d), fixed),
                  pl.BlockSpec((N_EXPERTS, 1), fixed)],
        out_specs=[pl.BlockSpec((1, tm, d), row),
                   pl.BlockSpec((tm * ROW_TILE, LANES), tok),
                   pl.BlockSpec((ROW_TILE, tm), lambda bi, i: (0, bi * nb + i)),
                   pl.BlockSpec((ROW_TILE, tm), lambda bi, i: (0, bi * nb + i)),
                   pl.BlockSpec((N_EXPERTS, 1), fixed)],
        out_shape=[jax.ShapeDtypeStruct((b, s, d), F32),
                   jax.ShapeDtypeStruct((b * s * ROW_TILE, LANES), F32),
                   jax.ShapeDtypeStruct((ROW_TILE, b * s), jnp.int32),
                   jax.ShapeDtypeStruct((ROW_TILE, b * s), F32),
                   jax.ShapeDtypeStruct((N_EXPERTS, 1), F32)],
        scratch_shapes=[pltpu.VMEM((N_EXPERTS, 1), F32)],
        compiler_params=_cparams(("arbitrary", "arbitrary")),
        name="outproj_router",
    )(ya, yb, woa, wob, x, g1, n2, sh2, sc2, w_r_parts, b_r)


def _sorted_rows_kernel(pbound_ref, code_ref, dest_ref):
    code = code_ref[...]
    expert = code >> 16
    base = jnp.zeros_like(code)
    for e in range(N_EXPERTS):
        base = jnp.where(expert == e, pbound_ref[e], base)
    dest_ref[...] = base + (code & 0xFFFF)


def _sorted_rows(pbound, code, tn):
    rows, n_tok = code.shape
    return pl.pallas_call(
        _sorted_rows_kernel,
        grid_spec=pltpu.PrefetchScalarGridSpec(
            num_scalar_prefetch=1,
            grid=(n_tok // tn,),
            in_specs=[pl.BlockSpec((rows, tn), lambda i, pb: (0, i))],
            out_specs=pl.BlockSpec((rows, tn), lambda i, pb: (0, i))),
        out_shape=jax.ShapeDtypeStruct((rows, n_tok), jnp.int32),
        compiler_params=_cparams(("arbitrary",)),
        name="moe_sorted_rows",
    )(pbound, code)


def _store_row_tiles(ref, val):
    n = val.shape[0]
    for c in range(ROW_TILE):
        ref[pl.ds(c, n, stride=ROW_TILE), :] = val[:, c * LANES:(c + 1) * LANES]


def _load_row_tiles(ref, n):
    return [ref[pl.ds(c, n, stride=ROW_TILE), :] for c in range(ROW_TILE)]


def _tile_rows(row, count=1):
    if isinstance(row, int):
        return pl.ds(row * ROW_TILE, count * ROW_TILE)
    return pl.ds(pl.multiple_of(row * ROW_TILE, ROW_TILE), count * ROW_TILE)


def _tile_copy(src_ref, src_row, dst_ref, dst_row, sem):
    return pltpu.make_async_copy(src_ref.at[_tile_rows(src_row)], dst_ref.at[_tile_rows(dst_row)], sem)


def _dispatch_kernel(pbound_ref, cnt_ref, dest_ref, hx_ref, xs_hbm, zeros, sem, zsem, *, chunk):
    n_rows = xs_hbm.shape[0] // ROW_TILE
    blk = zeros.shape[0] // ROW_TILE
    pad_sizes = [blk >> (k + 1) for k in range(blk.bit_length() - 1)]

    @pl.when(pl.program_id(0) == 0)
    def _():
        zeros[...] = jnp.zeros(zeros.shape, F32)
        tail_blocks = (n_rows - pbound_ref[N_EXPERTS]) // blk

        def pad_copy(row, size):
            return pltpu.make_async_copy(zeros.at[_tile_rows(0, size)], xs_hbm.at[_tile_rows(row, size)], zsem)

        def pad_expert(e, carry):
            row = pbound_ref[e] + cnt_ref[e]
            n_pad = pbound_ref[e + 1] - row
            for size in pad_sizes:
                take = (n_pad & size) != 0

                @pl.when(take)
                def _():
                    pad_copy(row, size).start()
                row = row + jnp.where(take, size, 0)
            return carry

        def tail_copy(t):
            return pltpu.make_async_copy(zeros, xs_hbm.at[_tile_rows(pbound_ref[N_EXPERTS] + t * blk, blk)], zsem)

        def tail_start(t, carry):
            tail_copy(t).start()
            return carry

        lax.fori_loop(0, N_EXPERTS, pad_expert, 0)
        lax.fori_loop(0, tail_blocks, tail_start, 0)

        def pad_expert_wait(e, carry):
            n_pad = pbound_ref[e + 1] - pbound_ref[e] - cnt_ref[e]
            for size in pad_sizes:
                @pl.when((n_pad & size) != 0)
                def _():
                    pad_copy(0, size).wait()
            return carry

        def tail_wait(t, carry):
            tail_copy(0).wait()
            return carry

        lax.fori_loop(0, N_EXPERTS, pad_expert_wait, 0)
        lax.fori_loop(0, tail_blocks, tail_wait, 0)

    for j in range(chunk):
        _tile_copy(hx_ref, j // TOP_K, xs_hbm, dest_ref[j % TOP_K, j // TOP_K], sem).start(priority=j % 2)
    pltpu.make_async_copy(xs_hbm.at[_tile_rows(0, chunk)], xs_hbm.at[_tile_rows(0, chunk)], sem).wait()


def _dispatch(pbound, counts_i, dest, hx_tiles, n_rows, chunk):
    n_assign = dest.shape[1] * TOP_K
    return pl.pallas_call(
        functools.partial(_dispatch_kernel, chunk=chunk),
        grid_spec=pltpu.PrefetchScalarGridSpec(
            num_scalar_prefetch=2,
            grid=(n_assign // chunk,),
            in_specs=[pl.BlockSpec((ROW_TILE, chunk // TOP_K), lambda i, pb, ct: (0, i), memory_space=pltpu.SMEM),
                      pl.BlockSpec((chunk // TOP_K * ROW_TILE, LANES), lambda i, pb, ct: (i, 0))],
            out_specs=pl.BlockSpec(memory_space=pl.ANY),
            scratch_shapes=[pltpu.VMEM((EXPERT_BLOCK * ROW_TILE, LANES), F32),
                            pltpu.SemaphoreType.DMA(()),
                            pltpu.SemaphoreType.DMA(())]),
        out_shape=jax.ShapeDtypeStruct((n_rows * ROW_TILE, LANES), F32),
        compiler_params=_cparams(("arbitrary",)),
        name="moe_dispatch",
    )(pbound, counts_i, dest, hx_tiles)


def _expert_kernel(bexp_ref, run_ref, nxt_ref, nact_ref, xs_ref, wgu_hbm, wdn_hbm, bg_ref, bu_ref, bd_ref,
                   o_ref, wgu_f, wdn_f, wg_s, wu_s, wd_s, wsem):
    i = pl.program_id(0)
    nact = nact_ref[0]

    def weight_copies(expert, w):
        return (pltpu.make_async_copy(wgu_hbm.at[expert], wgu_f.at[w], wsem.at[w, 0]),
                pltpu.make_async_copy(wdn_hbm.at[expert], wdn_f.at[w], wsem.at[w, 1]))

    @pl.when(i == 0)
    def _():
        for cp in weight_copies(bexp_ref[0], 0):
            cp.start()

    @pl.when(i < nact)
    def _():
        changed = jnp.logical_or(i == 0, bexp_ref[i] != bexp_ref[jnp.maximum(i - 1, 0)])

        @pl.when(changed)
        def _():
            w = run_ref[i] % 2
            for cp in weight_copies(bexp_ref[i], w):
                cp.wait()
            half = MXU_COLS // 2
            src = lax.broadcasted_iota(jnp.int32, (MXU_COLS, MXU_COLS), 0)
            dst = lax.broadcasted_iota(jnp.int32, (MXU_COLS, MXU_COLS), 1)
            perm = (src == jnp.where(dst < half, 2 * dst, 2 * (dst - half) + 1)).astype(BF16)
            for k in range(wgu_f.shape[2] // MXU_COLS):
                wk = wgu_f[w, :, k * MXU_COLS:(k + 1) * MXU_COLS].astype(BF16)
                sep = jnp.dot(wk, perm, preferred_element_type=F32).astype(BF16)
                wg_s[:, k * half:(k + 1) * half] = sep[:, :half]
                wu_s[:, k * half:(k + 1) * half] = sep[:, half:]
            wd_s[...] = wdn_f[w].astype(BF16)

            @pl.when(nxt_ref[i] >= 0)
            def _():
                for cp in weight_copies(nxt_ref[i], 1 - w):
                    cp.start()

        xb = jnp.concatenate([c.astype(BF16) for c in _load_row_tiles(xs_ref, EXPERT_BLOCK)], axis=1)
        g = jnp.dot(xb, wg_s[...], preferred_element_type=F32) + bg_ref[0]
        u = jnp.dot(xb, wu_s[...], preferred_element_type=F32) + bu_ref[0]
        g = jnp.minimum(g, SWIGLU_LIMIT)
        u = jnp.clip(u, -SWIGLU_LIMIT, SWIGLU_LIMIT)
        a = g * (1.0 / (1.0 + jnp.exp(-SWIGLU_ALPHA * g))) * (u + 1.0)
        _store_row_tiles(o_ref, jnp.dot(a.astype(BF16), wd_s[...], preferred_element_type=F32) + bd_ref[0])

    @pl.when(i >= nact)
    def _():
        o_ref[...] = jnp.zeros(o_ref.shape, F32)


def _experts(block_exp, run_id, next_exp, n_active, xs, w_gu, w_dn, bg, bu, bd):
    f, d = w_dn.shape[1:]
    n_rows = xs.shape[0] // ROW_TILE
    nblk = n_rows // EXPERT_BLOCK
    bsel = lambda i, be, ru, nx, na: (be[i], 0, 0)
    blk = lambda i, be, ru, nx, na: (i, 0)
    anyspace = pl.BlockSpec(memory_space=pl.ANY)
    return pl.pallas_call(
        _expert_kernel,
        grid_spec=pltpu.PrefetchScalarGridSpec(
            num_scalar_prefetch=4,
            grid=(nblk,),
            in_specs=[pl.BlockSpec((EXPERT_BLOCK * ROW_TILE, LANES), blk),
                      anyspace, anyspace,
                      pl.BlockSpec((1, 1, f), bsel),
                      pl.BlockSpec((1, 1, f), bsel),
                      pl.BlockSpec((1, 1, d), bsel)],
            out_specs=pl.BlockSpec((EXPERT_BLOCK * ROW_TILE, LANES), blk),
            scratch_shapes=[pltpu.VMEM((2, d, 2 * f), F32),
                            pltpu.VMEM((2, f, d), F32),
                            pltpu.VMEM((d, f), BF16), pltpu.VMEM((d, f), BF16), pltpu.VMEM((f, d), BF16),
                            pltpu.SemaphoreType.DMA((2, 2))]),
        out_shape=jax.ShapeDtypeStruct((n_rows * ROW_TILE, LANES), F32),
        compiler_params=_cparams(("arbitrary",)),
        name="moe_experts",
    )(block_exp, run_id, next_exp, n_active, xs, w_gu, w_dn, bg, bu, bd)


def _combine_kernel(dest_ref, rows_hbm, gate_ref, x1_ref, g2_ref, fg_ref, o_ref, buf, sems, *, tm):
    n = pl.program_id(0)
    slot = n % 2

    def row_gather(step, to_slot, tok, k):
        n_tok = tm * pl.num_programs(0)
        return _tile_copy(rows_hbm, dest_ref[k * n_tok + step * tm + tok], buf.at[to_slot, k], tok, sems.at[to_slot])

    @pl.when(n == 0)
    def _():
        def issue(tok, carry):
            for k in range(TOP_K):
                row_gather(0, 0, tok, k).start()
            return carry
        lax.fori_loop(0, tm, issue, 0, unroll=4)

    @pl.when(n + 1 < pl.num_programs(0))
    def _():
        for tok in range(tm):
            for k in range(TOP_K):
                row_gather(n + 1, 1 - slot, tok, k).start(priority=k % 2)

    for k in range(TOP_K):
        pltpu.make_async_copy(rows_hbm.at[_tile_rows(0, tm)], buf.at[slot, k], sems.at[slot]).wait()

    gate = gate_ref[...]
    parts = None
    for k in range(TOP_K):
        gk = gate[:, k:k + 1]
        tiles = _load_row_tiles(buf.at[slot, k], tm)
        parts = [gk * r for r in tiles] if parts is None else [p + gk * r for p, r in zip(parts, tiles)]
    y = jnp.concatenate(parts, axis=1)
    xo = x1_ref[...] + g2_ref[0] * y
    ms = jnp.mean(xo * xo, axis=-1, keepdims=True)
    o_ref[...] = xo * lax.rsqrt(ms + NORM_EPS) * fg_ref[...]


def _combine(dest, rows, gates, x1, g2, final_g, tm):
    b, s, d = x1.shape
    nb = s // tm
    out = pl.pallas_call(
        functools.partial(_combine_kernel, tm=tm),
        grid_spec=pltpu.PrefetchScalarGridSpec(
            num_scalar_prefetch=1,
            grid=(b * nb,),
            in_specs=[pl.BlockSpec(memory_space=pl.ANY),
                      pl.BlockSpec((tm, TOP_K), lambda n, ds: (n, 0)),
                      pl.BlockSpec((tm, d), lambda n, ds: (n, 0)),
                      pl.BlockSpec((1, 1, d), lambda n, ds: (n // nb, 0, 0)),
                      pl.BlockSpec((1, d), lambda n, ds: (0, 0))],
            out_specs=pl.BlockSpec((tm, d), lambda n, ds: (n, 0)),
            scratch_shapes=[pltpu.VMEM((2, TOP_K, tm * ROW_TILE, LANES), F32), pltpu.SemaphoreType.DMA((2,))]),
        out_shape=jax.ShapeDtypeStruct((b * s, d), F32),
        compiler_params=_cparams(("arbitrary",)),
        name="moe_combine",
    )(dest, rows, gates, x1.reshape(b * s, d), g2, final_g)
    return out.reshape(b, s, d)


def _pick(n, prefs):
    for p in prefs:
        if n % p == 0:
            return p
    raise ValueError(f"no tile for {n}")


def kernel(x, c, ctx, c_ctx, w_mod, b_mod, norm1_g, w_in, lam_q1, lam_k1, lam_q2, lam_k2, subln_g,
           sink, w_out, norm2_g, w_router, b_router, w_gate_up, b_gate_up, w_down, b_down, final_g):
    b, s, d = x.shape
    c_len = ctx.shape[1]
    assert w_mod.shape[0] == 1, "single-layer block"
    assert d == ROW_TILE * LANES, "dispatched token rows are one (8, 128) f32 tile each"
    t = b * s

    pad = (-(b + 1)) % 8
    cvecs = jnp.concatenate([c, c_ctx[None, :], jnp.zeros((pad, d), F32)], axis=0)
    mod = _adaln(cvecs, w_mod[0], b_mod[0])
    sh1, sc1, g1, sh2, sc2, g2 = [mod[:b, k * d:(k + 1) * d].reshape(b, 1, d) for k in range(6)]
    csh1 = mod[b:b + 1, 0:d].reshape(1, 1, d)
    csc1 = mod[b:b + 1, d:2 * d].reshape(1, 1, d)

    w_in_bf = w_in[0].astype(BF16)
    cos, sin = _rope_tables(s)
    n1 = norm1_g[0].reshape(1, d)
    qat, ka_all, vat_all, qbt, kb, vbt, kbc, vbct = _inproj_latent(
        x, ctx, n1, sh1, sc1, csh1, csc1, w_in_bf, cos, sin, _pick(s, (1024, 512, 256)))

    sk = s + c_len
    ya = _diff_attn(qat, ka_all, vat_all,
                    lam_q1[0].reshape(1, -1), lam_k1[0].reshape(1, -1),
                    lam_q2[0].reshape(1, -1), lam_k2[0].reshape(1, -1),
                    subln_g[0].reshape(1, -1),
                    _pick(s, (1024, 512, 256, 128)), _pick(sk, (768, 512, 384, 256, 128)))

    yb = _win_attn(qbt, kb, vbt, kbc, vbct, sink[0].reshape(1, -1), _pick(s, (256, 128)))

    w_out_bf = w_out[0].astype(BF16)
    w_r_hi = w_router[0].astype(BF16)
    w_r_lo = (w_router[0] - w_r_hi.astype(F32)).astype(BF16)
    x1, hx2, code, gates, counts = _outproj_router(
        ya, yb, w_out_bf[:QA_COLS], w_out_bf[QA_COLS:], x, g1, norm2_g[0].reshape(1, d), sh2, sc2,
        jnp.concatenate([w_r_hi, w_r_lo], axis=1).T, b_router[0].reshape(-1, 1), _pick(s, (1024, 512, 256, 128)))

    n_assign = t * TOP_K
    n_rows = n_assign + N_EXPERTS * EXPERT_BLOCK
    counts_i = counts.reshape(-1).astype(jnp.int32)
    padded = ((counts_i + EXPERT_BLOCK - 1) // EXPERT_BLOCK) * EXPERT_BLOCK
    pend = jnp.cumsum(padded).astype(jnp.int32)
    pbound = jnp.concatenate([jnp.zeros((1,), jnp.int32), pend])
    block_start = jnp.arange(n_rows // EXPERT_BLOCK, dtype=jnp.int32) * EXPERT_BLOCK
    block_exp = jnp.minimum(jnp.sum((pend[None, :] <= block_start[:, None]).astype(jnp.int32), axis=1),
                            N_EXPERTS - 1)
    n_active = pend[-1:] // EXPERT_BLOCK
    changed = jnp.concatenate([jnp.ones((1,), jnp.int32), (block_exp[1:] != block_exp[:-1]).astype(jnp.int32)])
    run_id = jnp.cumsum(changed).astype(jnp.int32) - 1
    e_ids = jnp.arange(N_EXPERTS, dtype=jnp.int32)
    later_nonempty = (e_ids[None, :] > e_ids[:, None]) & (padded[None, :] > 0)
    next_of_expert = jnp.min(jnp.where(later_nonempty, e_ids[None, :], N_EXPERTS), axis=1)
    next_of_expert = jnp.where(next_of_expert < N_EXPERTS, next_of_expert, -1).astype(jnp.int32)
    next_exp = jnp.sum(jnp.where(block_exp[:, None] == e_ids[None, :], next_of_expert[None, :], 0), axis=1)

    dest = _sorted_rows(pbound, code, _pick(t, (2048, 1024, 512, 256)))
    xs = _dispatch(pbound, counts_i, dest, hx2, n_rows, _pick(n_assign, (1024,)))

    f = w_down.shape[2]
    bg = b_gate_up[0, :, 0::2].reshape(N_EXPERTS, 1, f)
    bu = b_gate_up[0, :, 1::2].reshape(N_EXPERTS, 1, f)
    bd = b_down[0].reshape(N_EXPERTS, 1, d)
    rows = _experts(block_exp, run_id, next_exp, n_active, xs, w_gate_up[0], w_down[0], bg, bu, bd)

    return _combine(dest[:TOP_K].reshape(-1), rows, gates[:TOP_K].T, x1, g2, final_g.reshape(1, d), _pick(s, (256,)))
```

```python
import functools
import math

import jax
import jax.numpy as jnp
import numpy as np
from jax import lax
from jax.experimental import pallas as pl
from jax.experimental.pallas import tpu as pltpu

F32 = jnp.float32
BF16 = jnp.bfloat16

GRID_W = 64
NORM_EPS = 1e-6
ROPE_BASE = 10000.0
MASK_VALUE = -1e30
DA_HEADS = 4
HEAD_DIM = 64
WA_HEADS = 8
WA_KV_HEADS = 2
WA_GROUP = WA_HEADS // WA_KV_HEADS
WINDOW = 128
N_EXPERTS = 32
TOP_K = 4
SWIGLU_LIMIT = 7.0
SWIGLU_ALPHA = 1.702
EXPERT_BLOCK = 256
LAM_INIT = 0.8 - 0.6 * math.exp(-0.3 * 0)
LOG2E = math.log2(math.e)

QA_COLS = DA_HEADS * 2 * HEAD_DIM
KA_COLS = QA_COLS
VA_COLS = QA_COLS
QB_COLS = WA_HEADS * HEAD_DIM
KB_COLS = WA_KV_HEADS * HEAD_DIM
VB_COLS = KB_COLS
O_QA = 0
O_KA = O_QA + QA_COLS
O_VA = O_KA + KA_COLS
O_QB = O_VA + VA_COLS
O_KB = O_QB + QB_COLS
O_VB = O_KB + KB_COLS
IN_COLS = O_VB + VB_COLS

LANES = 128
ROW_TILE = 8
INPROJ_SUBTILES = 2
MXU_COLS = 256
ONES_ROWS = 16
VMEM_LIMIT = 56 * 1024 * 1024


def _cparams(sem):
    return pltpu.CompilerParams(dimension_semantics=sem, vmem_limit_bytes=VMEM_LIMIT)


def _adaln_kernel(c_ref, w_ref, b_ref, o_ref):
    cv = c_ref[...]
    s = cv * (1.0 / (1.0 + jnp.exp(-cv)))
    rows = s.shape[0]
    w = w_ref[...]
    s_hi = s.astype(BF16)
    s_lo = (s - s_hi.astype(F32)).astype(BF16)
    w_hi = w.astype(BF16)
    w_lo = (w - w_hi.astype(F32)).astype(BF16)
    part = jnp.dot(jnp.concatenate([s_hi, s_lo], axis=0), w_hi, preferred_element_type=F32)
    o_ref[...] = part[:rows] + part[rows:] + jnp.dot(s_hi, w_lo, preferred_element_type=F32) + b_ref[...]


def _adaln(cvecs, w_mod, b_mod):
    rows, d = cvecs.shape
    n = w_mod.shape[1]
    tn = 1024
    return pl.pallas_call(
        _adaln_kernel,
        grid=(n // tn,),
        in_specs=[pl.BlockSpec((rows, d), lambda j: (0, 0)),
                  pl.BlockSpec((d, tn), lambda j: (0, j)),
                  pl.BlockSpec((1, tn), lambda j: (0, j))],
        out_specs=pl.BlockSpec((rows, tn), lambda j: (0, j)),
        out_shape=jax.ShapeDtypeStruct((rows, n), F32),
        compiler_params=_cparams(("arbitrary",)),
        name="adaln",
    )(cvecs, w_mod, b_mod.reshape(1, n))


def _rope_section(sec, cos, sin):
    tm = sec.shape[0]
    lane = lax.broadcasted_iota(jnp.int32, (tm, LANES), 1)
    low = (lane % 32) < 16
    outs = []
    for j in range(sec.shape[1] // LANES):
        c = sec[:, j * LANES:(j + 1) * LANES]
        partner = jnp.where(low, pltpu.roll(c, LANES - 16, 1), pltpu.roll(c, 16, 1))
        outs.append(c * cos + partner * sin)
    return jnp.concatenate(outs, axis=1)


def _modulated_norm(x, g, shift, scale):
    ms = jnp.mean(x * x, axis=-1, keepdims=True)
    return (x * lax.rsqrt(ms + NORM_EPS) * g) * (1.0 + scale) + shift


def _inproj_latent_kernel(x_ref, ctx_ref, g_ref, sh_ref, sc_ref, csh_ref, csc_ref, w_ref, cos_ref, sin_ref,
                          qat_ref, ka_ref, vat_ref, qbt_ref, kb_ref, vbt_ref, kbc_ref, vbct_ref, p_ref):
    tm = x_ref.shape[1]
    th = tm // INPROJ_SUBTILES
    qscale = HEAD_DIM ** -0.5 * LOG2E

    def project(sub):
        rows = slice(sub * th, (sub + 1) * th)
        h = _modulated_norm(x_ref[0, rows], g_ref[...], sh_ref[0], sc_ref[0])
        p_ref[sub] = jnp.dot(h.astype(BF16), w_ref[...], preferred_element_type=F32)

    def emit(sub):
        rows = slice(sub * th, (sub + 1) * th)
        p = p_ref[sub]
        cos = cos_ref[rows]
        sin = sin_ref[rows]
        qat_ref[0, :, rows] = (_rope_section(p[:, O_QA:O_KA], cos, sin) * qscale).T.astype(BF16)
        ka_ref[0, rows] = _rope_section(p[:, O_KA:O_VA], cos, sin).astype(BF16)
        vat_ref[0, :, rows] = p[:, O_VA:O_QB].T.astype(BF16)
        qbt_ref[0, :, rows] = (_rope_section(p[:, O_QB:O_KB], cos, sin) * qscale).T.astype(BF16)
        kb_ref[0, rows] = _rope_section(p[:, O_KB:O_VB], cos, sin).astype(BF16)
        vbt_ref[0, :, rows] = p[:, O_VB:IN_COLS].T.astype(BF16)

    latent_steps = pl.num_programs(1) - 1

    @pl.when(pl.program_id(1) < latent_steps)
    def _():
        project(0)
        for sub in range(1, INPROJ_SUBTILES):
            project(sub)
            emit(sub - 1)
        emit(INPROJ_SUBTILES - 1)

    @pl.when(pl.program_id(1) == latent_steps)
    def _():
        c = ctx_ref.shape[1]
        h = _modulated_norm(ctx_ref[0], g_ref[...], csh_ref[0], csc_ref[0])
        p = jnp.dot(h.astype(BF16), w_ref[...], preferred_element_type=F32)
        ka_ref[0, :c] = p[:, O_KA:O_VA].astype(BF16)
        vat_ref[0, :, :c] = p[:, O_VA:O_QB].T.astype(BF16)
        kbc_ref[0] = p[:, O_KB:O_VB].astype(BF16)
        vbct_ref[0] = p[:, O_VB:IN_COLS].T.astype(BF16)


def _rope_tables(n_tok):
    pos = np.arange(n_tok)
    nf = HEAD_DIM // 4
    inv = ROPE_BASE ** (-np.arange(nf) / nf)
    ar = (pos // GRID_W)[:, None] * inv
    ac = (pos % GRID_W)[:, None] * inv
    cos = np.concatenate([np.cos(ar), np.cos(ar), np.cos(ac), np.cos(ac)], axis=1)
    sin = np.concatenate([-np.sin(ar), np.sin(ar), -np.sin(ac), np.sin(ac)], axis=1)
    reps = (1, LANES // HEAD_DIM)
    return jnp.asarray(np.tile(cos, reps), F32), jnp.asarray(np.tile(sin, reps), F32)


def _inproj_latent(x, ctx, g, shift, scale, cshift, cscale, w_bf16, cos, sin, tm):
    b, s, d = x.shape
    c = ctx.shape[1]
    nb = s // tm
    assert c <= tm and c % LANES == 0
    lat = lambda i: jnp.minimum(i, nb - 1)
    row = lambda bi, i: (bi, lat(i), 0)
    colt = lambda bi, i: (bi, 0, lat(i))
    mod = lambda bi, i: (bi, 0, 0)
    cmod = lambda bi, i: (0, 0, 0)
    fixed = lambda bi, i: (0, 0)
    whole = lambda bi, i: (bi, 0, 0)
    return pl.pallas_call(
        _inproj_latent_kernel,
        grid=(b, nb + 1),
        in_specs=[pl.BlockSpec((1, tm, d), row),
                  pl.BlockSpec((1, c, d), whole),
                  pl.BlockSpec((1, d), fixed),
                  pl.BlockSpec((1, 1, d), mod),
                  pl.BlockSpec((1, 1, d), mod),
                  pl.BlockSpec((1, 1, d), cmod),
                  pl.BlockSpec((1, 1, d), cmod),
                  pl.BlockSpec((d, IN_COLS), fixed),
                  pl.BlockSpec((tm, LANES), lambda bi, i: (lat(i), 0)),
                  pl.BlockSpec((tm, LANES), lambda bi, i: (lat(i), 0))],
        out_specs=[pl.BlockSpec((1, QA_COLS, tm), colt),
                   pl.BlockSpec((1, tm, KA_COLS), lambda bi, i: (bi, i, 0)),
                   pl.BlockSpec((1, VA_COLS, tm), lambda bi, i: (bi, 0, i)),
                   pl.BlockSpec((1, QB_COLS, tm), colt),
                   pl.BlockSpec((1, tm, KB_COLS), row),
                   pl.BlockSpec((1, VB_COLS, tm), colt),
                   pl.BlockSpec((1, c, KB_COLS), whole),
                   pl.BlockSpec((1, VB_COLS, c), whole)],
        out_shape=[jax.ShapeDtypeStruct((b, QA_COLS, s), BF16),
                   jax.ShapeDtypeStruct((b, s + c, KA_COLS), BF16),
                   jax.ShapeDtypeStruct((b, VA_COLS, s + c), BF16),
                   jax.ShapeDtypeStruct((b, QB_COLS, s), BF16),
                   jax.ShapeDtypeStruct((b, s, KB_COLS), BF16),
                   jax.ShapeDtypeStruct((b, VB_COLS, s), BF16),
                   jax.ShapeDtypeStruct((b, c, KB_COLS), BF16),
                   jax.ShapeDtypeStruct((b, VB_COLS, c), BF16)],
        scratch_shapes=[pltpu.VMEM((INPROJ_SUBTILES, tm // INPROJ_SUBTILES, IN_COLS), F32)],
        compiler_params=_cparams(("arbitrary", "arbitrary")),
        name="inproj_latent",
    )(x, ctx, g, shift, scale, cshift, cscale, w_bf16, cos, sin)


def _diff_attn_kernel(qt_ref, k_ref, vt_ref, lq1_ref, lk1_ref, lq2_ref, lk2_ref, sg_ref, o_ref,
                      m_ref, acc_ref, s_ref, *, tq, tk):
    d = HEAD_DIM
    hw = 2 * d
    n_chunks = k_ref.shape[1] // tk
    n_tiles = qt_ref.shape[2] // tq
    ones = jnp.ones((ONES_ROWS, tk), BF16)
    lam = (jnp.exp(jnp.sum(lq1_ref[...] * lk1_ref[...], axis=-1, keepdims=True))
           - jnp.exp(jnp.sum(lq2_ref[...] * lk2_ref[...], axis=-1, keepdims=True)) + LAM_INIT)

    def query_rhs(t):
        qt = qt_ref[0, :, pl.ds(pl.multiple_of(t * tq, tq), tq)]
        row = lax.broadcasted_iota(jnp.int32, qt.shape, 0)
        zero = jnp.zeros_like(qt)
        return jnp.where(row < d, qt, zero), jnp.where(row >= d, qt, zero)

    def scores(c, j, rhs):
        off = pl.multiple_of(j * tk, tk)
        s_ref[c] = jnp.dot(k_ref[0, pl.ds(off, tk), :], rhs[c], preferred_element_type=F32)

    def accumulate(c, j):
        off = pl.multiple_of(j * tk, tk)
        vt = jnp.concatenate([vt_ref[0, :, pl.ds(off, tk)], ones], axis=0)
        st = s_ref[c]
        m_old = m_ref[c]
        m_new = jnp.maximum(m_old, jnp.max(st, axis=0, keepdims=True))
        alpha = jnp.exp2(m_old - m_new)
        p = jnp.exp2(st - m_new).astype(BF16)
        acc_ref[c] = alpha * acc_ref[c] + jnp.dot(vt, p, preferred_element_type=F32)
        m_ref[c] = m_new

    group = 5 if (n_chunks - 1) % 5 == 0 else 1

    scores(0, 0, query_rhs(0))

    def tile(t, carry):
        rhs = query_rhs(t)
        m_ref[...] = jnp.full(m_ref.shape, -jnp.inf, F32)
        acc_ref[...] = jnp.zeros(acc_ref.shape, F32)

        def chunk_group(jj, carry2):
            for r in range(group):
                j = group * jj + r
                scores(1, j, rhs)
                accumulate(0, j)
                scores(0, j + 1, rhs)
                accumulate(1, j)
            return carry2

        lax.fori_loop(0, (n_chunks - 1) // group, chunk_group, 0)
        scores(1, n_chunks - 1, rhs)
        accumulate(0, n_chunks - 1)
        scores(0, 0, query_rhs(jnp.minimum(t + 1, n_tiles - 1)))
        accumulate(1, n_chunks - 1)

        a1 = acc_ref[0]
        a2 = acc_ref[1]
        ot = a1[:hw] / a1[hw:hw + 1] - lam * (a2[:hw] / a2[hw:hw + 1])
        ms = jnp.mean(ot * ot, axis=0, keepdims=True)
        ot = ot * lax.rsqrt(ms + NORM_EPS)
        o_ref[0, pl.ds(pl.multiple_of(t * tq, tq), tq), :] = (ot.T * (sg_ref[...] * (1.0 - LAM_INIT))).astype(o_ref.dtype)
        return carry

    lax.fori_loop(0, n_tiles, tile, 0)


def _diff_attn(qat, ka, vat, lq1, lk1, lq2, lk2, subln_g, tq, tk):
    b, _, s = qat.shape
    sk = ka.shape[1]
    hw = 2 * HEAD_DIM
    vec = lambda bi, h: (0, 0)
    return pl.pallas_call(
        functools.partial(_diff_attn_kernel, tq=tq, tk=tk),
        grid=(b, DA_HEADS),
        in_specs=[pl.BlockSpec((1, hw, s), lambda bi, h: (bi, h, 0)),
                  pl.BlockSpec((1, sk, hw), lambda bi, h: (bi, 0, h)),
                  pl.BlockSpec((1, hw, sk), lambda bi, h: (bi, h, 0)),
                  pl.BlockSpec((1, HEAD_DIM), vec),
                  pl.BlockSpec((1, HEAD_DIM), vec),
                  pl.BlockSpec((1, HEAD_DIM), vec),
                  pl.BlockSpec((1, HEAD_DIM), vec),
                  pl.BlockSpec((1, hw), vec)],
        out_specs=pl.BlockSpec((1, s, hw), lambda bi, h: (bi, 0, h)),
        out_shape=jax.ShapeDtypeStruct((b, s, DA_HEADS * hw), BF16),
        scratch_shapes=[pltpu.VMEM((2, 1, tq), F32),
                        pltpu.VMEM((2, hw + ONES_ROWS, tq), F32),
                        pltpu.VMEM((2, tk, tq), F32)],
        compiler_params=_cparams(("arbitrary", "arbitrary")),
        name="diff_attn",
    )(qat, ka, vat, lq1, lk1, lq2, lk2, subln_g)


def _win_attn_kernel(qt_ref, k_ref, vt_ref, kc_ref, vct_ref, sink_ref, o_ref, s_ref, *, tq, lk):
    d = HEAD_DIM
    grp = WA_GROUP
    s_len = k_ref.shape[1]
    c_len = kc_ref.shape[1]
    nk = lk + c_len
    i = pl.program_id(1)
    q0 = i * tq
    start = pl.multiple_of(jnp.clip(q0 - WINDOW, 0, s_len - lk), LANES)
    keys = jnp.concatenate([k_ref[0, pl.ds(start, lk), :], kc_ref[0]], axis=0)
    kpos = start + lax.broadcasted_iota(jnp.int32, (lk, tq), 0)
    qpos = q0 + lax.broadcasted_iota(jnp.int32, (lk, tq), 1)
    visible = jnp.abs(kpos - qpos) <= WINDOW
    visible = jnp.concatenate([visible] * grp, axis=1)
    ones = jnp.ones((ONES_ROWS, nk), BF16)
    qt = qt_ref[0]
    blank = jnp.zeros((d, grp * tq), BF16)
    outs = []
    for kv in range(WA_KV_HEADS):
        heads = range(kv * grp, (kv + 1) * grp)
        qcat = jnp.concatenate([qt[h * d:(h + 1) * d, :] for h in heads], axis=1)
        rhs = jnp.concatenate([qcat if j == kv else blank for j in range(WA_KV_HEADS)], axis=0)
        s_ref[kv] = jnp.dot(keys, rhs, preferred_element_type=F32)
    for kv in range(WA_KV_HEADS):
        heads = range(kv * grp, (kv + 1) * grp)
        st = s_ref[kv]
        st = jnp.concatenate([jnp.where(visible, st[:lk], MASK_VALUE), st[lk:]], axis=0)
        sink = jnp.concatenate([jnp.broadcast_to(sink_ref[:, h:h + 1] * LOG2E, (1, tq)) for h in heads], axis=1)
        m = jnp.maximum(jnp.max(st, axis=0, keepdims=True), sink)
        p = jnp.exp2(st - m).astype(BF16)
        vt = jnp.concatenate([vt_ref[0, kv * d:(kv + 1) * d, pl.ds(start, lk)],
                              vct_ref[0, kv * d:(kv + 1) * d, :]], axis=1)
        acc = jnp.dot(jnp.concatenate([vt, ones], axis=0), p, preferred_element_type=F32)
        o = acc[:d] / (acc[d:d + 1] + jnp.exp2(sink - m))
        outs.extend(o[:, g * tq:(g + 1) * tq] for g in range(grp))
    o_ref[0] = jnp.concatenate(outs, axis=0).T.astype(o_ref.dtype)


def _win_attn(qbt, kb, vbt, kbc, vbct, sink, tq):
    b, _, s = qbt.shape
    c = kbc.shape[1]
    lk = tq + 2 * WINDOW
    assert s >= lk and tq % LANES == 0
    whole = lambda bi, i: (bi, 0, 0)
    return pl.pallas_call(
        functools.partial(_win_attn_kernel, tq=tq, lk=lk),
        grid=(b, s // tq),
        in_specs=[pl.BlockSpec((1, QB_COLS, tq), lambda bi, i: (bi, 0, i)),
                  pl.BlockSpec((1, s, KB_COLS), whole),
                  pl.BlockSpec((1, VB_COLS, s), whole),
                  pl.BlockSpec((1, c, KB_COLS), whole),
                  pl.BlockSpec((1, VB_COLS, c), whole),
                  pl.BlockSpec((1, WA_HEADS), lambda bi, i: (0, 0))],
        out_specs=pl.BlockSpec((1, tq, QB_COLS), lambda bi, i: (bi, i, 0)),
        out_shape=jax.ShapeDtypeStruct((b, s, QB_COLS), BF16),
        scratch_shapes=[pltpu.VMEM((WA_KV_HEADS, lk + c, WA_GROUP * tq), F32)],
        compiler_params=_cparams(("arbitrary", "arbitrary")),
        name="win_attn",
    )(qbt, kb, vbt, kbc, vbct, sink)


def _outproj_router_kernel(ya_ref, yb_ref, woa_ref, wob_ref, x_ref, g1_ref, n2_ref, sh_ref, sc_ref,
                           wr_ref, br_ref, x1_ref, hx_ref, code_ref, gate_ref, cnt_ref, carry_ref):
    first = jnp.logical_and(pl.program_id(0) == 0, pl.program_id(1) == 0)

    @pl.when(first)
    def _():
        carry_ref[...] = jnp.zeros(carry_ref.shape, F32)

    y = (jnp.dot(ya_ref[0], woa_ref[...], preferred_element_type=F32)
         + jnp.dot(yb_ref[0], wob_ref[...], preferred_element_type=F32))
    x1 = x_ref[0] + g1_ref[0] * y
    x1_ref[0] = x1
    hx = _modulated_norm(x1, n2_ref[...], sh_ref[0], sc_ref[0])
    _store_row_tiles(hx_ref, hx)
    tm = hx.shape[0]
    hx_hi = hx.astype(BF16)
    hx_lo = (hx - hx_hi.astype(F32)).astype(BF16)
    wr = wr_ref[...]
    nt = (((1,), (1,)), ((), ()))
    part = lax.dot_general(wr, hx_hi, nt, preferred_element_type=F32)
    logits = (part[:N_EXPERTS] + part[N_EXPERTS:]
              + lax.dot_general(wr[:N_EXPERTS], hx_lo, nt, preferred_element_type=F32) + br_ref[...])

    row_e = lax.broadcasted_iota(jnp.int32, (N_EXPERTS, tm), 0).astype(F32)
    work = logits
    tops, idxs, hots = [], [], []
    for _k in range(TOP_K):
        m = jnp.max(work, axis=0, keepdims=True)
        idx = jnp.min(jnp.where(work == m, row_e, float(N_EXPERTS)), axis=0, keepdims=True)
        hot = row_e == idx
        work = jnp.where(hot, -jnp.inf, work)
        tops.append(m)
        idxs.append(idx)
        hots.append(hot)
    es = [jnp.exp(t - tops[0]) for t in tops]
    den = es[0] + es[1] + es[2] + es[3]

    multi = jnp.zeros((N_EXPERTS, tm), F32)
    for hot in hots:
        multi = multi + hot.astype(F32)
    r_i = lax.broadcasted_iota(jnp.int32, (tm, tm), 0)
    c_i = lax.broadcasted_iota(jnp.int32, (tm, tm), 1)
    tri = (r_i <= c_i).astype(BF16)
    incl = jnp.dot(multi.astype(BF16), tri, preferred_element_type=F32)
    before = carry_ref[...] + incl - 1.0

    codes, gates = [], []
    for k in range(TOP_K):
        rank = jnp.sum(jnp.where(hots[k], before, 0.0), axis=0, keepdims=True)
        codes.append(idxs[k].astype(jnp.int32) * 65536 + rank.astype(jnp.int32))
        gates.append(es[k] / den)
    fill = ROW_TILE - TOP_K
    code_ref[...] = jnp.concatenate(codes + [jnp.zeros((fill, tm), jnp.int32)], axis=0)
    gate_ref[...] = jnp.concatenate(gates + [jnp.zeros((fill, tm), F32)], axis=0)
    carry_ref[...] = carry_ref[...] + jnp.sum(multi, axis=1, keepdims=True)
    cnt_ref[...] = carry_ref[...]


def _outproj_router(ya, yb, woa, wob, x, g1, n2, sh2, sc2, w_r_parts, b_r, tm):
    b, s, d = x.shape
    nb = s // tm
    row = lambda bi, i: (bi, i, 0)
    mod = lambda bi, i: (bi, 0, 0)
    fixed = lambda bi, i: (0, 0)
    tok = lambda bi, i: (bi * nb + i, 0)
    half = ya.shape[2]
    return pl.pallas_call(
        _outproj_router_kernel,
        grid=(b, nb),
        in_specs=[pl.BlockSpec((1, tm, half), row),
                  pl.BlockSpec((1, tm, half), row),
                  pl.BlockSpec((half, d), fixed),
                  pl.BlockSpec((half, d), lambda bi, i: (1, 0)),
                  pl.BlockSpec((1, tm, d), row),
                  pl.BlockSpec((1, 1, d), mod),
                  pl.BlockSpec((1, d), fixed),
                  pl.BlockSpec((1, 1, d), mod),
                  pl.BlockSpec((1, 1, d), mod),
                  pl.BlockSpec((2 * N_EXPERTS, d), fixed),
                  pl.BlockSpec((N_EXPERTS, 1), fixed)],
        out_specs=[pl.BlockSpec((1, tm, d), row),
                   pl.BlockSpec((tm * ROW_TILE, LANES), tok),
                   pl.BlockSpec((ROW_TILE, tm), lambda bi, i: (0, bi * nb + i)),
                   pl.BlockSpec((ROW_TILE, tm), lambda bi, i: (0, bi * nb + i)),
                   pl.BlockSpec((N_EXPERTS, 1), fixed)],
        out_shape=[jax.ShapeDtypeStruct((b, s, d), F32),
                   jax.ShapeDtypeStruct((b * s * ROW_TILE, LANES), F32),
                   jax.ShapeDtypeStruct((ROW_TILE, b * s), jnp.int32),
                   jax.ShapeDtypeStruct((ROW_TILE, b * s), F32),
                   jax.ShapeDtypeStruct((N_EXPERTS, 1), F32)],
        scratch_shapes=[pltpu.VMEM((N_EXPERTS, 1), F32)],
        compiler_params=_cparams(("arbitrary", "arbitrary")),
        name="outproj_router",
    )(ya, yb, woa, wob, x, g1, n2, sh2, sc2, w_r_parts, b_r)


def _sorted_rows_kernel(pbound_ref, code_ref, dest_ref):
    code = code_ref[...]
    expert = code >> 16
    base = jnp.zeros_like(code)
    for e in range(N_EXPERTS):
        base = jnp.where(expert == e, pbound_ref[e], base)
    dest_ref[...] = base + (code & 0xFFFF)


def _sorted_rows(pbound, code, tn):
    rows, n_tok = code.shape
    return pl.pallas_call(
        _sorted_rows_kernel,
        grid_spec=pltpu.PrefetchScalarGridSpec(
            num_scalar_prefetch=1,
            grid=(n_tok // tn,),
            in_specs=[pl.BlockSpec((rows, tn), lambda i, pb: (0, i))],
            out_specs=pl.BlockSpec((rows, tn), lambda i, pb: (0, i))),
        out_shape=jax.ShapeDtypeStruct((rows, n_tok), jnp.int32),
        compiler_params=_cparams(("arbitrary",)),
        name="moe_sorted_rows",
    )(pbound, code)


def _store_row_tiles(ref, val):
    n = val.shape[0]
    for c in range(ROW_TILE):
        ref[pl.ds(c, n, stride=ROW_TILE), :] = val[:, c * LANES:(c + 1) * LANES]


def _load_row_tiles(ref, n):
    return [ref[pl.ds(c, n, stride=ROW_TILE), :] for c in range(ROW_TILE)]


def _tile_rows(row, count=1):
    if isinstance(row, int):
        return pl.ds(row * ROW_TILE, count * ROW_TILE)
    return pl.ds(pl.multiple_of(row * ROW_TILE, ROW_TILE), count * ROW_TILE)


def _tile_copy(src_ref, src_row, dst_ref, dst_row, sem):
    return pltpu.make_async_copy(src_ref.at[_tile_rows(src_row)], dst_ref.at[_tile_rows(dst_row)], sem)


def _dispatch_kernel(pbound_ref, cnt_ref, dest_ref, hx_ref, xs_hbm, zeros, sem, zsem, *, chunk):
    n_rows = xs_hbm.shape[0] // ROW_TILE
    blk = zeros.shape[0] // ROW_TILE
    pad_sizes = [blk >> (k + 1) for k in range(blk.bit_length() - 1)]

    @pl.when(pl.program_id(0) == 0)
    def _():
        zeros[...] = jnp.zeros(zeros.shape, F32)
        tail_blocks = (n_rows - pbound_ref[N_EXPERTS]) // blk

        def pad_copy(row, size):
            return pltpu.make_async_copy(zeros.at[_tile_rows(0, size)], xs_hbm.at[_tile_rows(row, size)], zsem)

        def pad_expert(e, carry):
            row = pbound_ref[e] + cnt_ref[e]
            n_pad = pbound_ref[e + 1] - row
            for size in pad_sizes:
                take = (n_pad & size) != 0

                @pl.when(take)
                def _():
                    pad_copy(row, size).start()
                row = row + jnp.where(take, size, 0)
            return carry

        def tail_copy(t):
            return pltpu.make_async_copy(zeros, xs_hbm.at[_tile_rows(pbound_ref[N_EXPERTS] + t * blk, blk)], zsem)

        def tail_start(t, carry):
            tail_copy(t).start()
            return carry

        lax.fori_loop(0, N_EXPERTS, pad_expert, 0)
        lax.fori_loop(0, tail_blocks, tail_start, 0)

        def pad_expert_wait(e, carry):
            n_pad = pbound_ref[e + 1] - pbound_ref[e] - cnt_ref[e]
            for size in pad_sizes:
                @pl.when((n_pad & size) != 0)
                def _():
                    pad_copy(0, size).wait()
            return carry

        def tail_wait(t, carry):
            tail_copy(0).wait()
            return carry

        lax.fori_loop(0, N_EXPERTS, pad_expert_wait, 0)
        lax.fori_loop(0, tail_blocks, tail_wait, 0)

    for j in range(chunk):
        _tile_copy(hx_ref, j // TOP_K, xs_hbm, dest_ref[j % TOP_K, j // TOP_K], sem).start(priority=j % 2)
    pltpu.make_async_copy(xs_hbm.at[_tile_rows(0, chunk)], xs_hbm.at[_tile_rows(0, chunk)], sem).wait()


def _dispatch(pbound, counts_i, dest, hx_tiles, n_rows, chunk):
    n_assign = dest.shape[1] * TOP_K
    return pl.pallas_call(
        functools.partial(_dispatch_kernel, chunk=chunk),
        grid_spec=pltpu.PrefetchScalarGridSpec(
            num_scalar_prefetch=2,
            grid=(n_assign // chunk,),
            in_specs=[pl.BlockSpec((ROW_TILE, chunk // TOP_K), lambda i, pb, ct: (0, i), memory_space=pltpu.SMEM),
                      pl.BlockSpec((chunk // TOP_K * ROW_TILE, LANES), lambda i, pb, ct: (i, 0))],
            out_specs=pl.BlockSpec(memory_space=pl.ANY),
            scratch_shapes=[pltpu.VMEM((EXPERT_BLOCK * ROW_TILE, LANES), F32),
                            pltpu.SemaphoreType.DMA(()),
                            pltpu.SemaphoreType.DMA(())]),
        out_shape=jax.ShapeDtypeStruct((n_rows * ROW_TILE, LANES), F32),
        compiler_params=_cparams(("arbitrary",)),
        name="moe_dispatch",
    )(pbound, counts_i, dest, hx_tiles)


def _expert_kernel(bexp_ref, run_ref, nxt_ref, nact_ref, xs_ref, wgu_hbm, wdn_hbm, bg_ref, bu_ref, bd_ref,
                   o_ref, wgu_f, wdn_f, wg_s, wu_s, wd_s, wsem):
    i = pl.program_id(0)
    nact = nact_ref[0]

    def weight_copies(expert, w):
        return (pltpu.make_async_copy(wgu_hbm.at[expert], wgu_f.at[w], wsem.at[w, 0]),
                pltpu.make_async_copy(wdn_hbm.at[expert], wdn_f.at[w], wsem.at[w, 1]))

    @pl.when(i == 0)
    def _():
        for cp in weight_copies(bexp_ref[0], 0):
            cp.start()

    @pl.when(i < nact)
    def _():
        changed = jnp.logical_or(i == 0, bexp_ref[i] != bexp_ref[jnp.maximum(i - 1, 0)])

        @pl.when(changed)
        def _():
            w = run_ref[i] % 2
            for cp in weight_copies(bexp_ref[i], w):
                cp.wait()
            half = MXU_COLS // 2
            src = lax.broadcasted_iota(jnp.int32, (MXU_COLS, MXU_COLS), 0)
            dst = lax.broadcasted_iota(jnp.int32, (MXU_COLS, MXU_COLS), 1)
            perm = (src == jnp.where(dst < half, 2 * dst, 2 * (dst - half) + 1)).astype(BF16)
            for k in range(wgu_f.shape[2] // MXU_COLS):
                wk = wgu_f[w, :, k * MXU_COLS:(k + 1) * MXU_COLS].astype(BF16)
                sep = jnp.dot(wk, perm, preferred_element_type=F32).astype(BF16)
                wg_s[:, k * half:(k + 1) * half] = sep[:, :half]
                wu_s[:, k * half:(k + 1) * half] = sep[:, half:]
            wd_s[...] = wdn_f[w].astype(BF16)

            @pl.when(nxt_ref[i] >= 0)
            def _():
                for cp in weight_copies(nxt_ref[i], 1 - w):
                    cp.start()

        xb = jnp.concatenate([c.astype(BF16) for c in _load_row_tiles(xs_ref, EXPERT_BLOCK)], axis=1)
        g = jnp.dot(xb, wg_s[...], preferred_element_type=F32) + bg_ref[0]
        u = jnp.dot(xb, wu_s[...], preferred_element_type=F32) + bu_ref[0]
        g = jnp.minimum(g, SWIGLU_LIMIT)
        u = jnp.clip(u, -SWIGLU_LIMIT, SWIGLU_LIMIT)
        a = g * (1.0 / (1.0 + jnp.exp(-SWIGLU_ALPHA * g))) * (u + 1.0)
        _store_row_tiles(o_ref, jnp.dot(a.astype(BF16), wd_s[...], preferred_element_type=F32) + bd_ref[0])

    @pl.when(i >= nact)
    def _():
        o_ref[...] = jnp.zeros(o_ref.shape, F32)


def _experts(block_exp, run_id, next_exp, n_active, xs, w_gu, w_dn, bg, bu, bd):
    f, d = w_dn.shape[1:]
    n_rows = xs.shape[0] // ROW_TILE
    nblk = n_rows // EXPERT_BLOCK
    bsel = lambda i, be, ru, nx, na: (be[i], 0, 0)
    blk = lambda i, be, ru, nx, na: (i, 0)
    anyspace = pl.BlockSpec(memory_space=pl.ANY)
    return pl.pallas_call(
        _expert_kernel,
        grid_spec=pltpu.PrefetchScalarGridSpec(
            num_scalar_prefetch=4,
            grid=(nblk,),
            in_specs=[pl.BlockSpec((EXPERT_BLOCK * ROW_TILE, LANES), blk),
                      anyspace, anyspace,
                      pl.BlockSpec((1, 1, f), bsel),
                      pl.BlockSpec((1, 1, f), bsel),
                      pl.BlockSpec((1, 1, d), bsel)],
            out_specs=pl.BlockSpec((EXPERT_BLOCK * ROW_TILE, LANES), blk),
            scratch_shapes=[pltpu.VMEM((2, d, 2 * f), F32),
                            pltpu.VMEM((2, f, d), F32),
                            pltpu.VMEM((d, f), BF16), pltpu.VMEM((d, f), BF16), pltpu.VMEM((f, d), BF16),
                            pltpu.SemaphoreType.DMA((2, 2))]),
        out_shape=jax.ShapeDtypeStruct((n_rows * ROW_TILE, LANES), F32),
        compiler_params=_cparams(("arbitrary",)),
        name="moe_experts",
    )(block_exp, run_id, next_exp, n_active, xs, w_gu, w_dn, bg, bu, bd)


def _combine_kernel(dest_ref, rows_hbm, gate_ref, x1_ref, g2_ref, fg_ref, o_ref, buf, sems, *, tm):
    n = pl.program_id(0)
    slot = n % 2

    def row_gather(step, to_slot, tok, k):
        n_tok = tm * pl.num_programs(0)
        return _tile_copy(rows_hbm, dest_ref[k * n_tok + step * tm + tok], buf.at[to_slot, k], tok, sems.at[to_slot])

    @pl.when(n == 0)
    def _():
        def issue(tok, carry):
            for k in range(TOP_K):
                row_gather(0, 0, tok, k).start()
            return carry
        lax.fori_loop(0, tm, issue, 0, unroll=4)

    @pl.when(n + 1 < pl.num_programs(0))
    def _():
        for tok in range(tm):
            for k in range(TOP_K):
                row_gather(n + 1, 1 - slot, tok, k).start(priority=k % 2)

    for k in range(TOP_K):
        pltpu.make_async_copy(rows_hbm.at[_tile_rows(0, tm)], buf.at[slot, k], sems.at[slot]).wait()

    gate = gate_ref[...]
    parts = None
    for k in range(TOP_K):
        gk = gate[:, k:k + 1]
        tiles = _load_row_tiles(buf.at[slot, k], tm)
        parts = [gk * r for r in tiles] if parts is None else [p + gk * r for p, r in zip(parts, tiles)]
    y = jnp.concatenate(parts, axis=1)
    xo = x1_ref[...] + g2_ref[0] * y
    ms = jnp.mean(xo * xo, axis=-1, keepdims=True)
    o_ref[...] = xo * lax.rsqrt(ms + NORM_EPS) * fg_ref[...]


def _combine(dest, rows, gates, x1, g2, final_g, tm):
    b, s, d = x1.shape
    nb = s // tm
    out = pl.pallas_call(
        functools.partial(_combine_kernel, tm=tm),
        grid_spec=pltpu.PrefetchScalarGridSpec(
            num_scalar_prefetch=1,
            grid=(b * nb,),
            in_specs=[pl.BlockSpec(memory_space=pl.ANY),
                      pl.BlockSpec((tm, TOP_K), lambda n, ds: (n, 0)),
                      pl.BlockSpec((tm, d), lambda n, ds: (n, 0)),
                      pl.BlockSpec((1, 1, d), lambda n, ds: (n // nb, 0, 0)),
                      pl.BlockSpec((1, d), lambda n, ds: (0, 0))],
            out_specs=pl.BlockSpec((tm, d), lambda n, ds: (n, 0)),
            scratch_shapes=[pltpu.VMEM((2, TOP_K, tm * ROW_TILE, LANES), F32), pltpu.SemaphoreType.DMA((2,))]),
        out_shape=jax.ShapeDtypeStruct((b * s, d), F32),
        compiler_params=_cparams(("arbitrary",)),
        name="moe_combine",
    )(dest, rows, gates, x1.reshape(b * s, d), g2, final_g)
    return out.reshape(b, s, d)


def _pick(n, prefs):
    for p in prefs:
        if n % p == 0:
            return p
    raise ValueError(f"no tile for {n}")


def kernel(x, c, ctx, c_ctx, w_mod, b_mod, norm1_g, w_in, lam_q1, lam_k1, lam_q2, lam_k2, subln_g,
           sink, w_out, norm2_g, w_router, b_router, w_gate_up, b_gate_up, w_down, b_down, final_g):
    b, s, d = x.shape
    c_len = ctx.shape[1]
    assert w_mod.shape[0] == 1, "single-layer block"
    assert d == ROW_TILE * LANES, "dispatched token rows are one (8, 128) f32 tile each"
    t = b * s

    pad = (-(b + 1)) % 8
    cvecs = jnp.concatenate([c, c_ctx[None, :], jnp.zeros((pad, d), F32)], axis=0)
    mod = _adaln(cvecs, w_mod[0], b_mod[0])
    sh1, sc1, g1, sh2, sc2, g2 = [mod[:b, k * d:(k + 1) * d].reshape(b, 1, d) for k in range(6)]
    csh1 = mod[b:b + 1, 0:d].reshape(1, 1, d)
    csc1 = mod[b:b + 1, d:2 * d].reshape(1, 1, d)

    w_in_bf = w_in[0].astype(BF16)
    cos, sin = _rope_tables(s)
    n1 = norm1_g[0].reshape(1, d)
    qat, ka_all, vat_all, qbt, kb, vbt, kbc, vbct = _inproj_latent(
        x, ctx, n1, sh1, sc1, csh1, csc1, w_in_bf, cos, sin, _pick(s, (1024, 512, 256)))

    sk = s + c_len
    ya = _diff_attn(qat, ka_all, vat_all,
                    lam_q1[0].reshape(1, -1), lam_k1[0].reshape(1, -1),
                    lam_q2[0].reshape(1, -1), lam_k2[0].reshape(1, -1),
                    subln_g[0].reshape(1, -1),
                    _pick(s, (1024, 512, 256, 128)), _pick(sk, (768, 512, 384, 256, 128)))

    yb = _win_attn(qbt, kb, vbt, kbc, vbct, sink[0].reshape(1, -1), _pick(s, (256, 128)))

    w_out_bf = w_out[0].astype(BF16)
    w_r_hi = w_router[0].astype(BF16)
    w_r_lo = (w_router[0] - w_r_hi.astype(F32)).astype(BF16)
    x1, hx2, code, gates, counts = _outproj_router(
        ya, yb, w_out_bf, w_out_bf, x, g1, norm2_g[0].reshape(1, d), sh2, sc2,
        jnp.concatenate([w_r_hi, w_r_lo], axis=1).T, b_router[0].reshape(-1, 1), _pick(s, (1024, 512, 256, 128)))

    n_assign = t * TOP_K
    n_rows = n_assign + N_EXPERTS * EXPERT_BLOCK
    counts_i = counts.reshape(-1).astype(jnp.int32)
    padded = ((counts_i + EXPERT_BLOCK - 1) // EXPERT_BLOCK) * EXPERT_BLOCK
    pend = jnp.cumsum(padded).astype(jnp.int32)
    pbound = jnp.concatenate([jnp.zeros((1,), jnp.int32), pend])
    block_start = jnp.arange(n_rows // EXPERT_BLOCK, dtype=jnp.int32) * EXPERT_BLOCK
    block_exp = jnp.minimum(jnp.sum((pend[None, :] <= block_start[:, None]).astype(jnp.int32), axis=1),
                            N_EXPERTS - 1)
    n_active = pend[-1:] // EXPERT_BLOCK
    changed = jnp.concatenate([jnp.ones((1,), jnp.int32), (block_exp[1:] != block_exp[:-1]).astype(jnp.int32)])
    run_id = jnp.cumsum(changed).astype(jnp.int32) - 1
    e_ids = jnp.arange(N_EXPERTS, dtype=jnp.int32)
    later_nonempty = (e_ids[None, :] > e_ids[:, None]) & (padded[None, :] > 0)
    next_of_expert = jnp.min(jnp.where(later_nonempty, e_ids[None, :], N_EXPERTS), axis=1)
    next_of_expert = jnp.where(next_of_expert < N_EXPERTS, next_of_expert, -1).astype(jnp.int32)
    next_exp = jnp.sum(jnp.where(block_exp[:, None] == e_ids[None, :], next_of_expert[None, :], 0), axis=1)

    dest = _sorted_rows(pbound, code, _pick(t, (2048, 1024, 512, 256)))
    xs = _dispatch(pbound, counts_i, dest, hx2, n_rows, _pick(n_assign, (1024,)))

    f = w_down.shape[2]
    bg = b_gate_up[0, :, 0::2].reshape(N_EXPERTS, 1, f)
    bu = b_gate_up[0, :, 1::2].reshape(N_EXPERTS, 1, f)
    bd = b_down[0].reshape(N_EXPERTS, 1, d)
    rows = _experts(block_exp, run_id, next_exp, n_active, xs, w_gate_up[0], w_down[0], bg, bu, bd)

    return _combine(dest[:TOP_K].reshape(-1), rows, gates[:TOP_K].T, x1, g2, final_g.reshape(1, d), _pick(s, (256,)))
```
